```python
import math
import jax, jax.numpy as jnp
from jax import lax
import numpy as np

D_MODEL = 2048
BATCH = 4
SEQ = 2048
DEPTH = 2

HEAD_DIM = 128
ROPE_THETA = 10000.0
EPS = 1e-6
RET_HEADS = 8
RET_CHUNK = 128
NSA_HEADS = 8
NSA_KV_HEADS = 2
NSA_GROUP = NSA_HEADS // NSA_KV_HEADS
CMP_BLOCK = 32
CMP_STRIDE = 16
SLC_BLOCK = 64
SLC_TOPK = 16
N_LOCAL_BLOCKS = 2
WINDOW = 512
SLC_Q_BLOCK = 64
WIN_Q_BLOCK = 128
DIFF_HEADS = 8
DIFF_V_DIM = 2 * HEAD_DIM
ATTN_Q_BLOCK = 128

RET_WIDTH = RET_HEADS * HEAD_DIM
NSA_WIDTH = NSA_HEADS * HEAD_DIM
NSA_KV_WIDTH = NSA_KV_HEADS * HEAD_DIM
AB_IN_COLS = 4 * RET_WIDTH + 2 * NSA_WIDTH + 6 * NSA_KV_WIDTH + 3 * NSA_HEADS
AB_OUT_WIDTH = RET_WIDTH + NSA_WIDTH
DIFF_QK_WIDTH = DIFF_HEADS * 2 * HEAD_DIM
DIFF_WIDTH = DIFF_HEADS * DIFF_V_DIM
C_IN_COLS = 2 * DIFF_QK_WIDTH + 2 * DIFF_WIDTH

kernel_name = "hybrid_retention_nsa_diffattn"


def rms_norm(x, g):
    xf = x.astype(jnp.float32)
    y = xf * lax.rsqrt(jnp.mean(xf * xf, axis=-1, keepdims=True) + EPS)
    return (y * g.astype(jnp.float32)).astype(x.dtype)


def head_rms(x):
    xf = x.astype(jnp.float32)
    return (xf * lax.rsqrt(jnp.mean(xf * xf, axis=-1, keepdims=True) + EPS)).astype(x.dtype)


def rope_tables(pos):
    inv = 1.0 / (ROPE_THETA ** (jnp.arange(0, HEAD_DIM, 2, dtype=jnp.float32) / HEAD_DIM))
    ang = pos.astype(jnp.float32)[:, None] * inv[None, :]
    return jnp.cos(ang), jnp.sin(ang)


def apply_rope(x, cos, sin):
    half = x.shape[-1] // 2
    x1, x2 = x[..., :half], x[..., half:]
    c = cos.astype(x.dtype)
    s = sin.astype(x.dtype)
    return jnp.concatenate([x1 * c - x2 * s, x1 * s + x2 * c], axis=-1)


def masked_softmax(s, mask):
    s = jnp.where(mask, s.astype(jnp.float32), -1e30)
    return jax.nn.softmax(s, axis=-1)


def to_heads(t, n_heads):
    B, S, _ = t.shape
    return t.reshape(B, S, n_heads, HEAD_DIM).transpose(0, 2, 1, 3)


def retention(q, k, v):
    B, H, S, Dh = q.shape
    C = RET_CHUNK
    N = S // C
    dt = q.dtype
    log_g = jnp.log1p(-jnp.exp2(-5.0 - jnp.arange(H, dtype=jnp.float32)))
    j = jnp.arange(C, dtype=jnp.float32)
    diff = j[:, None] - j[None, :]
    intra = jnp.where(diff >= 0, jnp.exp(log_g[:, None, None] * jnp.maximum(diff, 0.0)), 0.0)
    q_dec = jnp.exp(log_g[:, None] * (j + 1.0))
    k_dec = jnp.exp(log_g[:, None] * (C - 1.0 - j))
    chunk_dec = jnp.exp(log_g * C)
    k = k * (Dh ** -0.5)
    qc = q.reshape(B, H, N, C, Dh)
    kc = k.reshape(B, H, N, C, Dh)
    vc = v.reshape(B, H, N, C, Dh)
    scores = jnp.einsum('bhnid,bhnjd->bhnij', qc, kc) * intra[:, None].astype(dt)
    inner = jnp.einsum('bhnij,bhnjd->bhnid', scores, vc)
    kv = jnp.einsum('bhnjd,bhnje->bhnde', kc * k_dec[:, None, :, None].astype(dt), vc)

    def step(state, kv_n):
        return state * chunk_dec[None, :, None, None] + kv_n, state

    init = jnp.zeros((B, H, Dh, Dh), jnp.float32)
    _, prev = lax.scan(step, init, jnp.moveaxis(kv, 2, 0).astype(jnp.float32))
    prev = jnp.moveaxis(prev, 0, 2).astype(dt)
    cross = jnp.einsum('bhnid,bhnde->bhnie', qc * q_dec[:, None, :, None].astype(dt), prev)
    return (inner + cross).reshape(B, H, S, Dh)


def nsa_compress(t, pe, w1, w2):
    B, G, S, D = t.shape
    n_cmp = (S - CMP_BLOCK) // CMP_STRIDE + 1
    idx = np.arange(n_cmp)[:, None] * CMP_STRIDE + np.arange(CMP_BLOCK)[None, :]
    blocks = (t[:, :, idx] + pe).reshape(B, G, n_cmp, CMP_BLOCK * D)
    return jax.nn.silu(blocks @ w1) @ w2


def nsa(qg, kc_raw, vc_raw, ks, vs, kw, vw, k_g, pe_k, w1_k, w2_k, pe_v, w1_v, w2_v, cos, sin):
    B, G, R, S, D = qg.shape
    dt = qg.dtype
    scale = HEAD_DIM ** -0.5
    t = jnp.arange(S)
    n_cmp = (S - CMP_BLOCK) // CMP_STRIDE + 1
    cmp_end = jnp.arange(n_cmp) * CMP_STRIDE + CMP_BLOCK - 1
    cos_c, sin_c = rope_tables(cmp_end)
    k_cmp = apply_rope(rms_norm(nsa_compress(kc_raw, pe_k, w1_k, w2_k), k_g), cos_c, sin_c)
    v_cmp = nsa_compress(vc_raw, pe_v, w1_v, w2_v)
    cmask = cmp_end[None, :] <= t[:, None]
    s_cmp = jnp.einsum('bgrsd,bgnd->bgrsn', qg, k_cmp) * scale
    p_cmp = masked_softmax(s_cmp, cmask) * cmask
    o_cmp = jnp.einsum('bgrsn,bgnd->bgrsd', p_cmp.astype(dt), v_cmp)
    n_slc = S // SLC_BLOCK
    c_start = np.arange(n_cmp) * CMP_STRIDE
    s_start = np.arange(n_slc) * SLC_BLOCK
    overlap = jnp.asarray(((c_start[:, None] <= s_start[None, :] + SLC_BLOCK - 1)
                           & (c_start[:, None] + CMP_BLOCK - 1 >= s_start[None, :])).astype(np.float32))
    imp = jnp.einsum('bgrsn,nm->bgsm', p_cmp, overlap)
    blk_t = t // SLC_BLOCK
    jb = jnp.arange(n_slc)
    back = blk_t[:, None] - jb[None, :]
    forced = (jb[None, :] == 0) | ((back >= 0) & (back < N_LOCAL_BLOCKS))
    causal_blk = back >= 0
    score = jnp.where(forced, 1e9, jnp.where(causal_blk, imp, -1e9))
    k_top = min(SLC_TOPK, n_slc)
    _, sel = lax.top_k(score, k_top)
    kb = ks.reshape(B, G, n_slc, SLC_BLOCK, D)
    vb = vs.reshape(B, G, n_slc, SLC_BLOCK, D)
    gather = jax.vmap(jax.vmap(lambda blocks, ids: blocks[ids]))

    def slc_block(q0):
        qb = lax.dynamic_slice_in_dim(qg, q0, SLC_Q_BLOCK, axis=3)
        ib = lax.dynamic_slice_in_dim(sel, q0, SLC_Q_BLOCK, axis=2)
        kg = gather(kb, ib).reshape(B, G, SLC_Q_BLOCK, k_top * SLC_BLOCK, D)
        vg = gather(vb, ib).reshape(B, G, SLC_Q_BLOCK, k_top * SLC_BLOCK, D)
        kpos = (ib[..., None] * SLC_BLOCK + jnp.arange(SLC_BLOCK)).reshape(B, G, SLC_Q_BLOCK, k_top * SLC_BLOCK)
        qpos = q0 + jnp.arange(SLC_Q_BLOCK)
        mask = kpos <= qpos[None, None, :, None]
        s = jnp.einsum('bgrqd,bgqkd->bgrqk', qb, kg) * scale
        p = masked_softmax(s, mask[:, :, None])
        return jnp.einsum('bgrqk,bgqkd->bgrqd', p.astype(dt), vg)

    o_slc = lax.map(slc_block, jnp.arange(0, S, SLC_Q_BLOCK))
    o_slc = jnp.moveaxis(o_slc, 0, 3).reshape(B, G, R, S, D)
    kwp = jnp.pad(kw, ((0, 0), (0, 0), (WINDOW, 0), (0, 0)))
    vwp = jnp.pad(vw, ((0, 0), (0, 0), (WINDOW, 0), (0, 0)))

    def win_block(q0):
        qb = lax.dynamic_slice_in_dim(qg, q0, WIN_Q_BLOCK, axis=3)
        kband = lax.dynamic_slice_in_dim(kwp, q0, WINDOW + WIN_Q_BLOCK, axis=2)
        vband = lax.dynamic_slice_in_dim(vwp, q0, WINDOW + WIN_Q_BLOCK, axis=2)
        kpos = q0 - WINDOW + jnp.arange(WINDOW + WIN_Q_BLOCK)
        qpos = q0 + jnp.arange(WIN_Q_BLOCK)
        d = qpos[:, None] - kpos[None, :]
        mask = (kpos[None, :] >= 0) & (d >= 0) & (d < WINDOW)
        s = jnp.einsum('bgrqd,bgkd->bgrqk', qb, kband) * scale
        p = masked_softmax(s, mask)
        return jnp.einsum('bgrqk,bgkd->bgrqd', p.astype(dt), vband)

    o_win = lax.map(win_block, jnp.arange(0, S, WIN_Q_BLOCK))
    o_win = jnp.moveaxis(o_win, 0, 3).reshape(B, G, R, S, D)
    return o_cmp, o_slc, o_win


def hybrid_ab_layer(x, norm_g, w_in, w_out, nsa_q_g, nsa_k_g, pe_k, w1_k, w2_k, pe_v, w1_v, w2_v, cos, sin):
    B, S, _ = x.shape
    h = rms_norm(x, norm_g)
    proj = h @ w_in
    sizes = [RET_WIDTH] * 4 + [NSA_WIDTH] + [NSA_KV_WIDTH] * 6 + [NSA_WIDTH]
    cuts = [int(c) for c in np.cumsum(sizes)]
    (rq, rk, rv, rgate, nq, kc, vc, ks, vs, kw, vw, ngate, ngl) = jnp.split(proj, cuts, axis=-1)
    y_ret = retention(apply_rope(to_heads(rq, RET_HEADS), cos, sin),
                      apply_rope(to_heads(rk, RET_HEADS), cos, sin),
                      to_heads(rv, RET_HEADS))
    y_ret = head_rms(y_ret).transpose(0, 2, 1, 3).reshape(B, S, RET_WIDTH) * jax.nn.silu(rgate)
    q = apply_rope(rms_norm(to_heads(nq, NSA_HEADS), nsa_q_g), cos, sin)
    qg = q.reshape(B, NSA_KV_HEADS, NSA_GROUP, S, HEAD_DIM)
    kv_heads = lambda t: to_heads(t, NSA_KV_HEADS)
    ks_h = apply_rope(rms_norm(kv_heads(ks), nsa_k_g), cos, sin)
    kw_h = apply_rope(rms_norm(kv_heads(kw), nsa_k_g), cos, sin)
    o_cmp, o_slc, o_win = nsa(qg, kv_heads(kc), kv_heads(vc), ks_h, kv_heads(vs), kw_h, kv_heads(vw),
                              nsa_k_g, pe_k, w1_k, w2_k, pe_v, w1_v, w2_v, cos, sin)
    gates = jax.nn.sigmoid(ngl.astype(jnp.float32)).reshape(B, S, 3, NSA_HEADS)
    gates = gates.transpose(2, 0, 3, 1)[..., None].astype(x.dtype)
    hs = (B, NSA_HEADS, S, HEAD_DIM)
    y_nsa = gates[0] * o_cmp.reshape(hs) + gates[1] * o_slc.reshape(hs) + gates[2] * o_win.reshape(hs)
    y_nsa = y_nsa.transpose(0, 2, 1, 3).reshape(B, S, NSA_WIDTH) * jax.nn.silu(ngate)
    y = jnp.concatenate([y_ret, y_nsa], axis=-1)
    return x + y @ w_out


def diff_layer(x, norm_g, w_in, w_out, q_g, k_g, lq1, lk1, lq2, lk2, lambda_init, cos, sin):
    B, S, _ = x.shape
    dt = x.dtype
    scale = HEAD_DIM ** -0.5
    h = rms_norm(x, norm_g)
    proj = h @ w_in
    q, k, v, gate = jnp.split(proj, [DIFF_QK_WIDTH, 2 * DIFF_QK_WIDTH, 2 * DIFF_QK_WIDTH + DIFF_WIDTH], axis=-1)
    q = q.reshape(B, S, DIFF_HEADS, 2, HEAD_DIM).transpose(0, 2, 3, 1, 4)
    k = k.reshape(B, S, DIFF_HEADS, 2, HEAD_DIM).transpose(0, 2, 3, 1, 4)
    q = apply_rope(rms_norm(q, q_g), cos, sin)
    k = apply_rope(rms_norm(k, k_g), cos, sin)
    v = v.reshape(B, S, DIFF_HEADS, DIFF_V_DIM).transpose(0, 2, 1, 3)
    lam = (jnp.exp(jnp.sum(lq1.astype(jnp.float32) * lk1.astype(jnp.float32)))
           - jnp.exp(jnp.sum(lq2.astype(jnp.float32) * lk2.astype(jnp.float32))) + lambda_init)
    kpos = jnp.arange(S)

    def blk(q0):
        qb = lax.dynamic_slice_in_dim(q, q0, ATTN_Q_BLOCK, axis=3)
        s = jnp.einsum('bhcqd,bhckd->bhcqk', qb, k) * scale
        qpos = q0 + jnp.arange(ATTN_Q_BLOCK)
        p = masked_softmax(s, kpos[None, :] <= qpos[:, None])
        a = p[:, :, 0] - lam * p[:, :, 1]
        return jnp.einsum('bhqk,bhkd->bhqd', a.astype(dt), v)

    o = lax.map(blk, jnp.arange(0, S, ATTN_Q_BLOCK))
    o = jnp.moveaxis(o, 0, 2).reshape(B, DIFF_HEADS, S, DIFF_V_DIM)
    o = head_rms(o) * (1.0 - lambda_init)
    o = o.transpose(0, 2, 1, 3).reshape(B, S, DIFF_WIDTH) * jax.nn.silu(gate)
    return x + o @ w_out


def setup_inputs(seed: int = 0) -> dict:
    key = jax.random.key(seed)
    ks = jax.random.split(key, 24)
    f32 = jnp.float32

    def nrm(k, shape, scale):
        return jax.random.normal(k, shape, f32) * scale

    def gain(k, n):
        return 1.0 + 0.02 * jax.random.normal(k, (n,), f32)

    cin = CMP_BLOCK * HEAD_DIM
    return {
        "x": nrm(ks[0], (BATCH, SEQ, D_MODEL), 1.0),
        "l0_norm_g": gain(ks[1], D_MODEL),
        "l0_w_in": nrm(ks[2], (D_MODEL, AB_IN_COLS), D_MODEL ** -0.5),
        "l0_w_out": nrm(ks[3], (AB_OUT_WIDTH, D_MODEL), AB_OUT_WIDTH ** -0.5),
        "l0_nsa_q_norm_g": gain(ks[4], HEAD_DIM),
        "l0_nsa_k_norm_g": gain(ks[5], HEAD_DIM),
        "l0_cmp_pe_k": nrm(ks[6], (CMP_BLOCK, HEAD_DIM), 0.1),
        "l0_cmp_w1_k": nrm(ks[7], (cin, HEAD_DIM), cin ** -0.5),
        "l0_cmp_w2_k": nrm(ks[8], (HEAD_DIM, HEAD_DIM), HEAD_DIM ** -0.5),
        "l0_cmp_pe_v": nrm(ks[9], (CMP_BLOCK, HEAD_DIM), 0.1),
        "l0_cmp_w1_v": nrm(ks[10], (cin, HEAD_DIM), cin ** -0.5),
        "l0_cmp_w2_v": nrm(ks[11], (HEAD_DIM, HEAD_DIM), HEAD_DIM ** -0.5),
        "l1_norm_g": gain(ks[12], D_MODEL),
        "l1_w_in": nrm(ks[13], (D_MODEL, C_IN_COLS), D_MODEL ** -0.5),
        "l1_w_out": nrm(ks[14], (DIFF_WIDTH, D_MODEL), DIFF_WIDTH ** -0.5),
        "l1_q_norm_g": gain(ks[15], HEAD_DIM),
        "l1_k_norm_g": gain(ks[16], HEAD_DIM),
        "l1_lambda_q1": nrm(ks[17], (HEAD_DIM,), 0.1),
        "l1_lambda_k1": nrm(ks[18], (HEAD_DIM,), 0.1),
        "l1_lambda_q2": nrm(ks[19], (HEAD_DIM,), 0.1),
        "l1_lambda_k2": nrm(ks[20], (HEAD_DIM,), 0.1),
    }


def reference(x, l0_norm_g, l0_w_in, l0_w_out, l0_nsa_q_norm_g, l0_nsa_k_norm_g,
              l0_cmp_pe_k, l0_cmp_w1_k, l0_cmp_w2_k, l0_cmp_pe_v, l0_cmp_w1_v, l0_cmp_w2_v,
              l1_norm_g, l1_w_in, l1_w_out, l1_q_norm_g, l1_k_norm_g,
              l1_lambda_q1, l1_lambda_k1, l1_lambda_q2, l1_lambda_k2):
    S = x.shape[1]
    cos, sin = rope_tables(jnp.arange(S))
    layer_params = [
        (l0_norm_g, l0_w_in, l0_w_out, l0_nsa_q_norm_g, l0_nsa_k_norm_g,
         l0_cmp_pe_k, l0_cmp_w1_k, l0_cmp_w2_k, l0_cmp_pe_v, l0_cmp_w1_v, l0_cmp_w2_v),
        (l1_norm_g, l1_w_in, l1_w_out, l1_q_norm_g, l1_k_norm_g,
         l1_lambda_q1, l1_lambda_k1, l1_lambda_q2, l1_lambda_k2),
    ]
    for i in range(DEPTH):
        if i % 2 == 0:
            x = hybrid_ab_layer(x, *layer_params[i], cos, sin)
        else:
            lambda_init = 0.8 - 0.6 * math.exp(-0.3 * i)
            x = diff_layer(x, *layer_params[i], lambda_init, cos, sin)
    return x
```

```python
import functools
import math

import numpy as np
import jax
import jax.numpy as jnp
from jax import lax
from jax.experimental import pallas as pl
from jax.experimental.pallas import tpu as pltpu

F32 = jnp.float32
BF16 = jnp.bfloat16

D_MODEL = 2048
HEAD_DIM = 128
HALF = HEAD_DIM // 2
ROPE_THETA = 10000.0
EPS = 1e-6
RET_HEADS = 8
RET_CHUNK = 128
NSA_HEADS = 8
NSA_KV_HEADS = 2
NSA_GROUP = NSA_HEADS // NSA_KV_HEADS
CMP_BLOCK = 32
CMP_STRIDE = 16
SLC_BLOCK = 64
SLC_TOPK = 16
N_LOCAL_BLOCKS = 2
WINDOW = 512
DIFF_HEADS = 8
DIFF_V_DIM = 2 * HEAD_DIM
QK_SCALE = HEAD_DIM ** -0.5

RET_WIDTH = RET_HEADS * HEAD_DIM
NSA_WIDTH = NSA_HEADS * HEAD_DIM
NSA_KV_WIDTH = NSA_KV_HEADS * HEAD_DIM
AB_MAIN_COLS = 4 * RET_WIDTH + 2 * NSA_WIDTH + 6 * NSA_KV_WIDTH
N_GATE_COLS = 3 * NSA_HEADS
DIFF_WIDTH = DIFF_HEADS * DIFF_V_DIM
C_IN_COLS = 4 * DIFF_WIDTH

LANES = 128
MASKED = -1e30
M_INIT = -1e29
VMEM_LIMIT = 56 * 1024 * 1024

PROJ_TM = 1024
PROJ_TN = 512
NSA_TQ = 128
NSA_TK = 128
DIFF_TQ = 256
DIFF_TK = 256


def _dot(a, b):
    return jnp.dot(a, b, preferred_element_type=F32)


def _dot_nt(a, b):
    return lax.dot_general(a, b, (((1,), (1,)), ((), ())), preferred_element_type=F32)


def _dot_tn(a, b):
    return lax.dot_general(a, b, (((0,), (0,)), ((), ())), preferred_element_type=F32)


def _silu(x):
    return x / (1.0 + jnp.exp(-x))


def _rms(x):
    return x * lax.rsqrt(jnp.mean(x * x, axis=-1, keepdims=True) + EPS)


def _rope(x, cos, sin_signed):
    return x * cos + pltpu.roll(x, HALF, axis=1) * sin_signed


def _segment_epilogue(seg, kind, cos, sin, gq, gk):
    op, scale = kind
    if op == "plain":
        return seg
    if op == "silu":
        return _silu(seg)
    if op == "nrq":
        seg = _rms(seg) * gq
    elif op == "nrk":
        seg = _rms(seg) * gk
    seg = _rope(seg, cos, sin)
    if scale != 1.0:
        seg = seg * scale
    return seg


def _proj_kernel(*refs, tile_modes, has_gates):
    if has_gates:
        x_ref, g_ref, w_ref, cos_ref, sin_ref, gq_ref, gk_ref, wg_ref, o_ref, og_ref, h_scr = refs
    else:
        x_ref, g_ref, w_ref, cos_ref, sin_ref, gq_ref, gk_ref, o_ref, h_scr = refs
    j = pl.program_id(1)

    @pl.when(j == 0)
    def _():
        x = x_ref[...]
        h = (_rms(x) * g_ref[...]).astype(BF16)
        h_scr[...] = h
        if has_gates:
            z = _dot(h, wg_ref[...])
            og_ref[...] = 1.0 / (1.0 + jnp.exp(-z))

    def make_branch(lo, hi, kinds):
        @pl.when((j >= lo) & (j < hi))
        def _():
            acc = _dot(h_scr[...], w_ref[...])
            for c, kind in enumerate(kinds):
                seg = acc[:, c * LANES:(c + 1) * LANES]
                seg = _segment_epilogue(seg, kind, cos_ref[...], sin_ref[...], gq_ref[...], gk_ref[...])
                o_ref[:, c * LANES:(c + 1) * LANES] = seg.astype(o_ref.dtype)

    for lo, hi, kinds in tile_modes:
        make_branch(lo, hi, kinds)


def _norm_proj(x2d, g, w, cos, sin, gq, gk, tile_modes, seq, wg=None):
    m, d = x2d.shape
    n = w.shape[1]
    tm, tn = PROJ_TM, PROJ_TN
    s_tiles = seq // tm
    has_gates = wg is not None
    in_specs = [
        pl.BlockSpec((tm, d), lambda i, j: (i, 0)),
        pl.BlockSpec((1, d), lambda i, j: (0, 0)),
        pl.BlockSpec((d, tn), lambda i, j: (0, j)),
        pl.BlockSpec((tm, LANES), lambda i, j: (i % s_tiles, 0)),
        pl.BlockSpec((tm, LANES), lambda i, j: (i % s_tiles, 0)),
        pl.BlockSpec((1, LANES), lambda i, j: (0, 0)),
        pl.BlockSpec((1, LANES), lambda i, j: (0, 0)),
    ]
    args = [x2d, g.reshape(1, d), w, cos, sin, gq.reshape(1, LANES), gk.reshape(1, LANES)]
    out_shape = [jax.ShapeDtypeStruct((m, n), BF16)]
    out_specs = [pl.BlockSpec((tm, tn), lambda i, j: (i, j))]
    if has_gates:
        ng = wg.shape[1]
        in_specs.append(pl.BlockSpec((d, ng), lambda i, j: (0, 0)))
        args.append(wg)
        out_shape.append(jax.ShapeDtypeStruct((m, ng), F32))
        out_specs.append(pl.BlockSpec((tm, ng), lambda i, j: (i, 0)))
    return pl.pallas_call(
        functools.partial(_proj_kernel, tile_modes=tile_modes, has_gates=has_gates),
        grid=(m // tm, n // tn),
        in_specs=in_specs,
        out_specs=out_specs,
        out_shape=out_shape,
        scratch_shapes=[pltpu.VMEM((tm, d), BF16)],
        compiler_params=pltpu.CompilerParams(
            dimension_semantics=("parallel", "arbitrary"), vmem_limit_bytes=VMEM_LIMIT),
        name="norm_proj",
    )(*args)


def _oproj_kernel(*refs, n_terms):
    y_refs, w_refs = refs[:n_terms], refs[n_terms:2 * n_terms]
    x_ref, o_ref = refs[2 * n_terms:]
    acc = x_ref[...]
    for y_ref, w_ref in zip(y_refs, w_refs):
        acc = acc + _dot(y_ref[...], w_ref[...])
    o_ref[...] = acc


def _out_proj(ys, ws, x2d):
    m, n = x2d.shape
    tm, tn = PROJ_TM, PROJ_TN
    y_specs = [pl.BlockSpec((tm, y.shape[1]), lambda i, j: (i, 0)) for y in ys]
    w_specs = [pl.BlockSpec((w.shape[0], tn), lambda i, j: (0, j)) for w in ws]
    return pl.pallas_call(
        functools.partial(_oproj_kernel, n_terms=len(ys)),
        grid=(m // tm, n // tn),
        in_specs=y_specs + w_specs + [pl.BlockSpec((tm, tn), lambda i, j: (i, j))],
        out_specs=pl.BlockSpec((tm, tn), lambda i, j: (i, j)),
        out_shape=jax.ShapeDtypeStruct((m, n), F32),
        compiler_params=pltpu.CompilerParams(
            dimension_semantics=("parallel", "arbitrary"), vmem_limit_bytes=VMEM_LIMIT),
        name="out_proj",
    )(*ys, *ws, x2d)


def _ret_kernel(q_ref, k_ref, v_ref, gate_ref, intra_ref, qdec_ref, kdec_ref, cdec_ref, o_ref, *, n_chunks):
    c = RET_CHUNK
    intra = intra_ref[0]
    qdec = qdec_ref[0]
    kdec = kdec_ref[0]
    cdec = cdec_ref[0]
    state = jnp.zeros((HEAD_DIM, HEAD_DIM), F32)
    for n in range(n_chunks):
        rows = slice(n * c, (n + 1) * c)
        q = q_ref[0, rows, :]
        k = k_ref[0, rows, :]
        v = v_ref[0, rows, :]
        scores = _dot_nt(q, k) * intra
        inner = _dot(scores.astype(BF16), v)
        cross = _dot((q.astype(F32) * qdec).astype(BF16), state.astype(BF16))
        kv = _dot_tn((k.astype(F32) * kdec).astype(BF16), v)
        state = state * cdec + kv
        o = _rms(inner + cross) * gate_ref[0, rows, :].astype(F32)
        o_ref[0, rows, :] = o.astype(o_ref.dtype)


def _retention(proj, intra, qdec, kdec, cdec):
    b, s, _ = proj.shape
    h = RET_HEADS
    y_shape = (b, s, RET_WIDTH)
    head = lambda off: pl.BlockSpec((1, s, HEAD_DIM), lambda bi, hi: (bi, 0, off + hi))
    table = lambda rows: pl.BlockSpec((1, rows, HEAD_DIM), lambda bi, hi: (hi, 0, 0))
    return pl.pallas_call(
        functools.partial(_ret_kernel, n_chunks=s // RET_CHUNK),
        grid=(b, h),
        in_specs=[head(0), head(h), head(2 * h), head(3 * h),
                  table(RET_CHUNK), table(RET_CHUNK), table(RET_CHUNK), table(1)],
        out_specs=pl.BlockSpec((1, s, HEAD_DIM), lambda bi, hi: (bi, 0, hi)),
        out_shape=jax.ShapeDtypeStruct(y_shape, BF16),
        compiler_params=pltpu.CompilerParams(
            dimension_semantics=("parallel", "arbitrary"), vmem_limit_bytes=VMEM_LIMIT),
        name="retention",
    )(proj, proj, proj, proj, intra, qdec, kdec, cdec)


def _cmp_kernel(kc_ref, vc_ref, pek_ref, w1k_ref, w2k_ref, pev_ref, w1v_ref, w2v_ref, kg_ref,
                cos_ref, sin_ref, ko_ref, vo_ref, t_scr, *, seq):
    n_rows = seq // CMP_STRIDE

    def compress(src_ref, pe_ref, w1_ref, w2_ref):
        t_scr[0:seq, :] = src_ref[0].astype(F32)
        t_scr[seq:seq + CMP_STRIDE, :] = jnp.zeros((CMP_STRIDE, HEAD_DIM), F32)
        acc = jnp.zeros((n_rows, HEAD_DIM), F32)
        for r in range(CMP_BLOCK):
            rows = t_scr[pl.ds(r, n_rows, stride=CMP_STRIDE), :] + pe_ref[r:r + 1, :]
            acc = acc + _dot(rows.astype(BF16), w1_ref[r])
        return _dot(_silu(acc).astype(BF16), w2_ref[...])

    kc = compress(kc_ref, pek_ref, w1k_ref, w2k_ref)
    kc = _rope(_rms(kc) * kg_ref[...], cos_ref[...], sin_ref[...])
    ko_ref[0, 0] = kc.astype(ko_ref.dtype)
    vo_ref[0, 0] = compress(vc_ref, pev_ref, w1v_ref, w2v_ref).astype(vo_ref.dtype)


def _nsa_compress(proj, kc_off, vc_off, pe_k, w1_k, w2_k, pe_v, w1_v, w2_v, k_g, cos_c, sin_c):
    b, s, _ = proj.shape
    g = NSA_KV_HEADS
    n_rows = s // CMP_STRIDE
    head = lambda off: pl.BlockSpec((1, s, HEAD_DIM), lambda bi, gi: (bi, 0, off + gi))
    full = lambda shape: pl.BlockSpec(shape, lambda bi, gi: (0,) * len(shape))
    out_spec = pl.BlockSpec((1, 1, n_rows, HEAD_DIM), lambda bi, gi: (bi, gi, 0, 0))
    out_sds = jax.ShapeDtypeStruct((b, g, n_rows, HEAD_DIM), BF16)
    return pl.pallas_call(
        functools.partial(_cmp_kernel, seq=s),
        grid=(b, g),
        in_specs=[head(kc_off), head(vc_off),
                  full((CMP_BLOCK, HEAD_DIM)), full((CMP_BLOCK, HEAD_DIM, HEAD_DIM)), full((HEAD_DIM, HEAD_DIM)),
                  full((CMP_BLOCK, HEAD_DIM)), full((CMP_BLOCK, HEAD_DIM, HEAD_DIM)), full((HEAD_DIM, HEAD_DIM)),
                  full((1, HEAD_DIM)), full((n_rows, HEAD_DIM)), full((n_rows, HEAD_DIM))],
        out_specs=[out_spec, out_spec],
        out_shape=[out_sds, out_sds],
        scratch_shapes=[pltpu.VMEM((s + CMP_STRIDE, HEAD_DIM), F32)],
        compiler_params=pltpu.CompilerParams(
            dimension_semantics=("parallel", "arbitrary"), vmem_limit_bytes=VMEM_LIMIT),
        name="nsa_compress",
    )(proj, proj, pe_k, w1_k, w2_k, pe_v, w1_v, w2_v, k_g.reshape(1, HEAD_DIM), cos_c, sin_c)


def _flash_init(m_scr, l_scr, acc_scr):
    m_scr[...] = jnp.full(m_scr.shape, M_INIT, F32)
    l_scr[...] = jnp.zeros(l_scr.shape, F32)
    acc_scr[...] = jnp.zeros(acc_scr.shape, F32)


def _flash_step(q, k, v, bias, m_scr, l_scr, acc_scr):
    s = _dot_nt(q, k)
    if bias is not None:
        s = s + bias
    m_old = m_scr[...]
    m_new = jnp.maximum(m_old, jnp.max(s, axis=-1, keepdims=True))
    alpha = jnp.exp(m_old - m_new)
    p = jnp.exp(s - m_new)
    l_scr[...] = alpha * l_scr[...] + jnp.sum(p, axis=-1, keepdims=True)
    acc_scr[...] = alpha * acc_scr[...] + _dot(p.astype(BF16), v)
    m_scr[...] = m_new


def _rep_heads(bias):
    return jnp.concatenate([bias] * NSA_GROUP, axis=0)


def _nsa_kernel(q_ref, ngate_ref, gates_ref, kcmp_ref, vcmp_ref, ks_ref, vs_ref, kw_ref, vw_ref,
                ovl_ref, eexp_ref, o_ref, bias_scr, m_scr, l_scr, acc_scr, *, seq):
    tq, tk = NSA_TQ, NSA_TK
    n_slc = seq // SLC_BLOCK
    qi = pl.program_id(2)
    q0 = qi * tq
    qblk = q_ref[0]
    q4 = jnp.concatenate([qblk[:, r * HEAD_DIM:(r + 1) * HEAD_DIM] for r in range(NSA_GROUP)], axis=0)

    tpos = q0 + lax.broadcasted_iota(jnp.int32, (tq, LANES), 0)
    cidx = lax.broadcasted_iota(jnp.int32, (tq, LANES), 1)
    cbias = jnp.where(cidx * CMP_STRIDE + (CMP_BLOCK - 1) <= tpos, 0.0, MASKED)
    s = _dot_nt(q4, kcmp_ref[0, 0]) + _rep_heads(cbias)
    m = jnp.maximum(jnp.max(s, axis=-1, keepdims=True), M_INIT)
    e = jnp.exp(s - m)
    l = jnp.sum(e, axis=-1, keepdims=True)
    p = e * (1.0 / jnp.maximum(l, 1e-30))
    o_cmp = _dot(p.astype(BF16), vcmp_ref[0, 0])
    psum = p[0:tq]
    for r in range(1, NSA_GROUP):
        psum = psum + p[r * tq:(r + 1) * tq]

    ovl = ovl_ref[...]
    p_hi = psum.astype(BF16)
    rem = psum - p_hi.astype(F32)
    p_mid = rem.astype(BF16)
    p_lo = (rem - p_mid.astype(F32)).astype(BF16)
    imp = _dot_nt(ovl, p_hi) + _dot_nt(ovl, p_mid) + _dot_nt(ovl, p_lo)
    jb = lax.broadcasted_iota(jnp.int32, (n_slc, tq), 0)
    blk_t = jnp.right_shift(q0 + lax.broadcasted_iota(jnp.int32, (n_slc, tq), 1), int(math.log2(SLC_BLOCK)))
    back = blk_t - jb
    forced = (jb == 0) | ((back >= 0) & (back < N_LOCAL_BLOCKS))
    score = jnp.where(forced, 1e9, jnp.where(back >= 0, imp, -1e9))
    rank = jnp.zeros((n_slc, tq), F32)
    for mp in range(n_slc):
        row = score[mp:mp + 1, :]
        ahead = (row > score) | ((row == score) & (jb > mp))
        rank = rank + jnp.where(ahead, 1.0, 0.0)
    sel_t = jnp.where(rank < float(min(SLC_TOPK, n_slc)), 1.0, 0.0)
    sel_t = jnp.concatenate([sel_t, jnp.zeros((LANES - n_slc, tq), F32)], axis=0).astype(BF16)
    ri = lax.broadcasted_iota(jnp.int32, (tq, tq), 0)
    ci = lax.broadcasted_iota(jnp.int32, (tq, tq), 1)
    eye = jnp.where(ri == ci, 1.0, 0.0).astype(BF16)
    sel = _dot_nt(eye, sel_t).astype(BF16)
    sel_keys = _dot(sel, eexp_ref[...])
    for kt in range(seq // tk):
        bias_scr[kt] = jnp.where(sel_keys[:, kt * tk:(kt + 1) * tk] > 0.5, 0.0, MASKED)

    causal = jnp.where(ri >= ci, 0.0, MASKED)

    _flash_init(m_scr, l_scr, acc_scr)

    def slc_body(kt, carry):
        k0 = pl.multiple_of(kt * tk, tk)
        _flash_step(q4, ks_ref[0, pl.ds(k0, tk), :], vs_ref[0, pl.ds(k0, tk), :],
                    _rep_heads(bias_scr[kt]), m_scr, l_scr, acc_scr)
        return carry

    lax.fori_loop(0, qi, slc_body, 0)
    kd = pl.multiple_of(q0, tk)
    _flash_step(q4, ks_ref[0, pl.ds(kd, tk), :], vs_ref[0, pl.ds(kd, tk), :],
                _rep_heads(bias_scr[qi] + causal), m_scr, l_scr, acc_scr)
    o_slc = acc_scr[...] * (1.0 / l_scr[...])

    _flash_init(m_scr, l_scr, acc_scr)

    def win_body(kt, carry):
        k0 = pl.multiple_of(kt * tk, tk)
        d = (q0 - k0) + ri - ci
        wbias = jnp.where((d >= 0) & (d < WINDOW), 0.0, MASKED)
        _flash_step(q4, kw_ref[0, pl.ds(k0, tk), :], vw_ref[0, pl.ds(k0, tk), :],
                    _rep_heads(wbias), m_scr, l_scr, acc_scr)
        return carry

    lax.fori_loop(jnp.maximum(qi - WINDOW // tk, 0), qi + 1, win_body, 0)
    o_win = acc_scr[...] * (1.0 / l_scr[...])

    gates = gates_ref[...]
    for r in range(NSA_GROUP):
        rows = slice(r * tq, (r + 1) * tq)
        g_cmp = gates[:, r:r + 1]
        g_slc = gates[:, NSA_GROUP + r:NSA_GROUP + r + 1]
        g_win = gates[:, 2 * NSA_GROUP + r:2 * NSA_GROUP + r + 1]
        y = g_cmp * o_cmp[rows] + g_slc * o_slc[rows] + g_win * o_win[rows]
        cols = slice(r * HEAD_DIM, (r + 1) * HEAD_DIM)
        o_ref[0, :, cols] = (y * ngate_ref[0, :, cols].astype(F32)).astype(o_ref.dtype)


def _nsa_attention(proj, gates, kcmp, vcmp, ovl_t, eexp, offs):
    b, s, _ = proj.shape
    g = NSA_KV_HEADS
    tq = NSA_TQ
    nq = s // tq
    gw = NSA_GROUP * HEAD_DIM
    q_spec = lambda off: pl.BlockSpec((1, tq, gw), lambda bi, gi, qi: (bi, qi, off + gi))
    kv_spec = lambda off: pl.BlockSpec((1, s, HEAD_DIM), lambda bi, gi, qi: (bi, 0, off + gi))
    cmp_spec = pl.BlockSpec((1, 1, s // CMP_STRIDE, HEAD_DIM), lambda bi, gi, qi: (bi, gi, 0, 0))
    full = lambda shape: pl.BlockSpec(shape, lambda bi, gi, qi: (0,) * len(shape))
    rows4 = NSA_GROUP * tq
    return pl.pallas_call(
        functools.partial(_nsa_kernel, seq=s),
        grid=(b, g, nq),
        in_specs=[q_spec(offs["nq"] // gw), q_spec(offs["ngate"] // gw),
                  pl.BlockSpec((tq, LANES), lambda bi, gi, qi: (bi * nq + qi, gi)),
                  cmp_spec, cmp_spec,
                  kv_spec(offs["ks"] // HEAD_DIM), kv_spec(offs["vs"] // HEAD_DIM),
                  kv_spec(offs["kw"] // HEAD_DIM), kv_spec(offs["vw"] // HEAD_DIM),
                  full(ovl_t.shape), full(eexp.shape)],
        out_specs=pl.BlockSpec((1, tq, gw), lambda bi, gi, qi: (bi, qi, gi)),
        out_shape=jax.ShapeDtypeStruct((b, s, NSA_WIDTH), BF16),
        scratch_shapes=[pltpu.VMEM((s // NSA_TK, tq, NSA_TK), F32),
                        pltpu.VMEM((rows4, 1), F32), pltpu.VMEM((rows4, 1), F32),
                        pltpu.VMEM((rows4, HEAD_DIM), F32)],
        compiler_params=pltpu.CompilerParams(
            dimension_semantics=("parallel", "parallel", "arbitrary"), vmem_limit_bytes=VMEM_LIMIT),
        name="nsa_attention",
    )(proj, proj, gates, kcmp, vcmp, proj, proj, proj, proj, ovl_t, eexp)


def _diff_kernel(q_ref, gate_ref, k_ref, v_ref, lam_ref, o_ref, m1, l1, a1, m2, l2, a2, *, lambda_init):
    tq, tk = DIFF_TQ, DIFF_TK
    qi = pl.program_id(2)
    q = q_ref[0]
    q1 = q[:, :HEAD_DIM]
    q2 = q[:, HEAD_DIM:]
    _flash_init(m1, l1, a1)
    _flash_init(m2, l2, a2)

    def step(k0, bias):
        k = k_ref[0, pl.ds(k0, tk), :]
        v = v_ref[0, pl.ds(k0, tk), :]
        _flash_step(q1, k[:, :HEAD_DIM], v, bias, m1, l1, a1)
        _flash_step(q2, k[:, HEAD_DIM:], v, bias, m2, l2, a2)

    def body(kt, carry):
        step(pl.multiple_of(kt * tk, tk), None)
        return carry

    lax.fori_loop(0, qi, body, 0)
    ri = lax.broadcasted_iota(jnp.int32, (tq, tk), 0)
    ci = lax.broadcasted_iota(jnp.int32, (tq, tk), 1)
    step(pl.multiple_of(qi * tq, tk), jnp.where(ri >= ci, 0.0, MASKED))

    lp = lam_ref[...]
    lam = (jnp.exp(jnp.sum(lp[0:1] * lp[1:2], axis=-1, keepdims=True))
           - jnp.exp(jnp.sum(lp[2:3] * lp[3:4], axis=-1, keepdims=True)) + lambda_init)
    o = a1[...] * (1.0 / l1[...]) - lam * (a2[...] * (1.0 / l2[...]))
    o = _rms(o) * (1.0 - lambda_init)
    o_ref[0] = (o * gate_ref[0].astype(F32)).astype(o_ref.dtype)


def _diff_attention(proj, lam_params, lambda_init):
    b, s, _ = proj.shape
    h = DIFF_HEADS
    tq = DIFF_TQ
    w = DIFF_V_DIM
    q_spec = lambda off: pl.BlockSpec((1, tq, w), lambda bi, hi, qi: (bi, qi, off + hi))
    kv_spec = lambda off: pl.BlockSpec((1, s, w), lambda bi, hi, qi: (bi, 0, off + hi))
    stat = pltpu.VMEM((tq, 1), F32)
    acc = pltpu.VMEM((tq, w), F32)
    return pl.pallas_call(
        functools.partial(_diff_kernel, lambda_init=lambda_init),
        grid=(b, h, s // tq),
        in_specs=[q_spec(0), q_spec(3 * h), kv_spec(h), kv_spec(2 * h),
                  pl.BlockSpec(lam_params.shape, lambda bi, hi, qi: (0, 0))],
        out_specs=pl.BlockSpec((1, tq, w), lambda bi, hi, qi: (bi, qi, hi)),
        out_shape=jax.ShapeDtypeStruct((b, s, DIFF_WIDTH), BF16),
        scratch_shapes=[stat, stat, acc, stat, stat, acc],
        compiler_params=pltpu.CompilerParams(
            dimension_semantics=("parallel", "parallel", "arbitrary"), vmem_limit_bytes=VMEM_LIMIT),
        name="diff_attention",
    )(proj, proj, proj, proj, lam_params)


def _rope_tables(pos):
    inv = 1.0 / (ROPE_THETA ** (jnp.arange(0, HEAD_DIM, 2, dtype=F32) / HEAD_DIM))
    ang = pos.astype(F32)[:, None] * inv[None, :]
    cos, sin = jnp.cos(ang), jnp.sin(ang)
    return jnp.concatenate([cos, cos], axis=-1), jnp.concatenate([-sin, sin], axis=-1)


def _retention_tables():
    h, c = RET_HEADS, RET_CHUNK
    log_g = jnp.log1p(-jnp.exp2(-5.0 - jnp.arange(h, dtype=F32)))
    j = jnp.arange(c, dtype=F32)
    diff = j[:, None] - j[None, :]
    intra = jnp.where(diff >= 0, jnp.exp(log_g[:, None, None] * jnp.maximum(diff, 0.0)), 0.0)
    q_dec = jnp.exp(log_g[:, None] * (j + 1.0))
    k_dec = jnp.exp(log_g[:, None] * (c - 1.0 - j))
    chunk_dec = jnp.exp(log_g * c)
    wide = lambda t: jnp.broadcast_to(t[:, :, None], (h, t.shape[1], HEAD_DIM))
    return intra, wide(q_dec), wide(k_dec), wide(chunk_dec[:, None])


def _selection_tables(seq):
    n_cmp_rows = seq // CMP_STRIDE
    n_slc = seq // SLC_BLOCK
    c_start = np.arange(n_cmp_rows) * CMP_STRIDE
    s_start = np.arange(n_slc) * SLC_BLOCK
    overlap_t = ((c_start[None, :] <= s_start[:, None] + SLC_BLOCK - 1)
                 & (c_start[None, :] + CMP_BLOCK - 1 >= s_start[:, None]))
    expand = (np.arange(seq)[None, :] // SLC_BLOCK) == np.arange(LANES)[:, None]
    return jnp.asarray(overlap_t, BF16), jnp.asarray(expand, BF16)


def _l0_tile_modes():
    t = PROJ_TN
    per = t // LANES
    seg = lambda op, scale=1.0: (op, scale)
    off = {}
    modes = []
    col = 0

    def add(name, width, kinds):
        nonlocal col
        off[name] = col
        n_tiles = width // t
        modes.append((col // t, col // t + n_tiles, kinds))
        col += width

    add("rq", RET_WIDTH, [seg("rope")] * per)
    add("rk", RET_WIDTH, [seg("rope", QK_SCALE)] * per)
    add("rv", RET_WIDTH, [seg("plain")] * per)
    add("rgate", RET_WIDTH, [seg("silu")] * per)
    add("nq", NSA_WIDTH, [seg("nrq", QK_SCALE)] * per)
    kvh = NSA_KV_HEADS
    add("kc", 2 * NSA_KV_WIDTH, [seg("plain")] * per)
    off["vc"] = off["kc"] + NSA_KV_WIDTH
    add("ks", 2 * NSA_KV_WIDTH, [seg("nrk")] * kvh + [seg("plain")] * kvh)
    off["vs"] = off["ks"] + NSA_KV_WIDTH
    add("kw", 2 * NSA_KV_WIDTH, [seg("nrk")] * kvh + [seg("plain")] * kvh)
    off["vw"] = off["kw"] + NSA_KV_WIDTH
    add("ngate", NSA_WIDTH, [seg("silu")] * per)
    assert col == AB_MAIN_COLS
    return modes, off


def _l1_tile_modes():
    per = PROJ_TN // LANES
    n = DIFF_WIDTH // PROJ_TN
    return [(0, n, [("nrq", QK_SCALE)] * per), (n, 2 * n, [("nrk", 1.0)] * per),
            (2 * n, 3 * n, [("plain", 1.0)] * per), (3 * n, 4 * n, [("silu", 1.0)] * per)]


def kernel(x, l0_norm_g, l0_w_in, l0_w_out, l0_nsa_q_norm_g, l0_nsa_k_norm_g, l0_cmp_pe_k, l0_cmp_w1_k, l0_cmp_w2_k, l0_cmp_pe_v, l0_cmp_w1_v, l0_cmp_w2_v, l1_norm_g, l1_w_in, l1_w_out, l1_q_norm_g, l1_k_norm_g, l1_lambda_q1, l1_lambda_k1, l1_lambda_q2, l1_lambda_k2):
    b, s, d = x.shape
    m = b * s
    x2d = x.reshape(m, d)
    cos, sin = _rope_tables(jnp.arange(s))
    cos_c, sin_c = _rope_tables(jnp.arange(s // CMP_STRIDE) * CMP_STRIDE + CMP_BLOCK - 1)
    intra, qdec, kdec, cdec = _retention_tables()
    ovl_t, eexp = _selection_tables(s)

    modes0, off = _l0_tile_modes()
    w0 = l0_w_in[:, :AB_MAIN_COLS].astype(BF16)
    wg = l0_w_in[:, AB_MAIN_COLS:].reshape(d, 3, NSA_KV_HEADS, NSA_GROUP).transpose(0, 2, 1, 3)
    wg = wg.reshape(d, NSA_KV_HEADS, 3 * NSA_GROUP)
    wg = jnp.pad(wg, ((0, 0), (0, 0), (0, LANES - 3 * NSA_GROUP))).reshape(d, NSA_KV_HEADS * LANES).astype(BF16)
    proj0, gates = _norm_proj(x2d, l0_norm_g, w0, cos, sin, l0_nsa_q_norm_g, l0_nsa_k_norm_g, modes0, s, wg=wg)
    proj0 = proj0.reshape(b, s, AB_MAIN_COLS)
    y_ret = _retention(proj0, intra, qdec, kdec, cdec)
    w1k = l0_cmp_w1_k.astype(BF16).reshape(CMP_BLOCK, HEAD_DIM, HEAD_DIM)
    w1v = l0_cmp_w1_v.astype(BF16).reshape(CMP_BLOCK, HEAD_DIM, HEAD_DIM)
    kcmp, vcmp = _nsa_compress(proj0, off["kc"] // HEAD_DIM, off["vc"] // HEAD_DIM,
                               l0_cmp_pe_k, w1k, l0_cmp_w2_k.astype(BF16),
                               l0_cmp_pe_v, w1v, l0_cmp_w2_v.astype(BF16),
                               l0_nsa_k_norm_g, cos_c, sin_c)
    y_nsa = _nsa_attention(proj0, gates, kcmp, vcmp, ovl_t, eexp, off)
    w_out0 = l0_w_out.astype(BF16)
    x1 = _out_proj([y_ret.reshape(m, RET_WIDTH), y_nsa.reshape(m, NSA_WIDTH)],
                   [w_out0[:RET_WIDTH], w_out0[RET_WIDTH:]], x2d)

    lambda_init = 0.8 - 0.6 * math.exp(-0.3 * 1)
    proj1 = _norm_proj(x1, l1_norm_g, l1_w_in.astype(BF16), cos, sin, l1_q_norm_g, l1_k_norm_g,
                       _l1_tile_modes(), s)[0]
    lam_params = jnp.stack([l1_lambda_q1, l1_lambda_k1, l1_lambda_q2, l1_lambda_k2]).astype(F32)
    y1 = _diff_attention(proj1.reshape(b, s, C_IN_COLS), lam_params, lambda_init)
    out = _out_proj([y1.reshape(m, DIFF_WIDTH)], [l1_w_out.astype(BF16)], x1)
    return out.reshape(b, s, d)
```

```python
import functools
import math

import numpy as np
import jax
import jax.numpy as jnp
from jax import lax
from jax.experimental import pallas as pl
from jax.experimental.pallas import tpu as pltpu

F32 = jnp.float32
BF16 = jnp.bfloat16

D_MODEL = 2048
HEAD_DIM = 128
HALF = HEAD_DIM // 2
ROPE_THETA = 10000.0
EPS = 1e-6
RET_HEADS = 8
RET_CHUNK = 128
NSA_HEADS = 8
NSA_KV_HEADS = 2
NSA_GROUP = NSA_HEADS // NSA_KV_HEADS
CMP_BLOCK = 32
CMP_STRIDE = 16
SLC_BLOCK = 64
SLC_TOPK = 16
N_LOCAL_BLOCKS = 2
WINDOW = 512
DIFF_HEADS = 8
DIFF_V_DIM = 2 * HEAD_DIM
QK_SCALE = HEAD_DIM ** -0.5

RET_WIDTH = RET_HEADS * HEAD_DIM
NSA_WIDTH = NSA_HEADS * HEAD_DIM
NSA_KV_WIDTH = NSA_KV_HEADS * HEAD_DIM
AB_MAIN_COLS = 4 * RET_WIDTH + 2 * NSA_WIDTH + 6 * NSA_KV_WIDTH
N_GATE_COLS = 3 * NSA_HEADS
DIFF_WIDTH = DIFF_HEADS * DIFF_V_DIM
C_IN_COLS = 4 * DIFF_WIDTH

LANES = 128
MASKED = -1e30
M_INIT = -1e29
VMEM_LIMIT = 56 * 1024 * 1024

PROJ_TM = 1024
PROJ_TN = 512
NSA_TQ = 128
NSA_TK = 256
DIFF_TQ = 512
DIFF_TK = 512
DIFF_DIAG = 256


def _dot(a, b):
    return jnp.dot(a, b, preferred_element_type=F32)


def _dot_nt(a, b):
    return lax.dot_general(a, b, (((1,), (1,)), ((), ())), preferred_element_type=F32)


def _dot_tn(a, b):
    return lax.dot_general(a, b, (((0,), (0,)), ((), ())), preferred_element_type=F32)


def _silu(x):
    return x / (1.0 + jnp.exp(-x))


def _rms(x):
    return x * lax.rsqrt(jnp.mean(x * x, axis=-1, keepdims=True) + EPS)


def _rope(x, cos, sin_signed):
    return x * cos + pltpu.roll(x, HALF, axis=1) * sin_signed


def _segment_epilogue(seg, kind, cos, sin, gq, gk):
    op, scale = kind
    if op == "plain":
        return seg
    if op == "silu":
        return _silu(seg)
    if op == "nrq":
        seg = _rms(seg) * gq
    elif op == "nrk":
        seg = _rms(seg) * gk
    seg = _rope(seg, cos, sin)
    if scale != 1.0:
        seg = seg * scale
    return seg


def _proj_kernel(*refs, tile_modes, has_gates):
    if has_gates:
        x_ref, g_ref, w_ref, cos_ref, sin_ref, gq_ref, gk_ref, wg_ref, o_ref, og_ref, h_scr = refs
    else:
        x_ref, g_ref, w_ref, cos_ref, sin_ref, gq_ref, gk_ref, o_ref, h_scr = refs
    j = pl.program_id(1)

    @pl.when(j == 0)
    def _():
        x = x_ref[...]
        h = (_rms(x) * g_ref[...]).astype(BF16)
        h_scr[...] = h
        if has_gates:
            z = _dot(h, wg_ref[...])
            og_ref[...] = 1.0 / (1.0 + jnp.exp(-z))

    def make_branch(lo, hi, kinds):
        @pl.when((j >= lo) & (j < hi))
        def _():
            acc = _dot(h_scr[...], w_ref[...])
            for c, kind in enumerate(kinds):
                seg = acc[:, c * LANES:(c + 1) * LANES]
                seg = _segment_epilogue(seg, kind, cos_ref[...], sin_ref[...], gq_ref[...], gk_ref[...])
                o_ref[:, c * LANES:(c + 1) * LANES] = seg.astype(o_ref.dtype)

    for lo, hi, kinds in tile_modes:
        make_branch(lo, hi, kinds)


def _norm_proj(x2d, g, w, cos, sin, gq, gk, tile_modes, seq, wg=None):
    m, d = x2d.shape
    n = w.shape[1]
    tm, tn = PROJ_TM, PROJ_TN
    s_tiles = seq // tm
    has_gates = wg is not None
    in_specs = [
        pl.BlockSpec((tm, d), lambda i, j: (i, 0)),
        pl.BlockSpec((1, d), lambda i, j: (0, 0)),
        pl.BlockSpec((d, tn), lambda i, j: (0, j)),
        pl.BlockSpec((tm, LANES), lambda i, j: (i % s_tiles, 0)),
        pl.BlockSpec((tm, LANES), lambda i, j: (i % s_tiles, 0)),
        pl.BlockSpec((1, LANES), lambda i, j: (0, 0)),
        pl.BlockSpec((1, LANES), lambda i, j: (0, 0)),
    ]
    args = [x2d, g.reshape(1, d), w, cos, sin, gq.reshape(1, LANES), gk.reshape(1, LANES)]
    out_shape = [jax.ShapeDtypeStruct((m, n), BF16)]
    out_specs = [pl.BlockSpec((tm, tn), lambda i, j: (i, j))]
    if has_gates:
        ng = wg.shape[1]
        in_specs.append(pl.BlockSpec((d, ng), lambda i, j: (0, 0)))
        args.append(wg)
        out_shape.append(jax.ShapeDtypeStruct((m, ng), F32))
        out_specs.append(pl.BlockSpec((tm, ng), lambda i, j: (i, 0)))
    return pl.pallas_call(
        functools.partial(_proj_kernel, tile_modes=tile_modes, has_gates=has_gates),
        grid=(m // tm, n // tn),
        in_specs=in_specs,
        out_specs=out_specs,
        out_shape=out_shape,
        scratch_shapes=[pltpu.VMEM((tm, d), BF16)],
        compiler_params=pltpu.CompilerParams(
            dimension_semantics=("parallel", "arbitrary"), vmem_limit_bytes=VMEM_LIMIT),
        name="norm_proj",
    )(*args)


def _oproj_kernel(*refs, n_terms):
    y_refs, w_refs = refs[:n_terms], refs[n_terms:2 * n_terms]
    x_ref, o_ref = refs[2 * n_terms:]
    acc = x_ref[...]
    for y_ref, w_ref in zip(y_refs, w_refs):
        acc = acc + _dot(y_ref[...], w_ref[...])
    o_ref[...] = acc


def _out_proj(ys, ws, x2d):
    m, n = x2d.shape
    tm, tn = PROJ_TM, PROJ_TN
    y_specs = [pl.BlockSpec((tm, y.shape[1]), lambda i, j: (i, 0)) for y in ys]
    w_specs = [pl.BlockSpec((w.shape[0], tn), lambda i, j: (0, j)) for w in ws]
    return pl.pallas_call(
        functools.partial(_oproj_kernel, n_terms=len(ys)),
        grid=(m // tm, n // tn),
        in_specs=y_specs + w_specs + [pl.BlockSpec((tm, tn), lambda i, j: (i, j))],
        out_specs=pl.BlockSpec((tm, tn), lambda i, j: (i, j)),
        out_shape=jax.ShapeDtypeStruct((m, n), F32),
        compiler_params=pltpu.CompilerParams(
            dimension_semantics=("parallel", "arbitrary"), vmem_limit_bytes=VMEM_LIMIT),
        name="out_proj",
    )(*ys, *ws, x2d)


def _ret_kernel(q_ref, k_ref, v_ref, gate_ref, intra_ref, qdec_ref, kdec_ref, cdec_ref, o_ref, *, n_chunks):
    c = RET_CHUNK
    intra = intra_ref[0]
    qdec = qdec_ref[0]
    kdec = kdec_ref[0]
    cdec = cdec_ref[0]
    state = jnp.zeros((HEAD_DIM, HEAD_DIM), F32)
    for n in range(n_chunks):
        rows = slice(n * c, (n + 1) * c)
        q = q_ref[0, rows, :]
        k = k_ref[0, rows, :]
        v = v_ref[0, rows, :]
        scores = _dot_nt(q, k) * intra
        inner = _dot(scores.astype(BF16), v)
        cross = _dot((q.astype(F32) * qdec).astype(BF16), state.astype(BF16))
        kv = _dot_tn((k.astype(F32) * kdec).astype(BF16), v)
        state = state * cdec + kv
        o = _rms(inner + cross) * gate_ref[0, rows, :].astype(F32)
        o_ref[0, rows, :] = o.astype(o_ref.dtype)


def _retention(proj, intra, qdec, kdec, cdec):
    b, s, _ = proj.shape
    h = RET_HEADS
    y_shape = (b, s, RET_WIDTH)
    head = lambda off: pl.BlockSpec((1, s, HEAD_DIM), lambda bi, hi: (bi, 0, off + hi))
    table = lambda rows: pl.BlockSpec((1, rows, HEAD_DIM), lambda bi, hi: (hi, 0, 0))
    return pl.pallas_call(
        functools.partial(_ret_kernel, n_chunks=s // RET_CHUNK),
        grid=(b, h),
        in_specs=[head(0), head(h), head(2 * h), head(3 * h),
                  table(RET_CHUNK), table(RET_CHUNK), table(RET_CHUNK), table(1)],
        out_specs=pl.BlockSpec((1, s, HEAD_DIM), lambda bi, hi: (bi, 0, hi)),
        out_shape=jax.ShapeDtypeStruct(y_shape, BF16),
        compiler_params=pltpu.CompilerParams(
            dimension_semantics=("parallel", "arbitrary"), vmem_limit_bytes=VMEM_LIMIT),
        name="retention",
    )(proj, proj, proj, proj, intra, qdec, kdec, cdec)


def _cmp_kernel(kc_ref, vc_ref, pek_ref, w1k_ref, w2k_ref, pev_ref, w1v_ref, w2v_ref, kg_ref,
                cos_ref, sin_ref, ko_ref, vo_ref, t_scr, *, seq):
    n_rows = seq // CMP_STRIDE

    def compress(src_ref, pe_ref, w1_ref, w2_ref):
        t_scr[0:seq, :] = src_ref[0].astype(F32)
        t_scr[seq:seq + CMP_STRIDE, :] = jnp.zeros((CMP_STRIDE, HEAD_DIM), F32)
        acc = jnp.zeros((n_rows, HEAD_DIM), F32)
        for r in range(CMP_BLOCK):
            rows = t_scr[pl.ds(r, n_rows, stride=CMP_STRIDE), :] + pe_ref[r:r + 1, :]
            acc = acc + _dot(rows.astype(BF16), w1_ref[r])
        return _dot(_silu(acc).astype(BF16), w2_ref[...])

    kc = compress(kc_ref, pek_ref, w1k_ref, w2k_ref)
    kc = _rope(_rms(kc) * kg_ref[...], cos_ref[...], sin_ref[...])
    ko_ref[0, 0] = kc.astype(ko_ref.dtype)
    vo_ref[0, 0] = compress(vc_ref, pev_ref, w1v_ref, w2v_ref).astype(vo_ref.dtype)


def _nsa_compress(proj, kc_off, vc_off, pe_k, w1_k, w2_k, pe_v, w1_v, w2_v, k_g, cos_c, sin_c):
    b, s, _ = proj.shape
    g = NSA_KV_HEADS
    n_rows = s // CMP_STRIDE
    head = lambda off: pl.BlockSpec((1, s, HEAD_DIM), lambda bi, gi: (bi, 0, off + gi))
    full = lambda shape: pl.BlockSpec(shape, lambda bi, gi: (0,) * len(shape))
    out_spec = pl.BlockSpec((1, 1, n_rows, HEAD_DIM), lambda bi, gi: (bi, gi, 0, 0))
    out_sds = jax.ShapeDtypeStruct((b, g, n_rows, HEAD_DIM), BF16)
    return pl.pallas_call(
        functools.partial(_cmp_kernel, seq=s),
        grid=(b, g),
        in_specs=[head(kc_off), head(vc_off),
                  full((CMP_BLOCK, HEAD_DIM)), full((CMP_BLOCK, HEAD_DIM, HEAD_DIM)), full((HEAD_DIM, HEAD_DIM)),
                  full((CMP_BLOCK, HEAD_DIM)), full((CMP_BLOCK, HEAD_DIM, HEAD_DIM)), full((HEAD_DIM, HEAD_DIM)),
                  full((1, HEAD_DIM)), full((n_rows, HEAD_DIM)), full((n_rows, HEAD_DIM))],
        out_specs=[out_spec, out_spec],
        out_shape=[out_sds, out_sds],
        scratch_shapes=[pltpu.VMEM((s + CMP_STRIDE, HEAD_DIM), F32)],
        compiler_params=pltpu.CompilerParams(
            dimension_semantics=("parallel", "arbitrary"), vmem_limit_bytes=VMEM_LIMIT),
        name="nsa_compress",
    )(proj, proj, pe_k, w1_k, w2_k, pe_v, w1_v, w2_v, k_g.reshape(1, HEAD_DIM), cos_c, sin_c)


def _flash_init(m_scr, l_scr, acc_scr):
    m_scr[...] = jnp.full(m_scr.shape, M_INIT, F32)
    l_scr[...] = jnp.zeros(l_scr.shape, F32)
    acc_scr[...] = jnp.zeros(acc_scr.shape, F32)


def _lane_tiles(x, width):
    return x if width == LANES else jnp.concatenate([x] * (width // LANES), axis=1)


def _flash_step(q, k, v, bias, m_scr, l_scr, acc_scr):
    s = _dot_nt(q, k)
    if bias is not None:
        s = s + bias
    tk = s.shape[1]
    m_old = m_scr[...]
    m_new = jnp.maximum(m_old, jnp.max(s, axis=-1, keepdims=True))
    alpha = jnp.exp(m_old - m_new)
    p = jnp.exp(s - _lane_tiles(m_new, tk))
    p_cols = p[:, 0:LANES]
    for c in range(1, tk // LANES):
        p_cols = p_cols + p[:, c * LANES:(c + 1) * LANES]
    l_scr[...] = alpha * l_scr[...] + p_cols
    acc_scr[...] = _lane_tiles(alpha, acc_scr.shape[1]) * acc_scr[...] + _dot(p.astype(BF16), v)
    m_scr[...] = m_new


def _flash_finish(l_scr, acc_scr):
    return acc_scr[...] * (1.0 / jnp.sum(l_scr[...], axis=-1, keepdims=True))


def _rep_heads(bias):
    return jnp.concatenate([bias] * NSA_GROUP, axis=0)


def _nsa_kernel(q_ref, ngate_ref, gates_ref, kcmp_ref, vcmp_ref, ks_ref, vs_ref, kw_ref, vw_ref,
                ovl_ref, eexp_ref, o_ref, bias_scr, m_scr, l_scr, acc_scr, *, seq):
    tq, tk = NSA_TQ, NSA_TK
    n_slc = seq // SLC_BLOCK
    qi = pl.program_id(2)
    q0 = qi * tq
    qblk = q_ref[0]
    q4 = jnp.concatenate([qblk[:, r * HEAD_DIM:(r + 1) * HEAD_DIM] for r in range(NSA_GROUP)], axis=0)

    tpos = q0 + lax.broadcasted_iota(jnp.int32, (tq, LANES), 0)
    cidx = lax.broadcasted_iota(jnp.int32, (tq, LANES), 1)
    cbias = jnp.where(cidx * CMP_STRIDE + (CMP_BLOCK - 1) <= tpos, 0.0, MASKED)
    s = _dot_nt(q4, kcmp_ref[0, 0]) + _rep_heads(cbias)
    m = jnp.maximum(jnp.max(s, axis=-1, keepdims=True), M_INIT)
    e = jnp.exp(s - m)
    l = jnp.sum(e, axis=-1, keepdims=True)
    p = e * (1.0 / jnp.maximum(l, 1e-30))
    o_cmp = _dot(p.astype(BF16), vcmp_ref[0, 0])
    psum = p[0:tq]
    for r in range(1, NSA_GROUP):
        psum = psum + p[r * tq:(r + 1) * tq]

    ovl = ovl_ref[...]
    p_hi = psum.astype(BF16)
    rem = psum - p_hi.astype(F32)
    p_mid = rem.astype(BF16)
    p_lo = (rem - p_mid.astype(F32)).astype(BF16)
    imp = _dot_nt(ovl, p_hi) + _dot_nt(ovl, p_mid) + _dot_nt(ovl, p_lo)
    jb = lax.broadcasted_iota(jnp.int32, (n_slc, tq), 0)
    blk_t = jnp.right_shift(q0 + lax.broadcasted_iota(jnp.int32, (n_slc, tq), 1), int(math.log2(SLC_BLOCK)))
    back = blk_t - jb
    forced = (jb == 0) | ((back >= 0) & (back < N_LOCAL_BLOCKS))
    score = jnp.where(forced, 1e9, jnp.where(back >= 0, imp, -1e9))
    rank = jnp.zeros((n_slc, tq), F32)
    for mp in range(n_slc):
        row = score[mp:mp + 1, :]
        ahead = (row > score) | ((row == score) & (jb > mp))
        rank = rank + jnp.where(ahead, 1.0, 0.0)
    sel_t = jnp.where(rank < float(min(SLC_TOPK, n_slc)), 1.0, 0.0)
    sel_t = jnp.concatenate([sel_t, jnp.zeros((LANES - n_slc, tq), F32)], axis=0).astype(BF16)
    ri = lax.broadcasted_iota(jnp.int32, (tq, tq), 0)
    ci = lax.broadcasted_iota(jnp.int32, (tq, tq), 1)
    eye = jnp.where(ri == ci, 1.0, 0.0).astype(BF16)
    sel = _dot_nt(eye, sel_t).astype(BF16)
    sel_keys = _dot(sel, eexp_ref[...])
    for kt in range(seq // tk):
        bias_scr[kt] = jnp.where(sel_keys[:, kt * tk:(kt + 1) * tk] > 0.5, 0.0, MASKED)

    _flash_init(m_scr, l_scr, acc_scr)

    def slc_body(kt, carry):
        k0 = pl.multiple_of(kt * tk, tk)
        _flash_step(q4, ks_ref[0, pl.ds(k0, tk), :], vs_ref[0, pl.ds(k0, tk), :],
                    _rep_heads(bias_scr[kt]), m_scr, l_scr, acc_scr)
        return carry

    n_full = lax.div(q0, tk)
    lax.fori_loop(0, n_full, slc_body, 0)
    kd = pl.multiple_of(n_full * tk, tk)
    qpos = q0 + lax.broadcasted_iota(jnp.int32, (tq, tk), 0)
    kpos = kd + lax.broadcasted_iota(jnp.int32, (tq, tk), 1)
    causal = jnp.where(kpos <= qpos, 0.0, MASKED)
    _flash_step(q4, ks_ref[0, pl.ds(kd, tk), :], vs_ref[0, pl.ds(kd, tk), :],
                _rep_heads(bias_scr[n_full] + causal), m_scr, l_scr, acc_scr)
    o_slc = _flash_finish(l_scr, acc_scr)

    wk = WINDOW + tq
    ws = pl.multiple_of(jnp.maximum(q0 - WINDOW, 0), tq)
    d = (q0 - ws) + lax.broadcasted_iota(jnp.int32, (tq, wk), 0) - lax.broadcasted_iota(jnp.int32, (tq, wk), 1)
    wbias = jnp.where((d >= 0) & (d < WINDOW), 0.0, MASKED)
    s = _dot_nt(q4, kw_ref[0, pl.ds(ws, wk), :]) + _rep_heads(wbias)
    e = jnp.exp(s - jnp.max(s, axis=-1, keepdims=True))
    o_win = _dot(e.astype(BF16), vw_ref[0, pl.ds(ws, wk), :]) * (1.0 / jnp.sum(e, axis=-1, keepdims=True))

    gates = gates_ref[...]
    for r in range(NSA_GROUP):
        rows = slice(r * tq, (r + 1) * tq)
        g_cmp = gates[:, r:r + 1]
        g_slc = gates[:, NSA_GROUP + r:NSA_GROUP + r + 1]
        g_win = gates[:, 2 * NSA_GROUP + r:2 * NSA_GROUP + r + 1]
        y = g_cmp * o_cmp[rows] + g_slc * o_slc[rows] + g_win * o_win[rows]
        cols = slice(r * HEAD_DIM, (r + 1) * HEAD_DIM)
        o_ref[0, :, cols] = (y * ngate_ref[0, :, cols].astype(F32)).astype(o_ref.dtype)


def _nsa_attention(proj, gates, kcmp, vcmp, ovl_t, eexp, offs):
    b, s, _ = proj.shape
    g = NSA_KV_HEADS
    tq = NSA_TQ
    nq = s // tq
    gw = NSA_GROUP * HEAD_DIM
    q_spec = lambda off: pl.BlockSpec((1, tq, gw), lambda bi, gi, qi: (bi, qi, off + gi))
    kv_spec = lambda off: pl.BlockSpec((1, s, HEAD_DIM), lambda bi, gi, qi: (bi, 0, off + gi))
    cmp_spec = pl.BlockSpec((1, 1, s // CMP_STRIDE, HEAD_DIM), lambda bi, gi, qi: (bi, gi, 0, 0))
    full = lambda shape: pl.BlockSpec(shape, lambda bi, gi, qi: (0,) * len(shape))
    rows4 = NSA_GROUP * tq
    return pl.pallas_call(
        functools.partial(_nsa_kernel, seq=s),
        grid=(b, g, nq),
        in_specs=[q_spec(offs["nq"] // gw), q_spec(offs["ngate"] // gw),
                  pl.BlockSpec((tq, LANES), lambda bi, gi, qi: (bi * nq + qi, gi)),
                  cmp_spec, cmp_spec,
                  kv_spec(offs["ks"] // HEAD_DIM), kv_spec(offs["vs"] // HEAD_DIM),
                  kv_spec(offs["kw"] // HEAD_DIM), kv_spec(offs["vw"] // HEAD_DIM),
                  full(ovl_t.shape), full(eexp.shape)],
        out_specs=pl.BlockSpec((1, tq, gw), lambda bi, gi, qi: (bi, qi, gi)),
        out_shape=jax.ShapeDtypeStruct((b, s, NSA_WIDTH), BF16),
        scratch_shapes=[pltpu.VMEM((s // NSA_TK, tq, NSA_TK), F32),
                        pltpu.VMEM((rows4, LANES), F32), pltpu.VMEM((rows4, LANES), F32),
                        pltpu.VMEM((rows4, HEAD_DIM), F32)],
        compiler_params=pltpu.CompilerParams(
            dimension_semantics=("parallel", "parallel", "arbitrary"), vmem_limit_bytes=VMEM_LIMIT),
        name="nsa_attention",
    )(proj, proj, gates, kcmp, vcmp, proj, proj, proj, proj, ovl_t, eexp)


def _diff_kernel(q_ref, gate_ref, k_ref, v_ref, lam_ref, o_ref, m1, l1, a1, m2, l2, a2, *, lambda_init):
    tq, tk = DIFF_TQ, DIFF_TK
    qi = pl.program_id(2)
    q = q_ref[0]
    q1 = q[:, :HEAD_DIM]
    q2 = q[:, HEAD_DIM:]
    _flash_init(m1, l1, a1)
    _flash_init(m2, l2, a2)

    def step(rows, k0, width, bias):
        k = k_ref[0, pl.ds(k0, width), :]
        v = v_ref[0, pl.ds(k0, width), :]
        _flash_step(q1[rows], k[:, :HEAD_DIM], v, bias, m1.at[rows], l1.at[rows], a1.at[rows])
        _flash_step(q2[rows], k[:, HEAD_DIM:], v, bias, m2.at[rows], l2.at[rows], a2.at[rows])

    def body(kt, carry):
        step(slice(0, tq), pl.multiple_of(kt * tk, tk), tk, None)
        return carry

    lax.fori_loop(0, qi * (tq // tk), body, 0)
    dd = DIFF_DIAG
    q0 = qi * tq
    for c in range(tq // dd):
        n_rows = tq - c * dd
        ri = lax.broadcasted_iota(jnp.int32, (n_rows, dd), 0)
        ci = lax.broadcasted_iota(jnp.int32, (n_rows, dd), 1)
        step(slice(c * dd, tq), pl.multiple_of(q0 + c * dd, dd), dd, jnp.where(ri >= ci, 0.0, MASKED))

    lp = lam_ref[...]
    lam = (jnp.exp(jnp.sum(lp[0:1] * lp[1:2], axis=-1, keepdims=True))
           - jnp.exp(jnp.sum(lp[2:3] * lp[3:4], axis=-1, keepdims=True)) + lambda_init)
    o = _flash_finish(l1, a1) - lam * _flash_finish(l2, a2)
    o = _rms(o) * (1.0 - lambda_init)
    o_ref[0] = (o * gate_ref[0].astype(F32)).astype(o_ref.dtype)


def _diff_attention(proj, lam_params, lambda_init):
    b, s, _ = proj.shape
    h = DIFF_HEADS
    tq = DIFF_TQ
    w = DIFF_V_DIM
    q_spec = lambda off: pl.BlockSpec((1, tq, w), lambda bi, hi, qi: (bi, qi, off + hi))
    kv_spec = lambda off: pl.BlockSpec((1, s, w), lambda bi, hi, qi: (bi, 0, off + hi))
    stat = pltpu.VMEM((tq, LANES), F32)
    acc = pltpu.VMEM((tq, w), F32)
    return pl.pallas_call(
        functools.partial(_diff_kernel, lambda_init=lambda_init),
        grid=(b, h, s // tq),
        in_specs=[q_spec(0), q_spec(3 * h), kv_spec(h), kv_spec(2 * h),
                  pl.BlockSpec(lam_params.shape, lambda bi, hi, qi: (0, 0))],
        out_specs=pl.BlockSpec((1, tq, w), lambda bi, hi, qi: (bi, qi, hi)),
        out_shape=jax.ShapeDtypeStruct((b, s, DIFF_WIDTH), BF16),
        scratch_shapes=[stat, stat, acc, stat, stat, acc],
        compiler_params=pltpu.CompilerParams(
            dimension_semantics=("parallel", "parallel", "arbitrary"), vmem_limit_bytes=VMEM_LIMIT),
        name="diff_attention",
    )(proj, proj, proj, proj, lam_params)


def _rope_tables(pos):
    inv = 1.0 / (ROPE_THETA ** (jnp.arange(0, HEAD_DIM, 2, dtype=F32) / HEAD_DIM))
    ang = pos.astype(F32)[:, None] * inv[None, :]
    cos, sin = jnp.cos(ang), jnp.sin(ang)
    return jnp.concatenate([cos, cos], axis=-1), jnp.concatenate([-sin, sin], axis=-1)


def _retention_tables():
    h, c = RET_HEADS, RET_CHUNK
    log_g = jnp.log1p(-jnp.exp2(-5.0 - jnp.arange(h, dtype=F32)))
    j = jnp.arange(c, dtype=F32)
    diff = j[:, None] - j[None, :]
    intra = jnp.where(diff >= 0, jnp.exp(log_g[:, None, None] * jnp.maximum(diff, 0.0)), 0.0)
    q_dec = jnp.exp(log_g[:, None] * (j + 1.0))
    k_dec = jnp.exp(log_g[:, None] * (c - 1.0 - j))
    chunk_dec = jnp.exp(log_g * c)
    wide = lambda t: jnp.broadcast_to(t[:, :, None], (h, t.shape[1], HEAD_DIM))
    return intra, wide(q_dec), wide(k_dec), wide(chunk_dec[:, None])


def _selection_tables(seq):
    n_cmp_rows = seq // CMP_STRIDE
    n_slc = seq // SLC_BLOCK
    c_start = np.arange(n_cmp_rows) * CMP_STRIDE
    s_start = np.arange(n_slc) * SLC_BLOCK
    overlap_t = ((c_start[None, :] <= s_start[:, None] + SLC_BLOCK - 1)
                 & (c_start[None, :] + CMP_BLOCK - 1 >= s_start[:, None]))
    expand = (np.arange(seq)[None, :] // SLC_BLOCK) == np.arange(LANES)[:, None]
    return jnp.asarray(overlap_t, BF16), jnp.asarray(expand, BF16)


def _l0_tile_modes():
    t = PROJ_TN
    per = t // LANES
    seg = lambda op, scale=1.0: (op, scale)
    off = {}
    modes = []
    col = 0

    def add(name, width, kinds):
        nonlocal col
        off[name] = col
        n_tiles = width // t
        modes.append((col // t, col // t + n_tiles, kinds))
        col += width

    add("rq", RET_WIDTH, [seg("rope")] * per)
    add("rk", RET_WIDTH, [seg("rope", QK_SCALE)] * per)
    add("rv", RET_WIDTH, [seg("plain")] * per)
    add("rgate", RET_WIDTH, [seg("silu")] * per)
    add("nq", NSA_WIDTH, [seg("nrq", QK_SCALE)] * per)
    kvh = NSA_KV_HEADS
    add("kc", 2 * NSA_KV_WIDTH, [seg("plain")] * per)
    off["vc"] = off["kc"] + NSA_KV_WIDTH
    add("ks", 2 * NSA_KV_WIDTH, [seg("nrk")] * kvh + [seg("plain")] * kvh)
    off["vs"] = off["ks"] + NSA_KV_WIDTH
    add("kw", 2 * NSA_KV_WIDTH, [seg("nrk")] * kvh + [seg("plain")] * kvh)
    off["vw"] = off["kw"] + NSA_KV_WIDTH
    add("ngate", NSA_WIDTH, [seg("silu")] * per)
    assert col == AB_MAIN_COLS
    return modes, off


def _l1_tile_modes():
    per = PROJ_TN // LANES
    n = DIFF_WIDTH // PROJ_TN
    return [(0, n, [("nrq", QK_SCALE)] * per), (n, 2 * n, [("nrk", 1.0)] * per),
            (2 * n, 3 * n, [("plain", 1.0)] * per), (3 * n, 4 * n, [("silu", 1.0)] * per)]


def kernel(x, l0_norm_g, l0_w_in, l0_w_out, l0_nsa_q_norm_g, l0_nsa_k_norm_g, l0_cmp_pe_k, l0_cmp_w1_k, l0_cmp_w2_k, l0_cmp_pe_v, l0_cmp_w1_v, l0_cmp_w2_v, l1_norm_g, l1_w_in, l1_w_out, l1_q_norm_g, l1_k_norm_g, l1_lambda_q1, l1_lambda_k1, l1_lambda_q2, l1_lambda_k2):
    b, s, d = x.shape
    m = b * s
    x2d = x.reshape(m, d)
    cos, sin = _rope_tables(jnp.arange(s))
    cos_c, sin_c = _rope_tables(jnp.arange(s // CMP_STRIDE) * CMP_STRIDE + CMP_BLOCK - 1)
    intra, qdec, kdec, cdec = _retention_tables()
    ovl_t, eexp = _selection_tables(s)

    modes0, off = _l0_tile_modes()
    w0 = l0_w_in[:, :AB_MAIN_COLS].astype(BF16)
    wg = l0_w_in[:, AB_MAIN_COLS:].reshape(d, 3, NSA_KV_HEADS, NSA_GROUP).transpose(0, 2, 1, 3)
    wg = wg.reshape(d, NSA_KV_HEADS, 3 * NSA_GROUP)
    wg = jnp.pad(wg, ((0, 0), (0, 0), (0, LANES - 3 * NSA_GROUP))).reshape(d, NSA_KV_HEADS * LANES).astype(BF16)
    proj0, gates = _norm_proj(x2d, l0_norm_g, w0, cos, sin, l0_nsa_q_norm_g, l0_nsa_k_norm_g, modes0, s, wg=wg)
    proj0 = proj0.reshape(b, s, AB_MAIN_COLS)
    y_ret = _retention(proj0, intra, qdec, kdec, cdec)
    w1k = l0_cmp_w1_k.astype(BF16).reshape(CMP_BLOCK, HEAD_DIM, HEAD_DIM)
    w1v = l0_cmp_w1_v.astype(BF16).reshape(CMP_BLOCK, HEAD_DIM, HEAD_DIM)
    kcmp, vcmp = _nsa_compress(proj0, off["kc"] // HEAD_DIM, off["vc"] // HEAD_DIM,
                               l0_cmp_pe_k, w1k, l0_cmp_w2_k.astype(BF16),
                               l0_cmp_pe_v, w1v, l0_cmp_w2_v.astype(BF16),
                               l0_nsa_k_norm_g, cos_c, sin_c)
    y_nsa = _nsa_attention(proj0, gates, kcmp, vcmp, ovl_t, eexp, off)
    w_out0 = l0_w_out.astype(BF16)
    x1 = _out_proj([y_ret.reshape(m, RET_WIDTH), y_nsa.reshape(m, NSA_WIDTH)],
                   [w_out0[:RET_WIDTH], w_out0[RET_WIDTH:]], x2d)

    lambda_init = 0.8 - 0.6 * math.exp(-0.3 * 1)
    proj1 = _norm_proj(x1, l1_norm_g, l1_w_in.astype(BF16), cos, sin, l1_q_norm_g, l1_k_norm_g,
                       _l1_tile_modes(), s)[0]
    lam_params = jnp.stack([l1_lambda_q1, l1_lambda_k1, l1_lambda_q2, l1_lambda_k2]).astype(F32)
    y1 = _diff_attention(proj1.reshape(b, s, C_IN_COLS), lam_params, lambda_init)
    out = _out_proj([y1.reshape(m, DIFF_WIDTH)], [l1_w_out.astype(BF16)], x1)
    return out.reshape(b, s, d)
```

```python
import functools
import math

import numpy as np
import jax
import jax.numpy as jnp
from jax import lax
from jax.experimental import pallas as pl
from jax.experimental.pallas import tpu as pltpu

F32 = jnp.float32
BF16 = jnp.bfloat16

D_MODEL = 2048
HEAD_DIM = 128
HALF = HEAD_DIM // 2
ROPE_THETA = 10000.0
EPS = 1e-6
RET_HEADS = 8
RET_CHUNK = 128
NSA_HEADS = 8
NSA_KV_HEADS = 2
NSA_GROUP = NSA_HEADS // NSA_KV_HEADS
CMP_BLOCK = 32
CMP_STRIDE = 16
SLC_BLOCK = 64
SLC_TOPK = 16
N_LOCAL_BLOCKS = 2
WINDOW = 512
DIFF_HEADS = 8
DIFF_V_DIM = 2 * HEAD_DIM
QK_SCALE = HEAD_DIM ** -0.5

RET_WIDTH = RET_HEADS * HEAD_DIM
NSA_WIDTH = NSA_HEADS * HEAD_DIM
NSA_KV_WIDTH = NSA_KV_HEADS * HEAD_DIM
AB_MAIN_COLS = 4 * RET_WIDTH + 2 * NSA_WIDTH + 6 * NSA_KV_WIDTH
N_GATE_COLS = 3 * NSA_HEADS
DIFF_WIDTH = DIFF_HEADS * DIFF_V_DIM
C_IN_COLS = 4 * DIFF_WIDTH

LANES = 128
MASKED = -1e30
M_INIT = -1e29
VMEM_LIMIT = 56 * 1024 * 1024

PROJ_TM = 1024
PROJ_TN = 512
PROJ_SUB = 256
L0_TAB_ROPE, L0_TAB_ROPE_SCALED, L0_TAB_Q, L0_TAB_K = range(4)
L1_TAB_Q, L1_TAB_K = range(2)
NSA_TQ = 128
NSA_TK = 256
DIFF_TQ = 512
DIFF_TK = 512
DIFF_DIAG = 256


def _dot(a, b):
    return jnp.dot(a, b, preferred_element_type=F32)


def _dot_nt(a, b):
    return lax.dot_general(a, b, (((1,), (1,)), ((), ())), preferred_element_type=F32)


def _dot_tn(a, b):
    return lax.dot_general(a, b, (((0,), (0,)), ((), ())), preferred_element_type=F32)


def _silu(x):
    return x / (1.0 + jnp.exp(-x))


def _rms(x):
    return x * lax.rsqrt(jnp.mean(x * x, axis=-1, keepdims=True) + EPS)


def _rope(x, cos, sin_signed):
    return x * cos + pltpu.roll(x, HALF, axis=1) * sin_signed


def _segment_epilogue(seg, kind, tabs_ref, rows):
    op, t = kind
    if op == "plain":
        return seg
    if op == "silu":
        return _silu(seg)
    if op == "nrope":
        mean_mat = jnp.full((LANES, LANES), 1.0 / HEAD_DIM, BF16)
        seg = seg * lax.rsqrt(_dot((seg * seg).astype(BF16), mean_mat) + EPS)
    return seg * tabs_ref[2 * t, rows, :] + pltpu.roll(seg, HALF, axis=1) * tabs_ref[2 * t + 1, rows, :]


def _proj_kernel(*refs, tile_modes, has_gates):
    if has_gates:
        x_ref, g_ref, w_ref, tabs_ref, wg_ref, o_ref, og_ref, h_scr = refs
    else:
        x_ref, g_ref, w_ref, tabs_ref, o_ref, h_scr = refs
    j = pl.program_id(1)

    @pl.when(j == 0)
    def _():
        x = x_ref[...]
        h = (_rms(x) * g_ref[...]).astype(BF16)
        h_scr[...] = h
        if has_gates:
            z = _dot(h, wg_ref[...])
            og_ref[...] = 1.0 / (1.0 + jnp.exp(-z))

    def make_branch(lo, hi, kinds):
        @pl.when((j >= lo) & (j < hi))
        def _():
            for r in range(h_scr.shape[0] // PROJ_SUB):
                rows = slice(r * PROJ_SUB, (r + 1) * PROJ_SUB)
                acc = _dot(h_scr[rows, :], w_ref[...])
                for c, kind in enumerate(kinds):
                    cols = slice(c * LANES, (c + 1) * LANES)
                    o_ref[rows, cols] = _segment_epilogue(acc[:, cols], kind, tabs_ref, rows).astype(o_ref.dtype)

    for lo, hi, kinds in tile_modes:
        make_branch(lo, hi, kinds)


def _norm_proj(x2d, g, w, tabs, tile_modes, seq, wg=None):
    m, d = x2d.shape
    n = w.shape[1]
    tm, tn = PROJ_TM, PROJ_TN
    s_tiles = seq // tm
    has_gates = wg is not None
    in_specs = [
        pl.BlockSpec((tm, d), lambda i, j: (i, 0)),
        pl.BlockSpec((1, d), lambda i, j: (0, 0)),
        pl.BlockSpec((d, tn), lambda i, j: (0, j)),
        pl.BlockSpec((tabs.shape[0], tm, LANES), lambda i, j: (0, i % s_tiles, 0)),
    ]
    args = [x2d, g.reshape(1, d), w, tabs]
    out_shape = [jax.ShapeDtypeStruct((m, n), BF16)]
    out_specs = [pl.BlockSpec((tm, tn), lambda i, j: (i, j))]
    if has_gates:
        ng = wg.shape[1]
        in_specs.append(pl.BlockSpec((d, ng), lambda i, j: (0, 0)))
        args.append(wg)
        out_shape.append(jax.ShapeDtypeStruct((m, ng), F32))
        out_specs.append(pl.BlockSpec((tm, ng), lambda i, j: (i, 0)))
    return pl.pallas_call(
        functools.partial(_proj_kernel, tile_modes=tile_modes, has_gates=has_gates),
        grid=(m // tm, n // tn),
        in_specs=in_specs,
        out_specs=out_specs,
        out_shape=out_shape,
        scratch_shapes=[pltpu.VMEM((tm, d), BF16)],
        compiler_params=pltpu.CompilerParams(
            dimension_semantics=("parallel", "arbitrary"), vmem_limit_bytes=VMEM_LIMIT),
        name="norm_proj",
    )(*args)


def _oproj_kernel(*refs, n_terms):
    y_refs, w_refs = refs[:n_terms], refs[n_terms:2 * n_terms]
    x_ref, o_ref = refs[2 * n_terms:]
    acc = x_ref[...]
    for y_ref, w_ref in zip(y_refs, w_refs):
        acc = acc + _dot(y_ref[...], w_ref[...])
    o_ref[...] = acc


def _out_proj(ys, ws, x2d):
    m, n = x2d.shape
    tm, tn = PROJ_TM, PROJ_TN
    y_specs = [pl.BlockSpec((tm, y.shape[1]), lambda i, j: (i, 0)) for y in ys]
    w_specs = [pl.BlockSpec((w.shape[0], tn), lambda i, j: (0, j)) for w in ws]
    return pl.pallas_call(
        functools.partial(_oproj_kernel, n_terms=len(ys)),
        grid=(m // tm, n // tn),
        in_specs=y_specs + w_specs + [pl.BlockSpec((tm, tn), lambda i, j: (i, j))],
        out_specs=pl.BlockSpec((tm, tn), lambda i, j: (i, j)),
        out_shape=jax.ShapeDtypeStruct((m, n), F32),
        compiler_params=pltpu.CompilerParams(
            dimension_semantics=("parallel", "arbitrary"), vmem_limit_bytes=VMEM_LIMIT),
        name="out_proj",
    )(*ys, *ws, x2d)


def _ret_kernel(q_ref, k_ref, v_ref, gate_ref, intra_ref, qdec_ref, kdec_ref, cdec_ref, o_ref, *, n_chunks):
    c = RET_CHUNK
    intra = intra_ref[0]
    qdec = qdec_ref[0]
    kdec = kdec_ref[0]
    cdec = cdec_ref[0]
    state = jnp.zeros((HEAD_DIM, HEAD_DIM), F32)
    for n in range(n_chunks):
        rows = slice(n * c, (n + 1) * c)
        q = q_ref[0, rows, :]
        k = k_ref[0, rows, :]
        v = v_ref[0, rows, :]
        scores = _dot_nt(q, k) * intra
        inner = _dot(scores.astype(BF16), v)
        cross = _dot((q.astype(F32) * qdec).astype(BF16), state.astype(BF16))
        kv = _dot_tn((k.astype(F32) * kdec).astype(BF16), v)
        state = state * cdec + kv
        o = _rms(inner + cross) * gate_ref[0, rows, :].astype(F32)
        o_ref[0, rows, :] = o.astype(o_ref.dtype)


def _retention(proj, intra, qdec, kdec, cdec):
    b, s, _ = proj.shape
    h = RET_HEADS
    y_shape = (b, s, RET_WIDTH)
    head = lambda off: pl.BlockSpec((1, s, HEAD_DIM), lambda bi, hi: (bi, 0, off + hi))
    table = lambda rows: pl.BlockSpec((1, rows, HEAD_DIM), lambda bi, hi: (hi, 0, 0))
    return pl.pallas_call(
        functools.partial(_ret_kernel, n_chunks=s // RET_CHUNK),
        grid=(b, h),
        in_specs=[head(0), head(h), head(2 * h), head(3 * h),
                  table(RET_CHUNK), table(RET_CHUNK), table(RET_CHUNK), table(1)],
        out_specs=pl.BlockSpec((1, s, HEAD_DIM), lambda bi, hi: (bi, 0, hi)),
        out_shape=jax.ShapeDtypeStruct(y_shape, BF16),
        compiler_params=pltpu.CompilerParams(
            dimension_semantics=("parallel", "arbitrary"), vmem_limit_bytes=VMEM_LIMIT),
        name="retention",
    )(proj, proj, proj, proj, intra, qdec, kdec, cdec)


def _cmp_kernel(kc_ref, vc_ref, pek_ref, w1k_ref, w2k_ref, pev_ref, w1v_ref, w2v_ref, kg_ref,
                cos_ref, sin_ref, ko_ref, vo_ref, t_scr, *, seq):
    n_rows = seq // CMP_STRIDE

    def compress(src_ref, pe_ref, w1_ref, w2_ref):
        t_scr[0:seq, :] = src_ref[0].astype(F32)
        t_scr[seq:seq + CMP_STRIDE, :] = jnp.zeros((CMP_STRIDE, HEAD_DIM), F32)
        acc = jnp.zeros((n_rows, HEAD_DIM), F32)
        for r in range(CMP_BLOCK):
            rows = t_scr[pl.ds(r, n_rows, stride=CMP_STRIDE), :] + pe_ref[r:r + 1, :]
            acc = acc + _dot(rows.astype(BF16), w1_ref[r])
        return _dot(_silu(acc).astype(BF16), w2_ref[...])

    kc = compress(kc_ref, pek_ref, w1k_ref, w2k_ref)
    kc = _rope(_rms(kc) * kg_ref[...], cos_ref[...], sin_ref[...])
    ko_ref[0, 0] = kc.astype(ko_ref.dtype)
    vo_ref[0, 0] = compress(vc_ref, pev_ref, w1v_ref, w2v_ref).astype(vo_ref.dtype)


def _nsa_compress(proj, kc_off, vc_off, pe_k, w1_k, w2_k, pe_v, w1_v, w2_v, k_g, cos_c, sin_c):
    b, s, _ = proj.shape
    g = NSA_KV_HEADS
    n_rows = s // CMP_STRIDE
    head = lambda off: pl.BlockSpec((1, s, HEAD_DIM), lambda bi, gi: (bi, 0, off + gi))
    full = lambda shape: pl.BlockSpec(shape, lambda bi, gi: (0,) * len(shape))
    out_spec = pl.BlockSpec((1, 1, n_rows, HEAD_DIM), lambda bi, gi: (bi, gi, 0, 0))
    out_sds = jax.ShapeDtypeStruct((b, g, n_rows, HEAD_DIM), BF16)
    return pl.pallas_call(
        functools.partial(_cmp_kernel, seq=s),
        grid=(b, g),
        in_specs=[head(kc_off), head(vc_off),
                  full((CMP_BLOCK, HEAD_DIM)), full((CMP_BLOCK, HEAD_DIM, HEAD_DIM)), full((HEAD_DIM, HEAD_DIM)),
                  full((CMP_BLOCK, HEAD_DIM)), full((CMP_BLOCK, HEAD_DIM, HEAD_DIM)), full((HEAD_DIM, HEAD_DIM)),
                  full((1, HEAD_DIM)), full((n_rows, HEAD_DIM)), full((n_rows, HEAD_DIM))],
        out_specs=[out_spec, out_spec],
        out_shape=[out_sds, out_sds],
        scratch_shapes=[pltpu.VMEM((s + CMP_STRIDE, HEAD_DIM), F32)],
        compiler_params=pltpu.CompilerParams(
            dimension_semantics=("parallel", "arbitrary"), vmem_limit_bytes=VMEM_LIMIT),
        name="nsa_compress",
    )(proj, proj, pe_k, w1_k, w2_k, pe_v, w1_v, w2_v, k_g.reshape(1, HEAD_DIM), cos_c, sin_c)


def _flash_init(m_scr, l_scr, acc_scr):
    m_scr[...] = jnp.full(m_scr.shape, M_INIT, F32)
    l_scr[...] = jnp.zeros(l_scr.shape, F32)
    acc_scr[...] = jnp.zeros(acc_scr.shape, F32)


def _lane_tiles(x, width):
    return x if width == LANES else jnp.concatenate([x] * (width // LANES), axis=1)


def _flash_step(q, k, v, bias, m_scr, l_scr, acc_scr):
    s = _dot_nt(q, k)
    if bias is not None:
        s = s + bias
    tk = s.shape[1]
    m_old = m_scr[...]
    m_new = jnp.maximum(m_old, jnp.max(s, axis=-1, keepdims=True))
    alpha = jnp.exp(m_old - m_new)
    p = jnp.exp(s - _lane_tiles(m_new, tk))
    p_cols = p[:, 0:LANES]
    for c in range(1, tk // LANES):
        p_cols = p_cols + p[:, c * LANES:(c + 1) * LANES]
    l_scr[...] = alpha * l_scr[...] + p_cols
    acc_scr[...] = _lane_tiles(alpha, acc_scr.shape[1]) * acc_scr[...] + _dot(p.astype(BF16), v)
    m_scr[...] = m_new


def _flash_finish(l_scr, acc_scr):
    return acc_scr[...] * (1.0 / jnp.sum(l_scr[...], axis=-1, keepdims=True))


def _rep_heads(bias):
    return jnp.concatenate([bias] * NSA_GROUP, axis=0)


def _nsa_kernel(q_ref, ngate_ref, gates_ref, kcmp_ref, vcmp_ref, ks_ref, vs_ref, kw_ref, vw_ref,
                ovl_ref, eexp_ref, o_ref, bias_scr, m_scr, l_scr, acc_scr, *, seq):
    tq, tk = NSA_TQ, NSA_TK
    n_slc = seq // SLC_BLOCK
    qi = pl.program_id(2)
    q0 = qi * tq
    qblk = q_ref[0]
    q4 = jnp.concatenate([qblk[:, r * HEAD_DIM:(r + 1) * HEAD_DIM] for r in range(NSA_GROUP)], axis=0)

    tpos = q0 + lax.broadcasted_iota(jnp.int32, (tq, LANES), 0)
    cidx = lax.broadcasted_iota(jnp.int32, (tq, LANES), 1)
    cbias = jnp.where(cidx * CMP_STRIDE + (CMP_BLOCK - 1) <= tpos, 0.0, MASKED)
    s = _dot_nt(q4, kcmp_ref[0, 0]) + _rep_heads(cbias)
    m = jnp.maximum(jnp.max(s, axis=-1, keepdims=True), M_INIT)
    e = jnp.exp(s - m)
    l = jnp.sum(e, axis=-1, keepdims=True)
    p = e * (1.0 / jnp.maximum(l, 1e-30))
    o_cmp = _dot(p.astype(BF16), vcmp_ref[0, 0])
    psum = p[0:tq]
    for r in range(1, NSA_GROUP):
        psum = psum + p[r * tq:(r + 1) * tq]

    ovl = ovl_ref[...]
    p_hi = psum.astype(BF16)
    rem = psum - p_hi.astype(F32)
    p_mid = rem.astype(BF16)
    p_lo = (rem - p_mid.astype(F32)).astype(BF16)
    imp = _dot_nt(ovl, p_hi) + _dot_nt(ovl, p_mid) + _dot_nt(ovl, p_lo)
    jb = lax.broadcasted_iota(jnp.int32, (n_slc, tq), 0)
    blk_t = jnp.right_shift(q0 + lax.broadcasted_iota(jnp.int32, (n_slc, tq), 1), int(math.log2(SLC_BLOCK)))
    back = blk_t - jb
    forced = (jb == 0) | ((back >= 0) & (back < N_LOCAL_BLOCKS))
    score = jnp.where(forced, 1e9, jnp.where(back >= 0, imp, -1e9))
    rank = jnp.zeros((n_slc, tq), F32)
    for mp in range(n_slc):
        row = score[mp:mp + 1, :]
        ahead = (row > score) | ((row == score) & (jb > mp))
        rank = rank + jnp.where(ahead, 1.0, 0.0)
    sel_t = jnp.where(rank < float(min(SLC_TOPK, n_slc)), 1.0, 0.0)
    sel_t = jnp.concatenate([sel_t, jnp.zeros((LANES - n_slc, tq), F32)], axis=0).astype(BF16)
    ri = lax.broadcasted_iota(jnp.int32, (tq, tq), 0)
    ci = lax.broadcasted_iota(jnp.int32, (tq, tq), 1)
    eye = jnp.where(ri == ci, 1.0, 0.0).astype(BF16)
    sel = _dot_nt(eye, sel_t).astype(BF16)
    sel_keys = _dot(sel, eexp_ref[...])
    for kt in range(seq // tk):
        bias_scr[kt] = jnp.where(sel_keys[:, kt * tk:(kt + 1) * tk] > 0.5, 0.0, MASKED)

    _flash_init(m_scr, l_scr, acc_scr)

    def slc_body(kt, carry):
        k0 = pl.multiple_of(kt * tk, tk)
        _flash_step(q4, ks_ref[0, pl.ds(k0, tk), :], vs_ref[0, pl.ds(k0, tk), :],
                    _rep_heads(bias_scr[kt]), m_scr, l_scr, acc_scr)
        return carry

    n_full = lax.div(q0, tk)
    lax.fori_loop(0, n_full, slc_body, 0)
    kd = pl.multiple_of(n_full * tk, tk)
    qpos = q0 + lax.broadcasted_iota(jnp.int32, (tq, tk), 0)
    kpos = kd + lax.broadcasted_iota(jnp.int32, (tq, tk), 1)
    causal = jnp.where(kpos <= qpos, 0.0, MASKED)
    _flash_step(q4, ks_ref[0, pl.ds(kd, tk), :], vs_ref[0, pl.ds(kd, tk), :],
                _rep_heads(bias_scr[n_full] + causal), m_scr, l_scr, acc_scr)
    o_slc = _flash_finish(l_scr, acc_scr)

    wk = WINDOW + tq
    ws = pl.multiple_of(jnp.maximum(q0 - WINDOW, 0), tq)
    d = (q0 - ws) + lax.broadcasted_iota(jnp.int32, (tq, wk), 0) - lax.broadcasted_iota(jnp.int32, (tq, wk), 1)
    wbias = jnp.where((d >= 0) & (d < WINDOW), 0.0, MASKED)
    s = _dot_nt(q4, kw_ref[0, pl.ds(ws, wk), :]) + _rep_heads(wbias)
    e = jnp.exp(s - jnp.max(s, axis=-1, keepdims=True))
    o_win = _dot(e.astype(BF16), vw_ref[0, pl.ds(ws, wk), :]) * (1.0 / jnp.sum(e, axis=-1, keepdims=True))

    gates = gates_ref[...]
    for r in range(NSA_GROUP):
        rows = slice(r * tq, (r + 1) * tq)
        g_cmp = gates[:, r:r + 1]
        g_slc = gates[:, NSA_GROUP + r:NSA_GROUP + r + 1]
        g_win = gates[:, 2 * NSA_GROUP + r:2 * NSA_GROUP + r + 1]
        y = g_cmp * o_cmp[rows] + g_slc * o_slc[rows] + g_win * o_win[rows]
        cols = slice(r * HEAD_DIM, (r + 1) * HEAD_DIM)
        o_ref[0, :, cols] = (y * ngate_ref[0, :, cols].astype(F32)).astype(o_ref.dtype)


def _nsa_attention(proj, gates, kcmp, vcmp, ovl_t, eexp, offs):
    b, s, _ = proj.shape
    g = NSA_KV_HEADS
    tq = NSA_TQ
    nq = s // tq
    gw = NSA_GROUP * HEAD_DIM
    q_spec = lambda off: pl.BlockSpec((1, tq, gw), lambda bi, gi, qi: (bi, qi, off + gi))
    kv_spec = lambda off: pl.BlockSpec((1, s, HEAD_DIM), lambda bi, gi, qi: (bi, 0, off + gi))
    cmp_spec = pl.BlockSpec((1, 1, s // CMP_STRIDE, HEAD_DIM), lambda bi, gi, qi: (bi, gi, 0, 0))
    full = lambda shape: pl.BlockSpec(shape, lambda bi, gi, qi: (0,) * len(shape))
    rows4 = NSA_GROUP * tq
    return pl.pallas_call(
        functools.partial(_nsa_kernel, seq=s),
        grid=(b, g, nq),
        in_specs=[q_spec(offs["nq"] // gw), q_spec(offs["ngate"] // gw),
                  pl.BlockSpec((tq, LANES), lambda bi, gi, qi: (bi * nq + qi, gi)),
                  cmp_spec, cmp_spec,
                  kv_spec(offs["ks"] // HEAD_DIM), kv_spec(offs["vs"] // HEAD_DIM),
                  kv_spec(offs["kw"] // HEAD_DIM), kv_spec(offs["vw"] // HEAD_DIM),
                  full(ovl_t.shape), full(eexp.shape)],
        out_specs=pl.BlockSpec((1, tq, gw), lambda bi, gi, qi: (bi, qi, gi)),
        out_shape=jax.ShapeDtypeStruct((b, s, NSA_WIDTH), BF16),
        scratch_shapes=[pltpu.VMEM((s // NSA_TK, tq, NSA_TK), F32),
                        pltpu.VMEM((rows4, LANES), F32), pltpu.VMEM((rows4, LANES), F32),
                        pltpu.VMEM((rows4, HEAD_DIM), F32)],
        compiler_params=pltpu.CompilerParams(
            dimension_semantics=("parallel", "parallel", "arbitrary"), vmem_limit_bytes=VMEM_LIMIT),
        name="nsa_attention",
    )(proj, proj, gates, kcmp, vcmp, proj, proj, proj, proj, ovl_t, eexp)


def _diff_kernel(q_ref, gate_ref, k_ref, v_ref, lam_ref, o_ref, m1, l1, a1, m2, l2, a2, *, lambda_init):
    tq, tk = DIFF_TQ, DIFF_TK
    qi = pl.program_id(2)
    q = q_ref[0]
    q1 = q[:, :HEAD_DIM]
    q2 = q[:, HEAD_DIM:]
    _flash_init(m1, l1, a1)
    _flash_init(m2, l2, a2)

    def step(rows, k0, width, bias):
        k = k_ref[0, pl.ds(k0, width), :]
        v = v_ref[0, pl.ds(k0, width), :]
        _flash_step(q1[rows], k[:, :HEAD_DIM], v, bias, m1.at[rows], l1.at[rows], a1.at[rows])
        _flash_step(q2[rows], k[:, HEAD_DIM:], v, bias, m2.at[rows], l2.at[rows], a2.at[rows])

    def body(kt, carry):
        step(slice(0, tq), pl.multiple_of(kt * tk, tk), tk, None)
        return carry

    lax.fori_loop(0, qi * (tq // tk), body, 0)
    dd = DIFF_DIAG
    q0 = qi * tq
    for c in range(tq // dd):
        n_rows = tq - c * dd
        ri = lax.broadcasted_iota(jnp.int32, (n_rows, dd), 0)
        ci = lax.broadcasted_iota(jnp.int32, (n_rows, dd), 1)
        step(slice(c * dd, tq), pl.multiple_of(q0 + c * dd, dd), dd, jnp.where(ri >= ci, 0.0, MASKED))

    lp = lam_ref[...]
    lam = (jnp.exp(jnp.sum(lp[0:1] * lp[1:2], axis=-1, keepdims=True))
           - jnp.exp(jnp.sum(lp[2:3] * lp[3:4], axis=-1, keepdims=True)) + lambda_init)
    o = _flash_finish(l1, a1) - lam * _flash_finish(l2, a2)
    o = _rms(o) * (1.0 - lambda_init)
    o_ref[0] = (o * gate_ref[0].astype(F32)).astype(o_ref.dtype)


def _diff_attention(proj, lam_params, lambda_init):
    b, s, _ = proj.shape
    h = DIFF_HEADS
    tq = DIFF_TQ
    w = DIFF_V_DIM
    q_spec = lambda off: pl.BlockSpec((1, tq, w), lambda bi, hi, qi: (bi, qi, off + hi))
    kv_spec = lambda off: pl.BlockSpec((1, s, w), lambda bi, hi, qi: (bi, 0, off + hi))
    stat = pltpu.VMEM((tq, LANES), F32)
    acc = pltpu.VMEM((tq, w), F32)
    return pl.pallas_call(
        functools.partial(_diff_kernel, lambda_init=lambda_init),
        grid=(b, h, s // tq),
        in_specs=[q_spec(0), q_spec(3 * h), kv_spec(h), kv_spec(2 * h),
                  pl.BlockSpec(lam_params.shape, lambda bi, hi, qi: (0, 0))],
        out_specs=pl.BlockSpec((1, tq, w), lambda bi, hi, qi: (bi, qi, hi)),
        out_shape=jax.ShapeDtypeStruct((b, s, DIFF_WIDTH), BF16),
        scratch_shapes=[stat, stat, acc, stat, stat, acc],
        compiler_params=pltpu.CompilerParams(
            dimension_semantics=("parallel", "parallel", "arbitrary"), vmem_limit_bytes=VMEM_LIMIT),
        name="diff_attention",
    )(proj, proj, proj, proj, lam_params)


def _rope_tables(pos):
    inv = 1.0 / (ROPE_THETA ** (jnp.arange(0, HEAD_DIM, 2, dtype=F32) / HEAD_DIM))
    ang = pos.astype(F32)[:, None] * inv[None, :]
    cos, sin = jnp.cos(ang), jnp.sin(ang)
    return jnp.concatenate([cos, cos], axis=-1), jnp.concatenate([-sin, sin], axis=-1)


def _retention_tables():
    h, c = RET_HEADS, RET_CHUNK
    log_g = jnp.log1p(-jnp.exp2(-5.0 - jnp.arange(h, dtype=F32)))
    j = jnp.arange(c, dtype=F32)
    diff = j[:, None] - j[None, :]
    intra = jnp.where(diff >= 0, jnp.exp(log_g[:, None, None] * jnp.maximum(diff, 0.0)), 0.0)
    q_dec = jnp.exp(log_g[:, None] * (j + 1.0))
    k_dec = jnp.exp(log_g[:, None] * (c - 1.0 - j))
    chunk_dec = jnp.exp(log_g * c)
    wide = lambda t: jnp.broadcast_to(t[:, :, None], (h, t.shape[1], HEAD_DIM))
    return intra, wide(q_dec), wide(k_dec), wide(chunk_dec[:, None])


def _selection_tables(seq):
    n_cmp_rows = seq // CMP_STRIDE
    n_slc = seq // SLC_BLOCK
    c_start = np.arange(n_cmp_rows) * CMP_STRIDE
    s_start = np.arange(n_slc) * SLC_BLOCK
    overlap_t = ((c_start[None, :] <= s_start[:, None] + SLC_BLOCK - 1)
                 & (c_start[None, :] + CMP_BLOCK - 1 >= s_start[:, None]))
    expand = (np.arange(seq)[None, :] // SLC_BLOCK) == np.arange(LANES)[:, None]
    return jnp.asarray(overlap_t, BF16), jnp.asarray(expand, BF16)


def _l0_tile_modes():
    t = PROJ_TN
    per = t // LANES
    seg = lambda op, table=0: (op, table)
    off = {}
    modes = []
    col = 0

    def add(name, width, kinds):
        nonlocal col
        off[name] = col
        n_tiles = width // t
        modes.append((col // t, col // t + n_tiles, kinds))
        col += width

    add("rq", RET_WIDTH, [seg("rope", L0_TAB_ROPE)] * per)
    add("rk", RET_WIDTH, [seg("rope", L0_TAB_ROPE_SCALED)] * per)
    add("rv", RET_WIDTH, [seg("plain")] * per)
    add("rgate", RET_WIDTH, [seg("silu")] * per)
    add("nq", NSA_WIDTH, [seg("nrope", L0_TAB_Q)] * per)
    kvh = NSA_KV_HEADS
    add("kc", 2 * NSA_KV_WIDTH, [seg("plain")] * per)
    off["vc"] = off["kc"] + NSA_KV_WIDTH
    add("ks", 2 * NSA_KV_WIDTH, [seg("nrope", L0_TAB_K)] * kvh + [seg("plain")] * kvh)
    off["vs"] = off["ks"] + NSA_KV_WIDTH
    add("kw", 2 * NSA_KV_WIDTH, [seg("nrope", L0_TAB_K)] * kvh + [seg("plain")] * kvh)
    off["vw"] = off["kw"] + NSA_KV_WIDTH
    add("ngate", NSA_WIDTH, [seg("silu")] * per)
    assert col == AB_MAIN_COLS
    return modes, off


def _l1_tile_modes():
    per = PROJ_TN // LANES
    n = DIFF_WIDTH // PROJ_TN
    return [(0, n, [("nrope", L1_TAB_Q)] * per), (n, 2 * n, [("nrope", L1_TAB_K)] * per),
            (2 * n, 3 * n, [("plain", 0)] * per), (3 * n, 4 * n, [("silu", 0)] * per)]


def _rope_pair(cos, sin_signed, gain=None, scale=1.0):
    if gain is None:
        return [cos * scale, sin_signed * scale]
    return [cos * (gain * scale)[None, :], sin_signed * (jnp.roll(gain, HALF) * scale)[None, :]]


def kernel(x, l0_norm_g, l0_w_in, l0_w_out, l0_nsa_q_norm_g, l0_nsa_k_norm_g, l0_cmp_pe_k, l0_cmp_w1_k, l0_cmp_w2_k, l0_cmp_pe_v, l0_cmp_w1_v, l0_cmp_w2_v, l1_norm_g, l1_w_in, l1_w_out, l1_q_norm_g, l1_k_norm_g, l1_lambda_q1, l1_lambda_k1, l1_lambda_q2, l1_lambda_k2):
    b, s, d = x.shape
    m = b * s
    x2d = x.reshape(m, d)
    cos, sin = _rope_tables(jnp.arange(s))
    cos_c, sin_c = _rope_tables(jnp.arange(s // CMP_STRIDE) * CMP_STRIDE + CMP_BLOCK - 1)
    intra, qdec, kdec, cdec = _retention_tables()
    ovl_t, eexp = _selection_tables(s)

    modes0, off = _l0_tile_modes()
    w0 = l0_w_in[:, :AB_MAIN_COLS].astype(BF16)
    wg = l0_w_in[:, AB_MAIN_COLS:].reshape(d, 3, NSA_KV_HEADS, NSA_GROUP).transpose(0, 2, 1, 3)
    wg = wg.reshape(d, NSA_KV_HEADS, 3 * NSA_GROUP)
    wg = jnp.pad(wg, ((0, 0), (0, 0), (0, LANES - 3 * NSA_GROUP))).reshape(d, NSA_KV_HEADS * LANES).astype(BF16)
    tabs0 = jnp.stack(_rope_pair(cos, sin) + _rope_pair(cos, sin, scale=QK_SCALE)
                      + _rope_pair(cos, sin, l0_nsa_q_norm_g, QK_SCALE) + _rope_pair(cos, sin, l0_nsa_k_norm_g))
    proj0, gates = _norm_proj(x2d, l0_norm_g, w0, tabs0, modes0, s, wg=wg)
    proj0 = proj0.reshape(b, s, AB_MAIN_COLS)
    y_ret = _retention(proj0, intra, qdec, kdec, cdec)
    w1k = l0_cmp_w1_k.astype(BF16).reshape(CMP_BLOCK, HEAD_DIM, HEAD_DIM)
    w1v = l0_cmp_w1_v.astype(BF16).reshape(CMP_BLOCK, HEAD_DIM, HEAD_DIM)
    kcmp, vcmp = _nsa_compress(proj0, off["kc"] // HEAD_DIM, off["vc"] // HEAD_DIM,
                               l0_cmp_pe_k, w1k, l0_cmp_w2_k.astype(BF16),
                               l0_cmp_pe_v, w1v, l0_cmp_w2_v.astype(BF16),
                               l0_nsa_k_norm_g, cos_c, sin_c)
    y_nsa = _nsa_attention(proj0, gates, kcmp, vcmp, ovl_t, eexp, off)
    w_out0 = l0_w_out.astype(BF16)
    x1 = _out_proj([y_ret.reshape(m, RET_WIDTH), y_nsa.reshape(m, NSA_WIDTH)],
                   [w_out0[:RET_WIDTH], w_out0[RET_WIDTH:]], x2d)

    lambda_init = 0.8 - 0.6 * math.exp(-0.3 * 1)
    tabs1 = jnp.stack(_rope_pair(cos, sin, l1_q_norm_g, QK_SCALE) + _rope_pair(cos, sin, l1_k_norm_g))
    proj1 = _norm_proj(x1, l1_norm_g, l1_w_in.astype(BF16), tabs1, _l1_tile_modes(), s)[0]
    lam_params = jnp.stack([l1_lambda_q1, l1_lambda_k1, l1_lambda_q2, l1_lambda_k2]).astype(F32)
    y1 = _diff_attention(proj1.reshape(b, s, C_IN_COLS), lam_params, lambda_init)
    out = _out_proj([y1.reshape(m, DIFF_WIDTH)], [l1_w_out.astype(BF16)], x1)
    return out.reshape(b, s, d)
```

```python
import functools
import math

import numpy as np
import jax
import jax.numpy as jnp
from jax import lax
from jax.experimental import pallas as pl
from jax.experimental.pallas import tpu as pltpu

F32 = jnp.float32
BF16 = jnp.bfloat16

D_MODEL = 2048
HEAD_DIM = 128
HALF = HEAD_DIM // 2
ROPE_THETA = 10000.0
EPS = 1e-6
RET_HEADS = 8
RET_CHUNK = 128
NSA_HEADS = 8
NSA_KV_HEADS = 2
NSA_GROUP = NSA_HEADS // NSA_KV_HEADS
CMP_BLOCK = 32
CMP_STRIDE = 16
SLC_BLOCK = 64
SLC_TOPK = 16
N_LOCAL_BLOCKS = 2
WINDOW = 512
DIFF_HEADS = 8
DIFF_V_DIM = 2 * HEAD_DIM
QK_SCALE = HEAD_DIM ** -0.5

RET_WIDTH = RET_HEADS * HEAD_DIM
NSA_WIDTH = NSA_HEADS * HEAD_DIM
NSA_KV_WIDTH = NSA_KV_HEADS * HEAD_DIM
AB_MAIN_COLS = 4 * RET_WIDTH + 2 * NSA_WIDTH + 6 * NSA_KV_WIDTH
N_GATE_COLS = 3 * NSA_HEADS
DIFF_WIDTH = DIFF_HEADS * DIFF_V_DIM
C_IN_COLS = 4 * DIFF_WIDTH

LANES = 128
MASKED = -1e30
M_INIT = -1e29
VMEM_LIMIT = 56 * 1024 * 1024

PROJ_TM = 1024
PROJ_TN = 512
PROJ_SUB = 256
L0_TAB_ROPE, L0_TAB_ROPE_SCALED, L0_TAB_Q, L0_TAB_K = range(4)
L1_TAB_Q, L1_TAB_K = range(2)
NSA_TQ = 256
NSA_TK = 256
DIFF_TQ = 512
DIFF_TK = 512
DIFF_DIAG = 256


def _dot(a, b):
    return jnp.dot(a, b, preferred_element_type=F32)


def _dot_nt(a, b):
    return lax.dot_general(a, b, (((1,), (1,)), ((), ())), preferred_element_type=F32)


def _dot_tn(a, b):
    return lax.dot_general(a, b, (((0,), (0,)), ((), ())), preferred_element_type=F32)


def _silu(x):
    return x / (1.0 + jnp.exp(-x))


def _rms(x):
    return x * lax.rsqrt(jnp.mean(x * x, axis=-1, keepdims=True) + EPS)


def _rope(x, cos, sin_signed):
    return x * cos + pltpu.roll(x, HALF, axis=1) * sin_signed


def _segment_epilogue(seg, kind, tabs_ref, rows):
    op, t = kind
    if op == "plain":
        return seg
    if op == "silu":
        return _silu(seg)
    if op == "nrope":
        mean_mat = jnp.full((LANES, LANES), 1.0 / HEAD_DIM, BF16)
        seg = seg * lax.rsqrt(_dot((seg * seg).astype(BF16), mean_mat) + EPS)
    return seg * tabs_ref[2 * t, rows, :] + pltpu.roll(seg, HALF, axis=1) * tabs_ref[2 * t + 1, rows, :]


def _proj_kernel(*refs, tile_modes, has_gates):
    if has_gates:
        x_ref, g_ref, w_ref, tabs_ref, wg_ref, o_ref, og_ref, h_scr = refs
    else:
        x_ref, g_ref, w_ref, tabs_ref, o_ref, h_scr = refs
    j = pl.program_id(1)

    @pl.when(j == 0)
    def _():
        x = x_ref[...]
        h = (_rms(x) * g_ref[...]).astype(BF16)
        h_scr[...] = h
        if has_gates:
            z = _dot(h, wg_ref[...])
            og_ref[...] = 1.0 / (1.0 + jnp.exp(-z))

    def make_branch(lo, hi, kinds):
        @pl.when((j >= lo) & (j < hi))
        def _():
            for r in range(h_scr.shape[0] // PROJ_SUB):
                rows = slice(r * PROJ_SUB, (r + 1) * PROJ_SUB)
                acc = _dot(h_scr[rows, :], w_ref[...])
                for c, kind in enumerate(kinds):
                    cols = slice(c * LANES, (c + 1) * LANES)
                    o_ref[rows, cols] = _segment_epilogue(acc[:, cols], kind, tabs_ref, rows).astype(o_ref.dtype)

    for lo, hi, kinds in tile_modes:
        make_branch(lo, hi, kinds)


def _norm_proj(x2d, g, w, tabs, tile_modes, seq, wg=None):
    m, d = x2d.shape
    tm, tn = PROJ_TM, PROJ_TN
    n = tile_modes[-1][1] * tn
    s_tiles = seq // tm
    has_gates = wg is not None
    in_specs = [
        pl.BlockSpec((tm, d), lambda i, j: (i, 0)),
        pl.BlockSpec((1, d), lambda i, j: (0, 0)),
        pl.BlockSpec((d, tn), lambda i, j: (0, j)),
        pl.BlockSpec((tabs.shape[0], tm, LANES), lambda i, j: (0, i % s_tiles, 0)),
    ]
    args = [x2d, g.reshape(1, d), w, tabs]
    out_shape = [jax.ShapeDtypeStruct((m, n), BF16)]
    out_specs = [pl.BlockSpec((tm, tn), lambda i, j: (i, j))]
    if has_gates:
        ng = wg.shape[1]
        in_specs.append(pl.BlockSpec((d, ng), lambda i, j: (0, 0)))
        args.append(wg)
        out_shape.append(jax.ShapeDtypeStruct((m, ng), F32))
        out_specs.append(pl.BlockSpec((tm, ng), lambda i, j: (i, 0)))
    return pl.pallas_call(
        functools.partial(_proj_kernel, tile_modes=tile_modes, has_gates=has_gates),
        grid=(m // tm, n // tn),
        in_specs=in_specs,
        out_specs=out_specs,
        out_shape=out_shape,
        scratch_shapes=[pltpu.VMEM((tm, d), BF16)],
        compiler_params=pltpu.CompilerParams(
            dimension_semantics=("parallel", "arbitrary"), vmem_limit_bytes=VMEM_LIMIT),
        name="norm_proj",
    )(*args)


def _oproj_kernel(*refs, n_terms):
    y_refs, w_refs = refs[:n_terms], refs[n_terms:2 * n_terms]
    x_ref, o_ref = refs[2 * n_terms:]
    acc = x_ref[...]
    for y_ref, w_ref in zip(y_refs, w_refs):
        acc = acc + _dot(y_ref[...], w_ref[...])
    o_ref[...] = acc


def _out_proj(ys, ws, x2d):
    m, n = x2d.shape
    tm, tn = PROJ_TM, PROJ_TN
    y_specs = [pl.BlockSpec((tm, y.shape[1]), lambda i, j: (i, 0)) for y in ys]
    w_specs = [pl.BlockSpec((w.shape[0], tn), lambda i, j: (0, j)) for w in ws]
    return pl.pallas_call(
        functools.partial(_oproj_kernel, n_terms=len(ys)),
        grid=(m // tm, n // tn),
        in_specs=y_specs + w_specs + [pl.BlockSpec((tm, tn), lambda i, j: (i, j))],
        out_specs=pl.BlockSpec((tm, tn), lambda i, j: (i, j)),
        out_shape=jax.ShapeDtypeStruct((m, n), F32),
        compiler_params=pltpu.CompilerParams(
            dimension_semantics=("parallel", "arbitrary"), vmem_limit_bytes=VMEM_LIMIT),
        name="out_proj",
    )(*ys, *ws, x2d)


def _ret_kernel(q_ref, k_ref, v_ref, gate_ref, intra_ref, qdec_ref, kdec_ref, cdec_ref, o_ref, *, n_chunks):
    c = RET_CHUNK
    intra = intra_ref[0]
    qdec = qdec_ref[0]
    kdec = kdec_ref[0]
    cdec = cdec_ref[0]
    state = jnp.zeros((HEAD_DIM, HEAD_DIM), F32)
    for n in range(n_chunks):
        rows = slice(n * c, (n + 1) * c)
        q = q_ref[0, rows, :]
        k = k_ref[0, rows, :]
        v = v_ref[0, rows, :]
        scores = _dot_nt(q, k) * intra
        inner = _dot(scores.astype(BF16), v)
        cross = _dot((q.astype(F32) * qdec).astype(BF16), state.astype(BF16))
        kv = _dot_tn((k.astype(F32) * kdec).astype(BF16), v)
        state = state * cdec + kv
        o = _rms(inner + cross) * gate_ref[0, rows, :].astype(F32)
        o_ref[0, rows, :] = o.astype(o_ref.dtype)


def _retention(proj, intra, qdec, kdec, cdec):
    b, s, _ = proj.shape
    h = RET_HEADS
    y_shape = (b, s, RET_WIDTH)
    head = lambda off: pl.BlockSpec((1, s, HEAD_DIM), lambda bi, hi: (bi, 0, off + hi))
    table = lambda rows: pl.BlockSpec((1, rows, HEAD_DIM), lambda bi, hi: (hi, 0, 0))
    return pl.pallas_call(
        functools.partial(_ret_kernel, n_chunks=s // RET_CHUNK),
        grid=(b, h),
        in_specs=[head(0), head(h), head(2 * h), head(3 * h),
                  table(RET_CHUNK), table(RET_CHUNK), table(RET_CHUNK), table(1)],
        out_specs=pl.BlockSpec((1, s, HEAD_DIM), lambda bi, hi: (bi, 0, hi)),
        out_shape=jax.ShapeDtypeStruct(y_shape, BF16),
        compiler_params=pltpu.CompilerParams(
            dimension_semantics=("parallel", "arbitrary"), vmem_limit_bytes=VMEM_LIMIT),
        name="retention",
    )(proj, proj, proj, proj, intra, qdec, kdec, cdec)


def _cmp_kernel(kc_ref, vc_ref, pek_ref, w1k_ref, w2k_ref, pev_ref, w1v_ref, w2v_ref, kg_ref,
                cos_ref, sin_ref, ko_ref, vo_ref, t_scr, *, seq):
    n_rows = seq // CMP_STRIDE

    def compress(src_ref, pe_ref, w1_ref, w2_ref):
        t_scr[0:seq, :] = src_ref[0].astype(F32)
        t_scr[seq:seq + CMP_STRIDE, :] = jnp.zeros((CMP_STRIDE, HEAD_DIM), F32)
        acc = jnp.zeros((n_rows, HEAD_DIM), F32)
        for r in range(CMP_BLOCK):
            rows = t_scr[pl.ds(r, n_rows, stride=CMP_STRIDE), :] + pe_ref[r:r + 1, :]
            acc = acc + _dot(rows.astype(BF16), w1_ref[r])
        return _dot(_silu(acc).astype(BF16), w2_ref[...])

    kc = compress(kc_ref, pek_ref, w1k_ref, w2k_ref)
    kc = _rope(_rms(kc) * kg_ref[...], cos_ref[...], sin_ref[...])
    ko_ref[0, 0] = kc.astype(ko_ref.dtype)
    vo_ref[0, 0] = compress(vc_ref, pev_ref, w1v_ref, w2v_ref).astype(vo_ref.dtype)


def _nsa_compress(proj, kc_off, vc_off, pe_k, w1_k, w2_k, pe_v, w1_v, w2_v, k_g, cos_c, sin_c):
    b, s, _ = proj.shape
    g = NSA_KV_HEADS
    n_rows = s // CMP_STRIDE
    head = lambda off: pl.BlockSpec((1, s, HEAD_DIM), lambda bi, gi: (bi, 0, off + gi))
    full = lambda shape: pl.BlockSpec(shape, lambda bi, gi: (0,) * len(shape))
    out_spec = pl.BlockSpec((1, 1, n_rows, HEAD_DIM), lambda bi, gi: (bi, gi, 0, 0))
    out_sds = jax.ShapeDtypeStruct((b, g, n_rows, HEAD_DIM), BF16)
    return pl.pallas_call(
        functools.partial(_cmp_kernel, seq=s),
        grid=(b, g),
        in_specs=[head(kc_off), head(vc_off),
                  full((CMP_BLOCK, HEAD_DIM)), full((CMP_BLOCK, HEAD_DIM, HEAD_DIM)), full((HEAD_DIM, HEAD_DIM)),
                  full((CMP_BLOCK, HEAD_DIM)), full((CMP_BLOCK, HEAD_DIM, HEAD_DIM)), full((HEAD_DIM, HEAD_DIM)),
                  full((1, HEAD_DIM)), full((n_rows, HEAD_DIM)), full((n_rows, HEAD_DIM))],
        out_specs=[out_spec, out_spec],
        out_shape=[out_sds, out_sds],
        scratch_shapes=[pltpu.VMEM((s + CMP_STRIDE, HEAD_DIM), F32)],
        compiler_params=pltpu.CompilerParams(
            dimension_semantics=("parallel", "arbitrary"), vmem_limit_bytes=VMEM_LIMIT),
        name="nsa_compress",
    )(proj, proj, pe_k, w1_k, w2_k, pe_v, w1_v, w2_v, k_g.reshape(1, HEAD_DIM), cos_c, sin_c)


def _flash_init(m_scr, l_scr, acc_scr):
    m_scr[...] = jnp.full(m_scr.shape, M_INIT, F32)
    l_scr[...] = jnp.zeros(l_scr.shape, F32)
    acc_scr[...] = jnp.zeros(acc_scr.shape, F32)


def _lane_tiles(x, width):
    return x if width == LANES else jnp.concatenate([x] * (width // LANES), axis=1)


def _flash_step(q, k, v, bias, m_scr, l_scr, acc_scr):
    s = _dot_nt(q, k)
    if bias is not None:
        s = s + bias
    tk = s.shape[1]
    m_old = m_scr[...]
    m_new = jnp.maximum(m_old, jnp.max(s, axis=-1, keepdims=True))
    alpha = jnp.exp(m_old - m_new)
    p = jnp.exp(s - _lane_tiles(m_new, tk))
    p_cols = p[:, 0:LANES]
    for c in range(1, tk // LANES):
        p_cols = p_cols + p[:, c * LANES:(c + 1) * LANES]
    l_scr[...] = alpha * l_scr[...] + p_cols
    acc_scr[...] = _lane_tiles(alpha, acc_scr.shape[1]) * acc_scr[...] + _dot(p.astype(BF16), v)
    m_scr[...] = m_new


def _flash_finish(l_scr, acc_scr):
    return acc_scr[...] * (1.0 / jnp.sum(l_scr[...], axis=-1, keepdims=True))


def _rep_heads(bias):
    return jnp.concatenate([bias] * NSA_GROUP, axis=0)


def _nsa_kernel(q_ref, ngate_ref, gates_ref, kcmp_ref, vcmp_ref, ks_ref, vs_ref, kw_ref, vw_ref,
                ovl_ref, eexp_ref, o_ref, bias_scr, m_scr, l_scr, acc_scr, *, seq):
    tq, tk = NSA_TQ, NSA_TK
    n_slc = seq // SLC_BLOCK
    qi = pl.program_id(2)
    q0 = qi * tq
    qblk = q_ref[0]
    q4 = jnp.concatenate([qblk[:, r * HEAD_DIM:(r + 1) * HEAD_DIM] for r in range(NSA_GROUP)], axis=0)

    tpos = q0 + lax.broadcasted_iota(jnp.int32, (tq, LANES), 0)
    cidx = lax.broadcasted_iota(jnp.int32, (tq, LANES), 1)
    cbias = jnp.where(cidx * CMP_STRIDE + (CMP_BLOCK - 1) <= tpos, 0.0, MASKED)
    s = _dot_nt(q4, kcmp_ref[0, 0]) + _rep_heads(cbias)
    m = jnp.maximum(jnp.max(s, axis=-1, keepdims=True), M_INIT)
    e = jnp.exp(s - m)
    l = jnp.sum(e, axis=-1, keepdims=True)
    p = e * (1.0 / jnp.maximum(l, 1e-30))
    o_cmp = _dot(p.astype(BF16), vcmp_ref[0, 0])
    psum = p[0:tq]
    for r in range(1, NSA_GROUP):
        psum = psum + p[r * tq:(r + 1) * tq]

    ovl = ovl_ref[...]
    p_hi = psum.astype(BF16)
    rem = psum - p_hi.astype(F32)
    p_mid = rem.astype(BF16)
    p_lo = (rem - p_mid.astype(F32)).astype(BF16)
    imp = _dot_nt(ovl, p_hi) + _dot_nt(ovl, p_mid) + _dot_nt(ovl, p_lo)
    jb = lax.broadcasted_iota(jnp.int32, (n_slc, tq), 0)
    blk_t = jnp.right_shift(q0 + lax.broadcasted_iota(jnp.int32, (n_slc, tq), 1), int(math.log2(SLC_BLOCK)))
    back = blk_t - jb
    forced = (jb == 0) | ((back >= 0) & (back < N_LOCAL_BLOCKS))
    score = jnp.where(forced, 1e9, jnp.where(back >= 0, imp, -1e9))
    rank = jnp.zeros((n_slc, tq), F32)
    for mp in range(n_slc):
        row = score[mp:mp + 1, :]
        ahead = (row > score) | ((row == score) & (jb > mp))
        rank = rank + jnp.where(ahead, 1.0, 0.0)
    sel_t = jnp.where(rank < float(min(SLC_TOPK, n_slc)), 1.0, 0.0)
    sel_t = jnp.concatenate([sel_t, jnp.zeros((LANES - n_slc, tq), F32)], axis=0).astype(BF16)
    ri = lax.broadcasted_iota(jnp.int32, (tq, tq), 0)
    ci = lax.broadcasted_iota(jnp.int32, (tq, tq), 1)
    eye = jnp.where(ri == ci, 1.0, 0.0).astype(BF16)
    sel = _dot_nt(eye, sel_t).astype(BF16)
    sel_keys = _dot(sel, eexp_ref[...])
    for kt in range(seq // tk):
        bias_scr[kt] = jnp.where(sel_keys[:, kt * tk:(kt + 1) * tk] > 0.5, 0.0, MASKED)

    _flash_init(m_scr, l_scr, acc_scr)

    def slc_body(kt, carry):
        k0 = pl.multiple_of(kt * tk, tk)
        _flash_step(q4, ks_ref[0, pl.ds(k0, tk), :], vs_ref[0, pl.ds(k0, tk), :],
                    _rep_heads(bias_scr[kt]), m_scr, l_scr, acc_scr)
        return carry

    n_full = lax.div(q0, tk)
    lax.fori_loop(0, n_full, slc_body, 0)
    kd = pl.multiple_of(n_full * tk, tk)
    qpos = q0 + lax.broadcasted_iota(jnp.int32, (tq, tk), 0)
    kpos = kd + lax.broadcasted_iota(jnp.int32, (tq, tk), 1)
    causal = jnp.where(kpos <= qpos, 0.0, MASKED)
    _flash_step(q4, ks_ref[0, pl.ds(kd, tk), :], vs_ref[0, pl.ds(kd, tk), :],
                _rep_heads(bias_scr[n_full] + causal), m_scr, l_scr, acc_scr)
    o_slc = _flash_finish(l_scr, acc_scr)

    wk = WINDOW + tq
    ws = pl.multiple_of(jnp.maximum(q0 - WINDOW, 0), tq)
    d = (q0 - ws) + lax.broadcasted_iota(jnp.int32, (tq, wk), 0) - lax.broadcasted_iota(jnp.int32, (tq, wk), 1)
    wbias = jnp.where((d >= 0) & (d < WINDOW), 0.0, MASKED)
    s = _dot_nt(q4, kw_ref[0, pl.ds(ws, wk), :]) + _rep_heads(wbias)
    e = jnp.exp(s - jnp.max(s, axis=-1, keepdims=True))
    o_win = _dot(e.astype(BF16), vw_ref[0, pl.ds(ws, wk), :]) * (1.0 / jnp.sum(e, axis=-1, keepdims=True))

    gates = gates_ref[...]
    for r in range(NSA_GROUP):
        rows = slice(r * tq, (r + 1) * tq)
        g_cmp = gates[:, r:r + 1]
        g_slc = gates[:, NSA_GROUP + r:NSA_GROUP + r + 1]
        g_win = gates[:, 2 * NSA_GROUP + r:2 * NSA_GROUP + r + 1]
        y = g_cmp * o_cmp[rows] + g_slc * o_slc[rows] + g_win * o_win[rows]
        cols = slice(r * HEAD_DIM, (r + 1) * HEAD_DIM)
        o_ref[0, :, cols] = (y * ngate_ref[0, :, cols].astype(F32)).astype(o_ref.dtype)


def _nsa_attention(proj, gates, kcmp, vcmp, ovl_t, eexp, offs):
    b, s, _ = proj.shape
    g = NSA_KV_HEADS
    tq = NSA_TQ
    nq = s // tq
    gw = NSA_GROUP * HEAD_DIM
    q_spec = lambda off: pl.BlockSpec((1, tq, gw), lambda bi, gi, qi: (bi, qi, off + gi))
    kv_spec = lambda off: pl.BlockSpec((1, s, HEAD_DIM), lambda bi, gi, qi: (bi, 0, off + gi))
    cmp_spec = pl.BlockSpec((1, 1, s // CMP_STRIDE, HEAD_DIM), lambda bi, gi, qi: (bi, gi, 0, 0))
    full = lambda shape: pl.BlockSpec(shape, lambda bi, gi, qi: (0,) * len(shape))
    rows4 = NSA_GROUP * tq
    return pl.pallas_call(
        functools.partial(_nsa_kernel, seq=s),
        grid=(b, g, nq),
        in_specs=[q_spec(offs["nq"] // gw), q_spec(offs["ngate"] // gw),
                  pl.BlockSpec((tq, LANES), lambda bi, gi, qi: (bi * nq + qi, gi)),
                  cmp_spec, cmp_spec,
                  kv_spec(offs["ks"] // HEAD_DIM), kv_spec(offs["vs"] // HEAD_DIM),
                  kv_spec(offs["kw"] // HEAD_DIM), kv_spec(offs["vw"] // HEAD_DIM),
                  full(ovl_t.shape), full(eexp.shape)],
        out_specs=pl.BlockSpec((1, tq, gw), lambda bi, gi, qi: (bi, qi, gi)),
        out_shape=jax.ShapeDtypeStruct((b, s, NSA_WIDTH), BF16),
        scratch_shapes=[pltpu.VMEM((s // NSA_TK, tq, NSA_TK), F32),
                        pltpu.VMEM((rows4, LANES), F32), pltpu.VMEM((rows4, LANES), F32),
                        pltpu.VMEM((rows4, HEAD_DIM), F32)],
        compiler_params=pltpu.CompilerParams(
            dimension_semantics=("parallel", "parallel", "arbitrary"), vmem_limit_bytes=VMEM_LIMIT),
        name="nsa_attention",
    )(proj, proj, gates, kcmp, vcmp, proj, proj, proj, proj, ovl_t, eexp)


def _diff_kernel(q_ref, gate_ref, k_ref, v_ref, lam_ref, o_ref, m1, l1, a1, m2, l2, a2, *, lambda_init):
    tq, tk = DIFF_TQ, DIFF_TK
    qi = pl.program_id(2)
    q = q_ref[0]
    q1 = q[:, :HEAD_DIM]
    q2 = q[:, HEAD_DIM:]
    _flash_init(m1, l1, a1)
    _flash_init(m2, l2, a2)

    def step(rows, k0, width, bias):
        k = k_ref[0, pl.ds(k0, width), :]
        v = v_ref[0, pl.ds(k0, width), :]
        _flash_step(q1[rows], k[:, :HEAD_DIM], v, bias, m1.at[rows], l1.at[rows], a1.at[rows])
        _flash_step(q2[rows], k[:, HEAD_DIM:], v, bias, m2.at[rows], l2.at[rows], a2.at[rows])

    def body(kt, carry):
        step(slice(0, tq), pl.multiple_of(kt * tk, tk), tk, None)
        return carry

    lax.fori_loop(0, qi * (tq // tk), body, 0)
    dd = DIFF_DIAG
    q0 = qi * tq
    for c in range(tq // dd):
        n_rows = tq - c * dd
        ri = lax.broadcasted_iota(jnp.int32, (n_rows, dd), 0)
        ci = lax.broadcasted_iota(jnp.int32, (n_rows, dd), 1)
        step(slice(c * dd, tq), pl.multiple_of(q0 + c * dd, dd), dd, jnp.where(ri >= ci, 0.0, MASKED))

    lp = lam_ref[...]
    lam = (jnp.exp(jnp.sum(lp[0:1] * lp[1:2], axis=-1, keepdims=True))
           - jnp.exp(jnp.sum(lp[2:3] * lp[3:4], axis=-1, keepdims=True)) + lambda_init)
    o = _flash_finish(l1, a1) - lam * _flash_finish(l2, a2)
    o = _rms(o) * (1.0 - lambda_init)
    o_ref[0] = (o * gate_ref[0].astype(F32)).astype(o_ref.dtype)


def _diff_attention(proj, lam_params, lambda_init):
    b, s, _ = proj.shape
    h = DIFF_HEADS
    tq = DIFF_TQ
    w = DIFF_V_DIM
    q_spec = lambda off: pl.BlockSpec((1, tq, w), lambda bi, hi, qi: (bi, qi, off + hi))
    kv_spec = lambda off: pl.BlockSpec((1, s, w), lambda bi, hi, qi: (bi, 0, off + hi))
    stat = pltpu.VMEM((tq, LANES), F32)
    acc = pltpu.VMEM((tq, w), F32)
    return pl.pallas_call(
        functools.partial(_diff_kernel, lambda_init=lambda_init),
        grid=(b, h, s // tq),
        in_specs=[q_spec(0), q_spec(3 * h), kv_spec(h), kv_spec(2 * h),
                  pl.BlockSpec(lam_params.shape, lambda bi, hi, qi: (0, 0))],
        out_specs=pl.BlockSpec((1, tq, w), lambda bi, hi, qi: (bi, qi, hi)),
        out_shape=jax.ShapeDtypeStruct((b, s, DIFF_WIDTH), BF16),
        scratch_shapes=[stat, stat, acc, stat, stat, acc],
        compiler_params=pltpu.CompilerParams(
            dimension_semantics=("parallel", "parallel", "arbitrary"), vmem_limit_bytes=VMEM_LIMIT),
        name="diff_attention",
    )(proj, proj, proj, proj, lam_params)


def _rope_tables(pos):
    inv = 1.0 / (ROPE_THETA ** (jnp.arange(0, HEAD_DIM, 2, dtype=F32) / HEAD_DIM))
    ang = pos.astype(F32)[:, None] * inv[None, :]
    cos, sin = jnp.cos(ang), jnp.sin(ang)
    return jnp.concatenate([cos, cos], axis=-1), jnp.concatenate([-sin, sin], axis=-1)


def _retention_tables():
    h, c = RET_HEADS, RET_CHUNK
    log_g = jnp.log1p(-jnp.exp2(-5.0 - jnp.arange(h, dtype=F32)))
    j = jnp.arange(c, dtype=F32)
    diff = j[:, None] - j[None, :]
    intra = jnp.where(diff >= 0, jnp.exp(log_g[:, None, None] * jnp.maximum(diff, 0.0)), 0.0)
    q_dec = jnp.exp(log_g[:, None] * (j + 1.0))
    k_dec = jnp.exp(log_g[:, None] * (c - 1.0 - j))
    chunk_dec = jnp.exp(log_g * c)
    wide = lambda t: jnp.broadcast_to(t[:, :, None], (h, t.shape[1], HEAD_DIM))
    return intra, wide(q_dec), wide(k_dec), wide(chunk_dec[:, None])


def _selection_tables(seq):
    n_cmp_rows = seq // CMP_STRIDE
    n_slc = seq // SLC_BLOCK
    c_start = np.arange(n_cmp_rows) * CMP_STRIDE
    s_start = np.arange(n_slc) * SLC_BLOCK
    overlap_t = ((c_start[None, :] <= s_start[:, None] + SLC_BLOCK - 1)
                 & (c_start[None, :] + CMP_BLOCK - 1 >= s_start[:, None]))
    expand = (np.arange(seq)[None, :] // SLC_BLOCK) == np.arange(LANES)[:, None]
    return jnp.asarray(overlap_t, BF16), jnp.asarray(expand, BF16)


def _l0_tile_modes():
    t = PROJ_TN
    per = t // LANES
    seg = lambda op, table=0: (op, table)
    off = {}
    modes = []
    col = 0

    def add(name, width, kinds):
        nonlocal col
        off[name] = col
        n_tiles = width // t
        modes.append((col // t, col // t + n_tiles, kinds))
        col += width

    add("rq", RET_WIDTH, [seg("rope", L0_TAB_ROPE)] * per)
    add("rk", RET_WIDTH, [seg("rope", L0_TAB_ROPE_SCALED)] * per)
    add("rv", RET_WIDTH, [seg("plain")] * per)
    add("rgate", RET_WIDTH, [seg("silu")] * per)
    add("nq", NSA_WIDTH, [seg("nrope", L0_TAB_Q)] * per)
    kvh = NSA_KV_HEADS
    add("kc", 2 * NSA_KV_WIDTH, [seg("plain")] * per)
    off["vc"] = off["kc"] + NSA_KV_WIDTH
    add("ks", 2 * NSA_KV_WIDTH, [seg("nrope", L0_TAB_K)] * kvh + [seg("plain")] * kvh)
    off["vs"] = off["ks"] + NSA_KV_WIDTH
    add("kw", 2 * NSA_KV_WIDTH, [seg("nrope", L0_TAB_K)] * kvh + [seg("plain")] * kvh)
    off["vw"] = off["kw"] + NSA_KV_WIDTH
    add("ngate", NSA_WIDTH, [seg("silu")] * per)
    assert col == AB_MAIN_COLS
    return modes, off


def _l1_tile_modes():
    per = PROJ_TN // LANES
    n = DIFF_WIDTH // PROJ_TN
    return [(0, n, [("nrope", L1_TAB_Q)] * per), (n, 2 * n, [("nrope", L1_TAB_K)] * per),
            (2 * n, 3 * n, [("plain", 0)] * per), (3 * n, 4 * n, [("silu", 0)] * per)]


def _rope_pair(cos, sin_signed, gain=None, scale=1.0):
    if gain is None:
        return [cos * scale, sin_signed * scale]
    return [cos * (gain * scale)[None, :], sin_signed * (jnp.roll(gain, HALF) * scale)[None, :]]


def kernel(x, l0_norm_g, l0_w_in, l0_w_out, l0_nsa_q_norm_g, l0_nsa_k_norm_g, l0_cmp_pe_k, l0_cmp_w1_k, l0_cmp_w2_k, l0_cmp_pe_v, l0_cmp_w1_v, l0_cmp_w2_v, l1_norm_g, l1_w_in, l1_w_out, l1_q_norm_g, l1_k_norm_g, l1_lambda_q1, l1_lambda_k1, l1_lambda_q2, l1_lambda_k2):
    b, s, d = x.shape
    m = b * s
    x2d = x.reshape(m, d)
    cos, sin = _rope_tables(jnp.arange(s))
    cos_c, sin_c = _rope_tables(jnp.arange(s // CMP_STRIDE) * CMP_STRIDE + CMP_BLOCK - 1)
    intra, qdec, kdec, cdec = _retention_tables()
    ovl_t, eexp = _selection_tables(s)

    modes0, off = _l0_tile_modes()
    w0 = l0_w_in.astype(BF16)
    wg = l0_w_in[:, AB_MAIN_COLS:].reshape(d, 3, NSA_KV_HEADS, NSA_GROUP).transpose(0, 2, 1, 3)
    wg = wg.reshape(d, NSA_KV_HEADS, 3 * NSA_GROUP)
    wg = jnp.pad(wg, ((0, 0), (0, 0), (0, LANES - 3 * NSA_GROUP))).reshape(d, NSA_KV_HEADS * LANES).astype(BF16)
    tabs0 = jnp.stack(_rope_pair(cos, sin) + _rope_pair(cos, sin, scale=QK_SCALE)
                      + _rope_pair(cos, sin, l0_nsa_q_norm_g, QK_SCALE) + _rope_pair(cos, sin, l0_nsa_k_norm_g))
    proj0, gates = _norm_proj(x2d, l0_norm_g, w0, tabs0, modes0, s, wg=wg)
    proj0 = proj0.reshape(b, s, AB_MAIN_COLS)
    y_ret = _retention(proj0, intra, qdec, kdec, cdec)
    w1k = l0_cmp_w1_k.astype(BF16).reshape(CMP_BLOCK, HEAD_DIM, HEAD_DIM)
    w1v = l0_cmp_w1_v.astype(BF16).reshape(CMP_BLOCK, HEAD_DIM, HEAD_DIM)
    kcmp, vcmp = _nsa_compress(proj0, off["kc"] // HEAD_DIM, off["vc"] // HEAD_DIM,
                               l0_cmp_pe_k, w1k, l0_cmp_w2_k.astype(BF16),
                               l0_cmp_pe_v, w1v, l0_cmp_w2_v.astype(BF16),
                               l0_nsa_k_norm_g, cos_c, sin_c)
    y_nsa = _nsa_attention(proj0, gates, kcmp, vcmp, ovl_t, eexp, off)
    w_out0 = l0_w_out.astype(BF16)
    x1 = _out_proj([y_ret.reshape(m, RET_WIDTH), y_nsa.reshape(m, NSA_WIDTH)],
                   [w_out0[:RET_WIDTH], w_out0[RET_WIDTH:]], x2d)

    lambda_init = 0.8 - 0.6 * math.exp(-0.3 * 1)
    tabs1 = jnp.stack(_rope_pair(cos, sin, l1_q_norm_g, QK_SCALE) + _rope_pair(cos, sin, l1_k_norm_g))
    proj1 = _norm_proj(x1, l1_norm_g, l1_w_in.astype(BF16), tabs1, _l1_tile_modes(), s)[0]
    lam_params = jnp.stack([l1_lambda_q1, l1_lambda_k1, l1_lambda_q2, l1_lambda_k2]).astype(F32)
    y1 = _diff_attention(proj1.reshape(b, s, C_IN_COLS), lam_params, lambda_init)
    out = _out_proj([y1.reshape(m, DIFF_WIDTH)], [l1_w_out.astype(BF16)], x1)
    return out.reshape(b, s, d)
```

```python
import functools
import math

import numpy as np
import jax
import jax.numpy as jnp
from jax import lax
from jax.experimental import pallas as pl
from jax.experimental.pallas import tpu as pltpu

F32 = jnp.float32
BF16 = jnp.bfloat16

D_MODEL = 2048
HEAD_DIM = 128
HALF = HEAD_DIM // 2
ROPE_THETA = 10000.0
EPS = 1e-6
RET_HEADS = 8
RET_CHUNK = 128
NSA_HEADS = 8
NSA_KV_HEADS = 2
NSA_GROUP = NSA_HEADS // NSA_KV_HEADS
CMP_BLOCK = 32
CMP_STRIDE = 16
SLC_BLOCK = 64
SLC_TOPK = 16
N_LOCAL_BLOCKS = 2
WINDOW = 512
DIFF_HEADS = 8
DIFF_V_DIM = 2 * HEAD_DIM
QK_SCALE = HEAD_DIM ** -0.5
LOG2E = math.log2(math.e)
Q_SCALE = QK_SCALE * LOG2E

RET_WIDTH = RET_HEADS * HEAD_DIM
NSA_WIDTH = NSA_HEADS * HEAD_DIM
NSA_KV_WIDTH = NSA_KV_HEADS * HEAD_DIM
AB_MAIN_COLS = 4 * RET_WIDTH + 2 * NSA_WIDTH + 6 * NSA_KV_WIDTH
N_GATE_COLS = 3 * NSA_HEADS
DIFF_WIDTH = DIFF_HEADS * DIFF_V_DIM
C_IN_COLS = 4 * DIFF_WIDTH

LANES = 128
MASKED = -1e30
M_INIT = -1e29
MAX_SCORE_BOUND = 60.0
BOUND_MARGIN = 1.05
AUG_ONE_LANE = 32
VMEM_LIMIT = 56 * 1024 * 1024

PROJ_TM = 1024
PROJ_TN = 512
L1_PROJ_TN = 1024
OPROJ_TN = 1024
PROJ_SUB = 256
L0_TAB_ROPE, L0_TAB_ROPE_SCALED, L0_TAB_Q, L0_TAB_K = range(4)
L1_TAB_Q, L1_TAB_K = range(2)
NSA_TQ = 256
NSA_TK = 256
DIFF_TQ = 512
DIFF_TK = 512
DIFF_DIAG = 256


def _dot(a, b):
    return jnp.dot(a, b, preferred_element_type=F32)


def _dot_nt(a, b):
    return lax.dot_general(a, b, (((1,), (1,)), ((), ())), preferred_element_type=F32)


def _dot_tn(a, b):
    return lax.dot_general(a, b, (((0,), (0,)), ((), ())), preferred_element_type=F32)


def _silu(x):
    return x / (1.0 + jnp.exp(-x))


def _rms(x):
    return x * lax.rsqrt(jnp.mean(x * x, axis=-1, keepdims=True) + EPS)


def _rope(x, cos, sin_signed):
    return x * cos + pltpu.roll(x, HALF, axis=1) * sin_signed


def _segment_epilogue(seg, kind, tabs_ref, rows):
    op, t = kind
    if op == "plain":
        return seg
    if op == "silu":
        return _silu(seg)
    if op == "nrope":
        mean_mat = jnp.full((LANES, LANES), 1.0 / HEAD_DIM, BF16)
        seg = seg * lax.rsqrt(_dot((seg * seg).astype(BF16), mean_mat) + EPS)
    return seg * tabs_ref[2 * t, rows, :] + pltpu.roll(seg, HALF, axis=1) * tabs_ref[2 * t + 1, rows, :]


def _proj_kernel(*refs, tile_modes, has_gates):
    if has_gates:
        x_ref, g_ref, w_ref, tabs_ref, wg_ref, o_ref, og_ref, h_scr = refs
    else:
        x_ref, g_ref, w_ref, tabs_ref, o_ref, h_scr = refs
    j = pl.program_id(1)

    @pl.when(j == 0)
    def _():
        x = x_ref[...]
        h = (_rms(x) * g_ref[...]).astype(BF16)
        h_scr[...] = h
        if has_gates:
            z = _dot(h, wg_ref[...])
            og_ref[...] = 1.0 / (1.0 + jnp.exp(-z))

    def make_branch(lo, hi, kinds):
        @pl.when((j >= lo) & (j < hi))
        def _():
            for r in range(h_scr.shape[0] // PROJ_SUB):
                rows = slice(r * PROJ_SUB, (r + 1) * PROJ_SUB)
                acc = _dot(h_scr[rows, :], w_ref[...])
                for c, kind in enumerate(kinds):
                    cols = slice(c * LANES, (c + 1) * LANES)
                    o_ref[rows, cols] = _segment_epilogue(acc[:, cols], kind, tabs_ref, rows).astype(o_ref.dtype)

    for lo, hi, kinds in tile_modes:
        make_branch(lo, hi, kinds)


def _norm_proj(x2d, g, w, tabs, tile_modes, seq, tn, wg=None):
    m, d = x2d.shape
    tm = PROJ_TM
    n = tile_modes[-1][1] * tn
    s_tiles = seq // tm
    has_gates = wg is not None
    in_specs = [
        pl.BlockSpec((tm, d), lambda i, j: (i, 0)),
        pl.BlockSpec((1, d), lambda i, j: (0, 0)),
        pl.BlockSpec((d, tn), lambda i, j: (0, j)),
        pl.BlockSpec((tabs.shape[0], tm, LANES), lambda i, j: (0, i % s_tiles, 0)),
    ]
    args = [x2d, g.reshape(1, d), w, tabs]
    out_shape = [jax.ShapeDtypeStruct((m, n), BF16)]
    out_specs = [pl.BlockSpec((tm, tn), lambda i, j: (i, j))]
    if has_gates:
        ng = wg.shape[1]
        in_specs.append(pl.BlockSpec((d, ng), lambda i, j: (0, 0)))
        args.append(wg)
        out_shape.append(jax.ShapeDtypeStruct((m, ng), F32))
        out_specs.append(pl.BlockSpec((tm, ng), lambda i, j: (i, 0)))
    return pl.pallas_call(
        functools.partial(_proj_kernel, tile_modes=tile_modes, has_gates=has_gates),
        grid=(m // tm, n // tn),
        in_specs=in_specs,
        out_specs=out_specs,
        out_shape=out_shape,
        scratch_shapes=[pltpu.VMEM((tm, d), BF16)],
        compiler_params=pltpu.CompilerParams(
            dimension_semantics=("parallel", "arbitrary"), vmem_limit_bytes=VMEM_LIMIT),
        name="norm_proj",
    )(*args)


def _oproj_kernel(*refs, n_terms):
    y_refs, w_refs = refs[:n_terms], refs[n_terms:2 * n_terms]
    x_ref, o_ref = refs[2 * n_terms:]
    acc = x_ref[...]
    for y_ref, w_ref in zip(y_refs, w_refs):
        acc = acc + _dot(y_ref[...], w_ref[...])
    o_ref[...] = acc


def _out_proj(ys, ws, x2d):
    m, n = x2d.shape
    tm, tn = PROJ_TM, OPROJ_TN
    y_specs = [pl.BlockSpec((tm, y.shape[1]), lambda i, j: (i, 0)) for y in ys]
    w_specs = [pl.BlockSpec((w.shape[0], tn), lambda i, j: (0, j)) for w in ws]
    return pl.pallas_call(
        functools.partial(_oproj_kernel, n_terms=len(ys)),
        grid=(m // tm, n // tn),
        in_specs=y_specs + w_specs + [pl.BlockSpec((tm, tn), lambda i, j: (i, j))],
        out_specs=pl.BlockSpec((tm, tn), lambda i, j: (i, j)),
        out_shape=jax.ShapeDtypeStruct((m, n), F32),
        compiler_params=pltpu.CompilerParams(
            dimension_semantics=("parallel", "arbitrary"), vmem_limit_bytes=VMEM_LIMIT),
        name="out_proj",
    )(*ys, *ws, x2d)


def _ret_kernel(q_ref, k_ref, v_ref, gate_ref, intra_ref, qdec_ref, kdec_ref, cdec_ref, o_ref, *, n_chunks):
    c = RET_CHUNK
    intra = intra_ref[0]
    qdec = qdec_ref[0]
    kdec = kdec_ref[0]
    cdec = cdec_ref[0]
    state = jnp.zeros((HEAD_DIM, HEAD_DIM), F32)
    for n in range(n_chunks):
        rows = slice(n * c, (n + 1) * c)
        q = q_ref[0, rows, :]
        k = k_ref[0, rows, :]
        v = v_ref[0, rows, :]
        scores = _dot_nt(q, k) * intra
        inner = _dot(scores.astype(BF16), v)
        cross = _dot((q.astype(F32) * qdec).astype(BF16), state.astype(BF16))
        kv = _dot_tn((k.astype(F32) * kdec).astype(BF16), v)
        state = state * cdec + kv
        o = _rms(inner + cross) * gate_ref[0, rows, :].astype(F32)
        o_ref[0, rows, :] = o.astype(o_ref.dtype)


def _retention(proj, intra, qdec, kdec, cdec):
    b, s, _ = proj.shape
    h = RET_HEADS
    y_shape = (b, s, RET_WIDTH)
    head = lambda off: pl.BlockSpec((1, s, HEAD_DIM), lambda bi, hi: (bi, 0, off + hi))
    table = lambda rows: pl.BlockSpec((1, rows, HEAD_DIM), lambda bi, hi: (hi, 0, 0))
    return pl.pallas_call(
        functools.partial(_ret_kernel, n_chunks=s // RET_CHUNK),
        grid=(b, h),
        in_specs=[head(0), head(h), head(2 * h), head(3 * h),
                  table(RET_CHUNK), table(RET_CHUNK), table(RET_CHUNK), table(1)],
        out_specs=pl.BlockSpec((1, s, HEAD_DIM), lambda bi, hi: (bi, 0, hi)),
        out_shape=jax.ShapeDtypeStruct(y_shape, BF16),
        compiler_params=pltpu.CompilerParams(
            dimension_semantics=("parallel", "arbitrary"), vmem_limit_bytes=VMEM_LIMIT),
        name="retention",
    )(proj, proj, proj, proj, intra, qdec, kdec, cdec)


def _cmp_kernel(kc_ref, vc_ref, pek_ref, w1k_ref, w2k_ref, pev_ref, w1v_ref, w2v_ref, kg_ref,
                cos_ref, sin_ref, ko_ref, vo_ref, t_scr, *, seq):
    n_rows = seq // CMP_STRIDE

    def compress(src_ref, pe_ref, w1_ref, w2_ref):
        t_scr[0:seq, :] = src_ref[0].astype(F32)
        t_scr[seq:seq + CMP_STRIDE, :] = jnp.zeros((CMP_STRIDE, HEAD_DIM), F32)
        acc = jnp.zeros((n_rows, HEAD_DIM), F32)
        for r in range(CMP_BLOCK):
            rows = t_scr[pl.ds(r, n_rows, stride=CMP_STRIDE), :] + pe_ref[r:r + 1, :]
            acc = acc + _dot(rows.astype(BF16), w1_ref[r])
        return _dot(_silu(acc).astype(BF16), w2_ref[...])

    kc = compress(kc_ref, pek_ref, w1k_ref, w2k_ref)
    kc = _rope(_rms(kc) * kg_ref[...], cos_ref[...], sin_ref[...])
    ko_ref[0, 0] = kc.astype(ko_ref.dtype)
    vo_ref[0, 0] = compress(vc_ref, pev_ref, w1v_ref, w2v_ref).astype(vo_ref.dtype)


def _nsa_compress(proj, kc_off, vc_off, pe_k, w1_k, w2_k, pe_v, w1_v, w2_v, k_g, cos_c, sin_c):
    b, s, _ = proj.shape
    g = NSA_KV_HEADS
    n_rows = s // CMP_STRIDE
    head = lambda off: pl.BlockSpec((1, s, HEAD_DIM), lambda bi, gi: (bi, 0, off + gi))
    full = lambda shape: pl.BlockSpec(shape, lambda bi, gi: (0,) * len(shape))
    out_spec = pl.BlockSpec((1, 1, n_rows, HEAD_DIM), lambda bi, gi: (bi, gi, 0, 0))
    out_sds = jax.ShapeDtypeStruct((b, g, n_rows, HEAD_DIM), BF16)
    return pl.pallas_call(
        functools.partial(_cmp_kernel, seq=s),
        grid=(b, g),
        in_specs=[head(kc_off), head(vc_off),
                  full((CMP_BLOCK, HEAD_DIM)), full((CMP_BLOCK, HEAD_DIM, HEAD_DIM)), full((HEAD_DIM, HEAD_DIM)),
                  full((CMP_BLOCK, HEAD_DIM)), full((CMP_BLOCK, HEAD_DIM, HEAD_DIM)), full((HEAD_DIM, HEAD_DIM)),
                  full((1, HEAD_DIM)), full((n_rows, HEAD_DIM)), full((n_rows, HEAD_DIM))],
        out_specs=[out_spec, out_spec],
        out_shape=[out_sds, out_sds],
        scratch_shapes=[pltpu.VMEM((s + CMP_STRIDE, HEAD_DIM), F32)],
        compiler_params=pltpu.CompilerParams(
            dimension_semantics=("parallel", "arbitrary"), vmem_limit_bytes=VMEM_LIMIT),
        name="nsa_compress",
    )(proj, proj, pe_k, w1_k, w2_k, pe_v, w1_v, w2_v, k_g.reshape(1, HEAD_DIM), cos_c, sin_c)


def _flash_init(m_scr, l_scr, acc_scr):
    m_scr[...] = jnp.full(m_scr.shape, M_INIT, F32)
    l_scr[...] = jnp.zeros(l_scr.shape, F32)
    acc_scr[...] = jnp.zeros(acc_scr.shape, F32)


def _lane_tiles(x, width):
    return x if width == LANES else jnp.concatenate([x] * (width // LANES), axis=1)


def _flash_step(q, k, v, bias, m_scr, l_scr, acc_scr):
    s = _dot_nt(q, k)
    if bias is not None:
        s = s + bias
    tk = s.shape[1]
    m_old = m_scr[...]
    m_new = jnp.maximum(m_old, jnp.max(s, axis=-1, keepdims=True))
    alpha = jnp.exp2(m_old - m_new)
    p = jnp.exp2(s - _lane_tiles(m_new, tk))
    p_cols = p[:, 0:LANES]
    for c in range(1, tk // LANES):
        p_cols = p_cols + p[:, c * LANES:(c + 1) * LANES]
    l_scr[...] = alpha * l_scr[...] + p_cols
    acc_scr[...] = _lane_tiles(alpha, acc_scr.shape[1]) * acc_scr[...] + _dot(p.astype(BF16), v)
    m_scr[...] = m_new


def _flash_finish(l_scr, acc_scr):
    return acc_scr[...] * (1.0 / jnp.sum(l_scr[...], axis=-1, keepdims=True))


def _rep_heads(bias):
    return jnp.concatenate([bias] * NSA_GROUP, axis=0)


def _stack_heads(qblk):
    return jnp.concatenate([qblk[:, r * HEAD_DIM:(r + 1) * HEAD_DIM] for r in range(NSA_GROUP)], axis=0)


def _select_blocks(p, ovl, q0, tq, n_slc):
    psum = p[0:tq]
    for r in range(1, NSA_GROUP):
        psum = psum + p[r * tq:(r + 1) * tq]
    p_hi = psum.astype(BF16)
    rem = psum - p_hi.astype(F32)
    p_mid = rem.astype(BF16)
    p_lo = (rem - p_mid.astype(F32)).astype(BF16)
    imp = _dot_nt(ovl, p_hi) + _dot_nt(ovl, p_mid) + _dot_nt(ovl, p_lo)
    jb = lax.broadcasted_iota(jnp.int32, (n_slc, tq), 0)
    blk_t = jnp.right_shift(q0 + lax.broadcasted_iota(jnp.int32, (n_slc, tq), 1), int(math.log2(SLC_BLOCK)))
    back = blk_t - jb
    forced = (jb == 0) | ((back >= 0) & (back < N_LOCAL_BLOCKS))
    score = jnp.where(forced, 1e9, jnp.where(back >= 0, imp, -1e9))
    rank = jnp.zeros((n_slc, tq), F32)
    for mp in range(n_slc):
        row = score[mp:mp + 1, :]
        ahead = (row > score) | ((row == score) & (jb > mp))
        rank = rank + jnp.where(ahead, 1.0, 0.0)
    sel_t = jnp.where(rank < float(min(SLC_TOPK, n_slc)), 1.0, 0.0)
    sel_t = jnp.concatenate([sel_t, jnp.zeros((LANES - n_slc, tq), F32)], axis=0).astype(BF16)
    ri = lax.broadcasted_iota(jnp.int32, (tq, tq), 0)
    ci = lax.broadcasted_iota(jnp.int32, (tq, tq), 1)
    eye = jnp.where(ri == ci, 1.0, 0.0).astype(BF16)
    return _dot_nt(eye, sel_t)


def _nsa_combine(o_cmp, o_slc, o_win, gates_ref, ngate_ref, o_ref, tq):
    gates = gates_ref[...]
    for r in range(NSA_GROUP):
        rows = slice(r * tq, (r + 1) * tq)
        g_cmp = gates[:, r:r + 1]
        g_slc = gates[:, NSA_GROUP + r:NSA_GROUP + r + 1]
        g_win = gates[:, 2 * NSA_GROUP + r:2 * NSA_GROUP + r + 1]
        y = g_cmp * o_cmp[rows] + g_slc * o_slc[rows] + g_win * o_win[rows]
        cols = slice(r * HEAD_DIM, (r + 1) * HEAD_DIM)
        o_ref[0, :, cols] = (y * ngate_ref[0, :, cols].astype(F32)).astype(o_ref.dtype)


def _nsa_general(q4, q0, qi, gates_ref, ngate_ref, kcmp_ref, vcmp_ref, ks_ref, vs_ref, kw_ref, vw_ref,
                 ovl_ref, eaug_ref, o_ref, bias_scr, m_scr, l_scr, acc_scr, *, seq):
    tq, tk = NSA_TQ, NSA_TK
    n_slc = seq // SLC_BLOCK

    tpos = q0 + lax.broadcasted_iota(jnp.int32, (tq, LANES), 0)
    cidx = lax.broadcasted_iota(jnp.int32, (tq, LANES), 1)
    cbias = jnp.where(cidx * CMP_STRIDE + (CMP_BLOCK - 1) <= tpos, 0.0, MASKED)
    s = _dot_nt(q4, kcmp_ref[0, 0]) + _rep_heads(cbias)
    m = jnp.maximum(jnp.max(s, axis=-1, keepdims=True), M_INIT)
    e = jnp.exp2(s - m)
    l = jnp.sum(e, axis=-1, keepdims=True)
    p = e * (1.0 / jnp.maximum(l, 1e-30))
    o_cmp = _dot(p.astype(BF16), vcmp_ref[0, 0])

    sel = _select_blocks(p, ovl_ref[...], q0, tq, n_slc).astype(BF16)
    sel_keys = _dot_nt(sel, eaug_ref[...])
    for kt in range(seq // tk):
        bias_scr[kt] = jnp.where(sel_keys[:, kt * tk:(kt + 1) * tk] > 0.5, 0.0, MASKED)

    _flash_init(m_scr, l_scr, acc_scr)

    def slc_body(kt, carry):
        k0 = pl.multiple_of(kt * tk, tk)
        _flash_step(q4, ks_ref[0, pl.ds(k0, tk), :], vs_ref[0, pl.ds(k0, tk), :],
                    _rep_heads(bias_scr[kt]), m_scr, l_scr, acc_scr)
        return carry

    n_full = lax.div(q0, tk)
    lax.fori_loop(0, n_full, slc_body, 0)
    kd = pl.multiple_of(n_full * tk, tk)
    qpos = q0 + lax.broadcasted_iota(jnp.int32, (tq, tk), 0)
    kpos = kd + lax.broadcasted_iota(jnp.int32, (tq, tk), 1)
    causal = jnp.where(kpos <= qpos, 0.0, MASKED)
    _flash_step(q4, ks_ref[0, pl.ds(kd, tk), :], vs_ref[0, pl.ds(kd, tk), :],
                _rep_heads(bias_scr[n_full] + causal), m_scr, l_scr, acc_scr)
    o_slc = _flash_finish(l_scr, acc_scr)

    wk = WINDOW + tq
    ws = pl.multiple_of(jnp.maximum(q0 - WINDOW, 0), tq)
    d = (q0 - ws) + lax.broadcasted_iota(jnp.int32, (tq, wk), 0) - lax.broadcasted_iota(jnp.int32, (tq, wk), 1)
    wbias = jnp.where((d >= 0) & (d < WINDOW), 0.0, MASKED)
    s = _dot_nt(q4, kw_ref[0, pl.ds(ws, wk), :]) + _rep_heads(wbias)
    e = jnp.exp2(s - jnp.max(s, axis=-1, keepdims=True))
    o_win = _dot(e.astype(BF16), vw_ref[0, pl.ds(ws, wk), :]) * (1.0 / jnp.sum(e, axis=-1, keepdims=True))

    _nsa_combine(o_cmp, o_slc, o_win, gates_ref, ngate_ref, o_ref, tq)


def _nsa_bounded(q4, q0, qi, bound, gates_ref, ngate_ref, kcmp_ref, vcmp_ref, ks_ref, vs_ref, kw_ref, vw_ref,
                 ovl_ref, eaug_ref, cmask_ref, wmask_ref, tri_ref, o_ref,
                 ksa_scr, kwa_scr, kca_scr, l_scr, acc_scr, *, seq):
    tq, tk = NSA_TQ, NSA_TK
    n_slc = seq // SLC_BLOCK
    n_cmp_rows = kca_scr.shape[0]

    @pl.when(qi == 0)
    def _():
        one_col = eaug_ref[...]
        lane = lax.broadcasted_iota(jnp.int32, one_col.shape, 1)
        ksa_scr[:, :HEAD_DIM] = ks_ref[0]
        ksa_scr[:, HEAD_DIM:] = one_col
        one_col = jnp.where(lane == AUG_ONE_LANE, one_col, jnp.zeros_like(one_col))
        kwa_scr[:, :HEAD_DIM] = kw_ref[0]
        kwa_scr[:, HEAD_DIM:] = one_col
        kca_scr[:, :HEAD_DIM] = kcmp_ref[0, 0]
        kca_scr[:, HEAD_DIM:] = one_col[:n_cmp_rows]

    lane = lax.broadcasted_iota(jnp.int32, (tq, LANES), 1)
    shift_cols = jnp.where(lane == AUG_ONE_LANE, -bound, 0.0)
    qa = jnp.concatenate([q4, _rep_heads(shift_cols.astype(BF16))], axis=1)

    e = jnp.exp2(_dot_nt(qa, kca_scr[...])) * _rep_heads(cmask_ref[...])
    l = jnp.sum(e, axis=-1, keepdims=True)
    p = e * jnp.where(l > 0.0, 1.0 / l, 0.0)
    o_cmp = _dot(p.astype(BF16), vcmp_ref[0, 0])

    sel = _select_blocks(p, ovl_ref[...], q0, tq, n_slc)
    sel_cols = jnp.where(lane < n_slc, jnp.where(sel > 0.5, 0.0, MASKED), shift_cols)
    qs = jnp.concatenate([q4, _rep_heads(sel_cols.astype(BF16))], axis=1)

    l_scr[...] = jnp.zeros(l_scr.shape, F32)
    acc_scr[...] = jnp.zeros(acc_scr.shape, F32)

    def accumulate(k0, width, mask):
        p_t = jnp.exp2(_dot_nt(qs, ksa_scr[pl.ds(k0, width), :]))
        if mask is not None:
            p_t = p_t * mask
        cols = p_t[:, 0:LANES]
        for c in range(1, width // LANES):
            cols = cols + p_t[:, c * LANES:(c + 1) * LANES]
        l_scr[...] += cols
        acc_scr[...] += _dot(p_t.astype(BF16), vs_ref[0, pl.ds(k0, width), :])

    big = 2 * tk

    def slc_body(kt, carry):
        accumulate(pl.multiple_of(kt * big, big), big, None)
        return carry

    n_full = lax.div(q0, tk)
    n_big = lax.div(n_full, 2)
    lax.fori_loop(0, n_big, slc_body, 0)

    @pl.when(n_full > 2 * n_big)
    def _():
        accumulate(pl.multiple_of(n_big * big, big), tk, None)

    accumulate(pl.multiple_of(n_full * tk, tk), tk, _rep_heads(tri_ref[...]))
    o_slc = _flash_finish(l_scr, acc_scr)

    wk = WINDOW + tq
    ws = pl.multiple_of(jnp.maximum(q0 - WINDOW, 0), tq)
    e = jnp.exp2(_dot_nt(qa, kwa_scr[pl.ds(ws, wk), :])) * _rep_heads(wmask_ref[0])
    o_win = _dot(e.astype(BF16), vw_ref[0, pl.ds(ws, wk), :]) * (1.0 / jnp.sum(e, axis=-1, keepdims=True))

    _nsa_combine(o_cmp, o_slc, o_win, gates_ref, ngate_ref, o_ref, tq)


def _nsa_kernel(bound_ref, q_ref, ngate_ref, gates_ref, kcmp_ref, vcmp_ref, ks_ref, vs_ref, kw_ref, vw_ref,
                ovl_ref, eaug_ref, cmask_ref, wmask_ref, tri_ref, o_ref,
                bias_scr, m_scr, l_scr, acc_scr, ksa_scr, kwa_scr, kca_scr, *, seq):
    qi = pl.program_id(2)
    q0 = qi * NSA_TQ
    q4 = _stack_heads(q_ref[0])
    bound = bound_ref[0]

    @pl.when(bound <= MAX_SCORE_BOUND)
    def _():
        _nsa_bounded(q4, q0, qi, bound, gates_ref, ngate_ref, kcmp_ref, vcmp_ref, ks_ref, vs_ref, kw_ref, vw_ref,
                     ovl_ref, eaug_ref, cmask_ref, wmask_ref, tri_ref, o_ref,
                     ksa_scr, kwa_scr, kca_scr, l_scr, acc_scr, seq=seq)

    @pl.when(bound > MAX_SCORE_BOUND)
    def _():
        _nsa_general(q4, q0, qi, gates_ref, ngate_ref, kcmp_ref, vcmp_ref, ks_ref, vs_ref, kw_ref, vw_ref,
                     ovl_ref, eaug_ref, o_ref, bias_scr, m_scr, l_scr, acc_scr, seq=seq)


def _nsa_attention(proj, gates, kcmp, vcmp, bound, tables, offs):
    b, s, _ = proj.shape
    g = NSA_KV_HEADS
    tq = NSA_TQ
    nq = s // tq
    gw = NSA_GROUP * HEAD_DIM
    n_cmp_rows = s // CMP_STRIDE
    ovl_t, eaug, cmask, wmask, tri = tables
    q_spec = lambda off: pl.BlockSpec((1, tq, gw), lambda bi, gi, qi: (bi, qi, off + gi))
    kv_spec = lambda off: pl.BlockSpec((1, s, HEAD_DIM), lambda bi, gi, qi: (bi, 0, off + gi))
    cmp_spec = pl.BlockSpec((1, 1, n_cmp_rows, HEAD_DIM), lambda bi, gi, qi: (bi, gi, 0, 0))
    full = lambda shape: pl.BlockSpec(shape, lambda bi, gi, qi: (0,) * len(shape))
    n_wpat = wmask.shape[0]
    rows4 = NSA_GROUP * tq
    return pl.pallas_call(
        functools.partial(_nsa_kernel, seq=s),
        grid=(b, g, nq),
        in_specs=[pl.BlockSpec(memory_space=pltpu.SMEM),
                  q_spec(offs["nq"] // gw), q_spec(offs["ngate"] // gw),
                  pl.BlockSpec((tq, LANES), lambda bi, gi, qi: (bi * nq + qi, gi)),
                  cmp_spec, cmp_spec,
                  kv_spec(offs["ks"] // HEAD_DIM), kv_spec(offs["vs"] // HEAD_DIM),
                  kv_spec(offs["kw"] // HEAD_DIM), kv_spec(offs["vw"] // HEAD_DIM),
                  full(ovl_t.shape), full(eaug.shape),
                  pl.BlockSpec((tq, LANES), lambda bi, gi, qi: (qi, 0)),
                  pl.BlockSpec((1,) + wmask.shape[1:], lambda bi, gi, qi: (jnp.minimum(qi, n_wpat - 1), 0, 0)),
                  full(tri.shape)],
        out_specs=pl.BlockSpec((1, tq, gw), lambda bi, gi, qi: (bi, qi, gi)),
        out_shape=jax.ShapeDtypeStruct((b, s, NSA_WIDTH), BF16),
        scratch_shapes=[pltpu.VMEM((s // NSA_TK, tq, NSA_TK), F32),
                        pltpu.VMEM((rows4, LANES), F32), pltpu.VMEM((rows4, LANES), F32),
                        pltpu.VMEM((rows4, HEAD_DIM), F32),
                        pltpu.VMEM((s, 2 * HEAD_DIM), BF16), pltpu.VMEM((s, 2 * HEAD_DIM), BF16),
                        pltpu.VMEM((n_cmp_rows, 2 * HEAD_DIM), BF16)],
        compiler_params=pltpu.CompilerParams(
            dimension_semantics=("parallel", "parallel", "arbitrary"), vmem_limit_bytes=VMEM_LIMIT),
        name="nsa_attention",
    )(bound, proj, proj, gates, kcmp, vcmp, proj, proj, proj, proj, ovl_t, eaug, cmask, wmask, tri)


def _diff_kernel(q_ref, gate_ref, k_ref, v_ref, lam_ref, o_ref, m1, l1, a1, m2, l2, a2, *, lambda_init):
    tq, tk = DIFF_TQ, DIFF_TK
    qi = pl.program_id(2)
    q = q_ref[0]
    q1 = q[:, :HEAD_DIM]
    q2 = q[:, HEAD_DIM:]
    _flash_init(m1, l1, a1)
    _flash_init(m2, l2, a2)

    def step(rows, k0, width, bias):
        k = k_ref[0, pl.ds(k0, width), :]
        v = v_ref[0, pl.ds(k0, width), :]
        _flash_step(q1[rows], k[:, :HEAD_DIM], v, bias, m1.at[rows], l1.at[rows], a1.at[rows])
        _flash_step(q2[rows], k[:, HEAD_DIM:], v, bias, m2.at[rows], l2.at[rows], a2.at[rows])

    def body(kt, carry):
        step(slice(0, tq), pl.multiple_of(kt * tk, tk), tk, None)
        return carry

    lax.fori_loop(0, qi * (tq // tk), body, 0)
    dd = DIFF_DIAG
    q0 = qi * tq
    for c in range(tq // dd):
        n_rows = tq - c * dd
        ri = lax.broadcasted_iota(jnp.int32, (n_rows, dd), 0)
        ci = lax.broadcasted_iota(jnp.int32, (n_rows, dd), 1)
        step(slice(c * dd, tq), pl.multiple_of(q0 + c * dd, dd), dd, jnp.where(ri >= ci, 0.0, MASKED))

    lp = lam_ref[...]
    lam = (jnp.exp(jnp.sum(lp[0:1] * lp[1:2], axis=-1, keepdims=True))
           - jnp.exp(jnp.sum(lp[2:3] * lp[3:4], axis=-1, keepdims=True)) + lambda_init)
    o = _flash_finish(l1, a1) - lam * _flash_finish(l2, a2)
    o = _rms(o) * (1.0 - lambda_init)
    o_ref[0] = (o * gate_ref[0].astype(F32)).astype(o_ref.dtype)


def _diff_attention(proj, lam_params, lambda_init):
    b, s, _ = proj.shape
    h = DIFF_HEADS
    tq = DIFF_TQ
    w = DIFF_V_DIM
    q_spec = lambda off: pl.BlockSpec((1, tq, w), lambda bi, hi, qi: (bi, qi, off + hi))
    kv_spec = lambda off: pl.BlockSpec((1, s, w), lambda bi, hi, qi: (bi, 0, off + hi))
    stat = pltpu.VMEM((tq, LANES), F32)
    acc = pltpu.VMEM((tq, w), F32)
    return pl.pallas_call(
        functools.partial(_diff_kernel, lambda_init=lambda_init),
        grid=(b, h, s // tq),
        in_specs=[q_spec(0), q_spec(3 * h), kv_spec(h), kv_spec(2 * h),
                  pl.BlockSpec(lam_params.shape, lambda bi, hi, qi: (0, 0))],
        out_specs=pl.BlockSpec((1, tq, w), lambda bi, hi, qi: (bi, qi, hi)),
        out_shape=jax.ShapeDtypeStruct((b, s, DIFF_WIDTH), BF16),
        scratch_shapes=[stat, stat, acc, stat, stat, acc],
        compiler_params=pltpu.CompilerParams(
            dimension_semantics=("parallel", "parallel", "arbitrary"), vmem_limit_bytes=VMEM_LIMIT),
        name="diff_attention",
    )(proj, proj, proj, proj, lam_params)


def _rope_tables(pos):
    inv = 1.0 / (ROPE_THETA ** (jnp.arange(0, HEAD_DIM, 2, dtype=F32) / HEAD_DIM))
    ang = pos.astype(F32)[:, None] * inv[None, :]
    cos, sin = jnp.cos(ang), jnp.sin(ang)
    return jnp.concatenate([cos, cos], axis=-1), jnp.concatenate([-sin, sin], axis=-1)


def _retention_tables():
    h, c = RET_HEADS, RET_CHUNK
    log_g = jnp.log1p(-jnp.exp2(-5.0 - jnp.arange(h, dtype=F32)))
    j = jnp.arange(c, dtype=F32)
    diff = j[:, None] - j[None, :]
    intra = jnp.where(diff >= 0, jnp.exp(log_g[:, None, None] * jnp.maximum(diff, 0.0)), 0.0)
    q_dec = jnp.exp(log_g[:, None] * (j + 1.0))
    k_dec = jnp.exp(log_g[:, None] * (c - 1.0 - j))
    chunk_dec = jnp.exp(log_g * c)
    wide = lambda t: jnp.broadcast_to(t[:, :, None], (h, t.shape[1], HEAD_DIM))
    return intra, wide(q_dec), wide(k_dec), wide(chunk_dec[:, None])


def _selection_tables(seq):
    tq, tk = NSA_TQ, NSA_TK
    n_cmp_rows = seq // CMP_STRIDE
    n_slc = seq // SLC_BLOCK
    assert tq == tk and n_slc <= AUG_ONE_LANE < LANES
    c_start = np.arange(n_cmp_rows) * CMP_STRIDE
    s_start = np.arange(n_slc) * SLC_BLOCK
    overlap_t = ((c_start[None, :] <= s_start[:, None] + SLC_BLOCK - 1)
                 & (c_start[None, :] + CMP_BLOCK - 1 >= s_start[:, None]))
    lane = np.arange(LANES)[None, :]
    key = np.arange(seq)[:, None]
    eaug = ((key // SLC_BLOCK) == lane) | (lane == AUG_ONE_LANE)
    cmask = lane * CMP_STRIDE + CMP_BLOCK - 1 <= key
    r = np.arange(tq)[:, None]
    c = np.arange(WINDOW + tq)[None, :]
    wmask = []
    for pat in range(WINDOW // tq + 1):
        d = min(pat * tq, WINDOW) + r - c
        wmask.append((d >= 0) & (d < WINDOW))
    tri = np.arange(tq)[:, None] >= np.arange(tk)[None, :]
    return (jnp.asarray(overlap_t, BF16), jnp.asarray(eaug, BF16), jnp.asarray(cmask, F32),
            jnp.asarray(np.stack(wmask), F32), jnp.asarray(tri, F32))


def _l0_tile_modes():
    t = PROJ_TN
    per = t // LANES
    seg = lambda op, table=0: (op, table)
    off = {}
    modes = []
    col = 0

    def add(name, width, kinds):
        nonlocal col
        off[name] = col
        n_tiles = width // t
        modes.append((col // t, col // t + n_tiles, kinds))
        col += width

    add("rq", RET_WIDTH, [seg("rope", L0_TAB_ROPE)] * per)
    add("rk", RET_WIDTH, [seg("rope", L0_TAB_ROPE_SCALED)] * per)
    add("rv", RET_WIDTH, [seg("plain")] * per)
    add("rgate", RET_WIDTH, [seg("silu")] * per)
    add("nq", NSA_WIDTH, [seg("nrope", L0_TAB_Q)] * per)
    kvh = NSA_KV_HEADS
    add("kc", 2 * NSA_KV_WIDTH, [seg("plain")] * per)
    off["vc"] = off["kc"] + NSA_KV_WIDTH
    add("ks", 2 * NSA_KV_WIDTH, [seg("nrope", L0_TAB_K)] * kvh + [seg("plain")] * kvh)
    off["vs"] = off["ks"] + NSA_KV_WIDTH
    add("kw", 2 * NSA_KV_WIDTH, [seg("nrope", L0_TAB_K)] * kvh + [seg("plain")] * kvh)
    off["vw"] = off["kw"] + NSA_KV_WIDTH
    add("ngate", NSA_WIDTH, [seg("silu")] * per)
    assert col == AB_MAIN_COLS
    return modes, off


def _l1_tile_modes():
    per = L1_PROJ_TN // LANES
    n = DIFF_WIDTH // L1_PROJ_TN
    return [(0, n, [("nrope", L1_TAB_Q)] * per), (n, 2 * n, [("nrope", L1_TAB_K)] * per),
            (2 * n, 3 * n, [("plain", 0)] * per), (3 * n, 4 * n, [("silu", 0)] * per)]


def _rope_pair(cos, sin_signed, gain=None, scale=1.0):
    if gain is None:
        return [cos * scale, sin_signed * scale]
    return [cos * (gain * scale)[None, :], sin_signed * (jnp.roll(gain, HALF) * scale)[None, :]]


def kernel(x, l0_norm_g, l0_w_in, l0_w_out, l0_nsa_q_norm_g, l0_nsa_k_norm_g, l0_cmp_pe_k, l0_cmp_w1_k, l0_cmp_w2_k, l0_cmp_pe_v, l0_cmp_w1_v, l0_cmp_w2_v, l1_norm_g, l1_w_in, l1_w_out, l1_q_norm_g, l1_k_norm_g, l1_lambda_q1, l1_lambda_k1, l1_lambda_q2, l1_lambda_k2):
    b, s, d = x.shape
    m = b * s
    x2d = x.reshape(m, d)
    cos, sin = _rope_tables(jnp.arange(s))
    cos_c, sin_c = _rope_tables(jnp.arange(s // CMP_STRIDE) * CMP_STRIDE + CMP_BLOCK - 1)
    intra, qdec, kdec, cdec = _retention_tables()
    nsa_tables = _selection_tables(s)

    modes0, off = _l0_tile_modes()
    w0 = l0_w_in.astype(BF16)
    wg = l0_w_in[:, AB_MAIN_COLS:].reshape(d, 3, NSA_KV_HEADS, NSA_GROUP).transpose(0, 2, 1, 3)
    wg = wg.reshape(d, NSA_KV_HEADS, 3 * NSA_GROUP)
    wg = jnp.pad(wg, ((0, 0), (0, 0), (0, LANES - 3 * NSA_GROUP))).reshape(d, NSA_KV_HEADS * LANES).astype(BF16)
    tabs0 = jnp.stack(_rope_pair(cos, sin) + _rope_pair(cos, sin, scale=QK_SCALE)
                      + _rope_pair(cos, sin, l0_nsa_q_norm_g, Q_SCALE) + _rope_pair(cos, sin, l0_nsa_k_norm_g))
    proj0, gates = _norm_proj(x2d, l0_norm_g, w0, tabs0, modes0, s, PROJ_TN, wg=wg)
    proj0 = proj0.reshape(b, s, AB_MAIN_COLS)
    y_ret = _retention(proj0, intra, qdec, kdec, cdec)
    w1k = l0_cmp_w1_k.astype(BF16).reshape(CMP_BLOCK, HEAD_DIM, HEAD_DIM)
    w1v = l0_cmp_w1_v.astype(BF16).reshape(CMP_BLOCK, HEAD_DIM, HEAD_DIM)
    kcmp, vcmp = _nsa_compress(proj0, off["kc"] // HEAD_DIM, off["vc"] // HEAD_DIM,
                               l0_cmp_pe_k, w1k, l0_cmp_w2_k.astype(BF16),
                               l0_cmp_pe_v, w1v, l0_cmp_w2_v.astype(BF16),
                               l0_nsa_k_norm_g, cos_c, sin_c)
    bound = (BOUND_MARGIN * HEAD_DIM * Q_SCALE * jnp.max(jnp.abs(l0_nsa_q_norm_g))
             * jnp.max(jnp.abs(l0_nsa_k_norm_g))).reshape(1).astype(F32)
    y_nsa = _nsa_attention(proj0, gates, kcmp, vcmp, bound, nsa_tables, off)
    w_out0 = l0_w_out.astype(BF16)
    x1 = _out_proj([y_ret.reshape(m, RET_WIDTH), y_nsa.reshape(m, NSA_WIDTH)],
                   [w_out0[:RET_WIDTH], w_out0[RET_WIDTH:]], x2d)

    lambda_init = 0.8 - 0.6 * math.exp(-0.3 * 1)
    tabs1 = jnp.stack(_rope_pair(cos, sin, l1_q_norm_g, Q_SCALE) + _rope_pair(cos, sin, l1_k_norm_g))
    proj1 = _norm_proj(x1, l1_norm_g, l1_w_in.astype(BF16), tabs1, _l1_tile_modes(), s, L1_PROJ_TN)[0]
    lam_params = jnp.stack([l1_lambda_q1, l1_lambda_k1, l1_lambda_q2, l1_lambda_k2]).astype(F32)
    y1 = _diff_attention(proj1.reshape(b, s, C_IN_COLS), lam_params, lambda_init)
    out = _out_proj([y1.reshape(m, DIFF_WIDTH)], [l1_w_out.astype(BF16)], x1)
    return out.reshape(b, s, d)
```

```python
import functools
import math

import numpy as np
import jax
import jax.numpy as jnp
from jax import lax
from jax.experimental import pallas as pl
from jax.experimental.pallas import tpu as pltpu

F32 = jnp.float32
BF16 = jnp.bfloat16

D_MODEL = 2048
HEAD_DIM = 128
HALF = HEAD_DIM // 2
ROPE_THETA = 10000.0
EPS = 1e-6
RET_HEADS = 8
RET_CHUNK = 128
NSA_HEADS = 8
NSA_KV_HEADS = 2
NSA_GROUP = NSA_HEADS // NSA_KV_HEADS
CMP_BLOCK = 32
CMP_STRIDE = 16
SLC_BLOCK = 64
SLC_TOPK = 16
N_LOCAL_BLOCKS = 2
WINDOW = 512
DIFF_HEADS = 8
DIFF_V_DIM = 2 * HEAD_DIM
QK_SCALE = HEAD_DIM ** -0.5
LOG2E = math.log2(math.e)
Q_SCALE = QK_SCALE * LOG2E

RET_WIDTH = RET_HEADS * HEAD_DIM
NSA_WIDTH = NSA_HEADS * HEAD_DIM
NSA_KV_WIDTH = NSA_KV_HEADS * HEAD_DIM
AB_MAIN_COLS = 4 * RET_WIDTH + 2 * NSA_WIDTH + 6 * NSA_KV_WIDTH
N_GATE_COLS = 3 * NSA_HEADS
DIFF_WIDTH = DIFF_HEADS * DIFF_V_DIM
C_IN_COLS = 4 * DIFF_WIDTH

LANES = 128
MASKED = -1e30
M_INIT = -1e29
MAX_SCORE_BOUND = 60.0
BOUND_MARGIN = 1.05
AUG_ONE_LANE = 32
VMEM_LIMIT = 56 * 1024 * 1024

PROJ_TM = 1024
PROJ_TN = 768
L1_PROJ_TN = 2048
OPROJ_TN = 1024
PROJ_SUB = 256
L0_TAB_ROPE, L0_TAB_ROPE_SCALED, L0_TAB_Q, L0_TAB_K = range(4)
L1_TAB_Q, L1_TAB_K = range(2)
NSA_TQ = 256
NSA_TK = 256
DIFF_TQ = 512
DIFF_TK = 512
DIFF_DIAG = 256


def _dot(a, b):
    return jnp.dot(a, b, preferred_element_type=F32)


def _dot_nt(a, b):
    return lax.dot_general(a, b, (((1,), (1,)), ((), ())), preferred_element_type=F32)


def _dot_tn(a, b):
    return lax.dot_general(a, b, (((0,), (0,)), ((), ())), preferred_element_type=F32)


def _silu(x):
    return x / (1.0 + jnp.exp(-x))


def _rms(x):
    return x * lax.rsqrt(jnp.mean(x * x, axis=-1, keepdims=True) + EPS)


def _rope(x, cos, sin_signed):
    return x * cos + pltpu.roll(x, HALF, axis=1) * sin_signed


def _segment_epilogue(seg, kind, tabs_ref, rows):
    op, t = kind
    if op == "plain":
        return seg
    if op == "silu":
        return _silu(seg)
    if op == "nrope":
        mean_mat = jnp.full((LANES, LANES), 1.0 / HEAD_DIM, BF16)
        seg = seg * lax.rsqrt(_dot((seg * seg).astype(BF16), mean_mat) + EPS)
    return seg * tabs_ref[2 * t, rows, :] + pltpu.roll(seg, HALF, axis=1) * tabs_ref[2 * t + 1, rows, :]


def _proj_kernel(*refs, tile_modes, has_gates):
    if has_gates:
        x_ref, g_ref, w_ref, tabs_ref, wg_ref, o_ref, og_ref, h_scr = refs
    else:
        x_ref, g_ref, w_ref, tabs_ref, o_ref, h_scr = refs
    j = pl.program_id(1)

    @pl.when(j == 0)
    def _():
        x = x_ref[...]
        h = (_rms(x) * g_ref[...]).astype(BF16)
        h_scr[...] = h
        if has_gates:
            z = _dot(h, wg_ref[...])
            og_ref[...] = 1.0 / (1.0 + jnp.exp(-z))

    def make_branch(lo, hi, kinds):
        @pl.when((j >= lo) & (j < hi))
        def _():
            for r in range(h_scr.shape[0] // PROJ_SUB):
                rows = slice(r * PROJ_SUB, (r + 1) * PROJ_SUB)
                acc = _dot(h_scr[rows, :], w_ref[...])
                for c, kind in enumerate(kinds):
                    cols = slice(c * LANES, (c + 1) * LANES)
                    o_ref[rows, cols] = _segment_epilogue(acc[:, cols], kind, tabs_ref, rows).astype(o_ref.dtype)

    for lo, hi, kinds in tile_modes:
        make_branch(lo, hi, kinds)


def _norm_proj(x2d, g, w, tabs, tile_modes, seq, tn, wg=None):
    m, d = x2d.shape
    tm = PROJ_TM
    n = tile_modes[-1][1] * tn
    s_tiles = seq // tm
    has_gates = wg is not None
    in_specs = [
        pl.BlockSpec((tm, d), lambda i, j: (i, 0)),
        pl.BlockSpec((1, d), lambda i, j: (0, 0)),
        pl.BlockSpec((d, tn), lambda i, j: (0, j)),
        pl.BlockSpec((tabs.shape[0], tm, LANES), lambda i, j: (0, i % s_tiles, 0)),
    ]
    args = [x2d, g.reshape(1, d), w, tabs]
    out_shape = [jax.ShapeDtypeStruct((m, n), BF16)]
    out_specs = [pl.BlockSpec((tm, tn), lambda i, j: (i, j))]
    if has_gates:
        ng = wg.shape[1]
        in_specs.append(pl.BlockSpec((d, ng), lambda i, j: (0, 0)))
        args.append(wg)
        out_shape.append(jax.ShapeDtypeStruct((m, ng), F32))
        out_specs.append(pl.BlockSpec((tm, ng), lambda i, j: (i, 0)))
    return pl.pallas_call(
        functools.partial(_proj_kernel, tile_modes=tile_modes, has_gates=has_gates),
        grid=(m // tm, n // tn),
        in_specs=in_specs,
        out_specs=out_specs,
        out_shape=out_shape,
        scratch_shapes=[pltpu.VMEM((tm, d), BF16)],
        compiler_params=pltpu.CompilerParams(
            dimension_semantics=("parallel", "arbitrary"), vmem_limit_bytes=VMEM_LIMIT),
        name="norm_proj",
    )(*args)


def _oproj_kernel(*refs, n_terms):
    y_refs, w_refs = refs[:n_terms], refs[n_terms:2 * n_terms]
    x_ref, o_ref = refs[2 * n_terms:]
    acc = x_ref[...]
    for y_ref, w_ref in zip(y_refs, w_refs):
        acc = acc + _dot(y_ref[...], w_ref[...])
    o_ref[...] = acc


def _out_proj(ys, ws, x2d):
    m, n = x2d.shape
    tm, tn = PROJ_TM, OPROJ_TN
    y_specs = [pl.BlockSpec((tm, y.shape[1]), lambda i, j: (i, 0)) for y in ys]
    w_specs = [pl.BlockSpec((w.shape[0], tn), lambda i, j: (0, j)) for w in ws]
    return pl.pallas_call(
        functools.partial(_oproj_kernel, n_terms=len(ys)),
        grid=(m // tm, n // tn),
        in_specs=y_specs + w_specs + [pl.BlockSpec((tm, tn), lambda i, j: (i, j))],
        out_specs=pl.BlockSpec((tm, tn), lambda i, j: (i, j)),
        out_shape=jax.ShapeDtypeStruct((m, n), F32),
        compiler_params=pltpu.CompilerParams(
            dimension_semantics=("parallel", "arbitrary"), vmem_limit_bytes=VMEM_LIMIT),
        name="out_proj",
    )(*ys, *ws, x2d)


def _ret_kernel(q_ref, k_ref, v_ref, gate_ref, intra_ref, qdec_ref, kdec_ref, cdec_ref, o_ref, *, n_chunks):
    c = RET_CHUNK
    intra = intra_ref[0]
    qdec = qdec_ref[0]
    kdec = kdec_ref[0]
    cdec = cdec_ref[0]
    state = jnp.zeros((HEAD_DIM, HEAD_DIM), F32)
    for n in range(n_chunks):
        rows = slice(n * c, (n + 1) * c)
        q = q_ref[0, rows, :]
        k = k_ref[0, rows, :]
        v = v_ref[0, rows, :]
        scores = _dot_nt(q, k) * intra
        inner = _dot(scores.astype(BF16), v)
        cross = _dot((q.astype(F32) * qdec).astype(BF16), state.astype(BF16))
        kv = _dot_tn((k.astype(F32) * kdec).astype(BF16), v)
        state = state * cdec + kv
        o = _rms(inner + cross) * gate_ref[0, rows, :].astype(F32)
        o_ref[0, rows, :] = o.astype(o_ref.dtype)


def _retention(proj, intra, qdec, kdec, cdec):
    b, s, _ = proj.shape
    h = RET_HEADS
    y_shape = (b, s, RET_WIDTH)
    head = lambda off: pl.BlockSpec((1, s, HEAD_DIM), lambda bi, hi: (bi, 0, off + hi))
    table = lambda rows: pl.BlockSpec((1, rows, HEAD_DIM), lambda bi, hi: (hi, 0, 0))
    return pl.pallas_call(
        functools.partial(_ret_kernel, n_chunks=s // RET_CHUNK),
        grid=(b, h),
        in_specs=[head(0), head(h), head(2 * h), head(3 * h),
                  table(RET_CHUNK), table(RET_CHUNK), table(RET_CHUNK), table(1)],
        out_specs=pl.BlockSpec((1, s, HEAD_DIM), lambda bi, hi: (bi, 0, hi)),
        out_shape=jax.ShapeDtypeStruct(y_shape, BF16),
        compiler_params=pltpu.CompilerParams(
            dimension_semantics=("parallel", "arbitrary"), vmem_limit_bytes=VMEM_LIMIT),
        name="retention",
    )(proj, proj, proj, proj, intra, qdec, kdec, cdec)


def _cmp_kernel(kc_ref, vc_ref, pek_ref, w1k_ref, w2k_ref, pev_ref, w1v_ref, w2v_ref, kg_ref,
                cos_ref, sin_ref, ko_ref, vo_ref, t_scr, *, seq):
    n_rows = seq // CMP_STRIDE

    def compress(src_ref, pe_ref, w1_ref, w2_ref):
        t_scr[0:seq, :] = src_ref[0].astype(F32)
        t_scr[seq:seq + CMP_STRIDE, :] = jnp.zeros((CMP_STRIDE, HEAD_DIM), F32)
        acc = jnp.zeros((n_rows, HEAD_DIM), F32)
        for r in range(CMP_BLOCK):
            rows = t_scr[pl.ds(r, n_rows, stride=CMP_STRIDE), :] + pe_ref[r:r + 1, :]
            acc = acc + _dot(rows.astype(BF16), w1_ref[r])
        return _dot(_silu(acc).astype(BF16), w2_ref[...])

    kc = compress(kc_ref, pek_ref, w1k_ref, w2k_ref)
    kc = _rope(_rms(kc) * kg_ref[...], cos_ref[...], sin_ref[...])
    ko_ref[0, 0] = kc.astype(ko_ref.dtype)
    vo_ref[0, 0] = compress(vc_ref, pev_ref, w1v_ref, w2v_ref).astype(vo_ref.dtype)


def _nsa_compress(proj, kc_off, vc_off, pe_k, w1_k, w2_k, pe_v, w1_v, w2_v, k_g, cos_c, sin_c):
    b, s, _ = proj.shape
    g = NSA_KV_HEADS
    n_rows = s // CMP_STRIDE
    head = lambda off: pl.BlockSpec((1, s, HEAD_DIM), lambda bi, gi: (bi, 0, off + gi))
    full = lambda shape: pl.BlockSpec(shape, lambda bi, gi: (0,) * len(shape))
    out_spec = pl.BlockSpec((1, 1, n_rows, HEAD_DIM), lambda bi, gi: (bi, gi, 0, 0))
    out_sds = jax.ShapeDtypeStruct((b, g, n_rows, HEAD_DIM), BF16)
    return pl.pallas_call(
        functools.partial(_cmp_kernel, seq=s),
        grid=(b, g),
        in_specs=[head(kc_off), head(vc_off),
                  full((CMP_BLOCK, HEAD_DIM)), full((CMP_BLOCK, HEAD_DIM, HEAD_DIM)), full((HEAD_DIM, HEAD_DIM)),
                  full((CMP_BLOCK, HEAD_DIM)), full((CMP_BLOCK, HEAD_DIM, HEAD_DIM)), full((HEAD_DIM, HEAD_DIM)),
                  full((1, HEAD_DIM)), full((n_rows, HEAD_DIM)), full((n_rows, HEAD_DIM))],
        out_specs=[out_spec, out_spec],
        out_shape=[out_sds, out_sds],
        scratch_shapes=[pltpu.VMEM((s + CMP_STRIDE, HEAD_DIM), F32)],
        compiler_params=pltpu.CompilerParams(
            dimension_semantics=("parallel", "arbitrary"), vmem_limit_bytes=VMEM_LIMIT),
        name="nsa_compress",
    )(proj, proj, pe_k, w1_k, w2_k, pe_v, w1_v, w2_v, k_g.reshape(1, HEAD_DIM), cos_c, sin_c)


def _flash_init(m_scr, l_scr, acc_scr):
    m_scr[...] = jnp.full(m_scr.shape, M_INIT, F32)
    l_scr[...] = jnp.zeros(l_scr.shape, F32)
    acc_scr[...] = jnp.zeros(acc_scr.shape, F32)


def _lane_tiles(x, width):
    return x if width == LANES else jnp.concatenate([x] * (width // LANES), axis=1)


def _flash_step(q, k, v, bias, m_scr, l_scr, acc_scr):
    s = _dot_nt(q, k)
    if bias is not None:
        s = s + bias
    tk = s.shape[1]
    m_old = m_scr[...]
    m_new = jnp.maximum(m_old, jnp.max(s, axis=-1, keepdims=True))
    alpha = jnp.exp2(m_old - m_new)
    p = jnp.exp2(s - _lane_tiles(m_new, tk))
    p_cols = p[:, 0:LANES]
    for c in range(1, tk // LANES):
        p_cols = p_cols + p[:, c * LANES:(c + 1) * LANES]
    l_scr[...] = alpha * l_scr[...] + p_cols
    acc_scr[...] = _lane_tiles(alpha, acc_scr.shape[1]) * acc_scr[...] + _dot(p.astype(BF16), v)
    m_scr[...] = m_new


def _flash_finish(l_scr, acc_scr):
    return acc_scr[...] * (1.0 / jnp.sum(l_scr[...], axis=-1, keepdims=True))


def _rep_heads(bias):
    return jnp.concatenate([bias] * NSA_GROUP, axis=0)


def _stack_heads(qblk):
    return jnp.concatenate([qblk[:, r * HEAD_DIM:(r + 1) * HEAD_DIM] for r in range(NSA_GROUP)], axis=0)


def _select_blocks(p, ovl, q0, tq, n_slc):
    psum = p[0:tq]
    for r in range(1, NSA_GROUP):
        psum = psum + p[r * tq:(r + 1) * tq]
    p_hi = psum.astype(BF16)
    rem = psum - p_hi.astype(F32)
    p_mid = rem.astype(BF16)
    p_lo = (rem - p_mid.astype(F32)).astype(BF16)
    imp = _dot_nt(ovl, p_hi) + _dot_nt(ovl, p_mid) + _dot_nt(ovl, p_lo)
    jb = lax.broadcasted_iota(jnp.int32, (n_slc, tq), 0)
    blk_t = jnp.right_shift(q0 + lax.broadcasted_iota(jnp.int32, (n_slc, tq), 1), int(math.log2(SLC_BLOCK)))
    back = blk_t - jb
    forced = (jb == 0) | ((back >= 0) & (back < N_LOCAL_BLOCKS))
    score = jnp.where(forced, 1e9, jnp.where(back >= 0, imp, -1e9))
    rank = jnp.zeros((n_slc, tq), F32)
    for mp in range(n_slc):
        row = score[mp:mp + 1, :]
        ahead = (row > score) | ((row == score) & (jb > mp))
        rank = rank + jnp.where(ahead, 1.0, 0.0)
    sel_t = jnp.where(rank < float(min(SLC_TOPK, n_slc)), 1.0, 0.0)
    sel_t = jnp.concatenate([sel_t, jnp.zeros((LANES - n_slc, tq), F32)], axis=0).astype(BF16)
    ri = lax.broadcasted_iota(jnp.int32, (tq, tq), 0)
    ci = lax.broadcasted_iota(jnp.int32, (tq, tq), 1)
    eye = jnp.where(ri == ci, 1.0, 0.0).astype(BF16)
    return _dot_nt(eye, sel_t)


def _nsa_combine(o_cmp, o_slc, o_win, gates_ref, ngate_ref, o_ref, tq):
    gates = gates_ref[...]
    for r in range(NSA_GROUP):
        rows = slice(r * tq, (r + 1) * tq)
        g_cmp = gates[:, r:r + 1]
        g_slc = gates[:, NSA_GROUP + r:NSA_GROUP + r + 1]
        g_win = gates[:, 2 * NSA_GROUP + r:2 * NSA_GROUP + r + 1]
        y = g_cmp * o_cmp[rows] + g_slc * o_slc[rows] + g_win * o_win[rows]
        cols = slice(r * HEAD_DIM, (r + 1) * HEAD_DIM)
        o_ref[0, :, cols] = (y * ngate_ref[0, :, cols].astype(F32)).astype(o_ref.dtype)


def _nsa_general(q4, q0, qi, gates_ref, ngate_ref, kcmp_ref, vcmp_ref, ks_ref, vs_ref, kw_ref, vw_ref,
                 ovl_ref, eaug_ref, o_ref, bias_scr, m_scr, l_scr, acc_scr, *, seq):
    tq, tk = NSA_TQ, NSA_TK
    n_slc = seq // SLC_BLOCK

    tpos = q0 + lax.broadcasted_iota(jnp.int32, (tq, LANES), 0)
    cidx = lax.broadcasted_iota(jnp.int32, (tq, LANES), 1)
    cbias = jnp.where(cidx * CMP_STRIDE + (CMP_BLOCK - 1) <= tpos, 0.0, MASKED)
    s = _dot_nt(q4, kcmp_ref[0, 0]) + _rep_heads(cbias)
    m = jnp.maximum(jnp.max(s, axis=-1, keepdims=True), M_INIT)
    e = jnp.exp2(s - m)
    l = jnp.sum(e, axis=-1, keepdims=True)
    p = e * (1.0 / jnp.maximum(l, 1e-30))
    o_cmp = _dot(p.astype(BF16), vcmp_ref[0, 0])

    sel = _select_blocks(p, ovl_ref[...], q0, tq, n_slc).astype(BF16)
    sel_keys = _dot_nt(sel, eaug_ref[...])
    for kt in range(seq // tk):
        bias_scr[kt] = jnp.where(sel_keys[:, kt * tk:(kt + 1) * tk] > 0.5, 0.0, MASKED)

    _flash_init(m_scr, l_scr, acc_scr)

    def slc_body(kt, carry):
        k0 = pl.multiple_of(kt * tk, tk)
        _flash_step(q4, ks_ref[0, pl.ds(k0, tk), :], vs_ref[0, pl.ds(k0, tk), :],
                    _rep_heads(bias_scr[kt]), m_scr, l_scr, acc_scr)
        return carry

    n_full = lax.div(q0, tk)
    lax.fori_loop(0, n_full, slc_body, 0)
    kd = pl.multiple_of(n_full * tk, tk)
    qpos = q0 + lax.broadcasted_iota(jnp.int32, (tq, tk), 0)
    kpos = kd + lax.broadcasted_iota(jnp.int32, (tq, tk), 1)
    causal = jnp.where(kpos <= qpos, 0.0, MASKED)
    _flash_step(q4, ks_ref[0, pl.ds(kd, tk), :], vs_ref[0, pl.ds(kd, tk), :],
                _rep_heads(bias_scr[n_full] + causal), m_scr, l_scr, acc_scr)
    o_slc = _flash_finish(l_scr, acc_scr)

    wk = WINDOW + tq
    ws = pl.multiple_of(jnp.maximum(q0 - WINDOW, 0), tq)
    d = (q0 - ws) + lax.broadcasted_iota(jnp.int32, (tq, wk), 0) - lax.broadcasted_iota(jnp.int32, (tq, wk), 1)
    wbias = jnp.where((d >= 0) & (d < WINDOW), 0.0, MASKED)
    s = _dot_nt(q4, kw_ref[0, pl.ds(ws, wk), :]) + _rep_heads(wbias)
    e = jnp.exp2(s - jnp.max(s, axis=-1, keepdims=True))
    o_win = _dot(e.astype(BF16), vw_ref[0, pl.ds(ws, wk), :]) * (1.0 / jnp.sum(e, axis=-1, keepdims=True))

    _nsa_combine(o_cmp, o_slc, o_win, gates_ref, ngate_ref, o_ref, tq)


def _nsa_bounded(q4, q0, qi, bound, gates_ref, ngate_ref, kcmp_ref, vcmp_ref, ks_ref, vs_ref, kw_ref, vw_ref,
                 ovl_ref, eaug_ref, cmask_ref, wmask_ref, tri_ref, o_ref,
                 ksa_scr, kwa_scr, kca_scr, l_scr, acc_scr, *, seq):
    tq, tk = NSA_TQ, NSA_TK
    n_slc = seq // SLC_BLOCK
    n_cmp_rows = kca_scr.shape[0]

    @pl.when(qi == 0)
    def _():
        one_col = eaug_ref[...]
        lane = lax.broadcasted_iota(jnp.int32, one_col.shape, 1)
        ksa_scr[:, :HEAD_DIM] = ks_ref[0]
        ksa_scr[:, HEAD_DIM:] = one_col
        one_col = jnp.where(lane == AUG_ONE_LANE, one_col, jnp.zeros_like(one_col))
        kwa_scr[:, :HEAD_DIM] = kw_ref[0]
        kwa_scr[:, HEAD_DIM:] = one_col
        kca_scr[:, :HEAD_DIM] = kcmp_ref[0, 0]
        kca_scr[:, HEAD_DIM:] = one_col[:n_cmp_rows]

    lane = lax.broadcasted_iota(jnp.int32, (tq, LANES), 1)
    shift_cols = jnp.where(lane == AUG_ONE_LANE, -bound, 0.0)
    qa = jnp.concatenate([q4, _rep_heads(shift_cols.astype(BF16))], axis=1)

    e = jnp.exp2(_dot_nt(qa, kca_scr[...])) * _rep_heads(cmask_ref[...])
    l = jnp.sum(e, axis=-1, keepdims=True)
    p = e * jnp.where(l > 0.0, 1.0 / l, 0.0)
    o_cmp = _dot(p.astype(BF16), vcmp_ref[0, 0])

    sel = _select_blocks(p, ovl_ref[...], q0, tq, n_slc)
    sel_cols = jnp.where(lane < n_slc, jnp.where(sel > 0.5, 0.0, MASKED), shift_cols)
    qs = jnp.concatenate([q4, _rep_heads(sel_cols.astype(BF16))], axis=1)

    l_scr[...] = jnp.zeros(l_scr.shape, F32)
    acc_scr[...] = jnp.zeros(acc_scr.shape, F32)

    def accumulate(k0, width, mask):
        p_t = jnp.exp2(_dot_nt(qs, ksa_scr[pl.ds(k0, width), :]))
        if mask is not None:
            p_t = p_t * mask
        cols = p_t[:, 0:LANES]
        for c in range(1, width // LANES):
            cols = cols + p_t[:, c * LANES:(c + 1) * LANES]
        l_scr[...] += cols
        acc_scr[...] += _dot(p_t.astype(BF16), vs_ref[0, pl.ds(k0, width), :])

    big = 2 * tk

    def slc_body(kt, carry):
        accumulate(pl.multiple_of(kt * big, big), big, None)
        return carry

    n_full = lax.div(q0, tk)
    n_big = lax.div(n_full, 2)
    lax.fori_loop(0, n_big, slc_body, 0)

    @pl.when(n_full > 2 * n_big)
    def _():
        accumulate(pl.multiple_of(n_big * big, big), tk, None)

    accumulate(pl.multiple_of(n_full * tk, tk), tk, _rep_heads(tri_ref[...]))
    o_slc = _flash_finish(l_scr, acc_scr)

    wk = WINDOW + tq
    ws = pl.multiple_of(jnp.maximum(q0 - WINDOW, 0), tq)
    e = jnp.exp2(_dot_nt(qa, kwa_scr[pl.ds(ws, wk), :])) * _rep_heads(wmask_ref[0])
    o_win = _dot(e.astype(BF16), vw_ref[0, pl.ds(ws, wk), :]) * (1.0 / jnp.sum(e, axis=-1, keepdims=True))

    _nsa_combine(o_cmp, o_slc, o_win, gates_ref, ngate_ref, o_ref, tq)


def _nsa_kernel(bound_ref, q_ref, ngate_ref, gates_ref, kcmp_ref, vcmp_ref, ks_ref, vs_ref, kw_ref, vw_ref,
                ovl_ref, eaug_ref, cmask_ref, wmask_ref, tri_ref, o_ref,
                bias_scr, m_scr, l_scr, acc_scr, ksa_scr, kwa_scr, kca_scr, *, seq):
    qi = pl.program_id(2)
    q0 = qi * NSA_TQ
    q4 = _stack_heads(q_ref[0])
    bound = bound_ref[0]

    @pl.when(bound <= MAX_SCORE_BOUND)
    def _():
        _nsa_bounded(q4, q0, qi, bound, gates_ref, ngate_ref, kcmp_ref, vcmp_ref, ks_ref, vs_ref, kw_ref, vw_ref,
                     ovl_ref, eaug_ref, cmask_ref, wmask_ref, tri_ref, o_ref,
                     ksa_scr, kwa_scr, kca_scr, l_scr, acc_scr, seq=seq)

    @pl.when(bound > MAX_SCORE_BOUND)
    def _():
        _nsa_general(q4, q0, qi, gates_ref, ngate_ref, kcmp_ref, vcmp_ref, ks_ref, vs_ref, kw_ref, vw_ref,
                     ovl_ref, eaug_ref, o_ref, bias_scr, m_scr, l_scr, acc_scr, seq=seq)


def _nsa_attention(proj, gates, kcmp, vcmp, bound, tables, offs):
    b, s, _ = proj.shape
    g = NSA_KV_HEADS
    tq = NSA_TQ
    nq = s // tq
    gw = NSA_GROUP * HEAD_DIM
    n_cmp_rows = s // CMP_STRIDE
    ovl_t, eaug, cmask, wmask, tri = tables
    q_spec = lambda off: pl.BlockSpec((1, tq, gw), lambda bi, gi, qi: (bi, qi, off + gi))
    kv_spec = lambda off: pl.BlockSpec((1, s, HEAD_DIM), lambda bi, gi, qi: (bi, 0, off + gi))
    cmp_spec = pl.BlockSpec((1, 1, n_cmp_rows, HEAD_DIM), lambda bi, gi, qi: (bi, gi, 0, 0))
    full = lambda shape: pl.BlockSpec(shape, lambda bi, gi, qi: (0,) * len(shape))
    n_wpat = wmask.shape[0]
    rows4 = NSA_GROUP * tq
    return pl.pallas_call(
        functools.partial(_nsa_kernel, seq=s),
        grid=(b, g, nq),
        in_specs=[pl.BlockSpec(memory_space=pltpu.SMEM),
                  q_spec(offs["nq"] // gw), q_spec(offs["ngate"] // gw),
                  pl.BlockSpec((tq, LANES), lambda bi, gi, qi: (bi * nq + qi, gi)),
                  cmp_spec, cmp_spec,
                  kv_spec(offs["ks"] // HEAD_DIM), kv_spec(offs["vs"] // HEAD_DIM),
                  kv_spec(offs["kw"] // HEAD_DIM), kv_spec(offs["vw"] // HEAD_DIM),
                  full(ovl_t.shape), full(eaug.shape),
                  pl.BlockSpec((tq, LANES), lambda bi, gi, qi: (qi, 0)),
                  pl.BlockSpec((1,) + wmask.shape[1:], lambda bi, gi, qi: (jnp.minimum(qi, n_wpat - 1), 0, 0)),
                  full(tri.shape)],
        out_specs=pl.BlockSpec((1, tq, gw), lambda bi, gi, qi: (bi, qi, gi)),
        out_shape=jax.ShapeDtypeStruct((b, s, NSA_WIDTH), BF16),
        scratch_shapes=[pltpu.VMEM((s // NSA_TK, tq, NSA_TK), F32),
                        pltpu.VMEM((rows4, LANES), F32), pltpu.VMEM((rows4, LANES), F32),
                        pltpu.VMEM((rows4, HEAD_DIM), F32),
                        pltpu.VMEM((s, 2 * HEAD_DIM), BF16), pltpu.VMEM((s, 2 * HEAD_DIM), BF16),
                        pltpu.VMEM((n_cmp_rows, 2 * HEAD_DIM), BF16)],
        compiler_params=pltpu.CompilerParams(
            dimension_semantics=("parallel", "parallel", "arbitrary"), vmem_limit_bytes=VMEM_LIMIT),
        name="nsa_attention",
    )(bound, proj, proj, gates, kcmp, vcmp, proj, proj, proj, proj, ovl_t, eaug, cmask, wmask, tri)


def _diff_kernel(q_ref, gate_ref, k_ref, v_ref, lam_ref, o_ref, m1, l1, a1, m2, l2, a2, *, lambda_init):
    tq, tk = DIFF_TQ, DIFF_TK
    qi = pl.program_id(2)
    q = q_ref[0]
    q1 = q[:, :HEAD_DIM]
    q2 = q[:, HEAD_DIM:]
    _flash_init(m1, l1, a1)
    _flash_init(m2, l2, a2)

    def step(rows, k0, width, bias):
        k = k_ref[0, pl.ds(k0, width), :]
        v = v_ref[0, pl.ds(k0, width), :]
        _flash_step(q1[rows], k[:, :HEAD_DIM], v, bias, m1.at[rows], l1.at[rows], a1.at[rows])
        _flash_step(q2[rows], k[:, HEAD_DIM:], v, bias, m2.at[rows], l2.at[rows], a2.at[rows])

    def body(kt, carry):
        step(slice(0, tq), pl.multiple_of(kt * tk, tk), tk, None)
        return carry

    lax.fori_loop(0, qi * (tq // tk), body, 0)
    dd = DIFF_DIAG
    q0 = qi * tq
    for c in range(tq // dd):
        n_rows = tq - c * dd
        ri = lax.broadcasted_iota(jnp.int32, (n_rows, dd), 0)
        ci = lax.broadcasted_iota(jnp.int32, (n_rows, dd), 1)
        step(slice(c * dd, tq), pl.multiple_of(q0 + c * dd, dd), dd, jnp.where(ri >= ci, 0.0, MASKED))

    lp = lam_ref[...]
    lam = (jnp.exp(jnp.sum(lp[0:1] * lp[1:2], axis=-1, keepdims=True))
           - jnp.exp(jnp.sum(lp[2:3] * lp[3:4], axis=-1, keepdims=True)) + lambda_init)
    o = _flash_finish(l1, a1) - lam * _flash_finish(l2, a2)
    o = _rms(o) * (1.0 - lambda_init)
    o_ref[0] = (o * gate_ref[0].astype(F32)).astype(o_ref.dtype)


def _diff_attention(proj, lam_params, lambda_init):
    b, s, _ = proj.shape
    h = DIFF_HEADS
    tq = DIFF_TQ
    w = DIFF_V_DIM
    q_spec = lambda off: pl.BlockSpec((1, tq, w), lambda bi, hi, qi: (bi, qi, off + hi))
    kv_spec = lambda off: pl.BlockSpec((1, s, w), lambda bi, hi, qi: (bi, 0, off + hi))
    stat = pltpu.VMEM((tq, LANES), F32)
    acc = pltpu.VMEM((tq, w), F32)
    return pl.pallas_call(
        functools.partial(_diff_kernel, lambda_init=lambda_init),
        grid=(b, h, s // tq),
        in_specs=[q_spec(0), q_spec(3 * h), kv_spec(h), kv_spec(2 * h),
                  pl.BlockSpec(lam_params.shape, lambda bi, hi, qi: (0, 0))],
        out_specs=pl.BlockSpec((1, tq, w), lambda bi, hi, qi: (bi, qi, hi)),
        out_shape=jax.ShapeDtypeStruct((b, s, DIFF_WIDTH), BF16),
        scratch_shapes=[stat, stat, acc, stat, stat, acc],
        compiler_params=pltpu.CompilerParams(
            dimension_semantics=("parallel", "parallel", "arbitrary"), vmem_limit_bytes=VMEM_LIMIT),
        name="diff_attention",
    )(proj, proj, proj, proj, lam_params)


def _rope_tables(pos):
    inv = 1.0 / (ROPE_THETA ** (jnp.arange(0, HEAD_DIM, 2, dtype=F32) / HEAD_DIM))
    ang = pos.astype(F32)[:, None] * inv[None, :]
    cos, sin = jnp.cos(ang), jnp.sin(ang)
    return jnp.concatenate([cos, cos], axis=-1), jnp.concatenate([-sin, sin], axis=-1)


def _retention_tables():
    h, c = RET_HEADS, RET_CHUNK
    log_g = jnp.log1p(-jnp.exp2(-5.0 - jnp.arange(h, dtype=F32)))
    j = jnp.arange(c, dtype=F32)
    diff = j[:, None] - j[None, :]
    intra = jnp.where(diff >= 0, jnp.exp(log_g[:, None, None] * jnp.maximum(diff, 0.0)), 0.0)
    q_dec = jnp.exp(log_g[:, None] * (j + 1.0))
    k_dec = jnp.exp(log_g[:, None] * (c - 1.0 - j))
    chunk_dec = jnp.exp(log_g * c)
    wide = lambda t: jnp.broadcast_to(t[:, :, None], (h, t.shape[1], HEAD_DIM))
    return intra, wide(q_dec), wide(k_dec), wide(chunk_dec[:, None])


def _selection_tables(seq):
    tq, tk = NSA_TQ, NSA_TK
    n_cmp_rows = seq // CMP_STRIDE
    n_slc = seq // SLC_BLOCK
    assert tq == tk and n_slc <= AUG_ONE_LANE < LANES
    c_start = np.arange(n_cmp_rows) * CMP_STRIDE
    s_start = np.arange(n_slc) * SLC_BLOCK
    overlap_t = ((c_start[None, :] <= s_start[:, None] + SLC_BLOCK - 1)
                 & (c_start[None, :] + CMP_BLOCK - 1 >= s_start[:, None]))
    lane = np.arange(LANES)[None, :]
    key = np.arange(seq)[:, None]
    eaug = ((key // SLC_BLOCK) == lane) | (lane == AUG_ONE_LANE)
    cmask = lane * CMP_STRIDE + CMP_BLOCK - 1 <= key
    r = np.arange(tq)[:, None]
    c = np.arange(WINDOW + tq)[None, :]
    wmask = []
    for pat in range(WINDOW // tq + 1):
        d = min(pat * tq, WINDOW) + r - c
        wmask.append((d >= 0) & (d < WINDOW))
    tri = np.arange(tq)[:, None] >= np.arange(tk)[None, :]
    return (jnp.asarray(overlap_t, BF16), jnp.asarray(eaug, BF16), jnp.asarray(cmask, F32),
            jnp.asarray(np.stack(wmask), F32), jnp.asarray(tri, F32))


def _tile_modes(segments, tn):
    off, kinds, col = {}, [], 0
    for name, width, kind in segments:
        off[name] = col
        kinds += [kind] * (width // LANES)
        col += width
    per = tn // LANES
    tiles = [kinds[i:i + per] for i in range(0, len(kinds), per)]
    assert col % tn == 0
    modes = []
    for j, tile in enumerate(tiles):
        if modes and modes[-1][2] == tile:
            modes[-1] = (modes[-1][0], j + 1, tile)
        else:
            modes.append((j, j + 1, tile))
    return modes, off


def _l0_tile_modes():
    plain, silu = ("plain", 0), ("silu", 0)
    k_norm = ("nrope", L0_TAB_K)
    return _tile_modes([
        ("rq", RET_WIDTH, ("rope", L0_TAB_ROPE)), ("rk", RET_WIDTH, ("rope", L0_TAB_ROPE_SCALED)),
        ("rv", RET_WIDTH, plain), ("rgate", RET_WIDTH, silu), ("nq", NSA_WIDTH, ("nrope", L0_TAB_Q)),
        ("kc", NSA_KV_WIDTH, plain), ("vc", NSA_KV_WIDTH, plain), ("ks", NSA_KV_WIDTH, k_norm),
        ("vs", NSA_KV_WIDTH, plain), ("kw", NSA_KV_WIDTH, k_norm), ("vw", NSA_KV_WIDTH, plain),
        ("ngate", NSA_WIDTH, silu)], PROJ_TN)


def _l1_tile_modes():
    return _tile_modes([("q", DIFF_WIDTH, ("nrope", L1_TAB_Q)), ("k", DIFF_WIDTH, ("nrope", L1_TAB_K)),
                        ("v", DIFF_WIDTH, ("plain", 0)), ("gate", DIFF_WIDTH, ("silu", 0))], L1_PROJ_TN)[0]


def _rope_pair(cos, sin_signed, gain=None, scale=1.0):
    if gain is None:
        return [cos * scale, sin_signed * scale]
    return [cos * (gain * scale)[None, :], sin_signed * (jnp.roll(gain, HALF) * scale)[None, :]]


def kernel(x, l0_norm_g, l0_w_in, l0_w_out, l0_nsa_q_norm_g, l0_nsa_k_norm_g, l0_cmp_pe_k, l0_cmp_w1_k, l0_cmp_w2_k, l0_cmp_pe_v, l0_cmp_w1_v, l0_cmp_w2_v, l1_norm_g, l1_w_in, l1_w_out, l1_q_norm_g, l1_k_norm_g, l1_lambda_q1, l1_lambda_k1, l1_lambda_q2, l1_lambda_k2):
    b, s, d = x.shape
    m = b * s
    x2d = x.reshape(m, d)
    cos, sin = _rope_tables(jnp.arange(s))
    cos_c, sin_c = _rope_tables(jnp.arange(s // CMP_STRIDE) * CMP_STRIDE + CMP_BLOCK - 1)
    intra, qdec, kdec, cdec = _retention_tables()
    nsa_tables = _selection_tables(s)

    modes0, off = _l0_tile_modes()
    w0 = l0_w_in.astype(BF16)
    wg = l0_w_in[:, AB_MAIN_COLS:].reshape(d, 3, NSA_KV_HEADS, NSA_GROUP).transpose(0, 2, 1, 3)
    wg = wg.reshape(d, NSA_KV_HEADS, 3 * NSA_GROUP)
    wg = jnp.pad(wg, ((0, 0), (0, 0), (0, LANES - 3 * NSA_GROUP))).reshape(d, NSA_KV_HEADS * LANES).astype(BF16)
    tabs0 = jnp.stack(_rope_pair(cos, sin) + _rope_pair(cos, sin, scale=QK_SCALE)
                      + _rope_pair(cos, sin, l0_nsa_q_norm_g, Q_SCALE) + _rope_pair(cos, sin, l0_nsa_k_norm_g))
    proj0, gates = _norm_proj(x2d, l0_norm_g, w0, tabs0, modes0, s, PROJ_TN, wg=wg)
    proj0 = proj0.reshape(b, s, AB_MAIN_COLS)
    y_ret = _retention(proj0, intra, qdec, kdec, cdec)
    w1k = l0_cmp_w1_k.astype(BF16).reshape(CMP_BLOCK, HEAD_DIM, HEAD_DIM)
    w1v = l0_cmp_w1_v.astype(BF16).reshape(CMP_BLOCK, HEAD_DIM, HEAD_DIM)
    kcmp, vcmp = _nsa_compress(proj0, off["kc"] // HEAD_DIM, off["vc"] // HEAD_DIM,
                               l0_cmp_pe_k, w1k, l0_cmp_w2_k.astype(BF16),
                               l0_cmp_pe_v, w1v, l0_cmp_w2_v.astype(BF16),
                               l0_nsa_k_norm_g, cos_c, sin_c)
    bound = (BOUND_MARGIN * HEAD_DIM * Q_SCALE * jnp.max(jnp.abs(l0_nsa_q_norm_g))
             * jnp.max(jnp.abs(l0_nsa_k_norm_g))).reshape(1).astype(F32)
    y_nsa = _nsa_attention(proj0, gates, kcmp, vcmp, bound, nsa_tables, off)
    w_out0 = l0_w_out.astype(BF16)
    x1 = _out_proj([y_ret.reshape(m, RET_WIDTH), y_nsa.reshape(m, NSA_WIDTH)],
                   [w_out0[:RET_WIDTH], w_out0[RET_WIDTH:]], x2d)

    lambda_init = 0.8 - 0.6 * math.exp(-0.3 * 1)
    tabs1 = jnp.stack(_rope_pair(cos, sin, l1_q_norm_g, Q_SCALE) + _rope_pair(cos, sin, l1_k_norm_g))
    proj1 = _norm_proj(x1, l1_norm_g, l1_w_in.astype(BF16), tabs1, _l1_tile_modes(), s, L1_PROJ_TN)[0]
    lam_params = jnp.stack([l1_lambda_q1, l1_lambda_k1, l1_lambda_q2, l1_lambda_k2]).astype(F32)
    y1 = _diff_attention(proj1.reshape(b, s, C_IN_COLS), lam_params, lambda_init)
    out = _out_proj([y1.reshape(m, DIFF_WIDTH)], [l1_w_out.astype(BF16)], x1)
    return out.reshape(b, s, d)
```

```python
import functools
import math

import numpy as np
import jax
import jax.numpy as jnp
from jax import lax
from jax.experimental import pallas as pl
from jax.experimental.pallas import tpu as pltpu

F32 = jnp.float32
BF16 = jnp.bfloat16

D_MODEL = 2048
HEAD_DIM = 128
HALF = HEAD_DIM // 2
ROPE_THETA = 10000.0
EPS = 1e-6
RET_HEADS = 8
RET_CHUNK = 128
NSA_HEADS = 8
NSA_KV_HEADS = 2
NSA_GROUP = NSA_HEADS // NSA_KV_HEADS
CMP_BLOCK = 32
CMP_STRIDE = 16
SLC_BLOCK = 64
SLC_TOPK = 16
N_LOCAL_BLOCKS = 2
WINDOW = 512
DIFF_HEADS = 8
DIFF_V_DIM = 2 * HEAD_DIM
QK_SCALE = HEAD_DIM ** -0.5
LOG2E = math.log2(math.e)
Q_SCALE = QK_SCALE * LOG2E

RET_WIDTH = RET_HEADS * HEAD_DIM
NSA_WIDTH = NSA_HEADS * HEAD_DIM
NSA_KV_WIDTH = NSA_KV_HEADS * HEAD_DIM
AB_MAIN_COLS = 4 * RET_WIDTH + 2 * NSA_WIDTH + 6 * NSA_KV_WIDTH
N_GATE_COLS = 3 * NSA_HEADS
DIFF_WIDTH = DIFF_HEADS * DIFF_V_DIM
C_IN_COLS = 4 * DIFF_WIDTH

LANES = 128
MASKED = -1e30
M_INIT = -1e29
MAX_SCORE_BOUND = 60.0
BOUND_MARGIN = 1.05
AUG_ONE_LANE = 32
VMEM_LIMIT = 56 * 1024 * 1024

PROJ_TM = 1024
PROJ_TN = 1280
L1_PROJ_TN = 1024
OPROJ_TN = 1024
PROJ_SUB = 256
L0_TAB_ROPE, L0_TAB_ROPE_SCALED, L0_TAB_Q, L0_TAB_K = range(4)
L1_TAB_Q, L1_TAB_K = range(2)
NSA_TQ = 512
NSA_TK = 512
DIFF_TQ = 512
DIFF_TK = 512
DIFF_DIAG = 256


def _dot(a, b):
    return jnp.dot(a, b, preferred_element_type=F32)


def _dot_nt(a, b):
    return lax.dot_general(a, b, (((1,), (1,)), ((), ())), preferred_element_type=F32)


def _dot_tn(a, b):
    return lax.dot_general(a, b, (((0,), (0,)), ((), ())), preferred_element_type=F32)


def _silu(x):
    return x / (1.0 + jnp.exp(-x))


def _rms(x):
    return x * lax.rsqrt(jnp.mean(x * x, axis=-1, keepdims=True) + EPS)


def _rope(x, cos, sin_signed):
    return x * cos + pltpu.roll(x, HALF, axis=1) * sin_signed


def _segment_epilogue(seg, kind, tabs_ref, rows):
    op, t = kind
    if op == "plain":
        return seg
    if op == "silu":
        return _silu(seg)
    if op == "nrope":
        mean_mat = jnp.full((LANES, LANES), 1.0 / HEAD_DIM, BF16)
        seg = seg * lax.rsqrt(_dot((seg * seg).astype(BF16), mean_mat) + EPS)
    return seg * tabs_ref[2 * t, rows, :] + pltpu.roll(seg, HALF, axis=1) * tabs_ref[2 * t + 1, rows, :]


def _proj_kernel(*refs, tile_modes, has_gates):
    if has_gates:
        x_ref, g_ref, w_ref, tabs_ref, wg_ref, o_ref, og_ref, h_scr = refs
    else:
        x_ref, g_ref, w_ref, tabs_ref, o_ref, h_scr = refs
    j = pl.program_id(1)

    @pl.when(j == 0)
    def _():
        x = x_ref[...]
        h = (_rms(x) * g_ref[...]).astype(BF16)
        h_scr[...] = h
        if has_gates:
            z = _dot(h, wg_ref[...])
            og_ref[...] = 1.0 / (1.0 + jnp.exp(-z))

    def make_branch(lo, hi, kinds):
        @pl.when((j >= lo) & (j < hi))
        def _():
            for r in range(h_scr.shape[0] // PROJ_SUB):
                rows = slice(r * PROJ_SUB, (r + 1) * PROJ_SUB)
                acc = _dot(h_scr[rows, :], w_ref[...])
                for c, kind in enumerate(kinds):
                    cols = slice(c * LANES, (c + 1) * LANES)
                    o_ref[rows, cols] = _segment_epilogue(acc[:, cols], kind, tabs_ref, rows).astype(o_ref.dtype)

    for lo, hi, kinds in tile_modes:
        make_branch(lo, hi, kinds)


def _norm_proj(x2d, g, w, tabs, tile_modes, seq, tn, wg=None):
    m, d = x2d.shape
    tm = PROJ_TM
    n = tile_modes[-1][1] * tn
    s_tiles = seq // tm
    has_gates = wg is not None
    in_specs = [
        pl.BlockSpec((tm, d), lambda i, j: (i, 0)),
        pl.BlockSpec((1, d), lambda i, j: (0, 0)),
        pl.BlockSpec((d, tn), lambda i, j: (0, j)),
        pl.BlockSpec((tabs.shape[0], tm, LANES), lambda i, j: (0, i % s_tiles, 0)),
    ]
    args = [x2d, g.reshape(1, d), w, tabs]
    out_shape = [jax.ShapeDtypeStruct((m, n), BF16)]
    out_specs = [pl.BlockSpec((tm, tn), lambda i, j: (i, j))]
    if has_gates:
        ng = wg.shape[1]
        in_specs.append(pl.BlockSpec((d, ng), lambda i, j: (0, 0)))
        args.append(wg)
        out_shape.append(jax.ShapeDtypeStruct((m, ng), F32))
        out_specs.append(pl.BlockSpec((tm, ng), lambda i, j: (i, 0)))
    return pl.pallas_call(
        functools.partial(_proj_kernel, tile_modes=tile_modes, has_gates=has_gates),
        grid=(m // tm, n // tn),
        in_specs=in_specs,
        out_specs=out_specs,
        out_shape=out_shape,
        scratch_shapes=[pltpu.VMEM((tm, d), BF16)],
        compiler_params=pltpu.CompilerParams(
            dimension_semantics=("parallel", "arbitrary"), vmem_limit_bytes=VMEM_LIMIT),
        name="norm_proj",
    )(*args)


def _oproj_kernel(*refs, n_terms):
    y_refs, w_refs = refs[:n_terms], refs[n_terms:2 * n_terms]
    x_ref, o_ref = refs[2 * n_terms:]
    acc = x_ref[...]
    for y_ref, w_ref in zip(y_refs, w_refs):
        acc = acc + _dot(y_ref[...], w_ref[...])
    o_ref[...] = acc


def _out_proj(ys, ws, x2d):
    m, n = x2d.shape
    tm, tn = PROJ_TM, OPROJ_TN
    y_specs = [pl.BlockSpec((tm, y.shape[1]), lambda i, j: (i, 0)) for y in ys]
    w_specs = [pl.BlockSpec((w.shape[0], tn), lambda i, j: (0, j)) for w in ws]
    return pl.pallas_call(
        functools.partial(_oproj_kernel, n_terms=len(ys)),
        grid=(m // tm, n // tn),
        in_specs=y_specs + w_specs + [pl.BlockSpec((tm, tn), lambda i, j: (i, j))],
        out_specs=pl.BlockSpec((tm, tn), lambda i, j: (i, j)),
        out_shape=jax.ShapeDtypeStruct((m, n), F32),
        compiler_params=pltpu.CompilerParams(
            dimension_semantics=("parallel", "arbitrary"), vmem_limit_bytes=VMEM_LIMIT),
        name="out_proj",
    )(*ys, *ws, x2d)


def _ret_kernel(q_ref, k_ref, v_ref, gate_ref, intra_ref, qdec_ref, kdec_ref, cdec_ref, o_ref, *, n_chunks):
    c = RET_CHUNK
    intra = intra_ref[0]
    qdec = qdec_ref[0]
    kdec = kdec_ref[0]
    cdec = cdec_ref[0]
    state = jnp.zeros((HEAD_DIM, HEAD_DIM), F32)
    for n in range(n_chunks):
        rows = slice(n * c, (n + 1) * c)
        q = q_ref[0, rows, :]
        k = k_ref[0, rows, :]
        v = v_ref[0, rows, :]
        scores = _dot_nt(q, k) * intra
        inner = _dot(scores.astype(BF16), v)
        cross = _dot((q.astype(F32) * qdec).astype(BF16), state.astype(BF16))
        kv = _dot_tn((k.astype(F32) * kdec).astype(BF16), v)
        state = state * cdec + kv
        o = _rms(inner + cross) * gate_ref[0, rows, :].astype(F32)
        o_ref[0, rows, :] = o.astype(o_ref.dtype)


def _retention(proj, intra, qdec, kdec, cdec):
    b, s, _ = proj.shape
    h = RET_HEADS
    y_shape = (b, s, RET_WIDTH)
    head = lambda off: pl.BlockSpec((1, s, HEAD_DIM), lambda bi, hi: (bi, 0, off + hi))
    table = lambda rows: pl.BlockSpec((1, rows, HEAD_DIM), lambda bi, hi: (hi, 0, 0))
    return pl.pallas_call(
        functools.partial(_ret_kernel, n_chunks=s // RET_CHUNK),
        grid=(b, h),
        in_specs=[head(0), head(h), head(2 * h), head(3 * h),
                  table(RET_CHUNK), table(RET_CHUNK), table(RET_CHUNK), table(1)],
        out_specs=pl.BlockSpec((1, s, HEAD_DIM), lambda bi, hi: (bi, 0, hi)),
        out_shape=jax.ShapeDtypeStruct(y_shape, BF16),
        compiler_params=pltpu.CompilerParams(
            dimension_semantics=("parallel", "arbitrary"), vmem_limit_bytes=VMEM_LIMIT),
        name="retention",
    )(proj, proj, proj, proj, intra, qdec, kdec, cdec)


def _cmp_kernel(kc_ref, vc_ref, pek_ref, w1k_ref, w2k_ref, pev_ref, w1v_ref, w2v_ref, kg_ref,
                cos_ref, sin_ref, ko_ref, vo_ref, t_scr, *, seq):
    n_rows = seq // CMP_STRIDE

    def compress(src_ref, pe_ref, w1_ref, w2_ref):
        t_scr[0:seq, :] = src_ref[0].astype(F32)
        t_scr[seq:seq + CMP_STRIDE, :] = jnp.zeros((CMP_STRIDE, HEAD_DIM), F32)
        acc = jnp.zeros((n_rows, HEAD_DIM), F32)
        for r in range(CMP_BLOCK):
            rows = t_scr[pl.ds(r, n_rows, stride=CMP_STRIDE), :] + pe_ref[r:r + 1, :]
            acc = acc + _dot(rows.astype(BF16), w1_ref[r])
        return _dot(_silu(acc).astype(BF16), w2_ref[...])

    kc = compress(kc_ref, pek_ref, w1k_ref, w2k_ref)
    kc = _rope(_rms(kc) * kg_ref[...], cos_ref[...], sin_ref[...])
    ko_ref[0, 0] = kc.astype(ko_ref.dtype)
    vo_ref[0, 0] = compress(vc_ref, pev_ref, w1v_ref, w2v_ref).astype(vo_ref.dtype)


def _nsa_compress(proj, kc_off, vc_off, pe_k, w1_k, w2_k, pe_v, w1_v, w2_v, k_g, cos_c, sin_c):
    b, s, _ = proj.shape
    g = NSA_KV_HEADS
    n_rows = s // CMP_STRIDE
    head = lambda off: pl.BlockSpec((1, s, HEAD_DIM), lambda bi, gi: (bi, 0, off + gi))
    full = lambda shape: pl.BlockSpec(shape, lambda bi, gi: (0,) * len(shape))
    out_spec = pl.BlockSpec((1, 1, n_rows, HEAD_DIM), lambda bi, gi: (bi, gi, 0, 0))
    out_sds = jax.ShapeDtypeStruct((b, g, n_rows, HEAD_DIM), BF16)
    return pl.pallas_call(
        functools.partial(_cmp_kernel, seq=s),
        grid=(b, g),
        in_specs=[head(kc_off), head(vc_off),
                  full((CMP_BLOCK, HEAD_DIM)), full((CMP_BLOCK, HEAD_DIM, HEAD_DIM)), full((HEAD_DIM, HEAD_DIM)),
                  full((CMP_BLOCK, HEAD_DIM)), full((CMP_BLOCK, HEAD_DIM, HEAD_DIM)), full((HEAD_DIM, HEAD_DIM)),
                  full((1, HEAD_DIM)), full((n_rows, HEAD_DIM)), full((n_rows, HEAD_DIM))],
        out_specs=[out_spec, out_spec],
        out_shape=[out_sds, out_sds],
        scratch_shapes=[pltpu.VMEM((s + CMP_STRIDE, HEAD_DIM), F32)],
        compiler_params=pltpu.CompilerParams(
            dimension_semantics=("parallel", "arbitrary"), vmem_limit_bytes=VMEM_LIMIT),
        name="nsa_compress",
    )(proj, proj, pe_k, w1_k, w2_k, pe_v, w1_v, w2_v, k_g.reshape(1, HEAD_DIM), cos_c, sin_c)


def _flash_init(m_scr, l_scr, acc_scr):
    m_scr[...] = jnp.full(m_scr.shape, M_INIT, F32)
    l_scr[...] = jnp.zeros(l_scr.shape, F32)
    acc_scr[...] = jnp.zeros(acc_scr.shape, F32)


def _lane_tiles(x, width):
    return x if width == LANES else jnp.concatenate([x] * (width // LANES), axis=1)


def _flash_step(q, k, v, bias, m_scr, l_scr, acc_scr):
    s = _dot_nt(q, k)
    if bias is not None:
        s = s + bias
    tk = s.shape[1]
    m_old = m_scr[...]
    m_new = jnp.maximum(m_old, jnp.max(s, axis=-1, keepdims=True))
    alpha = jnp.exp2(m_old - m_new)
    p = jnp.exp2(s - _lane_tiles(m_new, tk))
    p_cols = p[:, 0:LANES]
    for c in range(1, tk // LANES):
        p_cols = p_cols + p[:, c * LANES:(c + 1) * LANES]
    l_scr[...] = alpha * l_scr[...] + p_cols
    acc_scr[...] = _lane_tiles(alpha, acc_scr.shape[1]) * acc_scr[...] + _dot(p.astype(BF16), v)
    m_scr[...] = m_new


def _flash_finish(l_scr, acc_scr):
    return acc_scr[...] * (1.0 / jnp.sum(l_scr[...], axis=-1, keepdims=True))


def _rep_heads(bias):
    return jnp.concatenate([bias] * NSA_GROUP, axis=0)


def _stack_heads(qblk):
    return jnp.concatenate([qblk[:, r * HEAD_DIM:(r + 1) * HEAD_DIM] for r in range(NSA_GROUP)], axis=0)


def _select_blocks(p, ovl, q0, tq, n_slc):
    psum = p[0:tq]
    for r in range(1, NSA_GROUP):
        psum = psum + p[r * tq:(r + 1) * tq]
    p_hi = psum.astype(BF16)
    rem = psum - p_hi.astype(F32)
    p_mid = rem.astype(BF16)
    p_lo = (rem - p_mid.astype(F32)).astype(BF16)
    imp = _dot_nt(ovl, p_hi) + _dot_nt(ovl, p_mid) + _dot_nt(ovl, p_lo)
    jb = lax.broadcasted_iota(jnp.int32, (n_slc, tq), 0)
    blk_t = jnp.right_shift(q0 + lax.broadcasted_iota(jnp.int32, (n_slc, tq), 1), int(math.log2(SLC_BLOCK)))
    back = blk_t - jb
    forced = (jb == 0) | ((back >= 0) & (back < N_LOCAL_BLOCKS))
    score = jnp.where(forced, 1e9, jnp.where(back >= 0, imp, -1e9))
    rank = jnp.zeros((n_slc, tq), F32)
    for mp in range(n_slc):
        row = score[mp:mp + 1, :]
        ahead = (row > score) | ((row == score) & (jb > mp))
        rank = rank + jnp.where(ahead, 1.0, 0.0)
    sel_t = jnp.where(rank < float(min(SLC_TOPK, n_slc)), 1.0, 0.0)
    sel_t = jnp.concatenate([sel_t, jnp.zeros((LANES - n_slc, tq), F32)], axis=0).astype(BF16)
    ri = lax.broadcasted_iota(jnp.int32, (tq, tq), 0)
    ci = lax.broadcasted_iota(jnp.int32, (tq, tq), 1)
    eye = jnp.where(ri == ci, 1.0, 0.0).astype(BF16)
    return _dot_nt(eye, sel_t)


def _nsa_combine(o_cmp, o_slc, o_win, gates_ref, ngate_ref, o_ref, tq):
    gates = gates_ref[...]
    for r in range(NSA_GROUP):
        rows = slice(r * tq, (r + 1) * tq)
        g_cmp = gates[:, r:r + 1]
        g_slc = gates[:, NSA_GROUP + r:NSA_GROUP + r + 1]
        g_win = gates[:, 2 * NSA_GROUP + r:2 * NSA_GROUP + r + 1]
        y = g_cmp * o_cmp[rows] + g_slc * o_slc[rows] + g_win * o_win[rows]
        cols = slice(r * HEAD_DIM, (r + 1) * HEAD_DIM)
        o_ref[0, :, cols] = (y * ngate_ref[0, :, cols].astype(F32)).astype(o_ref.dtype)


def _nsa_general(q4, q0, qi, gates_ref, ngate_ref, kcmp_ref, vcmp_ref, ks_ref, vs_ref, kw_ref, vw_ref,
                 ovl_ref, eaug_ref, o_ref, bias_scr, m_scr, l_scr, acc_scr, *, seq):
    tq, tk = NSA_TQ, NSA_TK
    n_slc = seq // SLC_BLOCK

    tpos = q0 + lax.broadcasted_iota(jnp.int32, (tq, LANES), 0)
    cidx = lax.broadcasted_iota(jnp.int32, (tq, LANES), 1)
    cbias = jnp.where(cidx * CMP_STRIDE + (CMP_BLOCK - 1) <= tpos, 0.0, MASKED)
    s = _dot_nt(q4, kcmp_ref[0, 0]) + _rep_heads(cbias)
    m = jnp.maximum(jnp.max(s, axis=-1, keepdims=True), M_INIT)
    e = jnp.exp2(s - m)
    l = jnp.sum(e, axis=-1, keepdims=True)
    p = e * (1.0 / jnp.maximum(l, 1e-30))
    o_cmp = _dot(p.astype(BF16), vcmp_ref[0, 0])

    sel = _select_blocks(p, ovl_ref[...], q0, tq, n_slc).astype(BF16)
    sel_keys = _dot_nt(sel, eaug_ref[...])
    for kt in range(seq // tk):
        bias_scr[kt] = jnp.where(sel_keys[:, kt * tk:(kt + 1) * tk] > 0.5, 0.0, MASKED)

    _flash_init(m_scr, l_scr, acc_scr)

    def slc_body(kt, carry):
        k0 = pl.multiple_of(kt * tk, tk)
        _flash_step(q4, ks_ref[0, pl.ds(k0, tk), :], vs_ref[0, pl.ds(k0, tk), :],
                    _rep_heads(bias_scr[kt]), m_scr, l_scr, acc_scr)
        return carry

    n_full = lax.div(q0, tk)
    lax.fori_loop(0, n_full, slc_body, 0)
    kd = pl.multiple_of(n_full * tk, tk)
    qpos = q0 + lax.broadcasted_iota(jnp.int32, (tq, tk), 0)
    kpos = kd + lax.broadcasted_iota(jnp.int32, (tq, tk), 1)
    causal = jnp.where(kpos <= qpos, 0.0, MASKED)
    _flash_step(q4, ks_ref[0, pl.ds(kd, tk), :], vs_ref[0, pl.ds(kd, tk), :],
                _rep_heads(bias_scr[n_full] + causal), m_scr, l_scr, acc_scr)
    o_slc = _flash_finish(l_scr, acc_scr)

    wk = WINDOW + tq
    ws = pl.multiple_of(jnp.maximum(q0 - WINDOW, 0), tq)
    d = (q0 - ws) + lax.broadcasted_iota(jnp.int32, (tq, wk), 0) - lax.broadcasted_iota(jnp.int32, (tq, wk), 1)
    wbias = jnp.where((d >= 0) & (d < WINDOW), 0.0, MASKED)
    s = _dot_nt(q4, kw_ref[0, pl.ds(ws, wk), :]) + _rep_heads(wbias)
    e = jnp.exp2(s - jnp.max(s, axis=-1, keepdims=True))
    o_win = _dot(e.astype(BF16), vw_ref[0, pl.ds(ws, wk), :]) * (1.0 / jnp.sum(e, axis=-1, keepdims=True))

    _nsa_combine(o_cmp, o_slc, o_win, gates_ref, ngate_ref, o_ref, tq)


def _nsa_bounded(q4, q0, qi, bound, gates_ref, ngate_ref, kcmp_ref, vcmp_ref, ks_ref, vs_ref, kw_ref, vw_ref,
                 ovl_ref, eaug_ref, cmask_ref, wmask_ref, tri_ref, o_ref,
                 ksa_scr, kwa_scr, kca_scr, l_scr, acc_scr, *, seq):
    tq, tk = NSA_TQ, NSA_TK
    n_slc = seq // SLC_BLOCK
    n_cmp_rows = kca_scr.shape[0]

    @pl.when(qi == 0)
    def _():
        one_col = eaug_ref[...]
        lane = lax.broadcasted_iota(jnp.int32, one_col.shape, 1)
        ksa_scr[:, :HEAD_DIM] = ks_ref[0]
        ksa_scr[:, HEAD_DIM:] = one_col
        one_col = jnp.where(lane == AUG_ONE_LANE, one_col, jnp.zeros_like(one_col))
        kwa_scr[:, :HEAD_DIM] = kw_ref[0]
        kwa_scr[:, HEAD_DIM:] = one_col
        kca_scr[:, :HEAD_DIM] = kcmp_ref[0, 0]
        kca_scr[:, HEAD_DIM:] = one_col[:n_cmp_rows]

    lane = lax.broadcasted_iota(jnp.int32, (tq, LANES), 1)
    shift_cols = jnp.where(lane == AUG_ONE_LANE, -bound, 0.0)
    qa = jnp.concatenate([q4, _rep_heads(shift_cols.astype(BF16))], axis=1)

    e = jnp.exp2(_dot_nt(qa, kca_scr[...])) * _rep_heads(cmask_ref[...])
    l = jnp.sum(e, axis=-1, keepdims=True)
    p = e * jnp.where(l > 0.0, 1.0 / l, 0.0)
    o_cmp = _dot(p.astype(BF16), vcmp_ref[0, 0])

    sel = _select_blocks(p, ovl_ref[...], q0, tq, n_slc)
    sel_cols = jnp.where(lane < n_slc, jnp.where(sel > 0.5, 0.0, MASKED), shift_cols)
    qs = jnp.concatenate([q4, _rep_heads(sel_cols.astype(BF16))], axis=1)

    l_scr[...] = jnp.zeros(l_scr.shape, F32)
    acc_scr[...] = jnp.zeros(acc_scr.shape, F32)

    def accumulate(k0, width, mask):
        p_t = jnp.exp2(_dot_nt(qs, ksa_scr[pl.ds(k0, width), :]))
        if mask is not None:
            p_t = p_t * mask
        cols = p_t[:, 0:LANES]
        for c in range(1, width // LANES):
            cols = cols + p_t[:, c * LANES:(c + 1) * LANES]
        l_scr[...] += cols
        acc_scr[...] += _dot(p_t.astype(BF16), vs_ref[0, pl.ds(k0, width), :])

    big = 2 * tk

    def slc_body(kt, carry):
        accumulate(pl.multiple_of(kt * big, big), big, None)
        return carry

    n_full = lax.div(q0, tk)
    n_big = lax.div(n_full, 2)
    lax.fori_loop(0, n_big, slc_body, 0)

    @pl.when(n_full > 2 * n_big)
    def _():
        accumulate(pl.multiple_of(n_big * big, big), tk, None)

    accumulate(pl.multiple_of(n_full * tk, tk), tk, _rep_heads(tri_ref[...]))
    o_slc = _flash_finish(l_scr, acc_scr)

    wk = WINDOW + tq
    ws = pl.multiple_of(jnp.maximum(q0 - WINDOW, 0), tq)
    e = jnp.exp2(_dot_nt(qa, kwa_scr[pl.ds(ws, wk), :])) * _rep_heads(wmask_ref[0])
    o_win = _dot(e.astype(BF16), vw_ref[0, pl.ds(ws, wk), :]) * (1.0 / jnp.sum(e, axis=-1, keepdims=True))

    _nsa_combine(o_cmp, o_slc, o_win, gates_ref, ngate_ref, o_ref, tq)


def _nsa_kernel(bound_ref, q_ref, ngate_ref, gates_ref, kcmp_ref, vcmp_ref, ks_ref, vs_ref, kw_ref, vw_ref,
                ovl_ref, eaug_ref, cmask_ref, wmask_ref, tri_ref, o_ref,
                bias_scr, m_scr, l_scr, acc_scr, ksa_scr, kwa_scr, kca_scr, *, seq):
    qi = pl.program_id(2)
    q0 = qi * NSA_TQ
    q4 = _stack_heads(q_ref[0])
    bound = bound_ref[0]

    @pl.when(bound <= MAX_SCORE_BOUND)
    def _():
        _nsa_bounded(q4, q0, qi, bound, gates_ref, ngate_ref, kcmp_ref, vcmp_ref, ks_ref, vs_ref, kw_ref, vw_ref,
                     ovl_ref, eaug_ref, cmask_ref, wmask_ref, tri_ref, o_ref,
                     ksa_scr, kwa_scr, kca_scr, l_scr, acc_scr, seq=seq)

    @pl.when(bound > MAX_SCORE_BOUND)
    def _():
        _nsa_general(q4, q0, qi, gates_ref, ngate_ref, kcmp_ref, vcmp_ref, ks_ref, vs_ref, kw_ref, vw_ref,
                     ovl_ref, eaug_ref, o_ref, bias_scr, m_scr, l_scr, acc_scr, seq=seq)


def _nsa_attention(proj, gates, kcmp, vcmp, bound, tables, offs):
    b, s, _ = proj.shape
    g = NSA_KV_HEADS
    tq = NSA_TQ
    nq = s // tq
    gw = NSA_GROUP * HEAD_DIM
    n_cmp_rows = s // CMP_STRIDE
    ovl_t, eaug, cmask, wmask, tri = tables
    q_spec = lambda off: pl.BlockSpec((1, tq, gw), lambda bi, gi, qi: (bi, qi, off + gi))
    kv_spec = lambda off: pl.BlockSpec((1, s, HEAD_DIM), lambda bi, gi, qi: (bi, 0, off + gi))
    cmp_spec = pl.BlockSpec((1, 1, n_cmp_rows, HEAD_DIM), lambda bi, gi, qi: (bi, gi, 0, 0))
    full = lambda shape: pl.BlockSpec(shape, lambda bi, gi, qi: (0,) * len(shape))
    n_wpat = wmask.shape[0]
    rows4 = NSA_GROUP * tq
    return pl.pallas_call(
        functools.partial(_nsa_kernel, seq=s),
        grid=(b, g, nq),
        in_specs=[pl.BlockSpec(memory_space=pltpu.SMEM),
                  q_spec(offs["nq"] // gw), q_spec(offs["ngate"] // gw),
                  pl.BlockSpec((tq, LANES), lambda bi, gi, qi: (bi * nq + qi, gi)),
                  cmp_spec, cmp_spec,
                  kv_spec(offs["ks"] // HEAD_DIM), kv_spec(offs["vs"] // HEAD_DIM),
                  kv_spec(offs["kw"] // HEAD_DIM), kv_spec(offs["vw"] // HEAD_DIM),
                  full(ovl_t.shape), full(eaug.shape),
                  pl.BlockSpec((tq, LANES), lambda bi, gi, qi: (qi, 0)),
                  pl.BlockSpec((1,) + wmask.shape[1:], lambda bi, gi, qi: (jnp.minimum(qi, n_wpat - 1), 0, 0)),
                  full(tri.shape)],
        out_specs=pl.BlockSpec((1, tq, gw), lambda bi, gi, qi: (bi, qi, gi)),
        out_shape=jax.ShapeDtypeStruct((b, s, NSA_WIDTH), BF16),
        scratch_shapes=[pltpu.VMEM((s // NSA_TK, tq, NSA_TK), F32),
                        pltpu.VMEM((rows4, LANES), F32), pltpu.VMEM((rows4, LANES), F32),
                        pltpu.VMEM((rows4, HEAD_DIM), F32),
                        pltpu.VMEM((s, 2 * HEAD_DIM), BF16), pltpu.VMEM((s, 2 * HEAD_DIM), BF16),
                        pltpu.VMEM((n_cmp_rows, 2 * HEAD_DIM), BF16)],
        compiler_params=pltpu.CompilerParams(
            dimension_semantics=("parallel", "parallel", "arbitrary"), vmem_limit_bytes=VMEM_LIMIT),
        name="nsa_attention",
    )(bound, proj, proj, gates, kcmp, vcmp, proj, proj, proj, proj, ovl_t, eaug, cmask, wmask, tri)


def _diff_kernel(q_ref, gate_ref, k_ref, v_ref, lam_ref, o_ref, m1, l1, a1, m2, l2, a2, *, lambda_init):
    tq, tk = DIFF_TQ, DIFF_TK
    qi = pl.program_id(2)
    q = q_ref[0]
    q1 = q[:, :HEAD_DIM]
    q2 = q[:, HEAD_DIM:]
    _flash_init(m1, l1, a1)
    _flash_init(m2, l2, a2)

    def step(rows, k0, width, bias):
        k = k_ref[0, pl.ds(k0, width), :]
        v = v_ref[0, pl.ds(k0, width), :]
        _flash_step(q1[rows], k[:, :HEAD_DIM], v, bias, m1.at[rows], l1.at[rows], a1.at[rows])
        _flash_step(q2[rows], k[:, HEAD_DIM:], v, bias, m2.at[rows], l2.at[rows], a2.at[rows])

    def body(kt, carry):
        step(slice(0, tq), pl.multiple_of(kt * tk, tk), tk, None)
        return carry

    lax.fori_loop(0, qi * (tq // tk), body, 0)
    dd = DIFF_DIAG
    q0 = qi * tq
    for c in range(tq // dd):
        n_rows = tq - c * dd
        ri = lax.broadcasted_iota(jnp.int32, (n_rows, dd), 0)
        ci = lax.broadcasted_iota(jnp.int32, (n_rows, dd), 1)
        step(slice(c * dd, tq), pl.multiple_of(q0 + c * dd, dd), dd, jnp.where(ri >= ci, 0.0, MASKED))

    lp = lam_ref[...]
    lam = (jnp.exp(jnp.sum(lp[0:1] * lp[1:2], axis=-1, keepdims=True))
           - jnp.exp(jnp.sum(lp[2:3] * lp[3:4], axis=-1, keepdims=True)) + lambda_init)
    o = _flash_finish(l1, a1) - lam * _flash_finish(l2, a2)
    o = _rms(o) * (1.0 - lambda_init)
    o_ref[0] = (o * gate_ref[0].astype(F32)).astype(o_ref.dtype)


def _diff_attention(proj, lam_params, lambda_init):
    b, s, _ = proj.shape
    h = DIFF_HEADS
    tq = DIFF_TQ
    w = DIFF_V_DIM
    q_spec = lambda off: pl.BlockSpec((1, tq, w), lambda bi, hi, qi: (bi, qi, off + hi))
    kv_spec = lambda off: pl.BlockSpec((1, s, w), lambda bi, hi, qi: (bi, 0, off + hi))
    stat = pltpu.VMEM((tq, LANES), F32)
    acc = pltpu.VMEM((tq, w), F32)
    return pl.pallas_call(
        functools.partial(_diff_kernel, lambda_init=lambda_init),
        grid=(b, h, s // tq),
        in_specs=[q_spec(0), q_spec(3 * h), kv_spec(h), kv_spec(2 * h),
                  pl.BlockSpec(lam_params.shape, lambda bi, hi, qi: (0, 0))],
        out_specs=pl.BlockSpec((1, tq, w), lambda bi, hi, qi: (bi, qi, hi)),
        out_shape=jax.ShapeDtypeStruct((b, s, DIFF_WIDTH), BF16),
        scratch_shapes=[stat, stat, acc, stat, stat, acc],
        compiler_params=pltpu.CompilerParams(
            dimension_semantics=("parallel", "parallel", "arbitrary"), vmem_limit_bytes=VMEM_LIMIT),
        name="diff_attention",
    )(proj, proj, proj, proj, lam_params)


def _rope_tables(pos):
    inv = 1.0 / (ROPE_THETA ** (jnp.arange(0, HEAD_DIM, 2, dtype=F32) / HEAD_DIM))
    ang = pos.astype(F32)[:, None] * inv[None, :]
    cos, sin = jnp.cos(ang), jnp.sin(ang)
    return jnp.concatenate([cos, cos], axis=-1), jnp.concatenate([-sin, sin], axis=-1)


def _retention_tables():
    h, c = RET_HEADS, RET_CHUNK
    log_g = jnp.log1p(-jnp.exp2(-5.0 - jnp.arange(h, dtype=F32)))
    j = jnp.arange(c, dtype=F32)
    diff = j[:, None] - j[None, :]
    intra = jnp.where(diff >= 0, jnp.exp(log_g[:, None, None] * jnp.maximum(diff, 0.0)), 0.0)
    q_dec = jnp.exp(log_g[:, None] * (j + 1.0))
    k_dec = jnp.exp(log_g[:, None] * (c - 1.0 - j))
    chunk_dec = jnp.exp(log_g * c)
    wide = lambda t: jnp.broadcast_to(t[:, :, None], (h, t.shape[1], HEAD_DIM))
    return intra, wide(q_dec), wide(k_dec), wide(chunk_dec[:, None])


def _selection_tables(seq):
    tq, tk = NSA_TQ, NSA_TK
    n_cmp_rows = seq // CMP_STRIDE
    n_slc = seq // SLC_BLOCK
    assert tq == tk and n_slc <= AUG_ONE_LANE < LANES
    c_start = np.arange(n_cmp_rows) * CMP_STRIDE
    s_start = np.arange(n_slc) * SLC_BLOCK
    overlap_t = ((c_start[None, :] <= s_start[:, None] + SLC_BLOCK - 1)
                 & (c_start[None, :] + CMP_BLOCK - 1 >= s_start[:, None]))
    lane = np.arange(LANES)[None, :]
    key = np.arange(seq)[:, None]
    eaug = ((key // SLC_BLOCK) == lane) | (lane == AUG_ONE_LANE)
    cmask = lane * CMP_STRIDE + CMP_BLOCK - 1 <= key
    r = np.arange(tq)[:, None]
    c = np.arange(WINDOW + tq)[None, :]
    wmask = []
    for pat in range(WINDOW // tq + 1):
        d = min(pat * tq, WINDOW) + r - c
        wmask.append((d >= 0) & (d < WINDOW))
    tri = np.arange(tq)[:, None] >= np.arange(tk)[None, :]
    return (jnp.asarray(overlap_t, BF16), jnp.asarray(eaug, BF16), jnp.asarray(cmask, F32),
            jnp.asarray(np.stack(wmask), F32), jnp.asarray(tri, F32))


def _tile_modes(segments, tn):
    off, kinds, col = {}, [], 0
    for name, width, kind in segments:
        off[name] = col
        kinds += [kind] * (width // LANES)
        col += width
    per = tn // LANES
    tiles = [kinds[i:i + per] for i in range(0, len(kinds), per)]
    assert col % tn == 0
    modes = []
    for j, tile in enumerate(tiles):
        if modes and modes[-1][2] == tile:
            modes[-1] = (modes[-1][0], j + 1, tile)
        else:
            modes.append((j, j + 1, tile))
    return modes, off


def _l0_tile_modes():
    plain, silu = ("plain", 0), ("silu", 0)
    k_norm = ("nrope", L0_TAB_K)
    return _tile_modes([
        ("rq", RET_WIDTH, ("rope", L0_TAB_ROPE)), ("rk", RET_WIDTH, ("rope", L0_TAB_ROPE_SCALED)),
        ("rv", RET_WIDTH, plain), ("rgate", RET_WIDTH, silu), ("nq", NSA_WIDTH, ("nrope", L0_TAB_Q)),
        ("kc", NSA_KV_WIDTH, plain), ("vc", NSA_KV_WIDTH, plain), ("ks", NSA_KV_WIDTH, k_norm),
        ("vs", NSA_KV_WIDTH, plain), ("kw", NSA_KV_WIDTH, k_norm), ("vw", NSA_KV_WIDTH, plain),
        ("ngate", NSA_WIDTH, silu)], PROJ_TN)


def _l1_tile_modes():
    return _tile_modes([("q", DIFF_WIDTH, ("nrope", L1_TAB_Q)), ("k", DIFF_WIDTH, ("nrope", L1_TAB_K)),
                        ("v", DIFF_WIDTH, ("plain", 0)), ("gate", DIFF_WIDTH, ("silu", 0))], L1_PROJ_TN)[0]


def _rope_pair(cos, sin_signed, gain=None, scale=1.0):
    if gain is None:
        return [cos * scale, sin_signed * scale]
    return [cos * (gain * scale)[None, :], sin_signed * (jnp.roll(gain, HALF) * scale)[None, :]]


def kernel(x, l0_norm_g, l0_w_in, l0_w_out, l0_nsa_q_norm_g, l0_nsa_k_norm_g, l0_cmp_pe_k, l0_cmp_w1_k, l0_cmp_w2_k, l0_cmp_pe_v, l0_cmp_w1_v, l0_cmp_w2_v, l1_norm_g, l1_w_in, l1_w_out, l1_q_norm_g, l1_k_norm_g, l1_lambda_q1, l1_lambda_k1, l1_lambda_q2, l1_lambda_k2):
    b, s, d = x.shape
    m = b * s
    x2d = x.reshape(m, d)
    cos, sin = _rope_tables(jnp.arange(s))
    cos_c, sin_c = _rope_tables(jnp.arange(s // CMP_STRIDE) * CMP_STRIDE + CMP_BLOCK - 1)
    intra, qdec, kdec, cdec = _retention_tables()
    nsa_tables = _selection_tables(s)

    modes0, off = _l0_tile_modes()
    w0 = l0_w_in.astype(BF16)
    wg = l0_w_in[:, AB_MAIN_COLS:].reshape(d, 3, NSA_KV_HEADS, NSA_GROUP).transpose(0, 2, 1, 3)
    wg = wg.reshape(d, NSA_KV_HEADS, 3 * NSA_GROUP)
    wg = jnp.pad(wg, ((0, 0), (0, 0), (0, LANES - 3 * NSA_GROUP))).reshape(d, NSA_KV_HEADS * LANES).astype(BF16)
    tabs0 = jnp.stack(_rope_pair(cos, sin) + _rope_pair(cos, sin, scale=QK_SCALE)
                      + _rope_pair(cos, sin, l0_nsa_q_norm_g, Q_SCALE) + _rope_pair(cos, sin, l0_nsa_k_norm_g))
    proj0, gates = _norm_proj(x2d, l0_norm_g, w0, tabs0, modes0, s, PROJ_TN, wg=wg)
    proj0 = proj0.reshape(b, s, AB_MAIN_COLS)
    y_ret = _retention(proj0, intra, qdec, kdec, cdec)
    w1k = l0_cmp_w1_k.astype(BF16).reshape(CMP_BLOCK, HEAD_DIM, HEAD_DIM)
    w1v = l0_cmp_w1_v.astype(BF16).reshape(CMP_BLOCK, HEAD_DIM, HEAD_DIM)
    kcmp, vcmp = _nsa_compress(proj0, off["kc"] // HEAD_DIM, off["vc"] // HEAD_DIM,
                               l0_cmp_pe_k, w1k, l0_cmp_w2_k.astype(BF16),
                               l0_cmp_pe_v, w1v, l0_cmp_w2_v.astype(BF16),
                               l0_nsa_k_norm_g, cos_c, sin_c)
    bound = (BOUND_MARGIN * HEAD_DIM * Q_SCALE * jnp.max(jnp.abs(l0_nsa_q_norm_g))
             * jnp.max(jnp.abs(l0_nsa_k_norm_g))).reshape(1).astype(F32)
    y_nsa = _nsa_attention(proj0, gates, kcmp, vcmp, bound, nsa_tables, off)
    w_out0 = l0_w_out.astype(BF16)
    x1 = _out_proj([y_ret.reshape(m, RET_WIDTH), y_nsa.reshape(m, NSA_WIDTH)],
                   [w_out0[:RET_WIDTH], w_out0[RET_WIDTH:]], x2d)

    lambda_init = 0.8 - 0.6 * math.exp(-0.3 * 1)
    tabs1 = jnp.stack(_rope_pair(cos, sin, l1_q_norm_g, Q_SCALE) + _rope_pair(cos, sin, l1_k_norm_g))
    proj1 = _norm_proj(x1, l1_norm_g, l1_w_in.astype(BF16), tabs1, _l1_tile_modes(), s, L1_PROJ_TN)[0]
    lam_params = jnp.stack([l1_lambda_q1, l1_lambda_k1, l1_lambda_q2, l1_lambda_k2]).astype(F32)
    y1 = _diff_attention(proj1.reshape(b, s, C_IN_COLS), lam_params, lambda_init)
    out = _out_proj([y1.reshape(m, DIFF_WIDTH)], [l1_w_out.astype(BF16)], x1)
    return out.reshape(b, s, d)
```

```python
import functools
import math

import numpy as np
import jax
import jax.numpy as jnp
from jax import lax
from jax.experimental import pallas as pl
from jax.experimental.pallas import tpu as pltpu

F32 = jnp.float32
BF16 = jnp.bfloat16

D_MODEL = 2048
HEAD_DIM = 128
HALF = HEAD_DIM // 2
ROPE_THETA = 10000.0
EPS = 1e-6
RET_HEADS = 8
RET_CHUNK = 256
NSA_HEADS = 8
NSA_KV_HEADS = 2
NSA_GROUP = NSA_HEADS // NSA_KV_HEADS
CMP_BLOCK = 32
CMP_STRIDE = 16
SLC_BLOCK = 64
SLC_TOPK = 16
N_LOCAL_BLOCKS = 2
WINDOW = 512
DIFF_HEADS = 8
DIFF_V_DIM = 2 * HEAD_DIM
QK_SCALE = HEAD_DIM ** -0.5
LOG2E = math.log2(math.e)
Q_SCALE = QK_SCALE * LOG2E

RET_WIDTH = RET_HEADS * HEAD_DIM
NSA_WIDTH = NSA_HEADS * HEAD_DIM
NSA_KV_WIDTH = NSA_KV_HEADS * HEAD_DIM
AB_MAIN_COLS = 4 * RET_WIDTH + 2 * NSA_WIDTH + 6 * NSA_KV_WIDTH
N_GATE_COLS = 3 * NSA_HEADS
DIFF_WIDTH = DIFF_HEADS * DIFF_V_DIM
C_IN_COLS = 4 * DIFF_WIDTH

LANES = 128
MASKED = -1e30
M_INIT = -1e29
MAX_SCORE_BOUND = 60.0
BOUND_MARGIN = 1.05
AUG_ONE_LANE = 32
VMEM_LIMIT = 56 * 1024 * 1024

PROJ_TM = 1024
PROJ_TN = 1280
L1_PROJ_TN = 1024
OPROJ_TN = 1024
PROJ_SUB = 512
L0_TAB_ROPE, L0_TAB_ROPE_SCALED, L0_TAB_Q, L0_TAB_K = range(4)
L1_TAB_Q, L1_TAB_K = range(2)
NSA_TQ = 512
NSA_TK = 512
DIFF_TQ = 512
DIFF_TK = 512
DIFF_DIAG = 256


def _dot(a, b):
    return jnp.dot(a, b, preferred_element_type=F32)


def _dot_nt(a, b):
    return lax.dot_general(a, b, (((1,), (1,)), ((), ())), preferred_element_type=F32)


def _dot_tn(a, b):
    return lax.dot_general(a, b, (((0,), (0,)), ((), ())), preferred_element_type=F32)


def _silu(x):
    return x / (1.0 + jnp.exp(-x))


def _rms(x):
    return x * lax.rsqrt(jnp.mean(x * x, axis=-1, keepdims=True) + EPS)


def _rope(x, cos, sin_signed):
    return x * cos + pltpu.roll(x, HALF, axis=1) * sin_signed


def _segment_epilogue(seg, kind, tabs_ref, rows):
    op, t = kind
    if op == "plain":
        return seg
    if op == "silu":
        return _silu(seg)
    if op == "nrope":
        mean_mat = jnp.full((LANES, LANES), 1.0 / HEAD_DIM, BF16)
        seg = seg * lax.rsqrt(_dot((seg * seg).astype(BF16), mean_mat) + EPS)
    return seg * tabs_ref[2 * t, rows, :] + pltpu.roll(seg, HALF, axis=1) * tabs_ref[2 * t + 1, rows, :]


def _proj_kernel(*refs, tile_modes, has_gates):
    if has_gates:
        x_ref, g_ref, w_ref, tabs_ref, wg_ref, o_ref, og_ref, h_scr = refs
    else:
        x_ref, g_ref, w_ref, tabs_ref, o_ref, h_scr = refs
    j = pl.program_id(1)

    @pl.when(j == 0)
    def _():
        x = x_ref[...]
        h = (_rms(x) * g_ref[...]).astype(BF16)
        h_scr[...] = h
        if has_gates:
            z = _dot(h, wg_ref[...])
            og_ref[...] = 1.0 / (1.0 + jnp.exp(-z))

    def make_branch(lo, hi, kinds):
        @pl.when((j >= lo) & (j < hi))
        def _():
            for r in range(h_scr.shape[0] // PROJ_SUB):
                rows = slice(r * PROJ_SUB, (r + 1) * PROJ_SUB)
                acc = _dot(h_scr[rows, :], w_ref[...])
                for c, kind in enumerate(kinds):
                    cols = slice(c * LANES, (c + 1) * LANES)
                    o_ref[rows, cols] = _segment_epilogue(acc[:, cols], kind, tabs_ref, rows).astype(o_ref.dtype)

    for lo, hi, kinds in tile_modes:
        make_branch(lo, hi, kinds)


def _norm_proj(x2d, g, w, tabs, tile_modes, seq, tn, wg=None):
    m, d = x2d.shape
    tm = PROJ_TM
    n = tile_modes[-1][1] * tn
    s_tiles = seq // tm
    has_gates = wg is not None
    in_specs = [
        pl.BlockSpec((tm, d), lambda i, j: (i, 0)),
        pl.BlockSpec((1, d), lambda i, j: (0, 0)),
        pl.BlockSpec((d, tn), lambda i, j: (0, j)),
        pl.BlockSpec((tabs.shape[0], tm, LANES), lambda i, j: (0, i % s_tiles, 0)),
    ]
    args = [x2d, g.reshape(1, d), w, tabs]
    out_shape = [jax.ShapeDtypeStruct((m, n), BF16)]
    out_specs = [pl.BlockSpec((tm, tn), lambda i, j: (i, j))]
    if has_gates:
        ng = wg.shape[1]
        in_specs.append(pl.BlockSpec((d, ng), lambda i, j: (0, 0)))
        args.append(wg)
        out_shape.append(jax.ShapeDtypeStruct((m, ng), F32))
        out_specs.append(pl.BlockSpec((tm, ng), lambda i, j: (i, 0)))
    return pl.pallas_call(
        functools.partial(_proj_kernel, tile_modes=tile_modes, has_gates=has_gates),
        grid=(m // tm, n // tn),
        in_specs=in_specs,
        out_specs=out_specs,
        out_shape=out_shape,
        scratch_shapes=[pltpu.VMEM((tm, d), BF16)],
        compiler_params=pltpu.CompilerParams(
            dimension_semantics=("parallel", "arbitrary"), vmem_limit_bytes=VMEM_LIMIT),
        name="norm_proj",
    )(*args)


def _oproj_kernel(*refs, n_terms):
    y_refs, w_refs = refs[:n_terms], refs[n_terms:2 * n_terms]
    x_ref, o_ref = refs[2 * n_terms:]
    acc = x_ref[...]
    for y_ref, w_ref in zip(y_refs, w_refs):
        acc = acc + _dot(y_ref[...], w_ref[...])
    o_ref[...] = acc


def _out_proj(ys, ws, x2d):
    m, n = x2d.shape
    tm, tn = PROJ_TM, OPROJ_TN
    y_specs = [pl.BlockSpec((tm, y.shape[1]), lambda i, j: (i, 0)) for y in ys]
    w_specs = [pl.BlockSpec((w.shape[0], tn), lambda i, j: (0, j)) for w in ws]
    return pl.pallas_call(
        functools.partial(_oproj_kernel, n_terms=len(ys)),
        grid=(m // tm, n // tn),
        in_specs=y_specs + w_specs + [pl.BlockSpec((tm, tn), lambda i, j: (i, j))],
        out_specs=pl.BlockSpec((tm, tn), lambda i, j: (i, j)),
        out_shape=jax.ShapeDtypeStruct((m, n), F32),
        compiler_params=pltpu.CompilerParams(
            dimension_semantics=("parallel", "arbitrary"), vmem_limit_bytes=VMEM_LIMIT),
        name="out_proj",
    )(*ys, *ws, x2d)


def _ret_kernel(q_ref, k_ref, v_ref, gate_ref, intra_ref, qdec_ref, kdec_ref, cdec_ref, o_ref, *, n_chunks):
    c = RET_CHUNK
    intra = intra_ref[0]
    qdec = qdec_ref[0]
    kdec = kdec_ref[0]
    cdec = cdec_ref[0]
    state = jnp.zeros((HEAD_DIM, HEAD_DIM), F32)
    for n in range(n_chunks):
        rows = slice(n * c, (n + 1) * c)
        q = q_ref[0, rows, :]
        k = k_ref[0, rows, :]
        v = v_ref[0, rows, :]
        scores = _dot_nt(q, k) * intra
        inner = _dot(scores.astype(BF16), v)
        cross = _dot((q.astype(F32) * qdec).astype(BF16), state.astype(BF16))
        kv = _dot_tn((k.astype(F32) * kdec).astype(BF16), v)
        state = state * cdec + kv
        o = _rms(inner + cross) * gate_ref[0, rows, :].astype(F32)
        o_ref[0, rows, :] = o.astype(o_ref.dtype)


def _retention(proj, intra, qdec, kdec, cdec):
    b, s, _ = proj.shape
    h = RET_HEADS
    y_shape = (b, s, RET_WIDTH)
    head = lambda off: pl.BlockSpec((1, s, HEAD_DIM), lambda bi, hi: (bi, 0, off + hi))
    table = lambda rows, cols=HEAD_DIM: pl.BlockSpec((1, rows, cols), lambda bi, hi: (hi, 0, 0))
    return pl.pallas_call(
        functools.partial(_ret_kernel, n_chunks=s // RET_CHUNK),
        grid=(b, h),
        in_specs=[head(0), head(h), head(2 * h), head(3 * h),
                  table(RET_CHUNK, RET_CHUNK), table(RET_CHUNK), table(RET_CHUNK), table(1)],
        out_specs=pl.BlockSpec((1, s, HEAD_DIM), lambda bi, hi: (bi, 0, hi)),
        out_shape=jax.ShapeDtypeStruct(y_shape, BF16),
        compiler_params=pltpu.CompilerParams(
            dimension_semantics=("parallel", "arbitrary"), vmem_limit_bytes=VMEM_LIMIT),
        name="retention",
    )(proj, proj, proj, proj, intra, qdec, kdec, cdec)


def _cmp_kernel(kc_ref, vc_ref, pek_ref, w1k_ref, w2k_ref, pev_ref, w1v_ref, w2v_ref, kg_ref,
                cos_ref, sin_ref, ko_ref, vo_ref, t_scr, *, seq):
    n_rows = seq // CMP_STRIDE

    def compress(src_ref, pe_ref, w1_ref, w2_ref):
        t_scr[0:seq, :] = src_ref[0].astype(F32)
        t_scr[seq:seq + CMP_STRIDE, :] = jnp.zeros((CMP_STRIDE, HEAD_DIM), F32)
        acc = jnp.zeros((n_rows, HEAD_DIM), F32)
        for r in range(CMP_BLOCK):
            rows = t_scr[pl.ds(r, n_rows, stride=CMP_STRIDE), :] + pe_ref[r:r + 1, :]
            acc = acc + _dot(rows.astype(BF16), w1_ref[r])
        return _dot(_silu(acc).astype(BF16), w2_ref[...])

    kc = compress(kc_ref, pek_ref, w1k_ref, w2k_ref)
    kc = _rope(_rms(kc) * kg_ref[...], cos_ref[...], sin_ref[...])
    ko_ref[0, 0] = kc.astype(ko_ref.dtype)
    vo_ref[0, 0] = compress(vc_ref, pev_ref, w1v_ref, w2v_ref).astype(vo_ref.dtype)


def _nsa_compress(proj, kc_off, vc_off, pe_k, w1_k, w2_k, pe_v, w1_v, w2_v, k_g, cos_c, sin_c):
    b, s, _ = proj.shape
    g = NSA_KV_HEADS
    n_rows = s // CMP_STRIDE
    head = lambda off: pl.BlockSpec((1, s, HEAD_DIM), lambda bi, gi: (bi, 0, off + gi))
    full = lambda shape: pl.BlockSpec(shape, lambda bi, gi: (0,) * len(shape))
    out_spec = pl.BlockSpec((1, 1, n_rows, HEAD_DIM), lambda bi, gi: (bi, gi, 0, 0))
    out_sds = jax.ShapeDtypeStruct((b, g, n_rows, HEAD_DIM), BF16)
    return pl.pallas_call(
        functools.partial(_cmp_kernel, seq=s),
        grid=(b, g),
        in_specs=[head(kc_off), head(vc_off),
                  full((CMP_BLOCK, HEAD_DIM)), full((CMP_BLOCK, HEAD_DIM, HEAD_DIM)), full((HEAD_DIM, HEAD_DIM)),
                  full((CMP_BLOCK, HEAD_DIM)), full((CMP_BLOCK, HEAD_DIM, HEAD_DIM)), full((HEAD_DIM, HEAD_DIM)),
                  full((1, HEAD_DIM)), full((n_rows, HEAD_DIM)), full((n_rows, HEAD_DIM))],
        out_specs=[out_spec, out_spec],
        out_shape=[out_sds, out_sds],
        scratch_shapes=[pltpu.VMEM((s + CMP_STRIDE, HEAD_DIM), F32)],
        compiler_params=pltpu.CompilerParams(
            dimension_semantics=("parallel", "arbitrary"), vmem_limit_bytes=VMEM_LIMIT),
        name="nsa_compress",
    )(proj, proj, pe_k, w1_k, w2_k, pe_v, w1_v, w2_v, k_g.reshape(1, HEAD_DIM), cos_c, sin_c)


def _flash_init(m_scr, l_scr, acc_scr):
    m_scr[...] = jnp.full(m_scr.shape, M_INIT, F32)
    l_scr[...] = jnp.zeros(l_scr.shape, F32)
    acc_scr[...] = jnp.zeros(acc_scr.shape, F32)


def _lane_tiles(x, width):
    return x if width == LANES else jnp.concatenate([x] * (width // LANES), axis=1)


def _flash_step(q, k, v, bias, m_scr, l_scr, acc_scr):
    s = _dot_nt(q, k)
    if bias is not None:
        s = s + bias
    tk = s.shape[1]
    m_old = m_scr[...]
    m_new = jnp.maximum(m_old, jnp.max(s, axis=-1, keepdims=True))
    alpha = jnp.exp2(m_old - m_new)
    p = jnp.exp2(s - _lane_tiles(m_new, tk))
    p_cols = p[:, 0:LANES]
    for c in range(1, tk // LANES):
        p_cols = p_cols + p[:, c * LANES:(c + 1) * LANES]
    l_scr[...] = alpha * l_scr[...] + p_cols
    acc_scr[...] = _lane_tiles(alpha, acc_scr.shape[1]) * acc_scr[...] + _dot(p.astype(BF16), v)
    m_scr[...] = m_new


def _flash_finish(l_scr, acc_scr):
    return acc_scr[...] * (1.0 / jnp.sum(l_scr[...], axis=-1, keepdims=True))


def _rep_heads(bias):
    return jnp.concatenate([bias] * NSA_GROUP, axis=0)


def _stack_heads(qblk):
    return jnp.concatenate([qblk[:, r * HEAD_DIM:(r + 1) * HEAD_DIM] for r in range(NSA_GROUP)], axis=0)


def _select_blocks(p, ovl, q0, tq, n_slc):
    psum = p[0:tq]
    for r in range(1, NSA_GROUP):
        psum = psum + p[r * tq:(r + 1) * tq]
    p_hi = psum.astype(BF16)
    rem = psum - p_hi.astype(F32)
    p_mid = rem.astype(BF16)
    p_lo = (rem - p_mid.astype(F32)).astype(BF16)
    imp = _dot_nt(ovl, p_hi) + _dot_nt(ovl, p_mid) + _dot_nt(ovl, p_lo)
    jb = lax.broadcasted_iota(jnp.int32, (n_slc, tq), 0)
    blk_t = jnp.right_shift(q0 + lax.broadcasted_iota(jnp.int32, (n_slc, tq), 1), int(math.log2(SLC_BLOCK)))
    back = blk_t - jb
    forced = (jb == 0) | ((back >= 0) & (back < N_LOCAL_BLOCKS))
    score = jnp.where(forced, 1e9, jnp.where(back >= 0, imp, -1e9))
    rank = jnp.zeros((n_slc, tq), F32)
    for mp in range(n_slc):
        row = score[mp:mp + 1, :]
        ahead = (row > score) | ((row == score) & (jb > mp))
        rank = rank + jnp.where(ahead, 1.0, 0.0)
    sel_t = jnp.where(rank < float(min(SLC_TOPK, n_slc)), 1.0, 0.0)
    sel_t = jnp.concatenate([sel_t, jnp.zeros((LANES - n_slc, tq), F32)], axis=0).astype(BF16)
    ri = lax.broadcasted_iota(jnp.int32, (tq, tq), 0)
    ci = lax.broadcasted_iota(jnp.int32, (tq, tq), 1)
    eye = jnp.where(ri == ci, 1.0, 0.0).astype(BF16)
    return _dot_nt(eye, sel_t)


def _nsa_combine(o_cmp, o_slc, o_win, gates_ref, ngate_ref, o_ref, tq):
    gates = gates_ref[...]
    for r in range(NSA_GROUP):
        rows = slice(r * tq, (r + 1) * tq)
        g_cmp = gates[:, r:r + 1]
        g_slc = gates[:, NSA_GROUP + r:NSA_GROUP + r + 1]
        g_win = gates[:, 2 * NSA_GROUP + r:2 * NSA_GROUP + r + 1]
        y = g_cmp * o_cmp[rows] + g_slc * o_slc[rows] + g_win * o_win[rows]
        cols = slice(r * HEAD_DIM, (r + 1) * HEAD_DIM)
        o_ref[0, :, cols] = (y * ngate_ref[0, :, cols].astype(F32)).astype(o_ref.dtype)


def _nsa_general(q4, q0, qi, gates_ref, ngate_ref, kcmp_ref, vcmp_ref, ks_ref, vs_ref, kw_ref, vw_ref,
                 ovl_ref, eaug_ref, o_ref, bias_scr, m_scr, l_scr, acc_scr, *, seq):
    tq, tk = NSA_TQ, NSA_TK
    n_slc = seq // SLC_BLOCK

    tpos = q0 + lax.broadcasted_iota(jnp.int32, (tq, LANES), 0)
    cidx = lax.broadcasted_iota(jnp.int32, (tq, LANES), 1)
    cbias = jnp.where(cidx * CMP_STRIDE + (CMP_BLOCK - 1) <= tpos, 0.0, MASKED)
    s = _dot_nt(q4, kcmp_ref[0, 0]) + _rep_heads(cbias)
    m = jnp.maximum(jnp.max(s, axis=-1, keepdims=True), M_INIT)
    e = jnp.exp2(s - m)
    l = jnp.sum(e, axis=-1, keepdims=True)
    p = e * (1.0 / jnp.maximum(l, 1e-30))
    o_cmp = _dot(p.astype(BF16), vcmp_ref[0, 0])

    sel = _select_blocks(p, ovl_ref[...], q0, tq, n_slc).astype(BF16)
    sel_keys = _dot_nt(sel, eaug_ref[...])
    for kt in range(seq // tk):
        bias_scr[kt] = jnp.where(sel_keys[:, kt * tk:(kt + 1) * tk] > 0.5, 0.0, MASKED)

    _flash_init(m_scr, l_scr, acc_scr)

    def slc_body(kt, carry):
        k0 = pl.multiple_of(kt * tk, tk)
        _flash_step(q4, ks_ref[0, pl.ds(k0, tk), :], vs_ref[0, pl.ds(k0, tk), :],
                    _rep_heads(bias_scr[kt]), m_scr, l_scr, acc_scr)
        return carry

    n_full = lax.div(q0, tk)
    lax.fori_loop(0, n_full, slc_body, 0)
    kd = pl.multiple_of(n_full * tk, tk)
    qpos = q0 + lax.broadcasted_iota(jnp.int32, (tq, tk), 0)
    kpos = kd + lax.broadcasted_iota(jnp.int32, (tq, tk), 1)
    causal = jnp.where(kpos <= qpos, 0.0, MASKED)
    _flash_step(q4, ks_ref[0, pl.ds(kd, tk), :], vs_ref[0, pl.ds(kd, tk), :],
                _rep_heads(bias_scr[n_full] + causal), m_scr, l_scr, acc_scr)
    o_slc = _flash_finish(l_scr, acc_scr)

    wk = WINDOW + tq
    ws = pl.multiple_of(jnp.maximum(q0 - WINDOW, 0), tq)
    d = (q0 - ws) + lax.broadcasted_iota(jnp.int32, (tq, wk), 0) - lax.broadcasted_iota(jnp.int32, (tq, wk), 1)
    wbias = jnp.where((d >= 0) & (d < WINDOW), 0.0, MASKED)
    s = _dot_nt(q4, kw_ref[0, pl.ds(ws, wk), :]) + _rep_heads(wbias)
    e = jnp.exp2(s - jnp.max(s, axis=-1, keepdims=True))
    o_win = _dot(e.astype(BF16), vw_ref[0, pl.ds(ws, wk), :]) * (1.0 / jnp.sum(e, axis=-1, keepdims=True))

    _nsa_combine(o_cmp, o_slc, o_win, gates_ref, ngate_ref, o_ref, tq)


def _nsa_bounded(q4, q0, qi, bound, gates_ref, ngate_ref, kcmp_ref, vcmp_ref, ks_ref, vs_ref, kw_ref, vw_ref,
                 ovl_ref, eaug_ref, cmask_ref, wmask_ref, tri_ref, o_ref,
                 ksa_scr, kwa_scr, kca_scr, l_scr, acc_scr, *, seq):
    tq, tk = NSA_TQ, NSA_TK
    n_slc = seq // SLC_BLOCK
    n_cmp_rows = kca_scr.shape[0]

    @pl.when(qi == 0)
    def _():
        one_col = eaug_ref[...]
        lane = lax.broadcasted_iota(jnp.int32, one_col.shape, 1)
        ksa_scr[:, :HEAD_DIM] = ks_ref[0]
        ksa_scr[:, HEAD_DIM:] = one_col
        one_col = jnp.where(lane == AUG_ONE_LANE, one_col, jnp.zeros_like(one_col))
        kwa_scr[:, :HEAD_DIM] = kw_ref[0]
        kwa_scr[:, HEAD_DIM:] = one_col
        kca_scr[:, :HEAD_DIM] = kcmp_ref[0, 0]
        kca_scr[:, HEAD_DIM:] = one_col[:n_cmp_rows]

    lane = lax.broadcasted_iota(jnp.int32, (tq, LANES), 1)
    shift_cols = jnp.where(lane == AUG_ONE_LANE, -bound, 0.0)
    qa = jnp.concatenate([q4, _rep_heads(shift_cols.astype(BF16))], axis=1)

    e = jnp.exp2(_dot_nt(qa, kca_scr[...])) * _rep_heads(cmask_ref[...])
    l = jnp.sum(e, axis=-1, keepdims=True)
    p = e * jnp.where(l > 0.0, 1.0 / l, 0.0)
    o_cmp = _dot(p.astype(BF16), vcmp_ref[0, 0])

    sel = _select_blocks(p, ovl_ref[...], q0, tq, n_slc)
    sel_cols = jnp.where(lane < n_slc, jnp.where(sel > 0.5, 0.0, MASKED), shift_cols)
    qs = jnp.concatenate([q4, _rep_heads(sel_cols.astype(BF16))], axis=1)

    l_scr[...] = jnp.zeros(l_scr.shape, F32)
    acc_scr[...] = jnp.zeros(acc_scr.shape, F32)

    def accumulate(k0, width, mask):
        p_t = jnp.exp2(_dot_nt(qs, ksa_scr[pl.ds(k0, width), :]))
        if mask is not None:
            p_t = p_t * mask
        cols = p_t[:, 0:LANES]
        for c in range(1, width // LANES):
            cols = cols + p_t[:, c * LANES:(c + 1) * LANES]
        l_scr[...] += cols
        acc_scr[...] += _dot(p_t.astype(BF16), vs_ref[0, pl.ds(k0, width), :])

    big = 2 * tk

    def slc_body(kt, carry):
        accumulate(pl.multiple_of(kt * big, big), big, None)
        return carry

    n_full = lax.div(q0, tk)
    n_big = lax.div(n_full, 2)
    lax.fori_loop(0, n_big, slc_body, 0)

    @pl.when(n_full > 2 * n_big)
    def _():
        accumulate(pl.multiple_of(n_big * big, big), tk, None)

    accumulate(pl.multiple_of(n_full * tk, tk), tk, _rep_heads(tri_ref[...]))
    o_slc = _flash_finish(l_scr, acc_scr)

    wk = WINDOW + tq
    ws = pl.multiple_of(jnp.maximum(q0 - WINDOW, 0), tq)
    e = jnp.exp2(_dot_nt(qa, kwa_scr[pl.ds(ws, wk), :])) * _rep_heads(wmask_ref[0])
    o_win = _dot(e.astype(BF16), vw_ref[0, pl.ds(ws, wk), :]) * (1.0 / jnp.sum(e, axis=-1, keepdims=True))

    _nsa_combine(o_cmp, o_slc, o_win, gates_ref, ngate_ref, o_ref, tq)


def _nsa_kernel(bound_ref, q_ref, ngate_ref, gates_ref, kcmp_ref, vcmp_ref, ks_ref, vs_ref, kw_ref, vw_ref,
                ovl_ref, eaug_ref, cmask_ref, wmask_ref, tri_ref, o_ref,
                bias_scr, m_scr, l_scr, acc_scr, ksa_scr, kwa_scr, kca_scr, *, seq):
    qi = pl.program_id(2)
    q0 = qi * NSA_TQ
    q4 = _stack_heads(q_ref[0])
    bound = bound_ref[0]

    @pl.when(bound <= MAX_SCORE_BOUND)
    def _():
        _nsa_bounded(q4, q0, qi, bound, gates_ref, ngate_ref, kcmp_ref, vcmp_ref, ks_ref, vs_ref, kw_ref, vw_ref,
                     ovl_ref, eaug_ref, cmask_ref, wmask_ref, tri_ref, o_ref,
                     ksa_scr, kwa_scr, kca_scr, l_scr, acc_scr, seq=seq)

    @pl.when(bound > MAX_SCORE_BOUND)
    def _():
        _nsa_general(q4, q0, qi, gates_ref, ngate_ref, kcmp_ref, vcmp_ref, ks_ref, vs_ref, kw_ref, vw_ref,
                     ovl_ref, eaug_ref, o_ref, bias_scr, m_scr, l_scr, acc_scr, seq=seq)


def _nsa_attention(proj, gates, kcmp, vcmp, bound, tables, offs):
    b, s, _ = proj.shape
    g = NSA_KV_HEADS
    tq = NSA_TQ
    nq = s // tq
    gw = NSA_GROUP * HEAD_DIM
    n_cmp_rows = s // CMP_STRIDE
    ovl_t, eaug, cmask, wmask, tri = tables
    q_spec = lambda off: pl.BlockSpec((1, tq, gw), lambda bi, gi, qi: (bi, qi, off + gi))
    kv_spec = lambda off: pl.BlockSpec((1, s, HEAD_DIM), lambda bi, gi, qi: (bi, 0, off + gi))
    cmp_spec = pl.BlockSpec((1, 1, n_cmp_rows, HEAD_DIM), lambda bi, gi, qi: (bi, gi, 0, 0))
    full = lambda shape: pl.BlockSpec(shape, lambda bi, gi, qi: (0,) * len(shape))
    n_wpat = wmask.shape[0]
    rows4 = NSA_GROUP * tq
    return pl.pallas_call(
        functools.partial(_nsa_kernel, seq=s),
        grid=(b, g, nq),
        in_specs=[pl.BlockSpec(memory_space=pltpu.SMEM),
                  q_spec(offs["nq"] // gw), q_spec(offs["ngate"] // gw),
                  pl.BlockSpec((tq, LANES), lambda bi, gi, qi: (bi * nq + qi, gi)),
                  cmp_spec, cmp_spec,
                  kv_spec(offs["ks"] // HEAD_DIM), kv_spec(offs["vs"] // HEAD_DIM),
                  kv_spec(offs["kw"] // HEAD_DIM), kv_spec(offs["vw"] // HEAD_DIM),
                  full(ovl_t.shape), full(eaug.shape),
                  pl.BlockSpec((tq, LANES), lambda bi, gi, qi: (qi, 0)),
                  pl.BlockSpec((1,) + wmask.shape[1:], lambda bi, gi, qi: (jnp.minimum(qi, n_wpat - 1), 0, 0)),
                  full(tri.shape)],
        out_specs=pl.BlockSpec((1, tq, gw), lambda bi, gi, qi: (bi, qi, gi)),
        out_shape=jax.ShapeDtypeStruct((b, s, NSA_WIDTH), BF16),
        scratch_shapes=[pltpu.VMEM((s // NSA_TK, tq, NSA_TK), F32),
                        pltpu.VMEM((rows4, LANES), F32), pltpu.VMEM((rows4, LANES), F32),
                        pltpu.VMEM((rows4, HEAD_DIM), F32),
                        pltpu.VMEM((s, 2 * HEAD_DIM), BF16), pltpu.VMEM((s, 2 * HEAD_DIM), BF16),
                        pltpu.VMEM((n_cmp_rows, 2 * HEAD_DIM), BF16)],
        compiler_params=pltpu.CompilerParams(
            dimension_semantics=("parallel", "parallel", "arbitrary"), vmem_limit_bytes=VMEM_LIMIT),
        name="nsa_attention",
    )(bound, proj, proj, gates, kcmp, vcmp, proj, proj, proj, proj, ovl_t, eaug, cmask, wmask, tri)


def _diff_kernel(q_ref, gate_ref, k_ref, v_ref, lam_ref, o_ref, m1, l1, a1, m2, l2, a2, *, lambda_init):
    tq, tk = DIFF_TQ, DIFF_TK
    qi = pl.program_id(2)
    q = q_ref[0]
    q1 = q[:, :HEAD_DIM]
    q2 = q[:, HEAD_DIM:]
    _flash_init(m1, l1, a1)
    _flash_init(m2, l2, a2)

    def step(rows, k0, width, bias):
        k = k_ref[0, pl.ds(k0, width), :]
        v = v_ref[0, pl.ds(k0, width), :]
        _flash_step(q1[rows], k[:, :HEAD_DIM], v, bias, m1.at[rows], l1.at[rows], a1.at[rows])
        _flash_step(q2[rows], k[:, HEAD_DIM:], v, bias, m2.at[rows], l2.at[rows], a2.at[rows])

    def body(kt, carry):
        step(slice(0, tq), pl.multiple_of(kt * tk, tk), tk, None)
        return carry

    lax.fori_loop(0, qi * (tq // tk), body, 0)
    dd = DIFF_DIAG
    q0 = qi * tq
    for c in range(tq // dd):
        n_rows = tq - c * dd
        ri = lax.broadcasted_iota(jnp.int32, (n_rows, dd), 0)
        ci = lax.broadcasted_iota(jnp.int32, (n_rows, dd), 1)
        step(slice(c * dd, tq), pl.multiple_of(q0 + c * dd, dd), dd, jnp.where(ri >= ci, 0.0, MASKED))

    lp = lam_ref[...]
    lam = (jnp.exp(jnp.sum(lp[0:1] * lp[1:2], axis=-1, keepdims=True))
           - jnp.exp(jnp.sum(lp[2:3] * lp[3:4], axis=-1, keepdims=True)) + lambda_init)
    o = _flash_finish(l1, a1) - lam * _flash_finish(l2, a2)
    o = _rms(o) * (1.0 - lambda_init)
    o_ref[0] = (o * gate_ref[0].astype(F32)).astype(o_ref.dtype)


def _diff_attention(proj, lam_params, lambda_init):
    b, s, _ = proj.shape
    h = DIFF_HEADS
    tq = DIFF_TQ
    w = DIFF_V_DIM
    q_spec = lambda off: pl.BlockSpec((1, tq, w), lambda bi, hi, qi: (bi, qi, off + hi))
    kv_spec = lambda off: pl.BlockSpec((1, s, w), lambda bi, hi, qi: (bi, 0, off + hi))
    stat = pltpu.VMEM((tq, LANES), F32)
    acc = pltpu.VMEM((tq, w), F32)
    return pl.pallas_call(
        functools.partial(_diff_kernel, lambda_init=lambda_init),
        grid=(b, h, s // tq),
        in_specs=[q_spec(0), q_spec(3 * h), kv_spec(h), kv_spec(2 * h),
                  pl.BlockSpec(lam_params.shape, lambda bi, hi, qi: (0, 0))],
        out_specs=pl.BlockSpec((1, tq, w), lambda bi, hi, qi: (bi, qi, hi)),
        out_shape=jax.ShapeDtypeStruct((b, s, DIFF_WIDTH), BF16),
        scratch_shapes=[stat, stat, acc, stat, stat, acc],
        compiler_params=pltpu.CompilerParams(
            dimension_semantics=("parallel", "parallel", "arbitrary"), vmem_limit_bytes=VMEM_LIMIT),
        name="diff_attention",
    )(proj, proj, proj, proj, lam_params)


def _rope_tables(pos):
    inv = 1.0 / (ROPE_THETA ** (jnp.arange(0, HEAD_DIM, 2, dtype=F32) / HEAD_DIM))
    ang = pos.astype(F32)[:, None] * inv[None, :]
    cos, sin = jnp.cos(ang), jnp.sin(ang)
    return jnp.concatenate([cos, cos], axis=-1), jnp.concatenate([-sin, sin], axis=-1)


def _retention_tables():
    h, c = RET_HEADS, RET_CHUNK
    log_g = jnp.log1p(-jnp.exp2(-5.0 - jnp.arange(h, dtype=F32)))
    j = jnp.arange(c, dtype=F32)
    diff = j[:, None] - j[None, :]
    intra = jnp.where(diff >= 0, jnp.exp(log_g[:, None, None] * jnp.maximum(diff, 0.0)), 0.0)
    q_dec = jnp.exp(log_g[:, None] * (j + 1.0))
    k_dec = jnp.exp(log_g[:, None] * (c - 1.0 - j))
    chunk_dec = jnp.exp(log_g * c)
    wide = lambda t: jnp.broadcast_to(t[:, :, None], (h, t.shape[1], HEAD_DIM))
    return intra, wide(q_dec), wide(k_dec), wide(chunk_dec[:, None])


def _selection_tables(seq):
    tq, tk = NSA_TQ, NSA_TK
    n_cmp_rows = seq // CMP_STRIDE
    n_slc = seq // SLC_BLOCK
    assert tq == tk and n_slc <= AUG_ONE_LANE < LANES
    c_start = np.arange(n_cmp_rows) * CMP_STRIDE
    s_start = np.arange(n_slc) * SLC_BLOCK
    overlap_t = ((c_start[None, :] <= s_start[:, None] + SLC_BLOCK - 1)
                 & (c_start[None, :] + CMP_BLOCK - 1 >= s_start[:, None]))
    lane = np.arange(LANES)[None, :]
    key = np.arange(seq)[:, None]
    eaug = ((key // SLC_BLOCK) == lane) | (lane == AUG_ONE_LANE)
    cmask = lane * CMP_STRIDE + CMP_BLOCK - 1 <= key
    r = np.arange(tq)[:, None]
    c = np.arange(WINDOW + tq)[None, :]
    wmask = []
    for pat in range(WINDOW // tq + 1):
        d = min(pat * tq, WINDOW) + r - c
        wmask.append((d >= 0) & (d < WINDOW))
    tri = np.arange(tq)[:, None] >= np.arange(tk)[None, :]
    return (jnp.asarray(overlap_t, BF16), jnp.asarray(eaug, BF16), jnp.asarray(cmask, F32),
            jnp.asarray(np.stack(wmask), F32), jnp.asarray(tri, F32))


def _tile_modes(segments, tn):
    off, kinds, col = {}, [], 0
    for name, width, kind in segments:
        off[name] = col
        kinds += [kind] * (width // LANES)
        col += width
    per = tn // LANES
    tiles = [kinds[i:i + per] for i in range(0, len(kinds), per)]
    assert col % tn == 0
    modes = []
    for j, tile in enumerate(tiles):
        if modes and modes[-1][2] == tile:
            modes[-1] = (modes[-1][0], j + 1, tile)
        else:
            modes.append((j, j + 1, tile))
    return modes, off


def _l0_tile_modes():
    plain, silu = ("plain", 0), ("silu", 0)
    k_norm = ("nrope", L0_TAB_K)
    return _tile_modes([
        ("rq", RET_WIDTH, ("rope", L0_TAB_ROPE)), ("rk", RET_WIDTH, ("rope", L0_TAB_ROPE_SCALED)),
        ("rv", RET_WIDTH, plain), ("rgate", RET_WIDTH, silu), ("nq", NSA_WIDTH, ("nrope", L0_TAB_Q)),
        ("kc", NSA_KV_WIDTH, plain), ("vc", NSA_KV_WIDTH, plain), ("ks", NSA_KV_WIDTH, k_norm),
        ("vs", NSA_KV_WIDTH, plain), ("kw", NSA_KV_WIDTH, k_norm), ("vw", NSA_KV_WIDTH, plain),
        ("ngate", NSA_WIDTH, silu)], PROJ_TN)


def _l1_tile_modes():
    return _tile_modes([("q", DIFF_WIDTH, ("nrope", L1_TAB_Q)), ("k", DIFF_WIDTH, ("nrope", L1_TAB_K)),
                        ("v", DIFF_WIDTH, ("plain", 0)), ("gate", DIFF_WIDTH, ("silu", 0))], L1_PROJ_TN)[0]


def _rope_pair(cos, sin_signed, gain=None, scale=1.0):
    if gain is None:
        return [cos * scale, sin_signed * scale]
    return [cos * (gain * scale)[None, :], sin_signed * (jnp.roll(gain, HALF) * scale)[None, :]]


def kernel(x, l0_norm_g, l0_w_in, l0_w_out, l0_nsa_q_norm_g, l0_nsa_k_norm_g, l0_cmp_pe_k, l0_cmp_w1_k, l0_cmp_w2_k, l0_cmp_pe_v, l0_cmp_w1_v, l0_cmp_w2_v, l1_norm_g, l1_w_in, l1_w_out, l1_q_norm_g, l1_k_norm_g, l1_lambda_q1, l1_lambda_k1, l1_lambda_q2, l1_lambda_k2):
    b, s, d = x.shape
    m = b * s
    x2d = x.reshape(m, d)
    cos, sin = _rope_tables(jnp.arange(s))
    cos_c, sin_c = _rope_tables(jnp.arange(s // CMP_STRIDE) * CMP_STRIDE + CMP_BLOCK - 1)
    intra, qdec, kdec, cdec = _retention_tables()
    nsa_tables = _selection_tables(s)

    modes0, off = _l0_tile_modes()
    w0 = l0_w_in.astype(BF16)
    wg = l0_w_in[:, AB_MAIN_COLS:].reshape(d, 3, NSA_KV_HEADS, NSA_GROUP).transpose(0, 2, 1, 3)
    wg = wg.reshape(d, NSA_KV_HEADS, 3 * NSA_GROUP)
    wg = jnp.pad(wg, ((0, 0), (0, 0), (0, LANES - 3 * NSA_GROUP))).reshape(d, NSA_KV_HEADS * LANES).astype(BF16)
    tabs0 = jnp.stack(_rope_pair(cos, sin) + _rope_pair(cos, sin, scale=QK_SCALE)
                      + _rope_pair(cos, sin, l0_nsa_q_norm_g, Q_SCALE) + _rope_pair(cos, sin, l0_nsa_k_norm_g))
    proj0, gates = _norm_proj(x2d, l0_norm_g, w0, tabs0, modes0, s, PROJ_TN, wg=wg)
    proj0 = proj0.reshape(b, s, AB_MAIN_COLS)
    y_ret = _retention(proj0, intra, qdec, kdec, cdec)
    w1k = l0_cmp_w1_k.astype(BF16).reshape(CMP_BLOCK, HEAD_DIM, HEAD_DIM)
    w1v = l0_cmp_w1_v.astype(BF16).reshape(CMP_BLOCK, HEAD_DIM, HEAD_DIM)
    kcmp, vcmp = _nsa_compress(proj0, off["kc"] // HEAD_DIM, off["vc"] // HEAD_DIM,
                               l0_cmp_pe_k, w1k, l0_cmp_w2_k.astype(BF16),
                               l0_cmp_pe_v, w1v, l0_cmp_w2_v.astype(BF16),
                               l0_nsa_k_norm_g, cos_c, sin_c)
    bound = (BOUND_MARGIN * HEAD_DIM * Q_SCALE * jnp.max(jnp.abs(l0_nsa_q_norm_g))
             * jnp.max(jnp.abs(l0_nsa_k_norm_g))).reshape(1).astype(F32)
    y_nsa = _nsa_attention(proj0, gates, kcmp, vcmp, bound, nsa_tables, off)
    w_out0 = l0_w_out.astype(BF16)
    x1 = _out_proj([y_ret.reshape(m, RET_WIDTH), y_nsa.reshape(m, NSA_WIDTH)],
                   [w_out0[:RET_WIDTH], w_out0[RET_WIDTH:]], x2d)

    lambda_init = 0.8 - 0.6 * math.exp(-0.3 * 1)
    tabs1 = jnp.stack(_rope_pair(cos, sin, l1_q_norm_g, Q_SCALE) + _rope_pair(cos, sin, l1_k_norm_g))
    proj1 = _norm_proj(x1, l1_norm_g, l1_w_in.astype(BF16), tabs1, _l1_tile_modes(), s, L1_PROJ_TN)[0]
    lam_params = jnp.stack([l1_lambda_q1, l1_lambda_k1, l1_lambda_q2, l1_lambda_k2]).astype(F32)
    y1 = _diff_attention(proj1.reshape(b, s, C_IN_COLS), lam_params, lambda_init)
    out = _out_proj([y1.reshape(m, DIFF_WIDTH)], [l1_w_out.astype(BF16)], x1)
    return out.reshape(b, s, d)
```

```python
import functools
import math

import numpy as np
import jax
import jax.numpy as jnp
from jax import lax
from jax.experimental import pallas as pl
from jax.experimental.pallas import tpu as pltpu

F32 = jnp.float32
BF16 = jnp.bfloat16

D_MODEL = 2048
HEAD_DIM = 128
HALF = HEAD_DIM // 2
ROPE_THETA = 10000.0
EPS = 1e-6
RET_HEADS = 8
RET_CHUNK = 256
NSA_HEADS = 8
NSA_KV_HEADS = 2
NSA_GROUP = NSA_HEADS // NSA_KV_HEADS
CMP_BLOCK = 32
CMP_STRIDE = 16
SLC_BLOCK = 64
SLC_TOPK = 16
N_LOCAL_BLOCKS = 2
WINDOW = 512
DIFF_HEADS = 8
DIFF_V_DIM = 2 * HEAD_DIM
QK_SCALE = HEAD_DIM ** -0.5
LOG2E = math.log2(math.e)
Q_SCALE = QK_SCALE * LOG2E

RET_WIDTH = RET_HEADS * HEAD_DIM
NSA_WIDTH = NSA_HEADS * HEAD_DIM
NSA_KV_WIDTH = NSA_KV_HEADS * HEAD_DIM
AB_MAIN_COLS = 4 * RET_WIDTH + 2 * NSA_WIDTH + 6 * NSA_KV_WIDTH
N_GATE_COLS = 3 * NSA_HEADS
DIFF_WIDTH = DIFF_HEADS * DIFF_V_DIM
C_IN_COLS = 4 * DIFF_WIDTH

LANES = 128
MASKED = -1e30
M_INIT = -1e29
MAX_SCORE_BOUND = 60.0
BOUND_MARGIN = 1.05
AUG_ONE_LANE = 32
VMEM_LIMIT = 56 * 1024 * 1024

PROJ_TM = 1024
PROJ_TN = 1280
L1_PROJ_TN = 1024
OPROJ_TN = 1024
PROJ_SUB = 256
L0_TAB_ROPE, L0_TAB_ROPE_SCALED, L0_TAB_Q, L0_TAB_K = range(4)
L1_TAB_Q, L1_TAB_K = range(2)
NSA_TQ = 512
NSA_TK = 512
DIFF_TQ = 512
DIFF_TK = 512
DIFF_DIAG = 256


def _dot(a, b):
    return jnp.dot(a, b, preferred_element_type=F32)


def _dot_nt(a, b):
    return lax.dot_general(a, b, (((1,), (1,)), ((), ())), preferred_element_type=F32)


def _dot_tn(a, b):
    return lax.dot_general(a, b, (((0,), (0,)), ((), ())), preferred_element_type=F32)


def _silu(x):
    return x / (1.0 + jnp.exp(-x))


def _rms(x):
    return x * lax.rsqrt(jnp.mean(x * x, axis=-1, keepdims=True) + EPS)


def _rope(x, cos, sin_signed):
    return x * cos + pltpu.roll(x, HALF, axis=1) * sin_signed


def _rope_tab(seg, t, tabs_ref, rows):
    return seg * tabs_ref[2 * t, rows, :] + pltpu.roll(seg, HALF, axis=1) * tabs_ref[2 * t + 1, rows, :]


def _tile_epilogue(acc, kinds, tabs_ref, rows, o_ref):
    n_seg = len(kinds)
    ri = lax.broadcasted_iota(jnp.int32, (2 * LANES, 2 * LANES), 0)
    ci = lax.broadcasted_iota(jnp.int32, (2 * LANES, 2 * LANES), 1)
    pair_mean = jnp.where((ri < LANES) == (ci < LANES), 1.0 / HEAD_DIM, 0.0).astype(BF16)
    c = 0
    while c < n_seg:
        op, t = kinds[c]
        cols = slice(c * LANES, (c + 1) * LANES)
        if op == "nrope" and c + 1 < n_seg and kinds[c + 1][0] == "nrope":
            cols2 = slice(c * LANES, (c + 2) * LANES)
            seg2 = acc[:, cols2]
            seg2 = seg2 * lax.rsqrt(_dot((seg2 * seg2).astype(BF16), pair_mean) + EPS)
            for half in range(2):
                out = _rope_tab(seg2[:, half * LANES:(half + 1) * LANES], kinds[c + half][1], tabs_ref, rows)
                o_ref[rows, (c + half) * LANES:(c + half + 1) * LANES] = out.astype(o_ref.dtype)
            c += 2
            continue
        seg = acc[:, cols]
        if op == "silu":
            seg = _silu(seg)
        elif op == "nrope":
            seg = seg * lax.rsqrt(_dot((seg * seg).astype(BF16), pair_mean[:LANES, :LANES]) + EPS)
            seg = _rope_tab(seg, t, tabs_ref, rows)
        elif op == "rope":
            seg = _rope_tab(seg, t, tabs_ref, rows)
        o_ref[rows, cols] = seg.astype(o_ref.dtype)
        c += 1


def _proj_kernel(*refs, tile_modes, has_gates):
    if has_gates:
        x_ref, g_ref, w_ref, tabs_ref, wg_ref, o_ref, og_ref, h_scr = refs
    else:
        x_ref, g_ref, w_ref, tabs_ref, o_ref, h_scr = refs
    j = pl.program_id(1)
    n_sub = h_scr.shape[0] // PROJ_SUB

    def make_branch(cond, kinds, first):
        @pl.when(cond)
        def _():
            for r in range(n_sub):
                rows = slice(r * PROJ_SUB, (r + 1) * PROJ_SUB)
                if first:
                    h = (_rms(x_ref[rows, :]) * g_ref[...]).astype(BF16)
                    h_scr[rows, :] = h
                    if has_gates:
                        og_ref[rows, :] = 1.0 / (1.0 + jnp.exp(-_dot(h, wg_ref[...])))
                else:
                    h = h_scr[rows, :]
                _tile_epilogue(_dot(h, w_ref[...]), kinds, tabs_ref, rows, o_ref)

    for lo, hi, kinds in tile_modes:
        if lo == 0:
            make_branch(j == 0, kinds, True)
            lo = 1
        if hi > lo:
            make_branch((j >= lo) & (j < hi), kinds, False)


def _norm_proj(x2d, g, w, tabs, tile_modes, seq, tn, wg=None):
    m, d = x2d.shape
    tm = PROJ_TM
    n = tile_modes[-1][1] * tn
    s_tiles = seq // tm
    has_gates = wg is not None
    in_specs = [
        pl.BlockSpec((tm, d), lambda i, j: (i, 0)),
        pl.BlockSpec((1, d), lambda i, j: (0, 0)),
        pl.BlockSpec((d, tn), lambda i, j: (0, j)),
        pl.BlockSpec((tabs.shape[0], tm, LANES), lambda i, j: (0, i % s_tiles, 0)),
    ]
    args = [x2d, g.reshape(1, d), w, tabs]
    out_shape = [jax.ShapeDtypeStruct((m, n), BF16)]
    out_specs = [pl.BlockSpec((tm, tn), lambda i, j: (i, j))]
    if has_gates:
        ng = wg.shape[1]
        in_specs.append(pl.BlockSpec((d, ng), lambda i, j: (0, 0)))
        args.append(wg)
        out_shape.append(jax.ShapeDtypeStruct((m, ng), F32))
        out_specs.append(pl.BlockSpec((tm, ng), lambda i, j: (i, 0)))
    return pl.pallas_call(
        functools.partial(_proj_kernel, tile_modes=tile_modes, has_gates=has_gates),
        grid=(m // tm, n // tn),
        in_specs=in_specs,
        out_specs=out_specs,
        out_shape=out_shape,
        scratch_shapes=[pltpu.VMEM((tm, d), BF16)],
        compiler_params=pltpu.CompilerParams(
            dimension_semantics=("parallel", "arbitrary"), vmem_limit_bytes=VMEM_LIMIT),
        name="norm_proj",
    )(*args)


def _oproj_kernel(*refs, n_terms):
    y_refs, w_refs = refs[:n_terms], refs[n_terms:2 * n_terms]
    x_ref, o_ref = refs[2 * n_terms:]
    acc = x_ref[...]
    for y_ref, w_ref in zip(y_refs, w_refs):
        acc = acc + _dot(y_ref[...], w_ref[...])
    o_ref[...] = acc


def _out_proj(ys, ws, x2d):
    m, n = x2d.shape
    tm, tn = PROJ_TM, OPROJ_TN
    y_specs = [pl.BlockSpec((tm, y.shape[1]), lambda i, j: (i, 0)) for y in ys]
    w_specs = [pl.BlockSpec((w.shape[0], tn), lambda i, j: (0, j)) for w in ws]
    return pl.pallas_call(
        functools.partial(_oproj_kernel, n_terms=len(ys)),
        grid=(m // tm, n // tn),
        in_specs=y_specs + w_specs + [pl.BlockSpec((tm, tn), lambda i, j: (i, j))],
        out_specs=pl.BlockSpec((tm, tn), lambda i, j: (i, j)),
        out_shape=jax.ShapeDtypeStruct((m, n), F32),
        compiler_params=pltpu.CompilerParams(
            dimension_semantics=("parallel", "arbitrary"), vmem_limit_bytes=VMEM_LIMIT),
        name="out_proj",
    )(*ys, *ws, x2d)


def _ret_kernel(q_ref, k_ref, v_ref, gate_ref, intra_ref, qdec_ref, kdec_ref, cdec_ref, o_ref, *, n_chunks):
    c = RET_CHUNK
    intra = intra_ref[0]
    qdec = qdec_ref[0]
    kdec = kdec_ref[0]
    cdec = cdec_ref[0]
    state = jnp.zeros((HEAD_DIM, HEAD_DIM), F32)
    for n in range(n_chunks):
        rows = slice(n * c, (n + 1) * c)
        q = q_ref[0, rows, :]
        k = k_ref[0, rows, :]
        v = v_ref[0, rows, :]
        scores = _dot_nt(q, k) * intra
        inner = _dot(scores.astype(BF16), v)
        cross = _dot((q.astype(F32) * qdec).astype(BF16), state.astype(BF16))
        kv = _dot_tn((k.astype(F32) * kdec).astype(BF16), v)
        state = state * cdec + kv
        o = _rms(inner + cross) * gate_ref[0, rows, :].astype(F32)
        o_ref[0, rows, :] = o.astype(o_ref.dtype)


def _retention(proj, intra, qdec, kdec, cdec):
    b, s, _ = proj.shape
    h = RET_HEADS
    y_shape = (b, s, RET_WIDTH)
    head = lambda off: pl.BlockSpec((1, s, HEAD_DIM), lambda bi, hi: (bi, 0, off + hi))
    table = lambda rows, cols=HEAD_DIM: pl.BlockSpec((1, rows, cols), lambda bi, hi: (hi, 0, 0))
    return pl.pallas_call(
        functools.partial(_ret_kernel, n_chunks=s // RET_CHUNK),
        grid=(b, h),
        in_specs=[head(0), head(h), head(2 * h), head(3 * h),
                  table(RET_CHUNK, RET_CHUNK), table(RET_CHUNK), table(RET_CHUNK), table(1)],
        out_specs=pl.BlockSpec((1, s, HEAD_DIM), lambda bi, hi: (bi, 0, hi)),
        out_shape=jax.ShapeDtypeStruct(y_shape, BF16),
        compiler_params=pltpu.CompilerParams(
            dimension_semantics=("parallel", "arbitrary"), vmem_limit_bytes=VMEM_LIMIT),
        name="retention",
    )(proj, proj, proj, proj, intra, qdec, kdec, cdec)


def _cmp_kernel(kc_ref, vc_ref, pek_ref, w1k_ref, w2k_ref, pev_ref, w1v_ref, w2v_ref, kg_ref,
                cos_ref, sin_ref, ko_ref, vo_ref, t_scr, *, seq):
    n_rows = seq // CMP_STRIDE

    def compress(src_ref, pe_ref, w1_ref, w2_ref):
        t_scr[0:seq, :] = src_ref[0].astype(F32)
        t_scr[seq:seq + CMP_STRIDE, :] = jnp.zeros((CMP_STRIDE, HEAD_DIM), F32)
        acc = jnp.zeros((n_rows, HEAD_DIM), F32)
        for r in range(CMP_BLOCK):
            rows = t_scr[pl.ds(r, n_rows, stride=CMP_STRIDE), :] + pe_ref[r:r + 1, :]
            acc = acc + _dot(rows.astype(BF16), w1_ref[r])
        return _dot(_silu(acc).astype(BF16), w2_ref[...])

    kc = compress(kc_ref, pek_ref, w1k_ref, w2k_ref)
    kc = _rope(_rms(kc) * kg_ref[...], cos_ref[...], sin_ref[...])
    ko_ref[0, 0] = kc.astype(ko_ref.dtype)
    vo_ref[0, 0] = compress(vc_ref, pev_ref, w1v_ref, w2v_ref).astype(vo_ref.dtype)


def _nsa_compress(proj, kc_off, vc_off, pe_k, w1_k, w2_k, pe_v, w1_v, w2_v, k_g, cos_c, sin_c):
    b, s, _ = proj.shape
    g = NSA_KV_HEADS
    n_rows = s // CMP_STRIDE
    head = lambda off: pl.BlockSpec((1, s, HEAD_DIM), lambda bi, gi: (bi, 0, off + gi))
    full = lambda shape: pl.BlockSpec(shape, lambda bi, gi: (0,) * len(shape))
    out_spec = pl.BlockSpec((1, 1, n_rows, HEAD_DIM), lambda bi, gi: (bi, gi, 0, 0))
    out_sds = jax.ShapeDtypeStruct((b, g, n_rows, HEAD_DIM), BF16)
    return pl.pallas_call(
        functools.partial(_cmp_kernel, seq=s),
        grid=(b, g),
        in_specs=[head(kc_off), head(vc_off),
                  full((CMP_BLOCK, HEAD_DIM)), full((CMP_BLOCK, HEAD_DIM, HEAD_DIM)), full((HEAD_DIM, HEAD_DIM)),
                  full((CMP_BLOCK, HEAD_DIM)), full((CMP_BLOCK, HEAD_DIM, HEAD_DIM)), full((HEAD_DIM, HEAD_DIM)),
                  full((1, HEAD_DIM)), full((n_rows, HEAD_DIM)), full((n_rows, HEAD_DIM))],
        out_specs=[out_spec, out_spec],
        out_shape=[out_sds, out_sds],
        scratch_shapes=[pltpu.VMEM((s + CMP_STRIDE, HEAD_DIM), F32)],
        compiler_params=pltpu.CompilerParams(
            dimension_semantics=("parallel", "arbitrary"), vmem_limit_bytes=VMEM_LIMIT),
        name="nsa_compress",
    )(proj, proj, pe_k, w1_k, w2_k, pe_v, w1_v, w2_v, k_g.reshape(1, HEAD_DIM), cos_c, sin_c)


def _flash_init(m_scr, l_scr, acc_scr):
    m_scr[...] = jnp.full(m_scr.shape, M_INIT, F32)
    l_scr[...] = jnp.zeros(l_scr.shape, F32)
    acc_scr[...] = jnp.zeros(acc_scr.shape, F32)


def _lane_tiles(x, width):
    return x if width == LANES else jnp.concatenate([x] * (width // LANES), axis=1)


def _flash_step(q, k, v, bias, m_scr, l_scr, acc_scr):
    s = _dot_nt(q, k)
    if bias is not None:
        s = s + bias
    tk = s.shape[1]
    m_old = m_scr[...]
    m_new = jnp.maximum(m_old, jnp.max(s, axis=-1, keepdims=True))
    alpha = jnp.exp2(m_old - m_new)
    p = jnp.exp2(s - _lane_tiles(m_new, tk))
    p_cols = p[:, 0:LANES]
    for c in range(1, tk // LANES):
        p_cols = p_cols + p[:, c * LANES:(c + 1) * LANES]
    l_scr[...] = alpha * l_scr[...] + p_cols
    acc_scr[...] = _lane_tiles(alpha, acc_scr.shape[1]) * acc_scr[...] + _dot(p.astype(BF16), v)
    m_scr[...] = m_new


def _flash_finish(l_scr, acc_scr):
    return acc_scr[...] * (1.0 / jnp.sum(l_scr[...], axis=-1, keepdims=True))


def _rep_heads(bias):
    return jnp.concatenate([bias] * NSA_GROUP, axis=0)


def _stack_heads(qblk):
    return jnp.concatenate([qblk[:, r * HEAD_DIM:(r + 1) * HEAD_DIM] for r in range(NSA_GROUP)], axis=0)


def _select_blocks(p, ovl, q0, tq, n_slc):
    psum = p[0:tq]
    for r in range(1, NSA_GROUP):
        psum = psum + p[r * tq:(r + 1) * tq]
    p_hi = psum.astype(BF16)
    rem = psum - p_hi.astype(F32)
    p_mid = rem.astype(BF16)
    p_lo = (rem - p_mid.astype(F32)).astype(BF16)
    imp = _dot_nt(ovl, p_hi) + _dot_nt(ovl, p_mid) + _dot_nt(ovl, p_lo)
    jb = lax.broadcasted_iota(jnp.int32, (n_slc, tq), 0)
    blk_t = jnp.right_shift(q0 + lax.broadcasted_iota(jnp.int32, (n_slc, tq), 1), int(math.log2(SLC_BLOCK)))
    back = blk_t - jb
    forced = (jb == 0) | ((back >= 0) & (back < N_LOCAL_BLOCKS))
    score = jnp.where(forced, 1e9, jnp.where(back >= 0, imp, -1e9))
    rank = jnp.zeros((n_slc, tq), F32)
    for mp in range(n_slc):
        row = score[mp:mp + 1, :]
        ahead = (row > score) | ((row == score) & (jb > mp))
        rank = rank + jnp.where(ahead, 1.0, 0.0)
    sel_t = jnp.where(rank < float(min(SLC_TOPK, n_slc)), 1.0, 0.0)
    sel_t = jnp.concatenate([sel_t, jnp.zeros((LANES - n_slc, tq), F32)], axis=0).astype(BF16)
    ri = lax.broadcasted_iota(jnp.int32, (tq, tq), 0)
    ci = lax.broadcasted_iota(jnp.int32, (tq, tq), 1)
    eye = jnp.where(ri == ci, 1.0, 0.0).astype(BF16)
    return _dot_nt(eye, sel_t)


def _nsa_combine(o_cmp, o_slc, o_win, gates_ref, ngate_ref, o_ref, tq):
    gates = gates_ref[...]
    for r in range(NSA_GROUP):
        rows = slice(r * tq, (r + 1) * tq)
        g_cmp = gates[:, r:r + 1]
        g_slc = gates[:, NSA_GROUP + r:NSA_GROUP + r + 1]
        g_win = gates[:, 2 * NSA_GROUP + r:2 * NSA_GROUP + r + 1]
        y = g_cmp * o_cmp[rows] + g_slc * o_slc[rows] + g_win * o_win[rows]
        cols = slice(r * HEAD_DIM, (r + 1) * HEAD_DIM)
        o_ref[0, :, cols] = (y * ngate_ref[0, :, cols].astype(F32)).astype(o_ref.dtype)


def _nsa_general(q4, q0, qi, gates_ref, ngate_ref, kcmp_ref, vcmp_ref, ks_ref, vs_ref, kw_ref, vw_ref,
                 ovl_ref, eaug_ref, o_ref, bias_scr, m_scr, l_scr, acc_scr, *, seq):
    tq, tk = NSA_TQ, NSA_TK
    n_slc = seq // SLC_BLOCK

    tpos = q0 + lax.broadcasted_iota(jnp.int32, (tq, LANES), 0)
    cidx = lax.broadcasted_iota(jnp.int32, (tq, LANES), 1)
    cbias = jnp.where(cidx * CMP_STRIDE + (CMP_BLOCK - 1) <= tpos, 0.0, MASKED)
    s = _dot_nt(q4, kcmp_ref[0, 0]) + _rep_heads(cbias)
    m = jnp.maximum(jnp.max(s, axis=-1, keepdims=True), M_INIT)
    e = jnp.exp2(s - m)
    l = jnp.sum(e, axis=-1, keepdims=True)
    p = e * (1.0 / jnp.maximum(l, 1e-30))
    o_cmp = _dot(p.astype(BF16), vcmp_ref[0, 0])

    sel = _select_blocks(p, ovl_ref[...], q0, tq, n_slc).astype(BF16)
    sel_keys = _dot_nt(sel, eaug_ref[...])
    for kt in range(seq // tk):
        bias_scr[kt] = jnp.where(sel_keys[:, kt * tk:(kt + 1) * tk] > 0.5, 0.0, MASKED)

    _flash_init(m_scr, l_scr, acc_scr)

    def slc_body(kt, carry):
        k0 = pl.multiple_of(kt * tk, tk)
        _flash_step(q4, ks_ref[0, pl.ds(k0, tk), :], vs_ref[0, pl.ds(k0, tk), :],
                    _rep_heads(bias_scr[kt]), m_scr, l_scr, acc_scr)
        return carry

    n_full = lax.div(q0, tk)
    lax.fori_loop(0, n_full, slc_body, 0)
    kd = pl.multiple_of(n_full * tk, tk)
    qpos = q0 + lax.broadcasted_iota(jnp.int32, (tq, tk), 0)
    kpos = kd + lax.broadcasted_iota(jnp.int32, (tq, tk), 1)
    causal = jnp.where(kpos <= qpos, 0.0, MASKED)
    _flash_step(q4, ks_ref[0, pl.ds(kd, tk), :], vs_ref[0, pl.ds(kd, tk), :],
                _rep_heads(bias_scr[n_full] + causal), m_scr, l_scr, acc_scr)
    o_slc = _flash_finish(l_scr, acc_scr)

    wk = WINDOW + tq
    ws = pl.multiple_of(jnp.maximum(q0 - WINDOW, 0), tq)
    d = (q0 - ws) + lax.broadcasted_iota(jnp.int32, (tq, wk), 0) - lax.broadcasted_iota(jnp.int32, (tq, wk), 1)
    wbias = jnp.where((d >= 0) & (d < WINDOW), 0.0, MASKED)
    s = _dot_nt(q4, kw_ref[0, pl.ds(ws, wk), :]) + _rep_heads(wbias)
    e = jnp.exp2(s - jnp.max(s, axis=-1, keepdims=True))
    o_win = _dot(e.astype(BF16), vw_ref[0, pl.ds(ws, wk), :]) * (1.0 / jnp.sum(e, axis=-1, keepdims=True))

    _nsa_combine(o_cmp, o_slc, o_win, gates_ref, ngate_ref, o_ref, tq)


def _nsa_bounded(q4, q0, qi, bound, gates_ref, ngate_ref, kcmp_ref, vcmp_ref, ks_ref, vs_ref, kw_ref, vw_ref,
                 ovl_ref, eaug_ref, cmask_ref, wmask_ref, tri_ref, o_ref,
                 ksa_scr, kwa_scr, kca_scr, l_scr, acc_scr, *, seq):
    tq, tk = NSA_TQ, NSA_TK
    n_slc = seq // SLC_BLOCK
    n_cmp_rows = kca_scr.shape[0]

    @pl.when(qi == 0)
    def _():
        one_col = eaug_ref[...]
        lane = lax.broadcasted_iota(jnp.int32, one_col.shape, 1)
        ksa_scr[:, :HEAD_DIM] = ks_ref[0]
        ksa_scr[:, HEAD_DIM:] = one_col
        one_col = jnp.where(lane == AUG_ONE_LANE, one_col, jnp.zeros_like(one_col))
        kwa_scr[:, :HEAD_DIM] = kw_ref[0]
        kwa_scr[:, HEAD_DIM:] = one_col
        kca_scr[:, :HEAD_DIM] = kcmp_ref[0, 0]
        kca_scr[:, HEAD_DIM:] = one_col[:n_cmp_rows]

    lane = lax.broadcasted_iota(jnp.int32, (tq, LANES), 1)
    shift_cols = jnp.where(lane == AUG_ONE_LANE, -bound, 0.0)
    qa = jnp.concatenate([q4, _rep_heads(shift_cols.astype(BF16))], axis=1)

    e = jnp.exp2(_dot_nt(qa, kca_scr[...])) * _rep_heads(cmask_ref[...])
    l = jnp.sum(e, axis=-1, keepdims=True)
    p = e * jnp.where(l > 0.0, 1.0 / l, 0.0)
    o_cmp = _dot(p.astype(BF16), vcmp_ref[0, 0])

    sel = _select_blocks(p, ovl_ref[...], q0, tq, n_slc)
    sel_cols = jnp.where(lane < n_slc, jnp.where(sel > 0.5, 0.0, MASKED), shift_cols)
    qs = jnp.concatenate([q4, _rep_heads(sel_cols.astype(BF16))], axis=1)

    l_scr[...] = jnp.zeros(l_scr.shape, F32)
    acc_scr[...] = jnp.zeros(acc_scr.shape, F32)

    def accumulate(k0, width, mask):
        p_t = jnp.exp2(_dot_nt(qs, ksa_scr[pl.ds(k0, width), :]))
        if mask is not None:
            p_t = p_t * mask
        cols = p_t[:, 0:LANES]
        for c in range(1, width // LANES):
            cols = cols + p_t[:, c * LANES:(c + 1) * LANES]
        l_scr[...] += cols
        acc_scr[...] += _dot(p_t.astype(BF16), vs_ref[0, pl.ds(k0, width), :])

    big = 2 * tk

    def slc_body(kt, carry):
        accumulate(pl.multiple_of(kt * big, big), big, None)
        return carry

    n_full = lax.div(q0, tk)
    n_big = lax.div(n_full, 2)
    lax.fori_loop(0, n_big, slc_body, 0)

    @pl.when(n_full > 2 * n_big)
    def _():
        accumulate(pl.multiple_of(n_big * big, big), tk, None)

    accumulate(pl.multiple_of(n_full * tk, tk), tk, _rep_heads(tri_ref[...]))
    o_slc = _flash_finish(l_scr, acc_scr)

    wk = WINDOW + tq
    ws = pl.multiple_of(jnp.maximum(q0 - WINDOW, 0), tq)
    e = jnp.exp2(_dot_nt(qa, kwa_scr[pl.ds(ws, wk), :])) * _rep_heads(wmask_ref[0])
    o_win = _dot(e.astype(BF16), vw_ref[0, pl.ds(ws, wk), :]) * (1.0 / jnp.sum(e, axis=-1, keepdims=True))

    _nsa_combine(o_cmp, o_slc, o_win, gates_ref, ngate_ref, o_ref, tq)


def _nsa_kernel(bound_ref, q_ref, ngate_ref, gates_ref, kcmp_ref, vcmp_ref, ks_ref, vs_ref, kw_ref, vw_ref,
                ovl_ref, eaug_ref, cmask_ref, wmask_ref, tri_ref, o_ref,
                bias_scr, m_scr, l_scr, acc_scr, ksa_scr, kwa_scr, kca_scr, *, seq):
    qi = pl.program_id(2)
    q0 = qi * NSA_TQ
    q4 = _stack_heads(q_ref[0])
    bound = bound_ref[0]

    @pl.when(bound <= MAX_SCORE_BOUND)
    def _():
        _nsa_bounded(q4, q0, qi, bound, gates_ref, ngate_ref, kcmp_ref, vcmp_ref, ks_ref, vs_ref, kw_ref, vw_ref,
                     ovl_ref, eaug_ref, cmask_ref, wmask_ref, tri_ref, o_ref,
                     ksa_scr, kwa_scr, kca_scr, l_scr, acc_scr, seq=seq)

    @pl.when(bound > MAX_SCORE_BOUND)
    def _():
        _nsa_general(q4, q0, qi, gates_ref, ngate_ref, kcmp_ref, vcmp_ref, ks_ref, vs_ref, kw_ref, vw_ref,
                     ovl_ref, eaug_ref, o_ref, bias_scr, m_scr, l_scr, acc_scr, seq=seq)


def _nsa_attention(proj, gates, kcmp, vcmp, bound, tables, offs):
    b, s, _ = proj.shape
    g = NSA_KV_HEADS
    tq = NSA_TQ
    nq = s // tq
    gw = NSA_GROUP * HEAD_DIM
    n_cmp_rows = s // CMP_STRIDE
    ovl_t, eaug, cmask, wmask, tri = tables
    q_spec = lambda off: pl.BlockSpec((1, tq, gw), lambda bi, gi, qi: (bi, qi, off + gi))
    kv_spec = lambda off: pl.BlockSpec((1, s, HEAD_DIM), lambda bi, gi, qi: (bi, 0, off + gi))
    cmp_spec = pl.BlockSpec((1, 1, n_cmp_rows, HEAD_DIM), lambda bi, gi, qi: (bi, gi, 0, 0))
    full = lambda shape: pl.BlockSpec(shape, lambda bi, gi, qi: (0,) * len(shape))
    n_wpat = wmask.shape[0]
    rows4 = NSA_GROUP * tq
    return pl.pallas_call(
        functools.partial(_nsa_kernel, seq=s),
        grid=(b, g, nq),
        in_specs=[pl.BlockSpec(memory_space=pltpu.SMEM),
                  q_spec(offs["nq"] // gw), q_spec(offs["ngate"] // gw),
                  pl.BlockSpec((tq, LANES), lambda bi, gi, qi: (bi * nq + qi, gi)),
                  cmp_spec, cmp_spec,
                  kv_spec(offs["ks"] // HEAD_DIM), kv_spec(offs["vs"] // HEAD_DIM),
                  kv_spec(offs["kw"] // HEAD_DIM), kv_spec(offs["vw"] // HEAD_DIM),
                  full(ovl_t.shape), full(eaug.shape),
                  pl.BlockSpec((tq, LANES), lambda bi, gi, qi: (qi, 0)),
                  pl.BlockSpec((1,) + wmask.shape[1:], lambda bi, gi, qi: (jnp.minimum(qi, n_wpat - 1), 0, 0)),
                  full(tri.shape)],
        out_specs=pl.BlockSpec((1, tq, gw), lambda bi, gi, qi: (bi, qi, gi)),
        out_shape=jax.ShapeDtypeStruct((b, s, NSA_WIDTH), BF16),
        scratch_shapes=[pltpu.VMEM((s // NSA_TK, tq, NSA_TK), F32),
                        pltpu.VMEM((rows4, LANES), F32), pltpu.VMEM((rows4, LANES), F32),
                        pltpu.VMEM((rows4, HEAD_DIM), F32),
                        pltpu.VMEM((s, 2 * HEAD_DIM), BF16), pltpu.VMEM((s, 2 * HEAD_DIM), BF16),
                        pltpu.VMEM((n_cmp_rows, 2 * HEAD_DIM), BF16)],
        compiler_params=pltpu.CompilerParams(
            dimension_semantics=("parallel", "parallel", "arbitrary"), vmem_limit_bytes=VMEM_LIMIT),
        name="nsa_attention",
    )(bound, proj, proj, gates, kcmp, vcmp, proj, proj, proj, proj, ovl_t, eaug, cmask, wmask, tri)


def _diff_kernel(q_ref, gate_ref, k_ref, v_ref, lam_ref, o_ref, m1, l1, a1, m2, l2, a2, *, lambda_init):
    tq, tk = DIFF_TQ, DIFF_TK
    qi = pl.program_id(2)
    q = q_ref[0]
    q1 = q[:, :HEAD_DIM]
    q2 = q[:, HEAD_DIM:]
    _flash_init(m1, l1, a1)
    _flash_init(m2, l2, a2)

    def step(rows, k0, width, bias):
        k = k_ref[0, pl.ds(k0, width), :]
        v = v_ref[0, pl.ds(k0, width), :]
        _flash_step(q1[rows], k[:, :HEAD_DIM], v, bias, m1.at[rows], l1.at[rows], a1.at[rows])
        _flash_step(q2[rows], k[:, HEAD_DIM:], v, bias, m2.at[rows], l2.at[rows], a2.at[rows])

    def body(kt, carry):
        step(slice(0, tq), pl.multiple_of(kt * tk, tk), tk, None)
        return carry

    lax.fori_loop(0, qi * (tq // tk), body, 0)
    dd = DIFF_DIAG
    q0 = qi * tq
    for c in range(tq // dd):
        n_rows = tq - c * dd
        ri = lax.broadcasted_iota(jnp.int32, (n_rows, dd), 0)
        ci = lax.broadcasted_iota(jnp.int32, (n_rows, dd), 1)
        step(slice(c * dd, tq), pl.multiple_of(q0 + c * dd, dd), dd, jnp.where(ri >= ci, 0.0, MASKED))

    lp = lam_ref[...]
    lam = (jnp.exp(jnp.sum(lp[0:1] * lp[1:2], axis=-1, keepdims=True))
           - jnp.exp(jnp.sum(lp[2:3] * lp[3:4], axis=-1, keepdims=True)) + lambda_init)
    o = _flash_finish(l1, a1) - lam * _flash_finish(l2, a2)
    o = _rms(o) * (1.0 - lambda_init)
    o_ref[0] = (o * gate_ref[0].astype(F32)).astype(o_ref.dtype)


def _diff_attention(proj, lam_params, lambda_init):
    b, s, _ = proj.shape
    h = DIFF_HEADS
    tq = DIFF_TQ
    w = DIFF_V_DIM
    q_spec = lambda off: pl.BlockSpec((1, tq, w), lambda bi, hi, qi: (bi, qi, off + hi))
    kv_spec = lambda off: pl.BlockSpec((1, s, w), lambda bi, hi, qi: (bi, 0, off + hi))
    stat = pltpu.VMEM((tq, LANES), F32)
    acc = pltpu.VMEM((tq, w), F32)
    return pl.pallas_call(
        functools.partial(_diff_kernel, lambda_init=lambda_init),
        grid=(b, h, s // tq),
        in_specs=[q_spec(0), q_spec(3 * h), kv_spec(h), kv_spec(2 * h),
                  pl.BlockSpec(lam_params.shape, lambda bi, hi, qi: (0, 0))],
        out_specs=pl.BlockSpec((1, tq, w), lambda bi, hi, qi: (bi, qi, hi)),
        out_shape=jax.ShapeDtypeStruct((b, s, DIFF_WIDTH), BF16),
        scratch_shapes=[stat, stat, acc, stat, stat, acc],
        compiler_params=pltpu.CompilerParams(
            dimension_semantics=("parallel", "parallel", "arbitrary"), vmem_limit_bytes=VMEM_LIMIT),
        name="diff_attention",
    )(proj, proj, proj, proj, lam_params)


def _rope_tables(pos):
    inv = 1.0 / (ROPE_THETA ** (jnp.arange(0, HEAD_DIM, 2, dtype=F32) / HEAD_DIM))
    ang = pos.astype(F32)[:, None] * inv[None, :]
    cos, sin = jnp.cos(ang), jnp.sin(ang)
    return jnp.concatenate([cos, cos], axis=-1), jnp.concatenate([-sin, sin], axis=-1)


def _retention_tables():
    h, c = RET_HEADS, RET_CHUNK
    log_g = jnp.log1p(-jnp.exp2(-5.0 - jnp.arange(h, dtype=F32)))
    j = jnp.arange(c, dtype=F32)
    diff = j[:, None] - j[None, :]
    intra = jnp.where(diff >= 0, jnp.exp(log_g[:, None, None] * jnp.maximum(diff, 0.0)), 0.0)
    q_dec = jnp.exp(log_g[:, None] * (j + 1.0))
    k_dec = jnp.exp(log_g[:, None] * (c - 1.0 - j))
    chunk_dec = jnp.exp(log_g * c)
    wide = lambda t: jnp.broadcast_to(t[:, :, None], (h, t.shape[1], HEAD_DIM))
    return intra, wide(q_dec), wide(k_dec), wide(chunk_dec[:, None])


def _selection_tables(seq):
    tq, tk = NSA_TQ, NSA_TK
    n_cmp_rows = seq // CMP_STRIDE
    n_slc = seq // SLC_BLOCK
    assert tq == tk and n_slc <= AUG_ONE_LANE < LANES
    c_start = np.arange(n_cmp_rows) * CMP_STRIDE
    s_start = np.arange(n_slc) * SLC_BLOCK
    overlap_t = ((c_start[None, :] <= s_start[:, None] + SLC_BLOCK - 1)
                 & (c_start[None, :] + CMP_BLOCK - 1 >= s_start[:, None]))
    lane = np.arange(LANES)[None, :]
    key = np.arange(seq)[:, None]
    eaug = ((key // SLC_BLOCK) == lane) | (lane == AUG_ONE_LANE)
    cmask = lane * CMP_STRIDE + CMP_BLOCK - 1 <= key
    r = np.arange(tq)[:, None]
    c = np.arange(WINDOW + tq)[None, :]
    wmask = []
    for pat in range(WINDOW // tq + 1):
        d = min(pat * tq, WINDOW) + r - c
        wmask.append((d >= 0) & (d < WINDOW))
    tri = np.arange(tq)[:, None] >= np.arange(tk)[None, :]
    return (jnp.asarray(overlap_t, BF16), jnp.asarray(eaug, BF16), jnp.asarray(cmask, F32),
            jnp.asarray(np.stack(wmask), F32), jnp.asarray(tri, F32))


def _tile_modes(segments, tn):
    off, kinds, col = {}, [], 0
    for name, width, kind in segments:
        off[name] = col
        kinds += [kind] * (width // LANES)
        col += width
    per = tn // LANES
    tiles = [kinds[i:i + per] for i in range(0, len(kinds), per)]
    assert col % tn == 0
    modes = []
    for j, tile in enumerate(tiles):
        if modes and modes[-1][2] == tile:
            modes[-1] = (modes[-1][0], j + 1, tile)
        else:
            modes.append((j, j + 1, tile))
    return modes, off


def _l0_tile_modes():
    plain, silu = ("plain", 0), ("silu", 0)
    k_norm = ("nrope", L0_TAB_K)
    return _tile_modes([
        ("rq", RET_WIDTH, ("rope", L0_TAB_ROPE)), ("rk", RET_WIDTH, ("rope", L0_TAB_ROPE_SCALED)),
        ("rv", RET_WIDTH, plain), ("rgate", RET_WIDTH, silu), ("nq", NSA_WIDTH, ("nrope", L0_TAB_Q)),
        ("kc", NSA_KV_WIDTH, plain), ("vc", NSA_KV_WIDTH, plain), ("ks", NSA_KV_WIDTH, k_norm),
        ("vs", NSA_KV_WIDTH, plain), ("kw", NSA_KV_WIDTH, k_norm), ("vw", NSA_KV_WIDTH, plain),
        ("ngate", NSA_WIDTH, silu)], PROJ_TN)


def _l1_tile_modes():
    return _tile_modes([("q", DIFF_WIDTH, ("nrope", L1_TAB_Q)), ("k", DIFF_WIDTH, ("nrope", L1_TAB_K)),
                        ("v", DIFF_WIDTH, ("plain", 0)), ("gate", DIFF_WIDTH, ("silu", 0))], L1_PROJ_TN)[0]


def _rope_pair(cos, sin_signed, gain=None, scale=1.0):
    if gain is None:
        return [cos * scale, sin_signed * scale]
    return [cos * (gain * scale)[None, :], sin_signed * (jnp.roll(gain, HALF) * scale)[None, :]]


def kernel(x, l0_norm_g, l0_w_in, l0_w_out, l0_nsa_q_norm_g, l0_nsa_k_norm_g, l0_cmp_pe_k, l0_cmp_w1_k, l0_cmp_w2_k, l0_cmp_pe_v, l0_cmp_w1_v, l0_cmp_w2_v, l1_norm_g, l1_w_in, l1_w_out, l1_q_norm_g, l1_k_norm_g, l1_lambda_q1, l1_lambda_k1, l1_lambda_q2, l1_lambda_k2):
    b, s, d = x.shape
    m = b * s
    x2d = x.reshape(m, d)
    cos, sin = _rope_tables(jnp.arange(s))
    cos_c, sin_c = _rope_tables(jnp.arange(s // CMP_STRIDE) * CMP_STRIDE + CMP_BLOCK - 1)
    intra, qdec, kdec, cdec = _retention_tables()
    nsa_tables = _selection_tables(s)

    modes0, off = _l0_tile_modes()
    w0 = l0_w_in.astype(BF16)
    wg = l0_w_in[:, AB_MAIN_COLS:].reshape(d, 3, NSA_KV_HEADS, NSA_GROUP).transpose(0, 2, 1, 3)
    wg = wg.reshape(d, NSA_KV_HEADS, 3 * NSA_GROUP)
    wg = jnp.pad(wg, ((0, 0), (0, 0), (0, LANES - 3 * NSA_GROUP))).reshape(d, NSA_KV_HEADS * LANES).astype(BF16)
    tabs0 = jnp.stack(_rope_pair(cos, sin) + _rope_pair(cos, sin, scale=QK_SCALE)
                      + _rope_pair(cos, sin, l0_nsa_q_norm_g, Q_SCALE) + _rope_pair(cos, sin, l0_nsa_k_norm_g))
    proj0, gates = _norm_proj(x2d, l0_norm_g, w0, tabs0, modes0, s, PROJ_TN, wg=wg)
    proj0 = proj0.reshape(b, s, AB_MAIN_COLS)
    y_ret = _retention(proj0, intra, qdec, kdec, cdec)
    w1k = l0_cmp_w1_k.astype(BF16).reshape(CMP_BLOCK, HEAD_DIM, HEAD_DIM)
    w1v = l0_cmp_w1_v.astype(BF16).reshape(CMP_BLOCK, HEAD_DIM, HEAD_DIM)
    kcmp, vcmp = _nsa_compress(proj0, off["kc"] // HEAD_DIM, off["vc"] // HEAD_DIM,
                               l0_cmp_pe_k, w1k, l0_cmp_w2_k.astype(BF16),
                               l0_cmp_pe_v, w1v, l0_cmp_w2_v.astype(BF16),
                               l0_nsa_k_norm_g, cos_c, sin_c)
    bound = (BOUND_MARGIN * HEAD_DIM * Q_SCALE * jnp.max(jnp.abs(l0_nsa_q_norm_g))
             * jnp.max(jnp.abs(l0_nsa_k_norm_g))).reshape(1).astype(F32)
    y_nsa = _nsa_attention(proj0, gates, kcmp, vcmp, bound, nsa_tables, off)
    w_out0 = l0_w_out.astype(BF16)
    x1 = _out_proj([y_ret.reshape(m, RET_WIDTH), y_nsa.reshape(m, NSA_WIDTH)],
                   [w_out0[:RET_WIDTH], w_out0[RET_WIDTH:]], x2d)

    lambda_init = 0.8 - 0.6 * math.exp(-0.3 * 1)
    tabs1 = jnp.stack(_rope_pair(cos, sin, l1_q_norm_g, Q_SCALE) + _rope_pair(cos, sin, l1_k_norm_g))
    proj1 = _norm_proj(x1, l1_norm_g, l1_w_in.astype(BF16), tabs1, _l1_tile_modes(), s, L1_PROJ_TN)[0]
    lam_params = jnp.stack([l1_lambda_q1, l1_lambda_k1, l1_lambda_q2, l1_lambda_k2]).astype(F32)
    y1 = _diff_attention(proj1.reshape(b, s, C_IN_COLS), lam_params, lambda_init)
    out = _out_proj([y1.reshape(m, DIFF_WIDTH)], [l1_w_out.astype(BF16)], x1)
    return out.reshape(b, s, d)
```

```python
import functools
import math

import numpy as np
import jax
import jax.numpy as jnp
from jax import lax
from jax.experimental import pallas as pl
from jax.experimental.pallas import tpu as pltpu

F32 = jnp.float32
BF16 = jnp.bfloat16

D_MODEL = 2048
HEAD_DIM = 128
HALF = HEAD_DIM // 2
ROPE_THETA = 10000.0
EPS = 1e-6
RET_HEADS = 8
RET_CHUNK = 256
NSA_HEADS = 8
NSA_KV_HEADS = 2
NSA_GROUP = NSA_HEADS // NSA_KV_HEADS
CMP_BLOCK = 32
CMP_STRIDE = 16
SLC_BLOCK = 64
SLC_TOPK = 16
N_LOCAL_BLOCKS = 2
WINDOW = 512
DIFF_HEADS = 8
DIFF_V_DIM = 2 * HEAD_DIM
QK_SCALE = HEAD_DIM ** -0.5
LOG2E = math.log2(math.e)
Q_SCALE = QK_SCALE * LOG2E

RET_WIDTH = RET_HEADS * HEAD_DIM
NSA_WIDTH = NSA_HEADS * HEAD_DIM
NSA_KV_WIDTH = NSA_KV_HEADS * HEAD_DIM
AB_MAIN_COLS = 4 * RET_WIDTH + 2 * NSA_WIDTH + 6 * NSA_KV_WIDTH
N_GATE_COLS = 3 * NSA_HEADS
DIFF_WIDTH = DIFF_HEADS * DIFF_V_DIM
C_IN_COLS = 4 * DIFF_WIDTH

LANES = 128
MASKED = -1e30
M_INIT = -1e29
MAX_SCORE_BOUND = 60.0
BOUND_MARGIN = 1.05
AUG_ONE_LANE = 32
VMEM_LIMIT = 56 * 1024 * 1024

PROJ_TM = 1024
PROJ_TN = 1280
L1_PROJ_TN = 1024
OPROJ_TN = 1024
PROJ_SUB = 256
L0_TAB_ROPE, L0_TAB_ROPE_SCALED, L0_TAB_Q, L0_TAB_K = range(4)
L1_TAB_Q, L1_TAB_K = range(2)
NSA_TQ = 512
NSA_TK = 512
DIFF_TQ = 512
DIFF_TK = 512
DIFF_DIAG = 256


def _dot(a, b):
    return jnp.dot(a, b, preferred_element_type=F32)


def _dot_nt(a, b):
    return lax.dot_general(a, b, (((1,), (1,)), ((), ())), preferred_element_type=F32)


def _dot_tn(a, b):
    return lax.dot_general(a, b, (((0,), (0,)), ((), ())), preferred_element_type=F32)


def _silu(x):
    return x / (1.0 + jnp.exp(-x))


def _rms(x):
    return x * lax.rsqrt(jnp.mean(x * x, axis=-1, keepdims=True) + EPS)


def _rope(x, cos, sin_signed):
    return x * cos + pltpu.roll(x, HALF, axis=1) * sin_signed


def _rope_tab(seg, t, tabs_ref, rows):
    return seg * tabs_ref[2 * t, rows, :] + pltpu.roll(seg, HALF, axis=1) * tabs_ref[2 * t + 1, rows, :]


def _tile_epilogue(acc, kinds, tabs_ref, rows, o_ref):
    n_seg = len(kinds)
    ri = lax.broadcasted_iota(jnp.int32, (2 * LANES, 2 * LANES), 0)
    ci = lax.broadcasted_iota(jnp.int32, (2 * LANES, 2 * LANES), 1)
    pair_mean = jnp.where((ri < LANES) == (ci < LANES), 1.0 / HEAD_DIM, 0.0).astype(BF16)
    c = 0
    while c < n_seg:
        op, t = kinds[c]
        cols = slice(c * LANES, (c + 1) * LANES)
        if op == "nrope" and c + 1 < n_seg and kinds[c + 1][0] == "nrope":
            cols2 = slice(c * LANES, (c + 2) * LANES)
            seg2 = acc[:, cols2]
            seg2 = seg2 * lax.rsqrt(_dot((seg2 * seg2).astype(BF16), pair_mean) + EPS)
            for half in range(2):
                out = _rope_tab(seg2[:, half * LANES:(half + 1) * LANES], kinds[c + half][1], tabs_ref, rows)
                o_ref[rows, (c + half) * LANES:(c + half + 1) * LANES] = out.astype(o_ref.dtype)
            c += 2
            continue
        seg = acc[:, cols]
        if op == "silu":
            seg = _silu(seg)
        elif op == "nrope":
            seg = seg * lax.rsqrt(_dot((seg * seg).astype(BF16), pair_mean[:LANES, :LANES]) + EPS)
            seg = _rope_tab(seg, t, tabs_ref, rows)
        elif op == "rope":
            seg = _rope_tab(seg, t, tabs_ref, rows)
        o_ref[rows, cols] = seg.astype(o_ref.dtype)
        c += 1


def _proj_kernel(*refs, tile_modes, has_gates):
    if has_gates:
        x_ref, g_ref, w_ref, tabs_ref, wg_ref, o_ref, og_ref, h_scr = refs
    else:
        x_ref, g_ref, w_ref, tabs_ref, o_ref, h_scr = refs
    j = pl.program_id(1)
    n_sub = h_scr.shape[0] // PROJ_SUB

    def make_branch(cond, kinds, first):
        @pl.when(cond)
        def _():
            for r in range(n_sub):
                rows = slice(r * PROJ_SUB, (r + 1) * PROJ_SUB)
                if first:
                    h = (_rms(x_ref[rows, :]) * g_ref[...]).astype(BF16)
                    h_scr[rows, :] = h
                    if has_gates:
                        og_ref[rows, :] = 1.0 / (1.0 + jnp.exp(-_dot(h, wg_ref[...])))
                else:
                    h = h_scr[rows, :]
                _tile_epilogue(_dot(h, w_ref[...]), kinds, tabs_ref, rows, o_ref)

    for lo, hi, kinds in tile_modes:
        if lo == 0:
            make_branch(j == 0, kinds, True)
            lo = 1
        if hi > lo:
            make_branch((j >= lo) & (j < hi), kinds, False)


def _norm_proj(x2d, g, w, tabs, tile_modes, seq, tn, wg=None):
    m, d = x2d.shape
    tm = PROJ_TM
    n = tile_modes[-1][1] * tn
    s_tiles = seq // tm
    has_gates = wg is not None
    in_specs = [
        pl.BlockSpec((tm, d), lambda i, j: (i, 0)),
        pl.BlockSpec((1, d), lambda i, j: (0, 0)),
        pl.BlockSpec((d, tn), lambda i, j: (0, j)),
        pl.BlockSpec((tabs.shape[0], tm, LANES), lambda i, j: (0, i % s_tiles, 0)),
    ]
    args = [x2d, g.reshape(1, d), w, tabs]
    out_shape = [jax.ShapeDtypeStruct((m, n), BF16)]
    out_specs = [pl.BlockSpec((tm, tn), lambda i, j: (i, j))]
    if has_gates:
        ng = wg.shape[1]
        in_specs.append(pl.BlockSpec((d, ng), lambda i, j: (0, 0)))
        args.append(wg)
        out_shape.append(jax.ShapeDtypeStruct((m, ng), F32))
        out_specs.append(pl.BlockSpec((tm, ng), lambda i, j: (i, 0)))
    return pl.pallas_call(
        functools.partial(_proj_kernel, tile_modes=tile_modes, has_gates=has_gates),
        grid=(m // tm, n // tn),
        in_specs=in_specs,
        out_specs=out_specs,
        out_shape=out_shape,
        scratch_shapes=[pltpu.VMEM((tm, d), BF16)],
        compiler_params=pltpu.CompilerParams(
            dimension_semantics=("parallel", "arbitrary"), vmem_limit_bytes=VMEM_LIMIT),
        name="norm_proj",
    )(*args)


def _oproj_kernel(*refs, n_terms):
    y_refs, w_refs = refs[:n_terms], refs[n_terms:2 * n_terms]
    x_ref, o_ref = refs[2 * n_terms:]
    acc = x_ref[...]
    for y_ref, w_ref in zip(y_refs, w_refs):
        acc = acc + _dot(y_ref[...], w_ref[...])
    o_ref[...] = acc


def _out_proj(ys, ws, x2d):
    m, n = x2d.shape
    tm, tn = PROJ_TM, OPROJ_TN
    y_specs = [pl.BlockSpec((tm, y.shape[1]), lambda i, j: (i, 0)) for y in ys]
    w_specs = [pl.BlockSpec((w.shape[0], tn), lambda i, j: (0, j)) for w in ws]
    return pl.pallas_call(
        functools.partial(_oproj_kernel, n_terms=len(ys)),
        grid=(m // tm, n // tn),
        in_specs=y_specs + w_specs + [pl.BlockSpec((tm, tn), lambda i, j: (i, j))],
        out_specs=pl.BlockSpec((tm, tn), lambda i, j: (i, j)),
        out_shape=jax.ShapeDtypeStruct((m, n), F32),
        compiler_params=pltpu.CompilerParams(
            dimension_semantics=("parallel", "arbitrary"), vmem_limit_bytes=VMEM_LIMIT),
        name="out_proj",
    )(*ys, *ws, x2d)


def _ret_kernel(q_ref, k_ref, v_ref, gate_ref, intra_ref, qdec_ref, kdec_ref, cdec_ref, o_ref, *, n_chunks):
    c = RET_CHUNK
    intra = intra_ref[0]
    qdec = qdec_ref[0]
    kdec = kdec_ref[0]
    cdec = cdec_ref[0]
    state = jnp.zeros((HEAD_DIM, HEAD_DIM), F32)
    for n in range(n_chunks):
        rows = slice(n * c, (n + 1) * c)
        q = q_ref[0, rows, :]
        k = k_ref[0, rows, :]
        v = v_ref[0, rows, :]
        scores = _dot_nt(q, k) * intra
        inner = _dot(scores.astype(BF16), v)
        cross = _dot((q.astype(F32) * qdec).astype(BF16), state.astype(BF16))
        kv = _dot_tn((k.astype(F32) * kdec).astype(BF16), v)
        state = state * cdec + kv
        o = _rms(inner + cross) * gate_ref[0, rows, :].astype(F32)
        o_ref[0, rows, :] = o.astype(o_ref.dtype)


def _retention(proj, intra, qdec, kdec, cdec):
    b, s, _ = proj.shape
    h = RET_HEADS
    y_shape = (b, s, RET_WIDTH)
    head = lambda off: pl.BlockSpec((1, s, HEAD_DIM), lambda bi, hi: (bi, 0, off + hi))
    table = lambda rows, cols=HEAD_DIM: pl.BlockSpec((1, rows, cols), lambda bi, hi: (hi, 0, 0))
    return pl.pallas_call(
        functools.partial(_ret_kernel, n_chunks=s // RET_CHUNK),
        grid=(b, h),
        in_specs=[head(0), head(h), head(2 * h), head(3 * h),
                  table(RET_CHUNK, RET_CHUNK), table(RET_CHUNK), table(RET_CHUNK), table(1)],
        out_specs=pl.BlockSpec((1, s, HEAD_DIM), lambda bi, hi: (bi, 0, hi)),
        out_shape=jax.ShapeDtypeStruct(y_shape, BF16),
        compiler_params=pltpu.CompilerParams(
            dimension_semantics=("parallel", "arbitrary"), vmem_limit_bytes=VMEM_LIMIT),
        name="retention",
    )(proj, proj, proj, proj, intra, qdec, kdec, cdec)


def _cmp_kernel(kc_ref, vc_ref, pek_ref, w1k_ref, w2k_ref, pev_ref, w1v_ref, w2v_ref, kg_ref,
                cos_ref, sin_ref, ko_ref, vo_ref, t_scr, *, seq):
    n_rows = seq // CMP_STRIDE

    def compress(src_ref, pe_ref, w1_ref, w2_ref):
        t_scr[0:seq, :] = src_ref[0].astype(F32)
        t_scr[seq:seq + CMP_STRIDE, :] = jnp.zeros((CMP_STRIDE, HEAD_DIM), F32)
        acc = jnp.zeros((n_rows, HEAD_DIM), F32)
        for r in range(CMP_BLOCK):
            rows = t_scr[pl.ds(r, n_rows, stride=CMP_STRIDE), :] + pe_ref[r:r + 1, :]
            acc = acc + _dot(rows.astype(BF16), w1_ref[r])
        return _dot(_silu(acc).astype(BF16), w2_ref[...])

    kc = compress(kc_ref, pek_ref, w1k_ref, w2k_ref)
    kc = _rope(_rms(kc) * kg_ref[...], cos_ref[...], sin_ref[...])
    ko_ref[0, 0] = kc.astype(ko_ref.dtype)
    vo_ref[0, 0] = compress(vc_ref, pev_ref, w1v_ref, w2v_ref).astype(vo_ref.dtype)


def _nsa_compress(proj, kc_off, vc_off, pe_k, w1_k, w2_k, pe_v, w1_v, w2_v, k_g, cos_c, sin_c):
    b, s, _ = proj.shape
    g = NSA_KV_HEADS
    n_rows = s // CMP_STRIDE
    head = lambda off: pl.BlockSpec((1, s, HEAD_DIM), lambda bi, gi: (bi, 0, off + gi))
    full = lambda shape: pl.BlockSpec(shape, lambda bi, gi: (0,) * len(shape))
    out_spec = pl.BlockSpec((1, 1, n_rows, HEAD_DIM), lambda bi, gi: (bi, gi, 0, 0))
    out_sds = jax.ShapeDtypeStruct((b, g, n_rows, HEAD_DIM), BF16)
    return pl.pallas_call(
        functools.partial(_cmp_kernel, seq=s),
        grid=(b, g),
        in_specs=[head(kc_off), head(vc_off),
                  full((CMP_BLOCK, HEAD_DIM)), full((CMP_BLOCK, HEAD_DIM, HEAD_DIM)), full((HEAD_DIM, HEAD_DIM)),
                  full((CMP_BLOCK, HEAD_DIM)), full((CMP_BLOCK, HEAD_DIM, HEAD_DIM)), full((HEAD_DIM, HEAD_DIM)),
                  full((1, HEAD_DIM)), full((n_rows, HEAD_DIM)), full((n_rows, HEAD_DIM))],
        out_specs=[out_spec, out_spec],
        out_shape=[out_sds, out_sds],
        scratch_shapes=[pltpu.VMEM((s + CMP_STRIDE, HEAD_DIM), F32)],
        compiler_params=pltpu.CompilerParams(
            dimension_semantics=("parallel", "arbitrary"), vmem_limit_bytes=VMEM_LIMIT),
        name="nsa_compress",
    )(proj, proj, pe_k, w1_k, w2_k, pe_v, w1_v, w2_v, k_g.reshape(1, HEAD_DIM), cos_c, sin_c)


def _flash_init(m_scr, l_scr, acc_scr):
    m_scr[...] = jnp.full(m_scr.shape, M_INIT, F32)
    l_scr[...] = jnp.zeros(l_scr.shape, F32)
    acc_scr[...] = jnp.zeros(acc_scr.shape, F32)


def _lane_tiles(x, width):
    return x if width == LANES else jnp.concatenate([x] * (width // LANES), axis=1)


def _flash_step(q, k, v, bias, m_scr, l_scr, acc_scr):
    s = _dot_nt(q, k)
    if bias is not None:
        s = s + bias
    tk = s.shape[1]
    m_old = m_scr[...]
    m_new = jnp.maximum(m_old, jnp.max(s, axis=-1, keepdims=True))
    alpha = jnp.exp2(m_old - m_new)
    p = jnp.exp2(s - _lane_tiles(m_new, tk))
    p_cols = p[:, 0:LANES]
    for c in range(1, tk // LANES):
        p_cols = p_cols + p[:, c * LANES:(c + 1) * LANES]
    l_scr[...] = alpha * l_scr[...] + p_cols
    acc_scr[...] = _lane_tiles(alpha, acc_scr.shape[1]) * acc_scr[...] + _dot(p.astype(BF16), v)
    m_scr[...] = m_new


def _flash_finish(l_scr, acc_scr):
    return acc_scr[...] * (1.0 / jnp.sum(l_scr[...], axis=-1, keepdims=True))


def _rep_heads(bias):
    return jnp.concatenate([bias] * NSA_GROUP, axis=0)


def _stack_heads(qblk):
    return jnp.concatenate([qblk[:, r * HEAD_DIM:(r + 1) * HEAD_DIM] for r in range(NSA_GROUP)], axis=0)


def _select_blocks(p, ovl, q0, tq, n_slc):
    psum = p[0:tq]
    for r in range(1, NSA_GROUP):
        psum = psum + p[r * tq:(r + 1) * tq]
    p_hi = psum.astype(BF16)
    rem = psum - p_hi.astype(F32)
    p_mid = rem.astype(BF16)
    p_lo = (rem - p_mid.astype(F32)).astype(BF16)
    imp = _dot_nt(ovl, p_hi) + _dot_nt(ovl, p_mid) + _dot_nt(ovl, p_lo)
    jb = lax.broadcasted_iota(jnp.int32, (n_slc, tq), 0)
    blk_t = jnp.right_shift(q0 + lax.broadcasted_iota(jnp.int32, (n_slc, tq), 1), int(math.log2(SLC_BLOCK)))
    back = blk_t - jb
    forced = (jb == 0) | ((back >= 0) & (back < N_LOCAL_BLOCKS))
    score = jnp.where(forced, 1e9, jnp.where(back >= 0, imp, -1e9))
    rank = jnp.zeros((n_slc, tq), F32)
    for mp in range(n_slc):
        row = score[mp:mp + 1, :]
        ahead = (row > score) | ((row == score) & (jb > mp))
        rank = rank + jnp.where(ahead, 1.0, 0.0)
    sel_t = jnp.where(rank < float(min(SLC_TOPK, n_slc)), 1.0, 0.0)
    sel_t = jnp.concatenate([sel_t, jnp.zeros((LANES - n_slc, tq), F32)], axis=0).astype(BF16)
    ri = lax.broadcasted_iota(jnp.int32, (tq, tq), 0)
    ci = lax.broadcasted_iota(jnp.int32, (tq, tq), 1)
    eye = jnp.where(ri == ci, 1.0, 0.0).astype(BF16)
    return _dot_nt(eye, sel_t)


def _nsa_combine(o_cmp, o_slc, o_win, gates_ref, ngate_ref, o_ref, tq):
    gates = gates_ref[...]
    for r in range(NSA_GROUP):
        rows = slice(r * tq, (r + 1) * tq)
        g_cmp = gates[:, r:r + 1]
        g_slc = gates[:, NSA_GROUP + r:NSA_GROUP + r + 1]
        g_win = gates[:, 2 * NSA_GROUP + r:2 * NSA_GROUP + r + 1]
        y = g_cmp * o_cmp[rows] + g_slc * o_slc[rows] + g_win * o_win[rows]
        cols = slice(r * HEAD_DIM, (r + 1) * HEAD_DIM)
        o_ref[0, :, cols] = (y * ngate_ref[0, :, cols].astype(F32)).astype(o_ref.dtype)


def _nsa_general(q4, q0, qi, gates_ref, ngate_ref, kcmp_ref, vcmp_ref, ks_ref, vs_ref, kw_ref, vw_ref,
                 ovl_ref, eaug_ref, o_ref, bias_scr, m_scr, l_scr, acc_scr, *, seq):
    tq, tk = NSA_TQ, NSA_TK
    n_slc = seq // SLC_BLOCK

    tpos = q0 + lax.broadcasted_iota(jnp.int32, (tq, LANES), 0)
    cidx = lax.broadcasted_iota(jnp.int32, (tq, LANES), 1)
    cbias = jnp.where(cidx * CMP_STRIDE + (CMP_BLOCK - 1) <= tpos, 0.0, MASKED)
    s = _dot_nt(q4, kcmp_ref[0, 0]) + _rep_heads(cbias)
    m = jnp.maximum(jnp.max(s, axis=-1, keepdims=True), M_INIT)
    e = jnp.exp2(s - m)
    l = jnp.sum(e, axis=-1, keepdims=True)
    p = e * (1.0 / jnp.maximum(l, 1e-30))
    o_cmp = _dot(p.astype(BF16), vcmp_ref[0, 0])

    sel = _select_blocks(p, ovl_ref[...], q0, tq, n_slc).astype(BF16)
    sel_keys = _dot_nt(sel, eaug_ref[...])
    for kt in range(seq // tk):
        bias_scr[kt] = jnp.where(sel_keys[:, kt * tk:(kt + 1) * tk] > 0.5, 0.0, MASKED)

    _flash_init(m_scr, l_scr, acc_scr)

    def slc_body(kt, carry):
        k0 = pl.multiple_of(kt * tk, tk)
        _flash_step(q4, ks_ref[0, pl.ds(k0, tk), :], vs_ref[0, pl.ds(k0, tk), :],
                    _rep_heads(bias_scr[kt]), m_scr, l_scr, acc_scr)
        return carry

    n_full = lax.div(q0, tk)
    lax.fori_loop(0, n_full, slc_body, 0)
    kd = pl.multiple_of(n_full * tk, tk)
    qpos = q0 + lax.broadcasted_iota(jnp.int32, (tq, tk), 0)
    kpos = kd + lax.broadcasted_iota(jnp.int32, (tq, tk), 1)
    causal = jnp.where(kpos <= qpos, 0.0, MASKED)
    _flash_step(q4, ks_ref[0, pl.ds(kd, tk), :], vs_ref[0, pl.ds(kd, tk), :],
                _rep_heads(bias_scr[n_full] + causal), m_scr, l_scr, acc_scr)
    o_slc = _flash_finish(l_scr, acc_scr)

    wk = WINDOW + tq
    ws = pl.multiple_of(jnp.maximum(q0 - WINDOW, 0), tq)
    d = (q0 - ws) + lax.broadcasted_iota(jnp.int32, (tq, wk), 0) - lax.broadcasted_iota(jnp.int32, (tq, wk), 1)
    wbias = jnp.where((d >= 0) & (d < WINDOW), 0.0, MASKED)
    s = _dot_nt(q4, kw_ref[0, pl.ds(ws, wk), :]) + _rep_heads(wbias)
    e = jnp.exp2(s - jnp.max(s, axis=-1, keepdims=True))
    o_win = _dot(e.astype(BF16), vw_ref[0, pl.ds(ws, wk), :]) * (1.0 / jnp.sum(e, axis=-1, keepdims=True))

    _nsa_combine(o_cmp, o_slc, o_win, gates_ref, ngate_ref, o_ref, tq)


def _nsa_bounded(q4, q0, qi, bound, gates_ref, ngate_ref, kcmp_ref, vcmp_ref, ks_ref, vs_ref, kw_ref, vw_ref,
                 ovl_ref, eaug_ref, cmask_ref, wmask_ref, tri_ref, o_ref,
                 ksa_scr, kwa_scr, kca_scr, l_scr, acc_scr, *, seq):
    tq, tk = NSA_TQ, NSA_TK
    n_slc = seq // SLC_BLOCK
    n_cmp_rows = kca_scr.shape[0]

    @pl.when(qi == 0)
    def _():
        one_col = eaug_ref[...]
        lane = lax.broadcasted_iota(jnp.int32, one_col.shape, 1)
        ksa_scr[:, :HEAD_DIM] = ks_ref[0]
        ksa_scr[:, HEAD_DIM:] = one_col
        one_col = jnp.where(lane == AUG_ONE_LANE, one_col, jnp.zeros_like(one_col))
        kwa_scr[:, :HEAD_DIM] = kw_ref[0]
        kwa_scr[:, HEAD_DIM:] = one_col
        kca_scr[:, :HEAD_DIM] = kcmp_ref[0, 0]
        kca_scr[:, HEAD_DIM:] = one_col[:n_cmp_rows]

    lane = lax.broadcasted_iota(jnp.int32, (tq, LANES), 1)
    shift_cols = jnp.where(lane == AUG_ONE_LANE, -bound, 0.0)
    qa = jnp.concatenate([q4, _rep_heads(shift_cols.astype(BF16))], axis=1)

    e = jnp.exp2(_dot_nt(qa, kca_scr[...])) * _rep_heads(cmask_ref[...])
    l = jnp.sum(e, axis=-1, keepdims=True)
    p = e * jnp.where(l > 0.0, 1.0 / l, 0.0)
    o_cmp = _dot(p.astype(BF16), vcmp_ref[0, 0])

    sel = _select_blocks(p, ovl_ref[...], q0, tq, n_slc)
    sel_cols = jnp.where(lane < n_slc, jnp.where(sel > 0.5, 0.0, MASKED), shift_cols)
    qs = jnp.concatenate([q4, _rep_heads(sel_cols.astype(BF16))], axis=1)

    l_scr[...] = jnp.zeros(l_scr.shape, F32)
    acc_scr[...] = jnp.zeros(acc_scr.shape, F32)

    def accumulate(k0, width, mask):
        p_t = jnp.exp2(_dot_nt(qs, ksa_scr[pl.ds(k0, width), :]))
        if mask is not None:
            p_t = p_t * mask
        cols = p_t[:, 0:LANES]
        for c in range(1, width // LANES):
            cols = cols + p_t[:, c * LANES:(c + 1) * LANES]
        l_scr[...] += cols
        acc_scr[...] += _dot(p_t.astype(BF16), vs_ref[0, pl.ds(k0, width), :])

    big = 2 * tk

    def slc_body(kt, carry):
        accumulate(pl.multiple_of(kt * big, big), big, None)
        return carry

    n_full = lax.div(q0, tk)
    n_big = lax.div(n_full, 2)
    lax.fori_loop(0, n_big, slc_body, 0)

    @pl.when(n_full > 2 * n_big)
    def _():
        accumulate(pl.multiple_of(n_big * big, big), tk, None)

    accumulate(pl.multiple_of(n_full * tk, tk), tk, _rep_heads(tri_ref[...]))
    o_slc = _flash_finish(l_scr, acc_scr)

    wk = WINDOW + tq
    ws = pl.multiple_of(jnp.maximum(q0 - WINDOW, 0), tq)
    e = jnp.exp2(_dot_nt(qa, kwa_scr[pl.ds(ws, wk), :])) * _rep_heads(wmask_ref[0])
    o_win = _dot(e.astype(BF16), vw_ref[0, pl.ds(ws, wk), :]) * (1.0 / jnp.sum(e, axis=-1, keepdims=True))

    _nsa_combine(o_cmp, o_slc, o_win, gates_ref, ngate_ref, o_ref, tq)


def _nsa_kernel(bound_ref, q_ref, ngate_ref, gates_ref, kcmp_ref, vcmp_ref, ks_ref, vs_ref, kw_ref, vw_ref,
                ovl_ref, eaug_ref, cmask_ref, wmask_ref, tri_ref, o_ref,
                bias_scr, m_scr, l_scr, acc_scr, ksa_scr, kwa_scr, kca_scr, *, seq):
    qi = pl.program_id(2)
    q0 = qi * NSA_TQ
    q4 = _stack_heads(q_ref[0])
    bound = bound_ref[0]

    @pl.when(bound <= MAX_SCORE_BOUND)
    def _():
        _nsa_bounded(q4, q0, qi, bound, gates_ref, ngate_ref, kcmp_ref, vcmp_ref, ks_ref, vs_ref, kw_ref, vw_ref,
                     ovl_ref, eaug_ref, cmask_ref, wmask_ref, tri_ref, o_ref,
                     ksa_scr, kwa_scr, kca_scr, l_scr, acc_scr, seq=seq)

    @pl.when(bound > MAX_SCORE_BOUND)
    def _():
        _nsa_general(q4, q0, qi, gates_ref, ngate_ref, kcmp_ref, vcmp_ref, ks_ref, vs_ref, kw_ref, vw_ref,
                     ovl_ref, eaug_ref, o_ref, bias_scr, m_scr, l_scr, acc_scr, seq=seq)


def _nsa_attention(proj, gates, kcmp, vcmp, bound, tables, offs):
    b, s, _ = proj.shape
    g = NSA_KV_HEADS
    tq = NSA_TQ
    nq = s // tq
    gw = NSA_GROUP * HEAD_DIM
    n_cmp_rows = s // CMP_STRIDE
    ovl_t, eaug, cmask, wmask, tri = tables
    q_spec = lambda off: pl.BlockSpec((1, tq, gw), lambda bi, gi, qi: (bi, qi, off + gi))
    kv_spec = lambda off: pl.BlockSpec((1, s, HEAD_DIM), lambda bi, gi, qi: (bi, 0, off + gi))
    cmp_spec = pl.BlockSpec((1, 1, n_cmp_rows, HEAD_DIM), lambda bi, gi, qi: (bi, gi, 0, 0))
    full = lambda shape: pl.BlockSpec(shape, lambda bi, gi, qi: (0,) * len(shape))
    n_wpat = wmask.shape[0]
    rows4 = NSA_GROUP * tq
    return pl.pallas_call(
        functools.partial(_nsa_kernel, seq=s),
        grid=(b, g, nq),
        in_specs=[pl.BlockSpec(memory_space=pltpu.SMEM),
                  q_spec(offs["nq"] // gw), q_spec(offs["ngate"] // gw),
                  pl.BlockSpec((tq, LANES), lambda bi, gi, qi: (bi * nq + qi, gi)),
                  cmp_spec, cmp_spec,
                  kv_spec(offs["ks"] // HEAD_DIM), kv_spec(offs["vs"] // HEAD_DIM),
                  kv_spec(offs["kw"] // HEAD_DIM), kv_spec(offs["vw"] // HEAD_DIM),
                  full(ovl_t.shape), full(eaug.shape),
                  pl.BlockSpec((tq, LANES), lambda bi, gi, qi: (qi, 0)),
                  pl.BlockSpec((1,) + wmask.shape[1:], lambda bi, gi, qi: (jnp.minimum(qi, n_wpat - 1), 0, 0)),
                  full(tri.shape)],
        out_specs=pl.BlockSpec((1, tq, gw), lambda bi, gi, qi: (bi, qi, gi)),
        out_shape=jax.ShapeDtypeStruct((b, s, NSA_WIDTH), BF16),
        scratch_shapes=[pltpu.VMEM((s // NSA_TK, tq, NSA_TK), F32),
                        pltpu.VMEM((rows4, LANES), F32), pltpu.VMEM((rows4, LANES), F32),
                        pltpu.VMEM((rows4, HEAD_DIM), F32),
                        pltpu.VMEM((s, 2 * HEAD_DIM), BF16), pltpu.VMEM((s, 2 * HEAD_DIM), BF16),
                        pltpu.VMEM((n_cmp_rows, 2 * HEAD_DIM), BF16)],
        compiler_params=pltpu.CompilerParams(
            dimension_semantics=("parallel", "parallel", "arbitrary"), vmem_limit_bytes=VMEM_LIMIT),
        name="nsa_attention",
    )(bound, proj, proj, gates, kcmp, vcmp, proj, proj, proj, proj, ovl_t, eaug, cmask, wmask, tri)


def _diff_general(q1, q2, qi, k_ref, v_ref, m1, l1, a1, m2, l2, a2):
    tq, tk = DIFF_TQ, DIFF_TK
    _flash_init(m1, l1, a1)
    _flash_init(m2, l2, a2)

    def step(rows, k0, width, bias):
        k = k_ref[0, pl.ds(k0, width), :]
        v = v_ref[0, pl.ds(k0, width), :]
        _flash_step(q1[rows], k[:, :HEAD_DIM], v, bias, m1.at[rows], l1.at[rows], a1.at[rows])
        _flash_step(q2[rows], k[:, HEAD_DIM:], v, bias, m2.at[rows], l2.at[rows], a2.at[rows])

    def body(kt, carry):
        step(slice(0, tq), pl.multiple_of(kt * tk, tk), tk, None)
        return carry

    lax.fori_loop(0, qi * (tq // tk), body, 0)
    dd = DIFF_DIAG
    q0 = qi * tq
    for c in range(tq // dd):
        n_rows = tq - c * dd
        ri = lax.broadcasted_iota(jnp.int32, (n_rows, dd), 0)
        ci = lax.broadcasted_iota(jnp.int32, (n_rows, dd), 1)
        step(slice(c * dd, tq), pl.multiple_of(q0 + c * dd, dd), dd, jnp.where(ri >= ci, 0.0, MASKED))
    return _flash_finish(l1, a1), _flash_finish(l2, a2)


def _lane_tile_sum(p):
    cols = p[:, 0:LANES]
    for c in range(1, p.shape[1] // LANES):
        cols = cols + p[:, c * LANES:(c + 1) * LANES]
    return cols


def _diff_bounded_stream(q, n, bound, k_ref, v_ref, tri_ref, kcols):
    tq, dd = DIFF_TQ, DIFF_DIAG
    kmain = n * tq + dd
    p = jnp.exp2(_dot_nt(q, k_ref[0, 0:kmain, kcols]) - bound)
    p_diag = p[:, kmain - dd:] * tri_ref[...]
    p = p_diag if kmain == dd else jnp.concatenate([p[:, :kmain - dd], p_diag], axis=1)
    l = _lane_tile_sum(p)
    acc = _dot(p.astype(BF16), v_ref[0, 0:kmain, :])
    pc = jnp.exp2(_dot_nt(q[dd:], k_ref[0, kmain:kmain + dd, kcols]) - bound) * tri_ref[0:tq - dd, :]
    l = jnp.concatenate([l[:dd], l[dd:] + _lane_tile_sum(pc)], axis=0)
    acc = jnp.concatenate([acc[:dd], acc[dd:] + _dot(pc.astype(BF16), v_ref[0, kmain:kmain + dd, :])], axis=0)
    return acc * (1.0 / jnp.sum(l, axis=-1, keepdims=True))


def _diff_kernel(bound_ref, q_ref, gate_ref, k_ref, v_ref, lam_ref, tri_ref, o_ref, m1, l1, a1, m2, l2, a2,
                 *, lambda_init, n_tiles):
    qi = pl.program_id(2)
    q = q_ref[0]
    q1 = q[:, :HEAD_DIM]
    q2 = q[:, HEAD_DIM:]
    bound = bound_ref[0]

    def finish(o1, o2):
        lp = lam_ref[...]
        lam = (jnp.exp(jnp.sum(lp[0:1] * lp[1:2], axis=-1, keepdims=True))
               - jnp.exp(jnp.sum(lp[2:3] * lp[3:4], axis=-1, keepdims=True)) + lambda_init)
        o = _rms(o1 - lam * o2) * (1.0 - lambda_init)
        o_ref[0] = (o * gate_ref[0].astype(F32)).astype(o_ref.dtype)

    def bounded_variant(n):
        @pl.when((bound <= MAX_SCORE_BOUND) & (qi == n))
        def _():
            finish(_diff_bounded_stream(q1, n, bound, k_ref, v_ref, tri_ref, slice(0, HEAD_DIM)),
                   _diff_bounded_stream(q2, n, bound, k_ref, v_ref, tri_ref, slice(HEAD_DIM, 2 * HEAD_DIM)))

    for n in range(n_tiles):
        bounded_variant(n)

    @pl.when(bound > MAX_SCORE_BOUND)
    def _():
        finish(*_diff_general(q1, q2, qi, k_ref, v_ref, m1, l1, a1, m2, l2, a2))


def _diff_attention(proj, lam_params, bound, lambda_init):
    b, s, _ = proj.shape
    h = DIFF_HEADS
    tq, dd = DIFF_TQ, DIFF_DIAG
    assert tq == 2 * dd
    w = DIFF_V_DIM
    tri = jnp.asarray(np.arange(tq)[:, None] >= np.arange(dd)[None, :], F32)
    q_spec = lambda off: pl.BlockSpec((1, tq, w), lambda bi, hi, qi: (bi, qi, off + hi))
    kv_spec = lambda off: pl.BlockSpec((1, s, w), lambda bi, hi, qi: (bi, 0, off + hi))
    stat = pltpu.VMEM((tq, LANES), F32)
    acc = pltpu.VMEM((tq, w), F32)
    return pl.pallas_call(
        functools.partial(_diff_kernel, lambda_init=lambda_init, n_tiles=s // tq),
        grid=(b, h, s // tq),
        in_specs=[pl.BlockSpec(memory_space=pltpu.SMEM),
                  q_spec(0), q_spec(3 * h), kv_spec(h), kv_spec(2 * h),
                  pl.BlockSpec(lam_params.shape, lambda bi, hi, qi: (0, 0)),
                  pl.BlockSpec(tri.shape, lambda bi, hi, qi: (0, 0))],
        out_specs=pl.BlockSpec((1, tq, w), lambda bi, hi, qi: (bi, qi, hi)),
        out_shape=jax.ShapeDtypeStruct((b, s, DIFF_WIDTH), BF16),
        scratch_shapes=[stat, stat, acc, stat, stat, acc],
        compiler_params=pltpu.CompilerParams(
            dimension_semantics=("parallel", "parallel", "arbitrary"), vmem_limit_bytes=VMEM_LIMIT),
        name="diff_attention",
    )(bound, proj, proj, proj, proj, lam_params, tri)


def _rope_tables(pos):
    inv = 1.0 / (ROPE_THETA ** (jnp.arange(0, HEAD_DIM, 2, dtype=F32) / HEAD_DIM))
    ang = pos.astype(F32)[:, None] * inv[None, :]
    cos, sin = jnp.cos(ang), jnp.sin(ang)
    return jnp.concatenate([cos, cos], axis=-1), jnp.concatenate([-sin, sin], axis=-1)


def _retention_tables():
    h, c = RET_HEADS, RET_CHUNK
    log_g = jnp.log1p(-jnp.exp2(-5.0 - jnp.arange(h, dtype=F32)))
    j = jnp.arange(c, dtype=F32)
    diff = j[:, None] - j[None, :]
    intra = jnp.where(diff >= 0, jnp.exp(log_g[:, None, None] * jnp.maximum(diff, 0.0)), 0.0)
    q_dec = jnp.exp(log_g[:, None] * (j + 1.0))
    k_dec = jnp.exp(log_g[:, None] * (c - 1.0 - j))
    chunk_dec = jnp.exp(log_g * c)
    wide = lambda t: jnp.broadcast_to(t[:, :, None], (h, t.shape[1], HEAD_DIM))
    return intra, wide(q_dec), wide(k_dec), wide(chunk_dec[:, None])


def _selection_tables(seq):
    tq, tk = NSA_TQ, NSA_TK
    n_cmp_rows = seq // CMP_STRIDE
    n_slc = seq // SLC_BLOCK
    assert tq == tk and n_slc <= AUG_ONE_LANE < LANES
    c_start = np.arange(n_cmp_rows) * CMP_STRIDE
    s_start = np.arange(n_slc) * SLC_BLOCK
    overlap_t = ((c_start[None, :] <= s_start[:, None] + SLC_BLOCK - 1)
                 & (c_start[None, :] + CMP_BLOCK - 1 >= s_start[:, None]))
    lane = np.arange(LANES)[None, :]
    key = np.arange(seq)[:, None]
    eaug = ((key // SLC_BLOCK) == lane) | (lane == AUG_ONE_LANE)
    cmask = lane * CMP_STRIDE + CMP_BLOCK - 1 <= key
    r = np.arange(tq)[:, None]
    c = np.arange(WINDOW + tq)[None, :]
    wmask = []
    for pat in range(WINDOW // tq + 1):
        d = min(pat * tq, WINDOW) + r - c
        wmask.append((d >= 0) & (d < WINDOW))
    tri = np.arange(tq)[:, None] >= np.arange(tk)[None, :]
    return (jnp.asarray(overlap_t, BF16), jnp.asarray(eaug, BF16), jnp.asarray(cmask, F32),
            jnp.asarray(np.stack(wmask), F32), jnp.asarray(tri, F32))


def _tile_modes(segments, tn):
    off, kinds, col = {}, [], 0
    for name, width, kind in segments:
        off[name] = col
        kinds += [kind] * (width // LANES)
        col += width
    per = tn // LANES
    tiles = [kinds[i:i + per] for i in range(0, len(kinds), per)]
    assert col % tn == 0
    modes = []
    for j, tile in enumerate(tiles):
        if modes and modes[-1][2] == tile:
            modes[-1] = (modes[-1][0], j + 1, tile)
        else:
            modes.append((j, j + 1, tile))
    return modes, off


def _l0_tile_modes():
    plain, silu = ("plain", 0), ("silu", 0)
    k_norm = ("nrope", L0_TAB_K)
    return _tile_modes([
        ("rq", RET_WIDTH, ("rope", L0_TAB_ROPE)), ("rk", RET_WIDTH, ("rope", L0_TAB_ROPE_SCALED)),
        ("rv", RET_WIDTH, plain), ("rgate", RET_WIDTH, silu), ("nq", NSA_WIDTH, ("nrope", L0_TAB_Q)),
        ("kc", NSA_KV_WIDTH, plain), ("vc", NSA_KV_WIDTH, plain), ("ks", NSA_KV_WIDTH, k_norm),
        ("vs", NSA_KV_WIDTH, plain), ("kw", NSA_KV_WIDTH, k_norm), ("vw", NSA_KV_WIDTH, plain),
        ("ngate", NSA_WIDTH, silu)], PROJ_TN)


def _l1_tile_modes():
    return _tile_modes([("q", DIFF_WIDTH, ("nrope", L1_TAB_Q)), ("k", DIFF_WIDTH, ("nrope", L1_TAB_K)),
                        ("v", DIFF_WIDTH, ("plain", 0)), ("gate", DIFF_WIDTH, ("silu", 0))], L1_PROJ_TN)[0]


def _rope_pair(cos, sin_signed, gain=None, scale=1.0):
    if gain is None:
        return [cos * scale, sin_signed * scale]
    return [cos * (gain * scale)[None, :], sin_signed * (jnp.roll(gain, HALF) * scale)[None, :]]


def kernel(x, l0_norm_g, l0_w_in, l0_w_out, l0_nsa_q_norm_g, l0_nsa_k_norm_g, l0_cmp_pe_k, l0_cmp_w1_k, l0_cmp_w2_k, l0_cmp_pe_v, l0_cmp_w1_v, l0_cmp_w2_v, l1_norm_g, l1_w_in, l1_w_out, l1_q_norm_g, l1_k_norm_g, l1_lambda_q1, l1_lambda_k1, l1_lambda_q2, l1_lambda_k2):
    b, s, d = x.shape
    m = b * s
    x2d = x.reshape(m, d)
    cos, sin = _rope_tables(jnp.arange(s))
    cos_c, sin_c = _rope_tables(jnp.arange(s // CMP_STRIDE) * CMP_STRIDE + CMP_BLOCK - 1)
    intra, qdec, kdec, cdec = _retention_tables()
    nsa_tables = _selection_tables(s)

    modes0, off = _l0_tile_modes()
    w0 = l0_w_in.astype(BF16)
    wg = l0_w_in[:, AB_MAIN_COLS:].reshape(d, 3, NSA_KV_HEADS, NSA_GROUP).transpose(0, 2, 1, 3)
    wg = wg.reshape(d, NSA_KV_HEADS, 3 * NSA_GROUP)
    wg = jnp.pad(wg, ((0, 0), (0, 0), (0, LANES - 3 * NSA_GROUP))).reshape(d, NSA_KV_HEADS * LANES).astype(BF16)
    tabs0 = jnp.stack(_rope_pair(cos, sin) + _rope_pair(cos, sin, scale=QK_SCALE)
                      + _rope_pair(cos, sin, l0_nsa_q_norm_g, Q_SCALE) + _rope_pair(cos, sin, l0_nsa_k_norm_g))
    proj0, gates = _norm_proj(x2d, l0_norm_g, w0, tabs0, modes0, s, PROJ_TN, wg=wg)
    proj0 = proj0.reshape(b, s, AB_MAIN_COLS)
    y_ret = _retention(proj0, intra, qdec, kdec, cdec)
    w1k = l0_cmp_w1_k.astype(BF16).reshape(CMP_BLOCK, HEAD_DIM, HEAD_DIM)
    w1v = l0_cmp_w1_v.astype(BF16).reshape(CMP_BLOCK, HEAD_DIM, HEAD_DIM)
    kcmp, vcmp = _nsa_compress(proj0, off["kc"] // HEAD_DIM, off["vc"] // HEAD_DIM,
                               l0_cmp_pe_k, w1k, l0_cmp_w2_k.astype(BF16),
                               l0_cmp_pe_v, w1v, l0_cmp_w2_v.astype(BF16),
                               l0_nsa_k_norm_g, cos_c, sin_c)
    bound = (BOUND_MARGIN * HEAD_DIM * Q_SCALE * jnp.max(jnp.abs(l0_nsa_q_norm_g))
             * jnp.max(jnp.abs(l0_nsa_k_norm_g))).reshape(1).astype(F32)
    y_nsa = _nsa_attention(proj0, gates, kcmp, vcmp, bound, nsa_tables, off)
    w_out0 = l0_w_out.astype(BF16)
    x1 = _out_proj([y_ret.reshape(m, RET_WIDTH), y_nsa.reshape(m, NSA_WIDTH)],
                   [w_out0[:RET_WIDTH], w_out0[RET_WIDTH:]], x2d)

    lambda_init = 0.8 - 0.6 * math.exp(-0.3 * 1)
    tabs1 = jnp.stack(_rope_pair(cos, sin, l1_q_norm_g, Q_SCALE) + _rope_pair(cos, sin, l1_k_norm_g))
    proj1 = _norm_proj(x1, l1_norm_g, l1_w_in.astype(BF16), tabs1, _l1_tile_modes(), s, L1_PROJ_TN)[0]
    lam_params = jnp.stack([l1_lambda_q1, l1_lambda_k1, l1_lambda_q2, l1_lambda_k2]).astype(F32)
    bound1 = (BOUND_MARGIN * HEAD_DIM * Q_SCALE * jnp.max(jnp.abs(l1_q_norm_g))
              * jnp.max(jnp.abs(l1_k_norm_g))).reshape(1).astype(F32)
    y1 = _diff_attention(proj1.reshape(b, s, C_IN_COLS), lam_params, bound1, lambda_init)
    out = _out_proj([y1.reshape(m, DIFF_WIDTH)], [l1_w_out.astype(BF16)], x1)
    return out.reshape(b, s, d)
```

```python
import functools
import math

import numpy as np
import jax
import jax.numpy as jnp
from jax import lax
from jax.experimental import pallas as pl
from jax.experimental.pallas import tpu as pltpu

F32 = jnp.float32
BF16 = jnp.bfloat16

D_MODEL = 2048
HEAD_DIM = 128
HALF = HEAD_DIM // 2
ROPE_THETA = 10000.0
EPS = 1e-6
RET_HEADS = 8
RET_CHUNK = 256
NSA_HEADS = 8
NSA_KV_HEADS = 2
NSA_GROUP = NSA_HEADS // NSA_KV_HEADS
CMP_BLOCK = 32
CMP_STRIDE = 16
SLC_BLOCK = 64
SLC_TOPK = 16
N_LOCAL_BLOCKS = 2
WINDOW = 512
DIFF_HEADS = 8
DIFF_V_DIM = 2 * HEAD_DIM
QK_SCALE = HEAD_DIM ** -0.5
LOG2E = math.log2(math.e)
Q_SCALE = QK_SCALE * LOG2E

RET_WIDTH = RET_HEADS * HEAD_DIM
NSA_WIDTH = NSA_HEADS * HEAD_DIM
NSA_KV_WIDTH = NSA_KV_HEADS * HEAD_DIM
AB_MAIN_COLS = 4 * RET_WIDTH + 2 * NSA_WIDTH + 6 * NSA_KV_WIDTH
N_GATE_COLS = 3 * NSA_HEADS
DIFF_WIDTH = DIFF_HEADS * DIFF_V_DIM
C_IN_COLS = 4 * DIFF_WIDTH

LANES = 128
MASKED = -1e30
M_INIT = -1e29
MAX_SCORE_BOUND = 60.0
BOUND_MARGIN = 1.05
AUG_ONE_LANE = 32
VMEM_LIMIT = 56 * 1024 * 1024

PROJ_TM = 1024
PROJ_TN = 1280
L1_PROJ_TN = 1024
OPROJ_TN = 1024
PROJ_SUB = 256
L0_TAB_ROPE, L0_TAB_ROPE_SCALED, L0_TAB_Q, L0_TAB_K = range(4)
L1_TAB_Q, L1_TAB_K = range(2)
NSA_TQ = 512
NSA_TK = 512
DIFF_TQ = 512
DIFF_TK = 512
DIFF_DIAG = 256


def _dot(a, b):
    return jnp.dot(a, b, preferred_element_type=F32)


def _dot_nt(a, b):
    return lax.dot_general(a, b, (((1,), (1,)), ((), ())), preferred_element_type=F32)


def _dot_tn(a, b):
    return lax.dot_general(a, b, (((0,), (0,)), ((), ())), preferred_element_type=F32)


def _silu(x):
    return x / (1.0 + jnp.exp(-x))


def _rms(x):
    return x * lax.rsqrt(jnp.mean(x * x, axis=-1, keepdims=True) + EPS)


def _rope(x, cos, sin_signed):
    return x * cos + pltpu.roll(x, HALF, axis=1) * sin_signed


def _rope_tab(seg, t, tabs_ref, rows):
    return seg * tabs_ref[2 * t, rows, :] + pltpu.roll(seg, HALF, axis=1) * tabs_ref[2 * t + 1, rows, :]


def _tile_epilogue(acc, kinds, tabs_ref, rows, o_ref):
    n_seg = len(kinds)
    ri = lax.broadcasted_iota(jnp.int32, (2 * LANES, 2 * LANES), 0)
    ci = lax.broadcasted_iota(jnp.int32, (2 * LANES, 2 * LANES), 1)
    pair_mean = jnp.where((ri < LANES) == (ci < LANES), 1.0 / HEAD_DIM, 0.0).astype(BF16)
    c = 0
    while c < n_seg:
        op, t = kinds[c]
        cols = slice(c * LANES, (c + 1) * LANES)
        if op == "nrope" and c + 1 < n_seg and kinds[c + 1][0] == "nrope":
            cols2 = slice(c * LANES, (c + 2) * LANES)
            seg2 = acc[:, cols2]
            seg2 = seg2 * lax.rsqrt(_dot((seg2 * seg2).astype(BF16), pair_mean) + EPS)
            for half in range(2):
                out = _rope_tab(seg2[:, half * LANES:(half + 1) * LANES], kinds[c + half][1], tabs_ref, rows)
                o_ref[rows, (c + half) * LANES:(c + half + 1) * LANES] = out.astype(o_ref.dtype)
            c += 2
            continue
        seg = acc[:, cols]
        if op == "silu":
            seg = _silu(seg)
        elif op == "nrope":
            seg = seg * lax.rsqrt(_dot((seg * seg).astype(BF16), pair_mean[:LANES, :LANES]) + EPS)
            seg = _rope_tab(seg, t, tabs_ref, rows)
        elif op == "rope":
            seg = _rope_tab(seg, t, tabs_ref, rows)
        o_ref[rows, cols] = seg.astype(o_ref.dtype)
        c += 1


def _proj_kernel(*refs, tile_modes, has_gates):
    if has_gates:
        x_ref, g_ref, w_ref, tabs_ref, wg_ref, o_ref, og_ref, h_scr = refs
    else:
        x_ref, g_ref, w_ref, tabs_ref, o_ref, h_scr = refs
    j = pl.program_id(1)
    n_sub = h_scr.shape[0] // PROJ_SUB

    def make_branch(cond, kinds, first):
        @pl.when(cond)
        def _():
            for r in range(n_sub):
                rows = slice(r * PROJ_SUB, (r + 1) * PROJ_SUB)
                if first:
                    h = (_rms(x_ref[rows, :]) * g_ref[...]).astype(BF16)
                    h_scr[rows, :] = h
                    if has_gates:
                        og_ref[rows, :] = 1.0 / (1.0 + jnp.exp(-_dot(h, wg_ref[...])))
                else:
                    h = h_scr[rows, :]
                _tile_epilogue(_dot(h, w_ref[0]), kinds, tabs_ref, rows, o_ref)

    for lo, hi, kinds in tile_modes:
        if lo == 0:
            make_branch(j == 0, kinds, True)
            lo = 1
        if hi > lo:
            make_branch((j >= lo) & (j < hi), kinds, False)


def _norm_proj(x2d, g, w, tabs, tile_modes, seq, tn, wg=None):
    m, d = x2d.shape
    tm = PROJ_TM
    n = tile_modes[-1][1] * tn
    s_tiles = seq // tm
    has_gates = wg is not None
    in_specs = [
        pl.BlockSpec((tm, d), lambda i, j: (i, 0)),
        pl.BlockSpec((1, d), lambda i, j: (0, 0)),
        pl.BlockSpec((1, d, tn), lambda i, j: (j, 0, 0)),
        pl.BlockSpec((tabs.shape[0], tm, LANES), lambda i, j: (0, i % s_tiles, 0)),
    ]
    args = [x2d, g.reshape(1, d), w, tabs]
    out_shape = [jax.ShapeDtypeStruct((m, n), BF16)]
    out_specs = [pl.BlockSpec((tm, tn), lambda i, j: (i, j))]
    if has_gates:
        ng = wg.shape[1]
        in_specs.append(pl.BlockSpec((d, ng), lambda i, j: (0, 0)))
        args.append(wg)
        out_shape.append(jax.ShapeDtypeStruct((m, ng), F32))
        out_specs.append(pl.BlockSpec((tm, ng), lambda i, j: (i, 0)))
    return pl.pallas_call(
        functools.partial(_proj_kernel, tile_modes=tile_modes, has_gates=has_gates),
        grid=(m // tm, n // tn),
        in_specs=in_specs,
        out_specs=out_specs,
        out_shape=out_shape,
        scratch_shapes=[pltpu.VMEM((tm, d), BF16)],
        compiler_params=pltpu.CompilerParams(
            dimension_semantics=("parallel", "arbitrary"), vmem_limit_bytes=VMEM_LIMIT),
        name="norm_proj",
    )(*args)


def _oproj_kernel(*refs, n_terms):
    y_refs, w_refs = refs[:n_terms], refs[n_terms:2 * n_terms]
    x_ref, o_ref = refs[2 * n_terms:]
    acc = x_ref[...]
    for y_ref, w_ref in zip(y_refs, w_refs):
        acc = acc + _dot(y_ref[...], w_ref[...])
    o_ref[...] = acc


def _out_proj(ys, ws, x2d):
    m, n = x2d.shape
    tm, tn = PROJ_TM, OPROJ_TN
    y_specs = [pl.BlockSpec((tm, y.shape[1]), lambda i, j: (i, 0)) for y in ys]
    w_specs = [pl.BlockSpec((w.shape[0], tn), lambda i, j: (0, j)) for w in ws]
    return pl.pallas_call(
        functools.partial(_oproj_kernel, n_terms=len(ys)),
        grid=(m // tm, n // tn),
        in_specs=y_specs + w_specs + [pl.BlockSpec((tm, tn), lambda i, j: (i, j))],
        out_specs=pl.BlockSpec((tm, tn), lambda i, j: (i, j)),
        out_shape=jax.ShapeDtypeStruct((m, n), F32),
        compiler_params=pltpu.CompilerParams(
            dimension_semantics=("parallel", "arbitrary"), vmem_limit_bytes=VMEM_LIMIT),
        name="out_proj",
    )(*ys, *ws, x2d)


def _ret_kernel(q_ref, k_ref, v_ref, gate_ref, intra_ref, qdec_ref, kdec_ref, cdec_ref, o_ref, *, n_chunks):
    c = RET_CHUNK
    intra = intra_ref[0]
    qdec = qdec_ref[0]
    kdec = kdec_ref[0]
    cdec = cdec_ref[0]
    state = jnp.zeros((HEAD_DIM, HEAD_DIM), F32)
    for n in range(n_chunks):
        rows = slice(n * c, (n + 1) * c)
        q = q_ref[0, rows, :]
        k = k_ref[0, rows, :]
        v = v_ref[0, rows, :]
        scores = _dot_nt(q, k) * intra
        inner = _dot(scores.astype(BF16), v)
        cross = _dot((q.astype(F32) * qdec).astype(BF16), state.astype(BF16))
        kv = _dot_tn((k.astype(F32) * kdec).astype(BF16), v)
        state = state * cdec + kv
        o = _rms(inner + cross) * gate_ref[0, rows, :].astype(F32)
        o_ref[0, rows, :] = o.astype(o_ref.dtype)


def _retention(proj, intra, qdec, kdec, cdec):
    b, s, _ = proj.shape
    h = RET_HEADS
    y_shape = (b, s, RET_WIDTH)
    head = lambda off: pl.BlockSpec((1, s, HEAD_DIM), lambda bi, hi: (bi, 0, off + hi))
    table = lambda rows, cols=HEAD_DIM: pl.BlockSpec((1, rows, cols), lambda bi, hi: (hi, 0, 0))
    return pl.pallas_call(
        functools.partial(_ret_kernel, n_chunks=s // RET_CHUNK),
        grid=(b, h),
        in_specs=[head(0), head(h), head(2 * h), head(3 * h),
                  table(RET_CHUNK, RET_CHUNK), table(RET_CHUNK), table(RET_CHUNK), table(1)],
        out_specs=pl.BlockSpec((1, s, HEAD_DIM), lambda bi, hi: (bi, 0, hi)),
        out_shape=jax.ShapeDtypeStruct(y_shape, BF16),
        compiler_params=pltpu.CompilerParams(
            dimension_semantics=("parallel", "arbitrary"), vmem_limit_bytes=VMEM_LIMIT),
        name="retention",
    )(proj, proj, proj, proj, intra, qdec, kdec, cdec)


def _cmp_kernel(kc_ref, vc_ref, pek_ref, w1k_ref, w2k_ref, pev_ref, w1v_ref, w2v_ref, kg_ref,
                cos_ref, sin_ref, ko_ref, vo_ref, t_scr, *, seq):
    n_rows = seq // CMP_STRIDE

    def compress(src_ref, pe_ref, w1_ref, w2_ref):
        t_scr[0:seq, :] = src_ref[0].astype(F32)
        t_scr[seq:seq + CMP_STRIDE, :] = jnp.zeros((CMP_STRIDE, HEAD_DIM), F32)
        acc = jnp.zeros((n_rows, HEAD_DIM), F32)
        for r in range(CMP_BLOCK):
            rows = t_scr[pl.ds(r, n_rows, stride=CMP_STRIDE), :] + pe_ref[r:r + 1, :]
            acc = acc + _dot(rows.astype(BF16), w1_ref[r])
        return _dot(_silu(acc).astype(BF16), w2_ref[...])

    kc = compress(kc_ref, pek_ref, w1k_ref, w2k_ref)
    kc = _rope(_rms(kc) * kg_ref[...], cos_ref[...], sin_ref[...])
    ko_ref[0, 0] = kc.astype(ko_ref.dtype)
    vo_ref[0, 0] = compress(vc_ref, pev_ref, w1v_ref, w2v_ref).astype(vo_ref.dtype)


def _nsa_compress(proj, kc_off, vc_off, pe_k, w1_k, w2_k, pe_v, w1_v, w2_v, k_g, cos_c, sin_c):
    b, s, _ = proj.shape
    g = NSA_KV_HEADS
    n_rows = s // CMP_STRIDE
    head = lambda off: pl.BlockSpec((1, s, HEAD_DIM), lambda bi, gi: (bi, 0, off + gi))
    full = lambda shape: pl.BlockSpec(shape, lambda bi, gi: (0,) * len(shape))
    out_spec = pl.BlockSpec((1, 1, n_rows, HEAD_DIM), lambda bi, gi: (bi, gi, 0, 0))
    out_sds = jax.ShapeDtypeStruct((b, g, n_rows, HEAD_DIM), BF16)
    return pl.pallas_call(
        functools.partial(_cmp_kernel, seq=s),
        grid=(b, g),
        in_specs=[head(kc_off), head(vc_off),
                  full((CMP_BLOCK, HEAD_DIM)), full((CMP_BLOCK, HEAD_DIM, HEAD_DIM)), full((HEAD_DIM, HEAD_DIM)),
                  full((CMP_BLOCK, HEAD_DIM)), full((CMP_BLOCK, HEAD_DIM, HEAD_DIM)), full((HEAD_DIM, HEAD_DIM)),
                  full((1, HEAD_DIM)), full((n_rows, HEAD_DIM)), full((n_rows, HEAD_DIM))],
        out_specs=[out_spec, out_spec],
        out_shape=[out_sds, out_sds],
        scratch_shapes=[pltpu.VMEM((s + CMP_STRIDE, HEAD_DIM), F32)],
        compiler_params=pltpu.CompilerParams(
            dimension_semantics=("parallel", "arbitrary"), vmem_limit_bytes=VMEM_LIMIT),
        name="nsa_compress",
    )(proj, proj, pe_k, w1_k, w2_k, pe_v, w1_v, w2_v, k_g.reshape(1, HEAD_DIM), cos_c, sin_c)


def _flash_init(m_scr, l_scr, acc_scr):
    m_scr[...] = jnp.full(m_scr.shape, M_INIT, F32)
    l_scr[...] = jnp.zeros(l_scr.shape, F32)
    acc_scr[...] = jnp.zeros(acc_scr.shape, F32)


def _lane_tiles(x, width):
    return x if width == LANES else jnp.concatenate([x] * (width // LANES), axis=1)


def _flash_step(q, k, v, bias, m_scr, l_scr, acc_scr):
    s = _dot_nt(q, k)
    if bias is not None:
        s = s + bias
    tk = s.shape[1]
    m_old = m_scr[...]
    m_new = jnp.maximum(m_old, jnp.max(s, axis=-1, keepdims=True))
    alpha = jnp.exp2(m_old - m_new)
    p = jnp.exp2(s - _lane_tiles(m_new, tk))
    p_cols = p[:, 0:LANES]
    for c in range(1, tk // LANES):
        p_cols = p_cols + p[:, c * LANES:(c + 1) * LANES]
    l_scr[...] = alpha * l_scr[...] + p_cols
    acc_scr[...] = _lane_tiles(alpha, acc_scr.shape[1]) * acc_scr[...] + _dot(p.astype(BF16), v)
    m_scr[...] = m_new


def _flash_finish(l_scr, acc_scr):
    return acc_scr[...] * (1.0 / jnp.sum(l_scr[...], axis=-1, keepdims=True))


def _rep_heads(bias):
    return jnp.concatenate([bias] * NSA_GROUP, axis=0)


def _stack_heads(qblk):
    return jnp.concatenate([qblk[:, r * HEAD_DIM:(r + 1) * HEAD_DIM] for r in range(NSA_GROUP)], axis=0)


def _select_blocks(p, ovl, q0, tq, n_slc):
    psum = p[0:tq]
    for r in range(1, NSA_GROUP):
        psum = psum + p[r * tq:(r + 1) * tq]
    p_hi = psum.astype(BF16)
    rem = psum - p_hi.astype(F32)
    p_mid = rem.astype(BF16)
    p_lo = (rem - p_mid.astype(F32)).astype(BF16)
    imp = _dot_nt(ovl, p_hi) + _dot_nt(ovl, p_mid) + _dot_nt(ovl, p_lo)
    jb = lax.broadcasted_iota(jnp.int32, (n_slc, tq), 0)
    blk_t = jnp.right_shift(q0 + lax.broadcasted_iota(jnp.int32, (n_slc, tq), 1), int(math.log2(SLC_BLOCK)))
    back = blk_t - jb
    forced = (jb == 0) | ((back >= 0) & (back < N_LOCAL_BLOCKS))
    score = jnp.where(forced, 1e9, jnp.where(back >= 0, imp, -1e9))
    rank = jnp.zeros((n_slc, tq), F32)
    for mp in range(n_slc):
        row = score[mp:mp + 1, :]
        ahead = (row > score) | ((row == score) & (jb > mp))
        rank = rank + jnp.where(ahead, 1.0, 0.0)
    sel_t = jnp.where(rank < float(min(SLC_TOPK, n_slc)), 1.0, 0.0)
    sel_t = jnp.concatenate([sel_t, jnp.zeros((LANES - n_slc, tq), F32)], axis=0).astype(BF16)
    ri = lax.broadcasted_iota(jnp.int32, (tq, tq), 0)
    ci = lax.broadcasted_iota(jnp.int32, (tq, tq), 1)
    eye = jnp.where(ri == ci, 1.0, 0.0).astype(BF16)
    return _dot_nt(eye, sel_t)


def _nsa_combine(o_cmp, o_slc, o_win, gates_ref, ngate_ref, o_ref, tq):
    gates = gates_ref[...]
    for r in range(NSA_GROUP):
        rows = slice(r * tq, (r + 1) * tq)
        g_cmp = gates[:, r:r + 1]
        g_slc = gates[:, NSA_GROUP + r:NSA_GROUP + r + 1]
        g_win = gates[:, 2 * NSA_GROUP + r:2 * NSA_GROUP + r + 1]
        y = g_cmp * o_cmp[rows] + g_slc * o_slc[rows] + g_win * o_win[rows]
        cols = slice(r * HEAD_DIM, (r + 1) * HEAD_DIM)
        o_ref[0, :, cols] = (y * ngate_ref[0, :, cols].astype(F32)).astype(o_ref.dtype)


def _nsa_general(q4, q0, qi, gates_ref, ngate_ref, kcmp_ref, vcmp_ref, ks_ref, vs_ref, kw_ref, vw_ref,
                 ovl_ref, eaug_ref, o_ref, bias_scr, m_scr, l_scr, acc_scr, *, seq):
    tq, tk = NSA_TQ, NSA_TK
    n_slc = seq // SLC_BLOCK

    tpos = q0 + lax.broadcasted_iota(jnp.int32, (tq, LANES), 0)
    cidx = lax.broadcasted_iota(jnp.int32, (tq, LANES), 1)
    cbias = jnp.where(cidx * CMP_STRIDE + (CMP_BLOCK - 1) <= tpos, 0.0, MASKED)
    s = _dot_nt(q4, kcmp_ref[0, 0]) + _rep_heads(cbias)
    m = jnp.maximum(jnp.max(s, axis=-1, keepdims=True), M_INIT)
    e = jnp.exp2(s - m)
    l = jnp.sum(e, axis=-1, keepdims=True)
    p = e * (1.0 / jnp.maximum(l, 1e-30))
    o_cmp = _dot(p.astype(BF16), vcmp_ref[0, 0])

    sel = _select_blocks(p, ovl_ref[...], q0, tq, n_slc).astype(BF16)
    sel_keys = _dot_nt(sel, eaug_ref[...])
    for kt in range(seq // tk):
        bias_scr[kt] = jnp.where(sel_keys[:, kt * tk:(kt + 1) * tk] > 0.5, 0.0, MASKED)

    _flash_init(m_scr, l_scr, acc_scr)

    def slc_body(kt, carry):
        k0 = pl.multiple_of(kt * tk, tk)
        _flash_step(q4, ks_ref[0, pl.ds(k0, tk), :], vs_ref[0, pl.ds(k0, tk), :],
                    _rep_heads(bias_scr[kt]), m_scr, l_scr, acc_scr)
        return carry

    n_full = lax.div(q0, tk)
    lax.fori_loop(0, n_full, slc_body, 0)
    kd = pl.multiple_of(n_full * tk, tk)
    qpos = q0 + lax.broadcasted_iota(jnp.int32, (tq, tk), 0)
    kpos = kd + lax.broadcasted_iota(jnp.int32, (tq, tk), 1)
    causal = jnp.where(kpos <= qpos, 0.0, MASKED)
    _flash_step(q4, ks_ref[0, pl.ds(kd, tk), :], vs_ref[0, pl.ds(kd, tk), :],
                _rep_heads(bias_scr[n_full] + causal), m_scr, l_scr, acc_scr)
    o_slc = _flash_finish(l_scr, acc_scr)

    wk = WINDOW + tq
    ws = pl.multiple_of(jnp.maximum(q0 - WINDOW, 0), tq)
    d = (q0 - ws) + lax.broadcasted_iota(jnp.int32, (tq, wk), 0) - lax.broadcasted_iota(jnp.int32, (tq, wk), 1)
    wbias = jnp.where((d >= 0) & (d < WINDOW), 0.0, MASKED)
    s = _dot_nt(q4, kw_ref[0, pl.ds(ws, wk), :]) + _rep_heads(wbias)
    e = jnp.exp2(s - jnp.max(s, axis=-1, keepdims=True))
    o_win = _dot(e.astype(BF16), vw_ref[0, pl.ds(ws, wk), :]) * (1.0 / jnp.sum(e, axis=-1, keepdims=True))

    _nsa_combine(o_cmp, o_slc, o_win, gates_ref, ngate_ref, o_ref, tq)


def _nsa_bounded(q4, q0, qi, bound, gates_ref, ngate_ref, kcmp_ref, vcmp_ref, ks_ref, vs_ref, kw_ref, vw_ref,
                 ovl_ref, eaug_ref, cmask_ref, wmask_ref, tri_ref, o_ref,
                 ksa_scr, kwa_scr, kca_scr, l_scr, acc_scr, *, seq):
    tq, tk = NSA_TQ, NSA_TK
    n_slc = seq // SLC_BLOCK
    n_cmp_rows = kca_scr.shape[0]

    @pl.when(qi == 0)
    def _():
        one_col = eaug_ref[...]
        lane = lax.broadcasted_iota(jnp.int32, one_col.shape, 1)
        ksa_scr[:, :HEAD_DIM] = ks_ref[0]
        ksa_scr[:, HEAD_DIM:] = one_col
        one_col = jnp.where(lane == AUG_ONE_LANE, one_col, jnp.zeros_like(one_col))
        kwa_scr[:, :HEAD_DIM] = kw_ref[0]
        kwa_scr[:, HEAD_DIM:] = one_col
        kca_scr[:, :HEAD_DIM] = kcmp_ref[0, 0]
        kca_scr[:, HEAD_DIM:] = one_col[:n_cmp_rows]

    lane = lax.broadcasted_iota(jnp.int32, (tq, LANES), 1)
    shift_cols = jnp.where(lane == AUG_ONE_LANE, -bound, 0.0)
    qa = jnp.concatenate([q4, _rep_heads(shift_cols.astype(BF16))], axis=1)

    e = jnp.exp2(_dot_nt(qa, kca_scr[...])) * _rep_heads(cmask_ref[...])
    l = jnp.sum(e, axis=-1, keepdims=True)
    p = e * jnp.where(l > 0.0, 1.0 / l, 0.0)
    o_cmp = _dot(p.astype(BF16), vcmp_ref[0, 0])

    sel = _select_blocks(p, ovl_ref[...], q0, tq, n_slc)
    sel_cols = jnp.where(lane < n_slc, jnp.where(sel > 0.5, 0.0, MASKED), shift_cols)
    qs = jnp.concatenate([q4, _rep_heads(sel_cols.astype(BF16))], axis=1)

    l_scr[...] = jnp.zeros(l_scr.shape, F32)
    acc_scr[...] = jnp.zeros(acc_scr.shape, F32)

    def accumulate(k0, width, mask):
        p_t = jnp.exp2(_dot_nt(qs, ksa_scr[pl.ds(k0, width), :]))
        if mask is not None:
            p_t = p_t * mask
        l_scr[...] += _lane_tile_sum(p_t)
        acc_scr[...] += _dot(p_t.astype(BF16), vs_ref[0, pl.ds(k0, width), :])

    big = 2 * tk

    def slc_body(kt, carry):
        accumulate(pl.multiple_of(kt * big, big), big, None)
        return carry

    n_full = lax.div(q0, tk)
    n_big = lax.div(n_full, 2)
    lax.fori_loop(0, n_big, slc_body, 0)

    @pl.when(n_full > 2 * n_big)
    def _():
        accumulate(pl.multiple_of(n_big * big, big), tk, None)

    accumulate(pl.multiple_of(n_full * tk, tk), tk, _rep_heads(tri_ref[...]))
    o_slc = _flash_finish(l_scr, acc_scr)

    wk = WINDOW + tq
    ws = pl.multiple_of(jnp.maximum(q0 - WINDOW, 0), tq)
    e = jnp.exp2(_dot_nt(qa, kwa_scr[pl.ds(ws, wk), :])) * _rep_heads(wmask_ref[0])
    o_win = _dot(e.astype(BF16), vw_ref[0, pl.ds(ws, wk), :]) * (1.0 / jnp.sum(e, axis=-1, keepdims=True))

    _nsa_combine(o_cmp, o_slc, o_win, gates_ref, ngate_ref, o_ref, tq)


def _nsa_kernel(bound_ref, q_ref, ngate_ref, gates_ref, kcmp_ref, vcmp_ref, ks_ref, vs_ref, kw_ref, vw_ref,
                ovl_ref, eaug_ref, cmask_ref, wmask_ref, tri_ref, o_ref,
                bias_scr, m_scr, l_scr, acc_scr, ksa_scr, kwa_scr, kca_scr, *, seq):
    qi = pl.program_id(2)
    q0 = qi * NSA_TQ
    q4 = _stack_heads(q_ref[0])
    bound = bound_ref[0]

    @pl.when(bound <= MAX_SCORE_BOUND)
    def _():
        _nsa_bounded(q4, q0, qi, bound, gates_ref, ngate_ref, kcmp_ref, vcmp_ref, ks_ref, vs_ref, kw_ref, vw_ref,
                     ovl_ref, eaug_ref, cmask_ref, wmask_ref, tri_ref, o_ref,
                     ksa_scr, kwa_scr, kca_scr, l_scr, acc_scr, seq=seq)

    @pl.when(bound > MAX_SCORE_BOUND)
    def _():
        _nsa_general(q4, q0, qi, gates_ref, ngate_ref, kcmp_ref, vcmp_ref, ks_ref, vs_ref, kw_ref, vw_ref,
                     ovl_ref, eaug_ref, o_ref, bias_scr, m_scr, l_scr, acc_scr, seq=seq)


def _nsa_attention(proj, gates, kcmp, vcmp, bound, tables, offs):
    b, s, _ = proj.shape
    g = NSA_KV_HEADS
    tq = NSA_TQ
    nq = s // tq
    gw = NSA_GROUP * HEAD_DIM
    n_cmp_rows = s // CMP_STRIDE
    ovl_t, eaug, cmask, wmask, tri = tables
    q_spec = lambda off: pl.BlockSpec((1, tq, gw), lambda bi, gi, qi: (bi, qi, off + gi))
    kv_spec = lambda off: pl.BlockSpec((1, s, HEAD_DIM), lambda bi, gi, qi: (bi, 0, off + gi))
    cmp_spec = pl.BlockSpec((1, 1, n_cmp_rows, HEAD_DIM), lambda bi, gi, qi: (bi, gi, 0, 0))
    full = lambda shape: pl.BlockSpec(shape, lambda bi, gi, qi: (0,) * len(shape))
    n_wpat = wmask.shape[0]
    rows4 = NSA_GROUP * tq
    return pl.pallas_call(
        functools.partial(_nsa_kernel, seq=s),
        grid=(b, g, nq),
        in_specs=[pl.BlockSpec(memory_space=pltpu.SMEM),
                  q_spec(offs["nq"] // gw), q_spec(offs["ngate"] // gw),
                  pl.BlockSpec((tq, LANES), lambda bi, gi, qi: (bi * nq + qi, gi)),
                  cmp_spec, cmp_spec,
                  kv_spec(offs["ks"] // HEAD_DIM), kv_spec(offs["vs"] // HEAD_DIM),
                  kv_spec(offs["kw"] // HEAD_DIM), kv_spec(offs["vw"] // HEAD_DIM),
                  full(ovl_t.shape), full(eaug.shape),
                  pl.BlockSpec((tq, LANES), lambda bi, gi, qi: (qi, 0)),
                  pl.BlockSpec((1,) + wmask.shape[1:], lambda bi, gi, qi: (jnp.minimum(qi, n_wpat - 1), 0, 0)),
                  full(tri.shape)],
        out_specs=pl.BlockSpec((1, tq, gw), lambda bi, gi, qi: (bi, qi, gi)),
        out_shape=jax.ShapeDtypeStruct((b, s, NSA_WIDTH), BF16),
        scratch_shapes=[pltpu.VMEM((s // NSA_TK, tq, NSA_TK), F32),
                        pltpu.VMEM((rows4, LANES), F32), pltpu.VMEM((rows4, LANES), F32),
                        pltpu.VMEM((rows4, HEAD_DIM), F32),
                        pltpu.VMEM((s, 2 * HEAD_DIM), BF16), pltpu.VMEM((s, 2 * HEAD_DIM), BF16),
                        pltpu.VMEM((n_cmp_rows, 2 * HEAD_DIM), BF16)],
        compiler_params=pltpu.CompilerParams(
            dimension_semantics=("parallel", "parallel", "arbitrary"), vmem_limit_bytes=VMEM_LIMIT),
        name="nsa_attention",
    )(bound, proj, proj, gates, kcmp, vcmp, proj, proj, proj, proj, ovl_t, eaug, cmask, wmask, tri)


def _diff_general(q1, q2, qi, k_ref, v_ref, m1, l1, a1, m2, l2, a2):
    tq, tk = DIFF_TQ, DIFF_TK
    _flash_init(m1, l1, a1)
    _flash_init(m2, l2, a2)

    def step(rows, k0, width, bias):
        k = k_ref[0, pl.ds(k0, width), :]
        v = v_ref[0, pl.ds(k0, width), :]
        _flash_step(q1[rows], k[:, :HEAD_DIM], v, bias, m1.at[rows], l1.at[rows], a1.at[rows])
        _flash_step(q2[rows], k[:, HEAD_DIM:], v, bias, m2.at[rows], l2.at[rows], a2.at[rows])

    def body(kt, carry):
        step(slice(0, tq), pl.multiple_of(kt * tk, tk), tk, None)
        return carry

    lax.fori_loop(0, qi * (tq // tk), body, 0)
    dd = DIFF_DIAG
    q0 = qi * tq
    for c in range(tq // dd):
        n_rows = tq - c * dd
        ri = lax.broadcasted_iota(jnp.int32, (n_rows, dd), 0)
        ci = lax.broadcasted_iota(jnp.int32, (n_rows, dd), 1)
        step(slice(c * dd, tq), pl.multiple_of(q0 + c * dd, dd), dd, jnp.where(ri >= ci, 0.0, MASKED))
    return _flash_finish(l1, a1), _flash_finish(l2, a2)


def _lane_tile_sum(p):
    cols = p[:, 0:LANES]
    for c in range(1, p.shape[1] // LANES):
        cols = cols + p[:, c * LANES:(c + 1) * LANES]
    return cols


def _diff_bounded_stream(q, n, bound, k_ref, v_ref, tri_ref, kcols):
    tq, dd = DIFF_TQ, DIFF_DIAG
    kmain = n * tq + dd
    p = jnp.exp2(_dot_nt(q, k_ref[0, 0:kmain, kcols]) - bound)
    p_diag = p[:, kmain - dd:] * tri_ref[...]
    p = p_diag if kmain == dd else jnp.concatenate([p[:, :kmain - dd], p_diag], axis=1)
    l = _lane_tile_sum(p)
    acc = _dot(p.astype(BF16), v_ref[0, 0:kmain, :])
    pc = jnp.exp2(_dot_nt(q[dd:], k_ref[0, kmain:kmain + dd, kcols]) - bound) * tri_ref[0:tq - dd, :]
    l = jnp.concatenate([l[:dd], l[dd:] + _lane_tile_sum(pc)], axis=0)
    acc = jnp.concatenate([acc[:dd], acc[dd:] + _dot(pc.astype(BF16), v_ref[0, kmain:kmain + dd, :])], axis=0)
    return acc * (1.0 / jnp.sum(l, axis=-1, keepdims=True))


def _diff_kernel(bound_ref, q_ref, gate_ref, k_ref, v_ref, lam_ref, tri_ref, o_ref, m1, l1, a1, m2, l2, a2,
                 *, lambda_init, n_tiles):
    qi = pl.program_id(2)
    q = q_ref[0]
    q1 = q[:, :HEAD_DIM]
    q2 = q[:, HEAD_DIM:]
    bound = bound_ref[0]

    def finish(o1, o2):
        lp = lam_ref[...]
        lam = (jnp.exp(jnp.sum(lp[0:1] * lp[1:2], axis=-1, keepdims=True))
               - jnp.exp(jnp.sum(lp[2:3] * lp[3:4], axis=-1, keepdims=True)) + lambda_init)
        o = _rms(o1 - lam * o2) * (1.0 - lambda_init)
        o_ref[0] = (o * gate_ref[0].astype(F32)).astype(o_ref.dtype)

    def bounded_variant(n):
        @pl.when((bound <= MAX_SCORE_BOUND) & (qi == n))
        def _():
            finish(_diff_bounded_stream(q1, n, bound, k_ref, v_ref, tri_ref, slice(0, HEAD_DIM)),
                   _diff_bounded_stream(q2, n, bound, k_ref, v_ref, tri_ref, slice(HEAD_DIM, 2 * HEAD_DIM)))

    for n in range(n_tiles):
        bounded_variant(n)

    @pl.when(bound > MAX_SCORE_BOUND)
    def _():
        finish(*_diff_general(q1, q2, qi, k_ref, v_ref, m1, l1, a1, m2, l2, a2))


def _diff_attention(proj, lam_params, bound, lambda_init):
    b, s, _ = proj.shape
    h = DIFF_HEADS
    tq, dd = DIFF_TQ, DIFF_DIAG
    assert tq == 2 * dd
    w = DIFF_V_DIM
    tri = jnp.asarray(np.arange(tq)[:, None] >= np.arange(dd)[None, :], F32)
    q_spec = lambda off: pl.BlockSpec((1, tq, w), lambda bi, hi, qi: (bi, qi, off + hi))
    kv_spec = lambda off: pl.BlockSpec((1, s, w), lambda bi, hi, qi: (bi, 0, off + hi))
    stat = pltpu.VMEM((tq, LANES), F32)
    acc = pltpu.VMEM((tq, w), F32)
    return pl.pallas_call(
        functools.partial(_diff_kernel, lambda_init=lambda_init, n_tiles=s // tq),
        grid=(b, h, s // tq),
        in_specs=[pl.BlockSpec(memory_space=pltpu.SMEM),
                  q_spec(0), q_spec(3 * h), kv_spec(h), kv_spec(2 * h),
                  pl.BlockSpec(lam_params.shape, lambda bi, hi, qi: (0, 0)),
                  pl.BlockSpec(tri.shape, lambda bi, hi, qi: (0, 0))],
        out_specs=pl.BlockSpec((1, tq, w), lambda bi, hi, qi: (bi, qi, hi)),
        out_shape=jax.ShapeDtypeStruct((b, s, DIFF_WIDTH), BF16),
        scratch_shapes=[stat, stat, acc, stat, stat, acc],
        compiler_params=pltpu.CompilerParams(
            dimension_semantics=("parallel", "parallel", "arbitrary"), vmem_limit_bytes=VMEM_LIMIT),
        name="diff_attention",
    )(bound, proj, proj, proj, proj, lam_params, tri)


def _rope_tables(pos):
    inv = 1.0 / (ROPE_THETA ** (np.arange(0, HEAD_DIM, 2, dtype=np.float64) / HEAD_DIM))
    ang = np.asarray(pos, np.float64)[:, None] * inv[None, :]
    cos, sin = np.cos(ang), np.sin(ang)
    return (jnp.asarray(np.concatenate([cos, cos], axis=-1), F32),
            jnp.asarray(np.concatenate([-sin, sin], axis=-1), F32))


def _retention_tables():
    h, c = RET_HEADS, RET_CHUNK
    log_g = np.log1p(-np.exp2(-5.0 - np.arange(h, dtype=np.float64)))
    j = np.arange(c, dtype=np.float64)
    diff = j[:, None] - j[None, :]
    intra = np.where(diff >= 0, np.exp(log_g[:, None, None] * np.maximum(diff, 0.0)), 0.0)
    q_dec = np.exp(log_g[:, None] * (j + 1.0))
    k_dec = np.exp(log_g[:, None] * (c - 1.0 - j))
    chunk_dec = np.exp(log_g * c)
    wide = lambda t: jnp.asarray(np.broadcast_to(t[:, :, None], (h, t.shape[1], HEAD_DIM)), F32)
    return jnp.asarray(intra, F32), wide(q_dec), wide(k_dec), wide(chunk_dec[:, None])


def _selection_tables(seq):
    tq, tk = NSA_TQ, NSA_TK
    n_cmp_rows = seq // CMP_STRIDE
    n_slc = seq // SLC_BLOCK
    assert tq == tk and n_slc <= AUG_ONE_LANE < LANES
    c_start = np.arange(n_cmp_rows) * CMP_STRIDE
    s_start = np.arange(n_slc) * SLC_BLOCK
    overlap_t = ((c_start[None, :] <= s_start[:, None] + SLC_BLOCK - 1)
                 & (c_start[None, :] + CMP_BLOCK - 1 >= s_start[:, None]))
    lane = np.arange(LANES)[None, :]
    key = np.arange(seq)[:, None]
    eaug = ((key // SLC_BLOCK) == lane) | (lane == AUG_ONE_LANE)
    cmask = lane * CMP_STRIDE + CMP_BLOCK - 1 <= key
    r = np.arange(tq)[:, None]
    c = np.arange(WINDOW + tq)[None, :]
    wmask = []
    for pat in range(WINDOW // tq + 1):
        d = min(pat * tq, WINDOW) + r - c
        wmask.append((d >= 0) & (d < WINDOW))
    tri = np.arange(tq)[:, None] >= np.arange(tk)[None, :]
    return (jnp.asarray(overlap_t, BF16), jnp.asarray(eaug, BF16), jnp.asarray(cmask, F32),
            jnp.asarray(np.stack(wmask), F32), jnp.asarray(tri, F32))


def _tile_modes(segments, tn):
    off, kinds, col = {}, [], 0
    for name, width, kind in segments:
        off[name] = col
        kinds += [kind] * (width // LANES)
        col += width
    per = tn // LANES
    tiles = [kinds[i:i + per] for i in range(0, len(kinds), per)]
    assert col % tn == 0
    modes = []
    for j, tile in enumerate(tiles):
        if modes and modes[-1][2] == tile:
            modes[-1] = (modes[-1][0], j + 1, tile)
        else:
            modes.append((j, j + 1, tile))
    return modes, off


def _l0_tile_modes():
    plain, silu = ("plain", 0), ("silu", 0)
    k_norm = ("nrope", L0_TAB_K)
    return _tile_modes([
        ("rq", RET_WIDTH, ("rope", L0_TAB_ROPE)), ("rk", RET_WIDTH, ("rope", L0_TAB_ROPE_SCALED)),
        ("rv", RET_WIDTH, plain), ("rgate", RET_WIDTH, silu), ("nq", NSA_WIDTH, ("nrope", L0_TAB_Q)),
        ("kc", NSA_KV_WIDTH, plain), ("vc", NSA_KV_WIDTH, plain), ("ks", NSA_KV_WIDTH, k_norm),
        ("vs", NSA_KV_WIDTH, plain), ("kw", NSA_KV_WIDTH, k_norm), ("vw", NSA_KV_WIDTH, plain),
        ("ngate", NSA_WIDTH, silu)], PROJ_TN)


def _l1_tile_modes():
    return _tile_modes([("q", DIFF_WIDTH, ("nrope", L1_TAB_Q)), ("k", DIFF_WIDTH, ("nrope", L1_TAB_K)),
                        ("v", DIFF_WIDTH, ("plain", 0)), ("gate", DIFF_WIDTH, ("silu", 0))], L1_PROJ_TN)[0]


def _column_tiles(w, tn):
    d, n = w.shape
    return w.astype(BF16).reshape(d, n // tn, tn).transpose(1, 0, 2)


def _rope_pair(cos, sin_signed, gain=None, scale=1.0):
    if gain is None:
        return [cos * scale, sin_signed * scale]
    return [cos * (gain * scale)[None, :], sin_signed * (jnp.roll(gain, HALF) * scale)[None, :]]


def kernel(x, l0_norm_g, l0_w_in, l0_w_out, l0_nsa_q_norm_g, l0_nsa_k_norm_g, l0_cmp_pe_k, l0_cmp_w1_k, l0_cmp_w2_k, l0_cmp_pe_v, l0_cmp_w1_v, l0_cmp_w2_v, l1_norm_g, l1_w_in, l1_w_out, l1_q_norm_g, l1_k_norm_g, l1_lambda_q1, l1_lambda_k1, l1_lambda_q2, l1_lambda_k2):
    b, s, d = x.shape
    m = b * s
    x2d = x.reshape(m, d)
    cos, sin = _rope_tables(np.arange(s))
    cos_c, sin_c = _rope_tables(np.arange(s // CMP_STRIDE) * CMP_STRIDE + CMP_BLOCK - 1)
    intra, qdec, kdec, cdec = _retention_tables()
    nsa_tables = _selection_tables(s)

    modes0, off = _l0_tile_modes()
    w0 = _column_tiles(l0_w_in[:, :AB_MAIN_COLS], PROJ_TN)
    wg = l0_w_in[:, AB_MAIN_COLS:].reshape(d, 3, NSA_KV_HEADS, NSA_GROUP).transpose(0, 2, 1, 3)
    wg = wg.reshape(d, NSA_KV_HEADS, 3 * NSA_GROUP)
    wg = jnp.pad(wg, ((0, 0), (0, 0), (0, LANES - 3 * NSA_GROUP))).reshape(d, NSA_KV_HEADS * LANES).astype(BF16)
    tabs0 = jnp.stack(_rope_pair(cos, sin) + _rope_pair(cos, sin, scale=QK_SCALE)
                      + _rope_pair(cos, sin, l0_nsa_q_norm_g, Q_SCALE) + _rope_pair(cos, sin, l0_nsa_k_norm_g))
    proj0, gates = _norm_proj(x2d, l0_norm_g, w0, tabs0, modes0, s, PROJ_TN, wg=wg)
    proj0 = proj0.reshape(b, s, AB_MAIN_COLS)
    y_ret = _retention(proj0, intra, qdec, kdec, cdec)
    w1k = l0_cmp_w1_k.astype(BF16).reshape(CMP_BLOCK, HEAD_DIM, HEAD_DIM)
    w1v = l0_cmp_w1_v.astype(BF16).reshape(CMP_BLOCK, HEAD_DIM, HEAD_DIM)
    kcmp, vcmp = _nsa_compress(proj0, off["kc"] // HEAD_DIM, off["vc"] // HEAD_DIM,
                               l0_cmp_pe_k, w1k, l0_cmp_w2_k.astype(BF16),
                               l0_cmp_pe_v, w1v, l0_cmp_w2_v.astype(BF16),
                               l0_nsa_k_norm_g, cos_c, sin_c)
    bound = (BOUND_MARGIN * HEAD_DIM * Q_SCALE * jnp.max(jnp.abs(l0_nsa_q_norm_g))
             * jnp.max(jnp.abs(l0_nsa_k_norm_g))).reshape(1).astype(F32)
    y_nsa = _nsa_attention(proj0, gates, kcmp, vcmp, bound, nsa_tables, off)
    w_out0 = l0_w_out.astype(BF16)
    x1 = _out_proj([y_ret.reshape(m, RET_WIDTH), y_nsa.reshape(m, NSA_WIDTH)],
                   [w_out0[:RET_WIDTH], w_out0[RET_WIDTH:]], x2d)

    lambda_init = 0.8 - 0.6 * math.exp(-0.3 * 1)
    tabs1 = jnp.stack(_rope_pair(cos, sin, l1_q_norm_g, Q_SCALE) + _rope_pair(cos, sin, l1_k_norm_g))
    proj1 = _norm_proj(x1, l1_norm_g, _column_tiles(l1_w_in, L1_PROJ_TN), tabs1, _l1_tile_modes(), s, L1_PROJ_TN)[0]
    lam_params = jnp.stack([l1_lambda_q1, l1_lambda_k1, l1_lambda_q2, l1_lambda_k2]).astype(F32)
    bound1 = (BOUND_MARGIN * HEAD_DIM * Q_SCALE * jnp.max(jnp.abs(l1_q_norm_g))
              * jnp.max(jnp.abs(l1_k_norm_g))).reshape(1).astype(F32)
    y1 = _diff_attention(proj1.reshape(b, s, C_IN_COLS), lam_params, bound1, lambda_init)
    out = _out_proj([y1.reshape(m, DIFF_WIDTH)], [l1_w_out.astype(BF16)], x1)
    return out.reshape(b, s, d)
```

```python
import functools
import math

import numpy as np
import jax
import jax.numpy as jnp
from jax import lax
from jax.experimental import pallas as pl
from jax.experimental.pallas import tpu as pltpu

F32 = jnp.float32
BF16 = jnp.bfloat16

D_MODEL = 2048
HEAD_DIM = 128
HALF = HEAD_DIM // 2
ROPE_THETA = 10000.0
EPS = 1e-6
RET_HEADS = 8
RET_CHUNK = 256
RET_HEADS_PER_STEP = 2
NSA_HEADS = 8
NSA_KV_HEADS = 2
NSA_GROUP = NSA_HEADS // NSA_KV_HEADS
CMP_BLOCK = 32
CMP_STRIDE = 16
SLC_BLOCK = 64
SLC_TOPK = 16
N_LOCAL_BLOCKS = 2
WINDOW = 512
DIFF_HEADS = 8
DIFF_V_DIM = 2 * HEAD_DIM
QK_SCALE = HEAD_DIM ** -0.5
LOG2E = math.log2(math.e)
Q_SCALE = QK_SCALE * LOG2E

RET_WIDTH = RET_HEADS * HEAD_DIM
NSA_WIDTH = NSA_HEADS * HEAD_DIM
NSA_KV_WIDTH = NSA_KV_HEADS * HEAD_DIM
AB_MAIN_COLS = 4 * RET_WIDTH + 2 * NSA_WIDTH + 6 * NSA_KV_WIDTH
N_GATE_COLS = 3 * NSA_HEADS
DIFF_WIDTH = DIFF_HEADS * DIFF_V_DIM
C_IN_COLS = 4 * DIFF_WIDTH

LANES = 128
MASKED = -1e30
M_INIT = -1e29
MAX_SCORE_BOUND = 60.0
BOUND_MARGIN = 1.05
AUG_ONE_LANE = 32
VMEM_LIMIT = 56 * 1024 * 1024

PROJ_TM = 1024
PROJ_TN = 1280
L1_PROJ_TN = 1024
OPROJ_TM = 2048
OPROJ_TN = 512
PROJ_SUB = 256
L0_TAB_ROPE, L0_TAB_ROPE_SCALED, L0_TAB_Q, L0_TAB_K = range(4)
L1_TAB_Q, L1_TAB_K = range(2)
NSA_TQ = 512
NSA_TK = 512
DIFF_TQ = 512
DIFF_TK = 512
DIFF_DIAG = 256


def _dot(a, b):
    return jnp.dot(a, b, preferred_element_type=F32)


def _dot_nt(a, b):
    return lax.dot_general(a, b, (((1,), (1,)), ((), ())), preferred_element_type=F32)


def _dot_tn(a, b):
    return lax.dot_general(a, b, (((0,), (0,)), ((), ())), preferred_element_type=F32)


def _silu(x):
    return x / (1.0 + jnp.exp(-x))


def _rms(x):
    return x * lax.rsqrt(jnp.mean(x * x, axis=-1, keepdims=True) + EPS)


def _rope(x, cos, sin_signed):
    return x * cos + pltpu.roll(x, HALF, axis=1) * sin_signed


def _rope_tab(seg, t, tabs_ref, rows):
    return seg * tabs_ref[2 * t, rows, :] + pltpu.roll(seg, HALF, axis=1) * tabs_ref[2 * t + 1, rows, :]


def _tile_epilogue(acc, kinds, tabs_ref, rows, o_ref):
    n_seg = len(kinds)
    ri = lax.broadcasted_iota(jnp.int32, (2 * LANES, 2 * LANES), 0)
    ci = lax.broadcasted_iota(jnp.int32, (2 * LANES, 2 * LANES), 1)
    pair_mean = jnp.where((ri < LANES) == (ci < LANES), 1.0 / HEAD_DIM, 0.0).astype(BF16)
    c = 0
    while c < n_seg:
        op, t = kinds[c]
        cols = slice(c * LANES, (c + 1) * LANES)
        if op == "nrope" and c + 1 < n_seg and kinds[c + 1][0] == "nrope":
            cols2 = slice(c * LANES, (c + 2) * LANES)
            seg2 = acc[:, cols2]
            seg2 = seg2 * lax.rsqrt(_dot((seg2 * seg2).astype(BF16), pair_mean) + EPS)
            for half in range(2):
                out = _rope_tab(seg2[:, half * LANES:(half + 1) * LANES], kinds[c + half][1], tabs_ref, rows)
                o_ref[rows, (c + half) * LANES:(c + half + 1) * LANES] = out.astype(o_ref.dtype)
            c += 2
            continue
        seg = acc[:, cols]
        if op == "silu":
            seg = _silu(seg)
        elif op == "nrope":
            seg = seg * lax.rsqrt(_dot((seg * seg).astype(BF16), pair_mean[:LANES, :LANES]) + EPS)
            seg = _rope_tab(seg, t, tabs_ref, rows)
        elif op == "rope":
            seg = _rope_tab(seg, t, tabs_ref, rows)
        o_ref[rows, cols] = seg.astype(o_ref.dtype)
        c += 1


def _proj_kernel(*refs, tile_modes, has_gates):
    if has_gates:
        x_ref, g_ref, w_ref, tabs_ref, wg_ref, o_ref, og_ref, h_scr = refs
    else:
        x_ref, g_ref, w_ref, tabs_ref, o_ref, h_scr = refs
    j = pl.program_id(1)
    n_sub = h_scr.shape[0] // PROJ_SUB

    def make_branch(cond, kinds, first):
        @pl.when(cond)
        def _():
            for r in range(n_sub):
                rows = slice(r * PROJ_SUB, (r + 1) * PROJ_SUB)
                if first:
                    h = (_rms(x_ref[rows, :]) * g_ref[...]).astype(BF16)
                    h_scr[rows, :] = h
                    if has_gates:
                        og_ref[rows, :] = 1.0 / (1.0 + jnp.exp(-_dot(h, wg_ref[...])))
                else:
                    h = h_scr[rows, :]
                _tile_epilogue(_dot(h, w_ref[...]), kinds, tabs_ref, rows, o_ref)

    for lo, hi, kinds in tile_modes:
        if lo == 0:
            make_branch(j == 0, kinds, True)
            lo = 1
        if hi > lo:
            make_branch((j >= lo) & (j < hi), kinds, False)


def _norm_proj(x2d, g, w, tabs, tile_modes, seq, tn, wg=None):
    m, d = x2d.shape
    tm = PROJ_TM
    n = tile_modes[-1][1] * tn
    s_tiles = seq // tm
    has_gates = wg is not None
    in_specs = [
        pl.BlockSpec((tm, d), lambda i, j: (i, 0)),
        pl.BlockSpec((1, d), lambda i, j: (0, 0)),
        pl.BlockSpec((d, tn), lambda i, j: (0, j)),
        pl.BlockSpec((tabs.shape[0], tm, LANES), lambda i, j: (0, i % s_tiles, 0)),
    ]
    args = [x2d, g.reshape(1, d), w, tabs]
    out_shape = [jax.ShapeDtypeStruct((m, n), BF16)]
    out_specs = [pl.BlockSpec((tm, tn), lambda i, j: (i, j))]
    if has_gates:
        ng = wg.shape[1]
        in_specs.append(pl.BlockSpec((d, ng), lambda i, j: (0, 0)))
        args.append(wg)
        out_shape.append(jax.ShapeDtypeStruct((m, ng), F32))
        out_specs.append(pl.BlockSpec((tm, ng), lambda i, j: (i, 0)))
    return pl.pallas_call(
        functools.partial(_proj_kernel, tile_modes=tile_modes, has_gates=has_gates),
        grid=(m // tm, n // tn),
        in_specs=in_specs,
        out_specs=out_specs,
        out_shape=out_shape,
        scratch_shapes=[pltpu.VMEM((tm, d), BF16)],
        compiler_params=pltpu.CompilerParams(
            dimension_semantics=("parallel", "arbitrary"), vmem_limit_bytes=VMEM_LIMIT),
        name="norm_proj",
    )(*args)


def _oproj_kernel(*refs, n_terms):
    y_refs, w_refs = refs[:n_terms], refs[n_terms:2 * n_terms]
    x_ref, o_ref = refs[2 * n_terms:]
    acc = x_ref[...]
    for y_ref, w_ref in zip(y_refs, w_refs):
        acc = acc + _dot(y_ref[...], w_ref[...])
    o_ref[...] = acc


def _out_proj(ys, ws, x2d):
    m, n = x2d.shape
    tm, tn = OPROJ_TM, OPROJ_TN
    y_specs = [pl.BlockSpec((tm, y.shape[1]), lambda i, j: (i, 0)) for y in ys]
    w_specs = [pl.BlockSpec((w.shape[0], tn), lambda i, j: (0, j)) for w in ws]
    return pl.pallas_call(
        functools.partial(_oproj_kernel, n_terms=len(ys)),
        grid=(m // tm, n // tn),
        in_specs=y_specs + w_specs + [pl.BlockSpec((tm, tn), lambda i, j: (i, j))],
        out_specs=pl.BlockSpec((tm, tn), lambda i, j: (i, j)),
        out_shape=jax.ShapeDtypeStruct((m, n), F32),
        compiler_params=pltpu.CompilerParams(
            dimension_semantics=("parallel", "arbitrary"), vmem_limit_bytes=VMEM_LIMIT),
        name="out_proj",
    )(*ys, *ws, x2d)


def _ret_kernel(q_ref, k_ref, v_ref, gate_ref, intra_ref, qdec_ref, kdec_ref, cdec_ref, o_ref, *, n_chunks):
    c = RET_CHUNK
    states = [jnp.zeros((HEAD_DIM, HEAD_DIM), F32)] * RET_HEADS_PER_STEP
    for n in range(n_chunks):
        rows = slice(n * c, (n + 1) * c)
        for hh in range(RET_HEADS_PER_STEP):
            cols = slice(hh * HEAD_DIM, (hh + 1) * HEAD_DIM)
            q = q_ref[0, rows, cols]
            k = k_ref[0, rows, cols]
            v = v_ref[0, rows, cols]
            scores = _dot_nt(q, k) * intra_ref[hh]
            inner = _dot(scores.astype(BF16), v)
            cross = _dot((q.astype(F32) * qdec_ref[hh]).astype(BF16), states[hh].astype(BF16))
            kv = _dot_tn((k.astype(F32) * kdec_ref[hh]).astype(BF16), v)
            states[hh] = states[hh] * cdec_ref[hh] + kv
            o = _rms(inner + cross) * gate_ref[0, rows, cols].astype(F32)
            o_ref[0, rows, cols] = o.astype(o_ref.dtype)


def _retention(proj, intra, qdec, kdec, cdec):
    b, s, _ = proj.shape
    per = RET_HEADS_PER_STEP
    h = RET_HEADS // per
    y_shape = (b, s, RET_WIDTH)
    head = lambda off: pl.BlockSpec((1, s, per * HEAD_DIM), lambda bi, hi: (bi, 0, off + hi))
    table = lambda rows, cols=HEAD_DIM: pl.BlockSpec((per, rows, cols), lambda bi, hi: (hi, 0, 0))
    return pl.pallas_call(
        functools.partial(_ret_kernel, n_chunks=s // RET_CHUNK),
        grid=(b, h),
        in_specs=[head(0), head(h), head(2 * h), head(3 * h),
                  table(RET_CHUNK, RET_CHUNK), table(RET_CHUNK), table(RET_CHUNK), table(1)],
        out_specs=pl.BlockSpec((1, s, per * HEAD_DIM), lambda bi, hi: (bi, 0, hi)),
        out_shape=jax.ShapeDtypeStruct(y_shape, BF16),
        compiler_params=pltpu.CompilerParams(
            dimension_semantics=("parallel", "arbitrary"), vmem_limit_bytes=VMEM_LIMIT),
        name="retention",
    )(proj, proj, proj, proj, intra, qdec, kdec, cdec)


def _cmp_kernel(kc_ref, vc_ref, pek_ref, w1k_ref, w2k_ref, pev_ref, w1v_ref, w2v_ref, kg_ref,
                cos_ref, sin_ref, ko_ref, vo_ref, t_scr, *, seq):
    n_rows = seq // CMP_STRIDE

    def compress(src_ref, pe_ref, w1_ref, w2_ref):
        t_scr[0:seq, :] = src_ref[0].astype(F32)
        t_scr[seq:seq + CMP_STRIDE, :] = jnp.zeros((CMP_STRIDE, HEAD_DIM), F32)
        acc = jnp.zeros((n_rows, HEAD_DIM), F32)
        for r in range(CMP_BLOCK):
            rows = t_scr[pl.ds(r, n_rows, stride=CMP_STRIDE), :] + pe_ref[r:r + 1, :]
            acc = acc + _dot(rows.astype(BF16), w1_ref[r])
        return _dot(_silu(acc).astype(BF16), w2_ref[...])

    kc = compress(kc_ref, pek_ref, w1k_ref, w2k_ref)
    kc = _rope(_rms(kc) * kg_ref[...], cos_ref[...], sin_ref[...])
    ko_ref[0, 0] = kc.astype(ko_ref.dtype)
    vo_ref[0, 0] = compress(vc_ref, pev_ref, w1v_ref, w2v_ref).astype(vo_ref.dtype)


def _nsa_compress(proj, kc_off, vc_off, pe_k, w1_k, w2_k, pe_v, w1_v, w2_v, k_g, cos_c, sin_c):
    b, s, _ = proj.shape
    g = NSA_KV_HEADS
    n_rows = s // CMP_STRIDE
    head = lambda off: pl.BlockSpec((1, s, HEAD_DIM), lambda bi, gi: (bi, 0, off + gi))
    full = lambda shape: pl.BlockSpec(shape, lambda bi, gi: (0,) * len(shape))
    out_spec = pl.BlockSpec((1, 1, n_rows, HEAD_DIM), lambda bi, gi: (bi, gi, 0, 0))
    out_sds = jax.ShapeDtypeStruct((b, g, n_rows, HEAD_DIM), BF16)
    return pl.pallas_call(
        functools.partial(_cmp_kernel, seq=s),
        grid=(b, g),
        in_specs=[head(kc_off), head(vc_off),
                  full((CMP_BLOCK, HEAD_DIM)), full((CMP_BLOCK, HEAD_DIM, HEAD_DIM)), full((HEAD_DIM, HEAD_DIM)),
                  full((CMP_BLOCK, HEAD_DIM)), full((CMP_BLOCK, HEAD_DIM, HEAD_DIM)), full((HEAD_DIM, HEAD_DIM)),
                  full((1, HEAD_DIM)), full((n_rows, HEAD_DIM)), full((n_rows, HEAD_DIM))],
        out_specs=[out_spec, out_spec],
        out_shape=[out_sds, out_sds],
        scratch_shapes=[pltpu.VMEM((s + CMP_STRIDE, HEAD_DIM), F32)],
        compiler_params=pltpu.CompilerParams(
            dimension_semantics=("parallel", "arbitrary"), vmem_limit_bytes=VMEM_LIMIT),
        name="nsa_compress",
    )(proj, proj, pe_k, w1_k, w2_k, pe_v, w1_v, w2_v, k_g.reshape(1, HEAD_DIM), cos_c, sin_c)


def _flash_init(m_scr, l_scr, acc_scr):
    m_scr[...] = jnp.full(m_scr.shape, M_INIT, F32)
    l_scr[...] = jnp.zeros(l_scr.shape, F32)
    acc_scr[...] = jnp.zeros(acc_scr.shape, F32)


def _lane_tiles(x, width):
    return x if width == LANES else jnp.concatenate([x] * (width // LANES), axis=1)


def _flash_step(q, k, v, bias, m_scr, l_scr, acc_scr):
    s = _dot_nt(q, k)
    if bias is not None:
        s = s + bias
    tk = s.shape[1]
    m_old = m_scr[...]
    m_new = jnp.maximum(m_old, jnp.max(s, axis=-1, keepdims=True))
    alpha = jnp.exp2(m_old - m_new)
    p = jnp.exp2(s - _lane_tiles(m_new, tk))
    p_cols = p[:, 0:LANES]
    for c in range(1, tk // LANES):
        p_cols = p_cols + p[:, c * LANES:(c + 1) * LANES]
    l_scr[...] = alpha * l_scr[...] + p_cols
    acc_scr[...] = _lane_tiles(alpha, acc_scr.shape[1]) * acc_scr[...] + _dot(p.astype(BF16), v)
    m_scr[...] = m_new


def _flash_finish(l_scr, acc_scr):
    return acc_scr[...] * (1.0 / jnp.sum(l_scr[...], axis=-1, keepdims=True))


def _rep_heads(bias):
    return jnp.concatenate([bias] * NSA_GROUP, axis=0)


def _stack_heads(qblk):
    return jnp.concatenate([qblk[:, r * HEAD_DIM:(r + 1) * HEAD_DIM] for r in range(NSA_GROUP)], axis=0)


def _select_blocks(p, ovl, q0, tq, n_slc):
    psum = p[0:tq]
    for r in range(1, NSA_GROUP):
        psum = psum + p[r * tq:(r + 1) * tq]
    p_hi = psum.astype(BF16)
    rem = psum - p_hi.astype(F32)
    p_mid = rem.astype(BF16)
    p_lo = (rem - p_mid.astype(F32)).astype(BF16)
    imp = _dot_nt(ovl, p_hi) + _dot_nt(ovl, p_mid) + _dot_nt(ovl, p_lo)
    jb = lax.broadcasted_iota(jnp.int32, (n_slc, tq), 0)
    blk_t = jnp.right_shift(q0 + lax.broadcasted_iota(jnp.int32, (n_slc, tq), 1), int(math.log2(SLC_BLOCK)))
    back = blk_t - jb
    forced = (jb == 0) | ((back >= 0) & (back < N_LOCAL_BLOCKS))
    score = jnp.where(forced, 1e9, jnp.where(back >= 0, imp, -1e9))
    rank = jnp.zeros((n_slc, tq), F32)
    for mp in range(n_slc):
        row = score[mp:mp + 1, :]
        ahead = (row > score) | ((row == score) & (jb > mp))
        rank = rank + jnp.where(ahead, 1.0, 0.0)
    sel_t = jnp.where(rank < float(min(SLC_TOPK, n_slc)), 1.0, 0.0)
    sel_t = jnp.concatenate([sel_t, jnp.zeros((LANES - n_slc, tq), F32)], axis=0).astype(BF16)
    ri = lax.broadcasted_iota(jnp.int32, (tq, tq), 0)
    ci = lax.broadcasted_iota(jnp.int32, (tq, tq), 1)
    eye = jnp.where(ri == ci, 1.0, 0.0).astype(BF16)
    return _dot_nt(eye, sel_t)


def _nsa_combine(o_cmp, o_slc, o_win, gates_ref, ngate_ref, o_ref, tq):
    gates = gates_ref[...]
    for r in range(NSA_GROUP):
        rows = slice(r * tq, (r + 1) * tq)
        g_cmp = gates[:, r:r + 1]
        g_slc = gates[:, NSA_GROUP + r:NSA_GROUP + r + 1]
        g_win = gates[:, 2 * NSA_GROUP + r:2 * NSA_GROUP + r + 1]
        y = g_cmp * o_cmp[rows] + g_slc * o_slc[rows] + g_win * o_win[rows]
        cols = slice(r * HEAD_DIM, (r + 1) * HEAD_DIM)
        o_ref[0, :, cols] = (y * ngate_ref[0, :, cols].astype(F32)).astype(o_ref.dtype)


def _nsa_general(q4, q0, qi, gates_ref, ngate_ref, kcmp_ref, vcmp_ref, ks_ref, vs_ref, kw_ref, vw_ref,
                 ovl_ref, eaug_ref, o_ref, bias_scr, m_scr, l_scr, acc_scr, *, seq):
    tq, tk = NSA_TQ, NSA_TK
    n_slc = seq // SLC_BLOCK

    tpos = q0 + lax.broadcasted_iota(jnp.int32, (tq, LANES), 0)
    cidx = lax.broadcasted_iota(jnp.int32, (tq, LANES), 1)
    cbias = jnp.where(cidx * CMP_STRIDE + (CMP_BLOCK - 1) <= tpos, 0.0, MASKED)
    s = _dot_nt(q4, kcmp_ref[0, 0]) + _rep_heads(cbias)
    m = jnp.maximum(jnp.max(s, axis=-1, keepdims=True), M_INIT)
    e = jnp.exp2(s - m)
    l = jnp.sum(e, axis=-1, keepdims=True)
    p = e * (1.0 / jnp.maximum(l, 1e-30))
    o_cmp = _dot(p.astype(BF16), vcmp_ref[0, 0])

    sel = _select_blocks(p, ovl_ref[...], q0, tq, n_slc).astype(BF16)
    sel_keys = _dot_nt(sel, eaug_ref[...])
    for kt in range(seq // tk):
        bias_scr[kt] = jnp.where(sel_keys[:, kt * tk:(kt + 1) * tk] > 0.5, 0.0, MASKED)

    _flash_init(m_scr, l_scr, acc_scr)

    def slc_body(kt, carry):
        k0 = pl.multiple_of(kt * tk, tk)
        _flash_step(q4, ks_ref[0, pl.ds(k0, tk), :], vs_ref[0, pl.ds(k0, tk), :],
                    _rep_heads(bias_scr[kt]), m_scr, l_scr, acc_scr)
        return carry

    n_full = lax.div(q0, tk)
    lax.fori_loop(0, n_full, slc_body, 0)
    kd = pl.multiple_of(n_full * tk, tk)
    qpos = q0 + lax.broadcasted_iota(jnp.int32, (tq, tk), 0)
    kpos = kd + lax.broadcasted_iota(jnp.int32, (tq, tk), 1)
    causal = jnp.where(kpos <= qpos, 0.0, MASKED)
    _flash_step(q4, ks_ref[0, pl.ds(kd, tk), :], vs_ref[0, pl.ds(kd, tk), :],
                _rep_heads(bias_scr[n_full] + causal), m_scr, l_scr, acc_scr)
    o_slc = _flash_finish(l_scr, acc_scr)

    wk = WINDOW + tq
    ws = pl.multiple_of(jnp.maximum(q0 - WINDOW, 0), tq)
    d = (q0 - ws) + lax.broadcasted_iota(jnp.int32, (tq, wk), 0) - lax.broadcasted_iota(jnp.int32, (tq, wk), 1)
    wbias = jnp.where((d >= 0) & (d < WINDOW), 0.0, MASKED)
    s = _dot_nt(q4, kw_ref[0, pl.ds(ws, wk), :]) + _rep_heads(wbias)
    e = jnp.exp2(s - jnp.max(s, axis=-1, keepdims=True))
    o_win = _dot(e.astype(BF16), vw_ref[0, pl.ds(ws, wk), :]) * (1.0 / jnp.sum(e, axis=-1, keepdims=True))

    _nsa_combine(o_cmp, o_slc, o_win, gates_ref, ngate_ref, o_ref, tq)


def _nsa_bounded(q4, q0, qi, bound, gates_ref, ngate_ref, kcmp_ref, vcmp_ref, ks_ref, vs_ref, kw_ref, vw_ref,
                 ovl_ref, eaug_ref, cmask_ref, wmask_ref, tri_ref, o_ref,
                 ksa_scr, kwa_scr, kca_scr, l_scr, acc_scr, *, seq):
    tq, tk = NSA_TQ, NSA_TK
    n_slc = seq // SLC_BLOCK
    n_cmp_rows = kca_scr.shape[0]

    @pl.when(qi == 0)
    def _():
        one_col = eaug_ref[...]
        lane = lax.broadcasted_iota(jnp.int32, one_col.shape, 1)
        ksa_scr[:, :HEAD_DIM] = ks_ref[0]
        ksa_scr[:, HEAD_DIM:] = one_col
        one_col = jnp.where(lane == AUG_ONE_LANE, one_col, jnp.zeros_like(one_col))
        kwa_scr[:, :HEAD_DIM] = kw_ref[0]
        kwa_scr[:, HEAD_DIM:] = one_col
        kca_scr[:, :HEAD_DIM] = kcmp_ref[0, 0]
        kca_scr[:, HEAD_DIM:] = one_col[:n_cmp_rows]

    lane = lax.broadcasted_iota(jnp.int32, (tq, LANES), 1)
    shift_cols = jnp.where(lane == AUG_ONE_LANE, -bound, 0.0)
    qa = jnp.concatenate([q4, _rep_heads(shift_cols.astype(BF16))], axis=1)

    e = jnp.exp2(_dot_nt(qa, kca_scr[...])) * _rep_heads(cmask_ref[...])
    l = jnp.sum(e, axis=-1, keepdims=True)
    p = e * jnp.where(l > 0.0, 1.0 / l, 0.0)
    o_cmp = _dot(p.astype(BF16), vcmp_ref[0, 0])

    sel = _select_blocks(p, ovl_ref[...], q0, tq, n_slc)
    sel_cols = jnp.where(lane < n_slc, jnp.where(sel > 0.5, 0.0, MASKED), shift_cols)
    qs = jnp.concatenate([q4, _rep_heads(sel_cols.astype(BF16))], axis=1)

    l_scr[...] = jnp.zeros(l_scr.shape, F32)
    acc_scr[...] = jnp.zeros(acc_scr.shape, F32)

    def accumulate(k0, width, mask):
        p_t = jnp.exp2(_dot_nt(qs, ksa_scr[pl.ds(k0, width), :]))
        if mask is not None:
            p_t = p_t * mask
        l_scr[...] += _lane_tile_sum(p_t)
        acc_scr[...] += _dot(p_t.astype(BF16), vs_ref[0, pl.ds(k0, width), :])

    big = 2 * tk

    def slc_body(kt, carry):
        accumulate(pl.multiple_of(kt * big, big), big, None)
        return carry

    n_full = lax.div(q0, tk)
    n_big = lax.div(n_full, 2)
    lax.fori_loop(0, n_big, slc_body, 0)

    @pl.when(n_full > 2 * n_big)
    def _():
        accumulate(pl.multiple_of(n_big * big, big), tk, None)

    accumulate(pl.multiple_of(n_full * tk, tk), tk, _rep_heads(tri_ref[...]))
    o_slc = _flash_finish(l_scr, acc_scr)

    wk = WINDOW + tq
    ws = pl.multiple_of(jnp.maximum(q0 - WINDOW, 0), tq)
    e = jnp.exp2(_dot_nt(qa, kwa_scr[pl.ds(ws, wk), :])) * _rep_heads(wmask_ref[0])
    o_win = _dot(e.astype(BF16), vw_ref[0, pl.ds(ws, wk), :]) * (1.0 / jnp.sum(e, axis=-1, keepdims=True))

    _nsa_combine(o_cmp, o_slc, o_win, gates_ref, ngate_ref, o_ref, tq)


def _nsa_kernel(bound_ref, q_ref, ngate_ref, gates_ref, kcmp_ref, vcmp_ref, ks_ref, vs_ref, kw_ref, vw_ref,
                ovl_ref, eaug_ref, cmask_ref, wmask_ref, tri_ref, o_ref,
                bias_scr, m_scr, l_scr, acc_scr, ksa_scr, kwa_scr, kca_scr, *, seq):
    qi = pl.program_id(2)
    q0 = qi * NSA_TQ
    q4 = _stack_heads(q_ref[0])
    bound = bound_ref[0]

    @pl.when(bound <= MAX_SCORE_BOUND)
    def _():
        _nsa_bounded(q4, q0, qi, bound, gates_ref, ngate_ref, kcmp_ref, vcmp_ref, ks_ref, vs_ref, kw_ref, vw_ref,
                     ovl_ref, eaug_ref, cmask_ref, wmask_ref, tri_ref, o_ref,
                     ksa_scr, kwa_scr, kca_scr, l_scr, acc_scr, seq=seq)

    @pl.when(bound > MAX_SCORE_BOUND)
    def _():
        _nsa_general(q4, q0, qi, gates_ref, ngate_ref, kcmp_ref, vcmp_ref, ks_ref, vs_ref, kw_ref, vw_ref,
                     ovl_ref, eaug_ref, o_ref, bias_scr, m_scr, l_scr, acc_scr, seq=seq)


def _nsa_attention(proj, gates, kcmp, vcmp, bound, tables, offs):
    b, s, _ = proj.shape
    g = NSA_KV_HEADS
    tq = NSA_TQ
    nq = s // tq
    gw = NSA_GROUP * HEAD_DIM
    n_cmp_rows = s // CMP_STRIDE
    ovl_t, eaug, cmask, wmask, tri = tables
    q_spec = lambda off: pl.BlockSpec((1, tq, gw), lambda bi, gi, qi: (bi, qi, off + gi))
    kv_spec = lambda off: pl.BlockSpec((1, s, HEAD_DIM), lambda bi, gi, qi: (bi, 0, off + gi))
    cmp_spec = pl.BlockSpec((1, 1, n_cmp_rows, HEAD_DIM), lambda bi, gi, qi: (bi, gi, 0, 0))
    full = lambda shape: pl.BlockSpec(shape, lambda bi, gi, qi: (0,) * len(shape))
    n_wpat = wmask.shape[0]
    rows4 = NSA_GROUP * tq
    return pl.pallas_call(
        functools.partial(_nsa_kernel, seq=s),
        grid=(b, g, nq),
        in_specs=[pl.BlockSpec(memory_space=pltpu.SMEM),
                  q_spec(offs["nq"] // gw), q_spec(offs["ngate"] // gw),
                  pl.BlockSpec((tq, LANES), lambda bi, gi, qi: (bi * nq + qi, gi)),
                  cmp_spec, cmp_spec,
                  kv_spec(offs["ks"] // HEAD_DIM), kv_spec(offs["vs"] // HEAD_DIM),
                  kv_spec(offs["kw"] // HEAD_DIM), kv_spec(offs["vw"] // HEAD_DIM),
                  full(ovl_t.shape), full(eaug.shape),
                  pl.BlockSpec((tq, LANES), lambda bi, gi, qi: (qi, 0)),
                  pl.BlockSpec((1,) + wmask.shape[1:], lambda bi, gi, qi: (jnp.minimum(qi, n_wpat - 1), 0, 0)),
                  full(tri.shape)],
        out_specs=pl.BlockSpec((1, tq, gw), lambda bi, gi, qi: (bi, qi, gi)),
        out_shape=jax.ShapeDtypeStruct((b, s, NSA_WIDTH), BF16),
        scratch_shapes=[pltpu.VMEM((s // NSA_TK, tq, NSA_TK), F32),
                        pltpu.VMEM((rows4, LANES), F32), pltpu.VMEM((rows4, LANES), F32),
                        pltpu.VMEM((rows4, HEAD_DIM), F32),
                        pltpu.VMEM((s, 2 * HEAD_DIM), BF16), pltpu.VMEM((s, 2 * HEAD_DIM), BF16),
                        pltpu.VMEM((n_cmp_rows, 2 * HEAD_DIM), BF16)],
        compiler_params=pltpu.CompilerParams(
            dimension_semantics=("parallel", "parallel", "arbitrary"), vmem_limit_bytes=VMEM_LIMIT),
        name="nsa_attention",
    )(bound, proj, proj, gates, kcmp, vcmp, proj, proj, proj, proj, ovl_t, eaug, cmask, wmask, tri)


def _diff_general(q1, q2, qi, k_ref, v_ref, m1, l1, a1, m2, l2, a2):
    tq, tk = DIFF_TQ, DIFF_TK
    _flash_init(m1, l1, a1)
    _flash_init(m2, l2, a2)

    def step(rows, k0, width, bias):
        k = k_ref[0, pl.ds(k0, width), :]
        v = v_ref[0, pl.ds(k0, width), :]
        _flash_step(q1[rows], k[:, :HEAD_DIM], v, bias, m1.at[rows], l1.at[rows], a1.at[rows])
        _flash_step(q2[rows], k[:, HEAD_DIM:], v, bias, m2.at[rows], l2.at[rows], a2.at[rows])

    def body(kt, carry):
        step(slice(0, tq), pl.multiple_of(kt * tk, tk), tk, None)
        return carry

    lax.fori_loop(0, qi * (tq // tk), body, 0)
    dd = DIFF_DIAG
    q0 = qi * tq
    for c in range(tq // dd):
        n_rows = tq - c * dd
        ri = lax.broadcasted_iota(jnp.int32, (n_rows, dd), 0)
        ci = lax.broadcasted_iota(jnp.int32, (n_rows, dd), 1)
        step(slice(c * dd, tq), pl.multiple_of(q0 + c * dd, dd), dd, jnp.where(ri >= ci, 0.0, MASKED))
    return _flash_finish(l1, a1), _flash_finish(l2, a2)


def _lane_tile_sum(p):
    cols = p[:, 0:LANES]
    for c in range(1, p.shape[1] // LANES):
        cols = cols + p[:, c * LANES:(c + 1) * LANES]
    return cols


def _diff_bounded_stream(q, n, bound, k_ref, v_ref, tri_ref, kcols):
    tq, dd = DIFF_TQ, DIFF_DIAG
    kmain = n * tq + dd
    p = jnp.exp2(_dot_nt(q, k_ref[0, 0:kmain, kcols]) - bound)
    p_diag = p[:, kmain - dd:] * tri_ref[...]
    p = p_diag if kmain == dd else jnp.concatenate([p[:, :kmain - dd], p_diag], axis=1)
    l = _lane_tile_sum(p)
    acc = _dot(p.astype(BF16), v_ref[0, 0:kmain, :])
    pc = jnp.exp2(_dot_nt(q[dd:], k_ref[0, kmain:kmain + dd, kcols]) - bound) * tri_ref[0:tq - dd, :]
    l = jnp.concatenate([l[:dd], l[dd:] + _lane_tile_sum(pc)], axis=0)
    acc = jnp.concatenate([acc[:dd], acc[dd:] + _dot(pc.astype(BF16), v_ref[0, kmain:kmain + dd, :])], axis=0)
    return acc * (1.0 / jnp.sum(l, axis=-1, keepdims=True))


def _diff_kernel(bound_ref, q_ref, gate_ref, k_ref, v_ref, lam_ref, tri_ref, o_ref, m1, l1, a1, m2, l2, a2,
                 *, lambda_init, n_tiles):
    qi = pl.program_id(2)
    q = q_ref[0]
    q1 = q[:, :HEAD_DIM]
    q2 = q[:, HEAD_DIM:]
    bound = bound_ref[0]

    def finish(o1, o2):
        lp = lam_ref[...]
        lam = (jnp.exp(jnp.sum(lp[0:1] * lp[1:2], axis=-1, keepdims=True))
               - jnp.exp(jnp.sum(lp[2:3] * lp[3:4], axis=-1, keepdims=True)) + lambda_init)
        o = _rms(o1 - lam * o2) * (1.0 - lambda_init)
        o_ref[0] = (o * gate_ref[0].astype(F32)).astype(o_ref.dtype)

    def bounded_variant(n):
        @pl.when((bound <= MAX_SCORE_BOUND) & (qi == n))
        def _():
            finish(_diff_bounded_stream(q1, n, bound, k_ref, v_ref, tri_ref, slice(0, HEAD_DIM)),
                   _diff_bounded_stream(q2, n, bound, k_ref, v_ref, tri_ref, slice(HEAD_DIM, 2 * HEAD_DIM)))

    for n in range(n_tiles):
        bounded_variant(n)

    @pl.when(bound > MAX_SCORE_BOUND)
    def _():
        finish(*_diff_general(q1, q2, qi, k_ref, v_ref, m1, l1, a1, m2, l2, a2))


def _diff_attention(proj, lam_params, bound, lambda_init):
    b, s, _ = proj.shape
    h = DIFF_HEADS
    tq, dd = DIFF_TQ, DIFF_DIAG
    assert tq == 2 * dd
    w = DIFF_V_DIM
    tri = jnp.asarray(np.arange(tq)[:, None] >= np.arange(dd)[None, :], F32)
    q_spec = lambda off: pl.BlockSpec((1, tq, w), lambda bi, hi, qi: (bi, qi, off + hi))
    kv_spec = lambda off: pl.BlockSpec((1, s, w), lambda bi, hi, qi: (bi, 0, off + hi))
    stat = pltpu.VMEM((tq, LANES), F32)
    acc = pltpu.VMEM((tq, w), F32)
    return pl.pallas_call(
        functools.partial(_diff_kernel, lambda_init=lambda_init, n_tiles=s // tq),
        grid=(b, h, s // tq),
        in_specs=[pl.BlockSpec(memory_space=pltpu.SMEM),
                  q_spec(0), q_spec(3 * h), kv_spec(h), kv_spec(2 * h),
                  pl.BlockSpec(lam_params.shape, lambda bi, hi, qi: (0, 0)),
                  pl.BlockSpec(tri.shape, lambda bi, hi, qi: (0, 0))],
        out_specs=pl.BlockSpec((1, tq, w), lambda bi, hi, qi: (bi, qi, hi)),
        out_shape=jax.ShapeDtypeStruct((b, s, DIFF_WIDTH), BF16),
        scratch_shapes=[stat, stat, acc, stat, stat, acc],
        compiler_params=pltpu.CompilerParams(
            dimension_semantics=("parallel", "parallel", "arbitrary"), vmem_limit_bytes=VMEM_LIMIT),
        name="diff_attention",
    )(bound, proj, proj, proj, proj, lam_params, tri)


def _rope_tables(pos):
    inv = 1.0 / (ROPE_THETA ** (np.arange(0, HEAD_DIM, 2, dtype=np.float64) / HEAD_DIM))
    ang = np.asarray(pos, np.float64)[:, None] * inv[None, :]
    cos, sin = np.cos(ang), np.sin(ang)
    return (jnp.asarray(np.concatenate([cos, cos], axis=-1), F32),
            jnp.asarray(np.concatenate([-sin, sin], axis=-1), F32))


def _retention_tables():
    h, c = RET_HEADS, RET_CHUNK
    log_g = np.log1p(-np.exp2(-5.0 - np.arange(h, dtype=np.float64)))
    j = np.arange(c, dtype=np.float64)
    diff = j[:, None] - j[None, :]
    intra = np.where(diff >= 0, np.exp(log_g[:, None, None] * np.maximum(diff, 0.0)), 0.0)
    q_dec = np.exp(log_g[:, None] * (j + 1.0))
    k_dec = np.exp(log_g[:, None] * (c - 1.0 - j))
    chunk_dec = np.exp(log_g * c)
    wide = lambda t: jnp.asarray(np.broadcast_to(t[:, :, None], (h, t.shape[1], HEAD_DIM)), F32)
    return jnp.asarray(intra, F32), wide(q_dec), wide(k_dec), wide(chunk_dec[:, None])


def _selection_tables(seq):
    tq, tk = NSA_TQ, NSA_TK
    n_cmp_rows = seq // CMP_STRIDE
    n_slc = seq // SLC_BLOCK
    assert tq == tk and n_slc <= AUG_ONE_LANE < LANES
    c_start = np.arange(n_cmp_rows) * CMP_STRIDE
    s_start = np.arange(n_slc) * SLC_BLOCK
    overlap_t = ((c_start[None, :] <= s_start[:, None] + SLC_BLOCK - 1)
                 & (c_start[None, :] + CMP_BLOCK - 1 >= s_start[:, None]))
    lane = np.arange(LANES)[None, :]
    key = np.arange(seq)[:, None]
    eaug = ((key // SLC_BLOCK) == lane) | (lane == AUG_ONE_LANE)
    cmask = lane * CMP_STRIDE + CMP_BLOCK - 1 <= key
    r = np.arange(tq)[:, None]
    c = np.arange(WINDOW + tq)[None, :]
    wmask = []
    for pat in range(WINDOW // tq + 1):
        d = min(pat * tq, WINDOW) + r - c
        wmask.append((d >= 0) & (d < WINDOW))
    tri = np.arange(tq)[:, None] >= np.arange(tk)[None, :]
    return (jnp.asarray(overlap_t, BF16), jnp.asarray(eaug, BF16), jnp.asarray(cmask, F32),
            jnp.asarray(np.stack(wmask), F32), jnp.asarray(tri, F32))


def _tile_modes(segments, tn):
    off, kinds, col = {}, [], 0
    for name, width, kind in segments:
        off[name] = col
        kinds += [kind] * (width // LANES)
        col += width
    per = tn // LANES
    tiles = [kinds[i:i + per] for i in range(0, len(kinds), per)]
    assert col % tn == 0
    modes = []
    for j, tile in enumerate(tiles):
        if modes and modes[-1][2] == tile:
            modes[-1] = (modes[-1][0], j + 1, tile)
        else:
            modes.append((j, j + 1, tile))
    return modes, off


def _l0_tile_modes():
    plain, silu = ("plain", 0), ("silu", 0)
    k_norm = ("nrope", L0_TAB_K)
    return _tile_modes([
        ("rq", RET_WIDTH, ("rope", L0_TAB_ROPE)), ("rk", RET_WIDTH, ("rope", L0_TAB_ROPE_SCALED)),
        ("rv", RET_WIDTH, plain), ("rgate", RET_WIDTH, silu), ("nq", NSA_WIDTH, ("nrope", L0_TAB_Q)),
        ("kc", NSA_KV_WIDTH, plain), ("vc", NSA_KV_WIDTH, plain), ("ks", NSA_KV_WIDTH, k_norm),
        ("vs", NSA_KV_WIDTH, plain), ("kw", NSA_KV_WIDTH, k_norm), ("vw", NSA_KV_WIDTH, plain),
        ("ngate", NSA_WIDTH, silu)], PROJ_TN)


def _l1_tile_modes():
    return _tile_modes([("q", DIFF_WIDTH, ("nrope", L1_TAB_Q)), ("k", DIFF_WIDTH, ("nrope", L1_TAB_K)),
                        ("v", DIFF_WIDTH, ("plain", 0)), ("gate", DIFF_WIDTH, ("silu", 0))], L1_PROJ_TN)[0]


def _rope_pair(cos, sin_signed, gain=None, scale=1.0):
    if gain is None:
        return [cos * scale, sin_signed * scale]
    return [cos * (gain * scale)[None, :], sin_signed * (jnp.roll(gain, HALF) * scale)[None, :]]


def kernel(x, l0_norm_g, l0_w_in, l0_w_out, l0_nsa_q_norm_g, l0_nsa_k_norm_g, l0_cmp_pe_k, l0_cmp_w1_k, l0_cmp_w2_k, l0_cmp_pe_v, l0_cmp_w1_v, l0_cmp_w2_v, l1_norm_g, l1_w_in, l1_w_out, l1_q_norm_g, l1_k_norm_g, l1_lambda_q1, l1_lambda_k1, l1_lambda_q2, l1_lambda_k2):
    b, s, d = x.shape
    m = b * s
    x2d = x.reshape(m, d)
    cos, sin = _rope_tables(np.arange(s))
    cos_c, sin_c = _rope_tables(np.arange(s // CMP_STRIDE) * CMP_STRIDE + CMP_BLOCK - 1)
    intra, qdec, kdec, cdec = _retention_tables()
    nsa_tables = _selection_tables(s)

    modes0, off = _l0_tile_modes()
    w0 = l0_w_in.astype(BF16)
    wg = l0_w_in[:, AB_MAIN_COLS:].reshape(d, 3, NSA_KV_HEADS, NSA_GROUP).transpose(0, 2, 1, 3)
    wg = wg.reshape(d, NSA_KV_HEADS, 3 * NSA_GROUP)
    wg = jnp.pad(wg, ((0, 0), (0, 0), (0, LANES - 3 * NSA_GROUP))).reshape(d, NSA_KV_HEADS * LANES).astype(BF16)
    tabs0 = jnp.stack(_rope_pair(cos, sin) + _rope_pair(cos, sin, scale=QK_SCALE)
                      + _rope_pair(cos, sin, l0_nsa_q_norm_g, Q_SCALE) + _rope_pair(cos, sin, l0_nsa_k_norm_g))
    proj0, gates = _norm_proj(x2d, l0_norm_g, w0, tabs0, modes0, s, PROJ_TN, wg=wg)
    proj0 = proj0.reshape(b, s, AB_MAIN_COLS)
    y_ret = _retention(proj0, intra, qdec, kdec, cdec)
    w1k = l0_cmp_w1_k.astype(BF16).reshape(CMP_BLOCK, HEAD_DIM, HEAD_DIM)
    w1v = l0_cmp_w1_v.astype(BF16).reshape(CMP_BLOCK, HEAD_DIM, HEAD_DIM)
    kcmp, vcmp = _nsa_compress(proj0, off["kc"] // HEAD_DIM, off["vc"] // HEAD_DIM,
                               l0_cmp_pe_k, w1k, l0_cmp_w2_k.astype(BF16),
                               l0_cmp_pe_v, w1v, l0_cmp_w2_v.astype(BF16),
                               l0_nsa_k_norm_g, cos_c, sin_c)
    bound = (BOUND_MARGIN * HEAD_DIM * Q_SCALE * jnp.max(jnp.abs(l0_nsa_q_norm_g))
             * jnp.max(jnp.abs(l0_nsa_k_norm_g))).reshape(1).astype(F32)
    y_nsa = _nsa_attention(proj0, gates, kcmp, vcmp, bound, nsa_tables, off)
    w_out0 = l0_w_out.astype(BF16)
    x1 = _out_proj([y_ret.reshape(m, RET_WIDTH), y_nsa.reshape(m, NSA_WIDTH)],
                   [w_out0[:RET_WIDTH], w_out0[RET_WIDTH:]], x2d)

    lambda_init = 0.8 - 0.6 * math.exp(-0.3 * 1)
    tabs1 = jnp.stack(_rope_pair(cos, sin, l1_q_norm_g, Q_SCALE) + _rope_pair(cos, sin, l1_k_norm_g))
    proj1 = _norm_proj(x1, l1_norm_g, l1_w_in.astype(BF16), tabs1, _l1_tile_modes(), s, L1_PROJ_TN)[0]
    lam_params = jnp.stack([l1_lambda_q1, l1_lambda_k1, l1_lambda_q2, l1_lambda_k2]).astype(F32)
    bound1 = (BOUND_MARGIN * HEAD_DIM * Q_SCALE * jnp.max(jnp.abs(l1_q_norm_g))
              * jnp.max(jnp.abs(l1_k_norm_g))).reshape(1).astype(F32)
    y1 = _diff_attention(proj1.reshape(b, s, C_IN_COLS), lam_params, bound1, lambda_init)
    out = _out_proj([y1.reshape(m, DIFF_WIDTH)], [l1_w_out.astype(BF16)], x1)
    return out.reshape(b, s, d)
```

```python
import functools
import math

import numpy as np
import jax
import jax.numpy as jnp
from jax import lax
from jax.experimental import pallas as pl
from jax.experimental.pallas import tpu as pltpu

F32 = jnp.float32
BF16 = jnp.bfloat16

D_MODEL = 2048
HEAD_DIM = 128
HALF = HEAD_DIM // 2
ROPE_THETA = 10000.0
EPS = 1e-6
RET_HEADS = 8
RET_CHUNK = 256
RET_HEADS_PER_STEP = 2
NSA_HEADS = 8
NSA_KV_HEADS = 2
NSA_GROUP = NSA_HEADS // NSA_KV_HEADS
CMP_BLOCK = 32
CMP_STRIDE = 16
SLC_BLOCK = 64
SLC_TOPK = 16
N_LOCAL_BLOCKS = 2
WINDOW = 512
DIFF_HEADS = 8
DIFF_V_DIM = 2 * HEAD_DIM
QK_SCALE = HEAD_DIM ** -0.5
LOG2E = math.log2(math.e)
Q_SCALE = QK_SCALE * LOG2E

RET_WIDTH = RET_HEADS * HEAD_DIM
NSA_WIDTH = NSA_HEADS * HEAD_DIM
NSA_KV_WIDTH = NSA_KV_HEADS * HEAD_DIM
AB_MAIN_COLS = 4 * RET_WIDTH + 2 * NSA_WIDTH + 6 * NSA_KV_WIDTH
N_GATE_COLS = 3 * NSA_HEADS
DIFF_WIDTH = DIFF_HEADS * DIFF_V_DIM
C_IN_COLS = 4 * DIFF_WIDTH

LANES = 128
MASKED = -1e30
M_INIT = -1e29
MAX_SCORE_BOUND = 60.0
BOUND_MARGIN = 1.05
AUG_ONE_LANE = 32
VMEM_LIMIT = 56 * 1024 * 1024

PROJ_TM = 1024
PROJ_TN = 1280
L1_PROJ_TN = 1024
OPROJ_TM = 2048
OPROJ_TN = 512
PROJ_SUB = 256
L0_TAB_ROPE, L0_TAB_ROPE_SCALED, L0_TAB_Q, L0_TAB_K = range(4)
L1_TAB_Q, L1_TAB_K = range(2)
NSA_TQ = 512
NSA_TK = 512
DIFF_TQ = 512
DIFF_TK = 512
DIFF_DIAG = 256


def _dot(a, b):
    return jnp.dot(a, b, preferred_element_type=F32)


def _dot_nt(a, b):
    return lax.dot_general(a, b, (((1,), (1,)), ((), ())), preferred_element_type=F32)


def _dot_tn(a, b):
    return lax.dot_general(a, b, (((0,), (0,)), ((), ())), preferred_element_type=F32)


def _silu(x):
    return x / (1.0 + jnp.exp(-x))


def _rms(x):
    return x * lax.rsqrt(jnp.mean(x * x, axis=-1, keepdims=True) + EPS)


def _rope(x, cos, sin_signed):
    return x * cos + pltpu.roll(x, HALF, axis=1) * sin_signed


def _rope_tab(seg, t, tabs_ref, rows):
    return seg * tabs_ref[2 * t, rows, :] + pltpu.roll(seg, HALF, axis=1) * tabs_ref[2 * t + 1, rows, :]


def _tile_epilogue(acc, kinds, tabs_ref, rows, o_ref):
    n_seg = len(kinds)
    ri = lax.broadcasted_iota(jnp.int32, (2 * LANES, 2 * LANES), 0)
    ci = lax.broadcasted_iota(jnp.int32, (2 * LANES, 2 * LANES), 1)
    pair_mean = jnp.where((ri < LANES) == (ci < LANES), 1.0 / HEAD_DIM, 0.0).astype(BF16)
    c = 0
    while c < n_seg:
        op, t = kinds[c]
        cols = slice(c * LANES, (c + 1) * LANES)
        if op == "nrope" and c + 1 < n_seg and kinds[c + 1][0] == "nrope":
            cols2 = slice(c * LANES, (c + 2) * LANES)
            seg2 = acc[:, cols2]
            seg2 = seg2 * lax.rsqrt(_dot((seg2 * seg2).astype(BF16), pair_mean) + EPS)
            for half in range(2):
                out = _rope_tab(seg2[:, half * LANES:(half + 1) * LANES], kinds[c + half][1], tabs_ref, rows)
                o_ref[rows, (c + half) * LANES:(c + half + 1) * LANES] = out.astype(o_ref.dtype)
            c += 2
            continue
        seg = acc[:, cols]
        if op == "silu":
            seg = _silu(seg)
        elif op == "nrope":
            seg = seg * lax.rsqrt(_dot((seg * seg).astype(BF16), pair_mean[:LANES, :LANES]) + EPS)
            seg = _rope_tab(seg, t, tabs_ref, rows)
        elif op == "rope":
            seg = _rope_tab(seg, t, tabs_ref, rows)
        o_ref[rows, cols] = seg.astype(o_ref.dtype)
        c += 1


def _proj_kernel(*refs, tile_modes, has_gates):
    if has_gates:
        x_ref, g_ref, w_ref, tabs_ref, wg_ref, o_ref, og_ref, h_scr = refs
    else:
        x_ref, g_ref, w_ref, tabs_ref, o_ref, h_scr = refs
    j = pl.program_id(1)
    n_sub = h_scr.shape[0] // PROJ_SUB

    def make_branch(cond, kinds, first):
        @pl.when(cond)
        def _():
            for r in range(n_sub):
                rows = slice(r * PROJ_SUB, (r + 1) * PROJ_SUB)
                if first:
                    h = (_rms(x_ref[rows, :]) * g_ref[...]).astype(BF16)
                    h_scr[rows, :] = h
                    if has_gates:
                        og_ref[rows, :] = 1.0 / (1.0 + jnp.exp(-_dot(h, wg_ref[...])))
                else:
                    h = h_scr[rows, :]
                _tile_epilogue(_dot(h, w_ref[...]), kinds, tabs_ref, rows, o_ref)

    for lo, hi, kinds in tile_modes:
        if lo == 0:
            make_branch(j == 0, kinds, True)
            lo = 1
        if hi > lo:
            make_branch((j >= lo) & (j < hi), kinds, False)


def _norm_proj(x2d, g, w, tabs, tile_modes, seq, tn, wg=None):
    m, d = x2d.shape
    tm = PROJ_TM
    n = tile_modes[-1][1] * tn
    s_tiles = seq // tm
    has_gates = wg is not None
    in_specs = [
        pl.BlockSpec((tm, d), lambda i, j: (i, 0)),
        pl.BlockSpec((1, d), lambda i, j: (0, 0)),
        pl.BlockSpec((d, tn), lambda i, j: (0, j)),
        pl.BlockSpec((tabs.shape[0], tm, LANES), lambda i, j: (0, i % s_tiles, 0)),
    ]
    args = [x2d, g.reshape(1, d), w, tabs]
    out_shape = [jax.ShapeDtypeStruct((m, n), BF16)]
    out_specs = [pl.BlockSpec((tm, tn), lambda i, j: (i, j))]
    if has_gates:
        ng = wg.shape[1]
        in_specs.append(pl.BlockSpec((d, ng), lambda i, j: (0, 0)))
        args.append(wg)
        out_shape.append(jax.ShapeDtypeStruct((m, ng), F32))
        out_specs.append(pl.BlockSpec((tm, ng), lambda i, j: (i, 0)))
    return pl.pallas_call(
        functools.partial(_proj_kernel, tile_modes=tile_modes, has_gates=has_gates),
        grid=(m // tm, n // tn),
        in_specs=in_specs,
        out_specs=out_specs,
        out_shape=out_shape,
        scratch_shapes=[pltpu.VMEM((tm, d), BF16)],
        compiler_params=pltpu.CompilerParams(
            dimension_semantics=("parallel", "arbitrary"), vmem_limit_bytes=VMEM_LIMIT),
        name="norm_proj",
    )(*args)


def _oproj_kernel(*refs, n_terms):
    y_refs, w_refs = refs[:n_terms], refs[n_terms:2 * n_terms]
    x_ref, o_ref = refs[2 * n_terms:]
    acc = x_ref[...]
    for y_ref, w_ref in zip(y_refs, w_refs):
        acc = acc + _dot(y_ref[...], w_ref[...])
    o_ref[...] = acc


def _out_proj(ys, ws, x2d):
    m, n = x2d.shape
    tm, tn = OPROJ_TM, OPROJ_TN
    y_specs = [pl.BlockSpec((tm, y.shape[1]), lambda i, j: (i, 0)) for y in ys]
    w_specs = [pl.BlockSpec((w.shape[0], tn), lambda i, j: (0, j)) for w in ws]
    return pl.pallas_call(
        functools.partial(_oproj_kernel, n_terms=len(ys)),
        grid=(m // tm, n // tn),
        in_specs=y_specs + w_specs + [pl.BlockSpec((tm, tn), lambda i, j: (i, j))],
        out_specs=pl.BlockSpec((tm, tn), lambda i, j: (i, j)),
        out_shape=jax.ShapeDtypeStruct((m, n), F32),
        compiler_params=pltpu.CompilerParams(
            dimension_semantics=("parallel", "arbitrary"), vmem_limit_bytes=VMEM_LIMIT),
        name="out_proj",
    )(*ys, *ws, x2d)


def _ret_kernel(q_ref, k_ref, v_ref, gate_ref, intra_ref, qdec_ref, kdec_ref, cdec_ref, o_ref, *, n_chunks):
    c = RET_CHUNK
    states = [jnp.zeros((HEAD_DIM, HEAD_DIM), F32)] * RET_HEADS_PER_STEP
    for n in range(n_chunks):
        rows = slice(n * c, (n + 1) * c)
        for hh in range(RET_HEADS_PER_STEP):
            cols = slice(hh * HEAD_DIM, (hh + 1) * HEAD_DIM)
            q = q_ref[0, rows, cols]
            k = k_ref[0, rows, cols]
            v = v_ref[0, rows, cols]
            scores = _dot_nt(q, k) * intra_ref[hh]
            inner = _dot(scores.astype(BF16), v)
            cross = _dot((q.astype(F32) * qdec_ref[hh]).astype(BF16), states[hh].astype(BF16))
            kv = _dot_tn((k.astype(F32) * kdec_ref[hh]).astype(BF16), v)
            states[hh] = states[hh] * cdec_ref[hh] + kv
            o = _rms(inner + cross) * gate_ref[0, rows, cols].astype(F32)
            o_ref[0, rows, cols] = o.astype(o_ref.dtype)


def _retention(proj, intra, qdec, kdec, cdec):
    b, s, _ = proj.shape
    per = RET_HEADS_PER_STEP
    h = RET_HEADS // per
    y_shape = (b, s, RET_WIDTH)
    head = lambda off: pl.BlockSpec((1, s, per * HEAD_DIM), lambda bi, hi: (bi, 0, off + hi))
    table = lambda rows, cols=HEAD_DIM: pl.BlockSpec((per, rows, cols), lambda bi, hi: (hi, 0, 0))
    return pl.pallas_call(
        functools.partial(_ret_kernel, n_chunks=s // RET_CHUNK),
        grid=(b, h),
        in_specs=[head(0), head(h), head(2 * h), head(3 * h),
                  table(RET_CHUNK, RET_CHUNK), table(RET_CHUNK), table(RET_CHUNK), table(1)],
        out_specs=pl.BlockSpec((1, s, per * HEAD_DIM), lambda bi, hi: (bi, 0, hi)),
        out_shape=jax.ShapeDtypeStruct(y_shape, BF16),
        compiler_params=pltpu.CompilerParams(
            dimension_semantics=("parallel", "arbitrary"), vmem_limit_bytes=VMEM_LIMIT),
        name="retention",
    )(proj, proj, proj, proj, intra, qdec, kdec, cdec)


def _cmp_kernel(kc_ref, vc_ref, pek_ref, w1k_ref, w2k_ref, pev_ref, w1v_ref, w2v_ref, kg_ref,
                cos_ref, sin_ref, ko_ref, vo_ref, t_scr, *, seq):
    n_rows = seq // CMP_STRIDE

    def compress(src_ref, pe_ref, w1_ref, w2_ref):
        t_scr[0:seq, :] = src_ref[0].astype(F32)
        t_scr[seq:seq + CMP_STRIDE, :] = jnp.zeros((CMP_STRIDE, HEAD_DIM), F32)
        acc = jnp.zeros((n_rows, HEAD_DIM), F32)
        for r in range(CMP_BLOCK):
            rows = t_scr[pl.ds(r, n_rows, stride=CMP_STRIDE), :] + pe_ref[r:r + 1, :]
            acc = acc + _dot(rows.astype(BF16), w1_ref[r])
        return _dot(_silu(acc).astype(BF16), w2_ref[...])

    kc = compress(kc_ref, pek_ref, w1k_ref, w2k_ref)
    kc = _rope(_rms(kc) * kg_ref[...], cos_ref[...], sin_ref[...])
    ko_ref[0, 0] = kc.astype(ko_ref.dtype)
    vo_ref[0, 0] = compress(vc_ref, pev_ref, w1v_ref, w2v_ref).astype(vo_ref.dtype)


def _nsa_compress(proj, kc_off, vc_off, pe_k, w1_k, w2_k, pe_v, w1_v, w2_v, k_g, cos_c, sin_c):
    b, s, _ = proj.shape
    g = NSA_KV_HEADS
    n_rows = s // CMP_STRIDE
    head = lambda off: pl.BlockSpec((1, s, HEAD_DIM), lambda bi, gi: (bi, 0, off + gi))
    full = lambda shape: pl.BlockSpec(shape, lambda bi, gi: (0,) * len(shape))
    out_spec = pl.BlockSpec((1, 1, n_rows, HEAD_DIM), lambda bi, gi: (bi, gi, 0, 0))
    out_sds = jax.ShapeDtypeStruct((b, g, n_rows, HEAD_DIM), BF16)
    return pl.pallas_call(
        functools.partial(_cmp_kernel, seq=s),
        grid=(b, g),
        in_specs=[head(kc_off), head(vc_off),
                  full((CMP_BLOCK, HEAD_DIM)), full((CMP_BLOCK, HEAD_DIM, HEAD_DIM)), full((HEAD_DIM, HEAD_DIM)),
                  full((CMP_BLOCK, HEAD_DIM)), full((CMP_BLOCK, HEAD_DIM, HEAD_DIM)), full((HEAD_DIM, HEAD_DIM)),
                  full((1, HEAD_DIM)), full((n_rows, HEAD_DIM)), full((n_rows, HEAD_DIM))],
        out_specs=[out_spec, out_spec],
        out_shape=[out_sds, out_sds],
        scratch_shapes=[pltpu.VMEM((s + CMP_STRIDE, HEAD_DIM), F32)],
        compiler_params=pltpu.CompilerParams(
            dimension_semantics=("parallel", "arbitrary"), vmem_limit_bytes=VMEM_LIMIT),
        name="nsa_compress",
    )(proj, proj, pe_k, w1_k, w2_k, pe_v, w1_v, w2_v, k_g.reshape(1, HEAD_DIM), cos_c, sin_c)


def _flash_init(m_scr, l_scr, acc_scr):
    m_scr[...] = jnp.full(m_scr.shape, M_INIT, F32)
    l_scr[...] = jnp.zeros(l_scr.shape, F32)
    acc_scr[...] = jnp.zeros(acc_scr.shape, F32)


def _lane_tiles(x, width):
    return x if width == LANES else jnp.concatenate([x] * (width // LANES), axis=1)


def _flash_step(q, k, v, bias, m_scr, l_scr, acc_scr):
    s = _dot_nt(q, k)
    if bias is not None:
        s = s + bias
    tk = s.shape[1]
    m_old = m_scr[...]
    m_new = jnp.maximum(m_old, jnp.max(s, axis=-1, keepdims=True))
    alpha = jnp.exp2(m_old - m_new)
    p = jnp.exp2(s - _lane_tiles(m_new, tk))
    p_cols = p[:, 0:LANES]
    for c in range(1, tk // LANES):
        p_cols = p_cols + p[:, c * LANES:(c + 1) * LANES]
    l_scr[...] = alpha * l_scr[...] + p_cols
    acc_scr[...] = _lane_tiles(alpha, acc_scr.shape[1]) * acc_scr[...] + _dot(p.astype(BF16), v)
    m_scr[...] = m_new


def _flash_finish(l_scr, acc_scr):
    return acc_scr[...] * (1.0 / jnp.sum(l_scr[...], axis=-1, keepdims=True))


def _rep_heads(bias):
    return jnp.concatenate([bias] * NSA_GROUP, axis=0)


def _stack_heads(qblk):
    return jnp.concatenate([qblk[:, r * HEAD_DIM:(r + 1) * HEAD_DIM] for r in range(NSA_GROUP)], axis=0)


def _select_blocks(p, ovl, q0, tq, n_slc):
    psum = p[0:tq]
    for r in range(1, NSA_GROUP):
        psum = psum + p[r * tq:(r + 1) * tq]
    p_hi = psum.astype(BF16)
    rem = psum - p_hi.astype(F32)
    p_mid = rem.astype(BF16)
    p_lo = (rem - p_mid.astype(F32)).astype(BF16)
    imp = _dot_nt(ovl, p_hi) + _dot_nt(ovl, p_mid) + _dot_nt(ovl, p_lo)
    jb = lax.broadcasted_iota(jnp.int32, (n_slc, tq), 0)
    blk_t = jnp.right_shift(q0 + lax.broadcasted_iota(jnp.int32, (n_slc, tq), 1), int(math.log2(SLC_BLOCK)))
    back = blk_t - jb
    forced = (jb == 0) | ((back >= 0) & (back < N_LOCAL_BLOCKS))
    score = jnp.where(forced, 1e9, jnp.where(back >= 0, imp, -1e9))
    rank = jnp.zeros((n_slc, tq), F32)
    for mp in range(n_slc):
        row = score[mp:mp + 1, :]
        ahead = (row > score) | ((row == score) & (jb > mp))
        rank = rank + jnp.where(ahead, 1.0, 0.0)
    sel_t = jnp.where(rank < float(min(SLC_TOPK, n_slc)), 1.0, 0.0)
    sel_t = jnp.concatenate([sel_t, jnp.zeros((LANES - n_slc, tq), F32)], axis=0).astype(BF16)
    ri = lax.broadcasted_iota(jnp.int32, (tq, tq), 0)
    ci = lax.broadcasted_iota(jnp.int32, (tq, tq), 1)
    eye = jnp.where(ri == ci, 1.0, 0.0).astype(BF16)
    return _dot_nt(eye, sel_t)


def _nsa_combine(o_cmp, o_slc, o_win, gates_ref, ngate_ref, o_ref, tq):
    gates = gates_ref[...]
    for r in range(NSA_GROUP):
        rows = slice(r * tq, (r + 1) * tq)
        g_cmp = gates[:, r:r + 1]
        g_slc = gates[:, NSA_GROUP + r:NSA_GROUP + r + 1]
        g_win = gates[:, 2 * NSA_GROUP + r:2 * NSA_GROUP + r + 1]
        y = g_cmp * o_cmp[rows] + g_slc * o_slc[rows] + g_win * o_win[rows]
        cols = slice(r * HEAD_DIM, (r + 1) * HEAD_DIM)
        o_ref[0, :, cols] = (y * ngate_ref[0, :, cols].astype(F32)).astype(o_ref.dtype)


def _nsa_general(q4, q0, qi, gates_ref, ngate_ref, kcmp_ref, vcmp_ref, ks_ref, vs_ref, kw_ref, vw_ref,
                 ovl_ref, eaug_ref, o_ref, bias_scr, m_scr, l_scr, acc_scr, *, seq):
    tq, tk = NSA_TQ, NSA_TK
    n_slc = seq // SLC_BLOCK

    tpos = q0 + lax.broadcasted_iota(jnp.int32, (tq, LANES), 0)
    cidx = lax.broadcasted_iota(jnp.int32, (tq, LANES), 1)
    cbias = jnp.where(cidx * CMP_STRIDE + (CMP_BLOCK - 1) <= tpos, 0.0, MASKED)
    s = _dot_nt(q4, kcmp_ref[0, 0]) + _rep_heads(cbias)
    m = jnp.maximum(jnp.max(s, axis=-1, keepdims=True), M_INIT)
    e = jnp.exp2(s - m)
    l = jnp.sum(e, axis=-1, keepdims=True)
    p = e * (1.0 / jnp.maximum(l, 1e-30))
    o_cmp = _dot(p.astype(BF16), vcmp_ref[0, 0])

    sel = _select_blocks(p, ovl_ref[...], q0, tq, n_slc).astype(BF16)
    sel_keys = _dot_nt(sel, eaug_ref[...])
    for kt in range(seq // tk):
        bias_scr[kt] = jnp.where(sel_keys[:, kt * tk:(kt + 1) * tk] > 0.5, 0.0, MASKED)

    _flash_init(m_scr, l_scr, acc_scr)

    def slc_body(kt, carry):
        k0 = pl.multiple_of(kt * tk, tk)
        _flash_step(q4, ks_ref[0, pl.ds(k0, tk), :], vs_ref[0, pl.ds(k0, tk), :],
                    _rep_heads(bias_scr[kt]), m_scr, l_scr, acc_scr)
        return carry

    n_full = lax.div(q0, tk)
    lax.fori_loop(0, n_full, slc_body, 0)
    kd = pl.multiple_of(n_full * tk, tk)
    qpos = q0 + lax.broadcasted_iota(jnp.int32, (tq, tk), 0)
    kpos = kd + lax.broadcasted_iota(jnp.int32, (tq, tk), 1)
    causal = jnp.where(kpos <= qpos, 0.0, MASKED)
    _flash_step(q4, ks_ref[0, pl.ds(kd, tk), :], vs_ref[0, pl.ds(kd, tk), :],
                _rep_heads(bias_scr[n_full] + causal), m_scr, l_scr, acc_scr)
    o_slc = _flash_finish(l_scr, acc_scr)

    wk = WINDOW + tq
    ws = pl.multiple_of(jnp.maximum(q0 - WINDOW, 0), tq)
    d = (q0 - ws) + lax.broadcasted_iota(jnp.int32, (tq, wk), 0) - lax.broadcasted_iota(jnp.int32, (tq, wk), 1)
    wbias = jnp.where((d >= 0) & (d < WINDOW), 0.0, MASKED)
    s = _dot_nt(q4, kw_ref[0, pl.ds(ws, wk), :]) + _rep_heads(wbias)
    e = jnp.exp2(s - jnp.max(s, axis=-1, keepdims=True))
    o_win = _dot(e.astype(BF16), vw_ref[0, pl.ds(ws, wk), :]) * (1.0 / jnp.sum(e, axis=-1, keepdims=True))

    _nsa_combine(o_cmp, o_slc, o_win, gates_ref, ngate_ref, o_ref, tq)


def _nsa_bounded(q4, q0, qi, bound, gates_ref, ngate_ref, kcmp_ref, vcmp_ref, ks_ref, vs_ref, kw_ref, vw_ref,
                 ovl_ref, eaug_ref, cmask_ref, wmask_ref, tri_ref, o_ref,
                 ksa_scr, kwa_scr, kca_scr, l_scr, acc_scr, *, seq):
    tq, tk = NSA_TQ, NSA_TK
    n_slc = seq // SLC_BLOCK
    n_cmp_rows = kca_scr.shape[0]

    @pl.when(qi == 0)
    def _():
        one_col = eaug_ref[...]
        lane = lax.broadcasted_iota(jnp.int32, one_col.shape, 1)
        ksa_scr[:, :HEAD_DIM] = ks_ref[0]
        ksa_scr[:, HEAD_DIM:] = one_col
        one_col = jnp.where(lane == AUG_ONE_LANE, one_col, jnp.zeros_like(one_col))
        kwa_scr[:, :HEAD_DIM] = kw_ref[0]
        kwa_scr[:, HEAD_DIM:] = one_col
        kca_scr[:, :HEAD_DIM] = kcmp_ref[0, 0]
        kca_scr[:, HEAD_DIM:] = one_col[:n_cmp_rows]

    lane = lax.broadcasted_iota(jnp.int32, (tq, LANES), 1)
    shift_cols = jnp.where(lane == AUG_ONE_LANE, -bound, 0.0)
    qa = jnp.concatenate([q4, _rep_heads(shift_cols.astype(BF16))], axis=1)

    e = jnp.exp2(_dot_nt(qa, kca_scr[...])) * _rep_heads(cmask_ref[...])
    l = jnp.sum(e, axis=-1, keepdims=True)
    p = e * jnp.where(l > 0.0, 1.0 / l, 0.0)
    o_cmp = _dot(p.astype(BF16), vcmp_ref[0, 0])

    sel = _select_blocks(p, ovl_ref[...], q0, tq, n_slc)
    sel_cols = jnp.where(lane < n_slc, jnp.where(sel > 0.5, 0.0, MASKED), shift_cols)
    qs = jnp.concatenate([q4, _rep_heads(sel_cols.astype(BF16))], axis=1)

    l_scr[...] = jnp.zeros(l_scr.shape, F32)
    acc_scr[...] = jnp.zeros(acc_scr.shape, F32)

    def accumulate(k0, width, mask):
        p_t = jnp.exp2(_dot_nt(qs, ksa_scr[pl.ds(k0, width), :]))
        if mask is not None:
            p_t = p_t * mask
        l_scr[...] += _lane_tile_sum(p_t)
        acc_scr[...] += _dot(p_t.astype(BF16), vs_ref[0, pl.ds(k0, width), :])

    big = 2 * tk

    def slc_body(kt, carry):
        accumulate(pl.multiple_of(kt * big, big), big, None)
        return carry

    n_full = lax.div(q0, tk)
    n_big = lax.div(n_full, 2)
    lax.fori_loop(0, n_big, slc_body, 0)

    @pl.when(n_full > 2 * n_big)
    def _():
        accumulate(pl.multiple_of(n_big * big, big), tk, None)

    accumulate(pl.multiple_of(n_full * tk, tk), tk, _rep_heads(tri_ref[...]))
    o_slc = _flash_finish(l_scr, acc_scr)

    wk = WINDOW + tq
    ws = pl.multiple_of(jnp.maximum(q0 - WINDOW, 0), tq)
    e = jnp.exp2(_dot_nt(qa, kwa_scr[pl.ds(ws, wk), :])) * _rep_heads(wmask_ref[0])
    o_win = _dot(e.astype(BF16), vw_ref[0, pl.ds(ws, wk), :]) * (1.0 / jnp.sum(e, axis=-1, keepdims=True))

    _nsa_combine(o_cmp, o_slc, o_win, gates_ref, ngate_ref, o_ref, tq)


def _nsa_kernel(bound_ref, q_ref, ngate_ref, gates_ref, kcmp_ref, vcmp_ref, ks_ref, vs_ref, kw_ref, vw_ref,
                ovl_ref, eaug_ref, cmask_ref, wmask_ref, tri_ref, *rest, seq, n_casts):
    cast_in, (o_ref, *cast_out) = rest[:n_casts], rest[n_casts:2 * n_casts + 1]
    bias_scr, m_scr, l_scr, acc_scr, ksa_scr, kwa_scr, kca_scr = rest[2 * n_casts + 1:]
    for src, dst in zip(cast_in, cast_out):
        dst[...] = src[...].astype(dst.dtype)
    qi = pl.program_id(2)
    q0 = qi * NSA_TQ
    q4 = _stack_heads(q_ref[0])
    bound = bound_ref[0]

    @pl.when(bound <= MAX_SCORE_BOUND)
    def _():
        _nsa_bounded(q4, q0, qi, bound, gates_ref, ngate_ref, kcmp_ref, vcmp_ref, ks_ref, vs_ref, kw_ref, vw_ref,
                     ovl_ref, eaug_ref, cmask_ref, wmask_ref, tri_ref, o_ref,
                     ksa_scr, kwa_scr, kca_scr, l_scr, acc_scr, seq=seq)

    @pl.when(bound > MAX_SCORE_BOUND)
    def _():
        _nsa_general(q4, q0, qi, gates_ref, ngate_ref, kcmp_ref, vcmp_ref, ks_ref, vs_ref, kw_ref, vw_ref,
                     ovl_ref, eaug_ref, o_ref, bias_scr, m_scr, l_scr, acc_scr, seq=seq)


def _nsa_attention(proj, gates, kcmp, vcmp, bound, tables, offs, cast_weights):
    b, s, _ = proj.shape
    g = NSA_KV_HEADS
    tq = NSA_TQ
    nq = s // tq
    gw = NSA_GROUP * HEAD_DIM
    n_cmp_rows = s // CMP_STRIDE
    ovl_t, eaug, cmask, wmask, tri = tables
    q_spec = lambda off: pl.BlockSpec((1, tq, gw), lambda bi, gi, qi: (bi, qi, off + gi))
    kv_spec = lambda off: pl.BlockSpec((1, s, HEAD_DIM), lambda bi, gi, qi: (bi, 0, off + gi))
    cmp_spec = pl.BlockSpec((1, 1, n_cmp_rows, HEAD_DIM), lambda bi, gi, qi: (bi, gi, 0, 0))
    full = lambda shape: pl.BlockSpec(shape, lambda bi, gi, qi: (0,) * len(shape))
    n_wpat = wmask.shape[0]
    rows4 = NSA_GROUP * tq
    n_steps = b * g * nq
    cast_specs = [pl.BlockSpec((w.shape[0] // n_steps, w.shape[1]), lambda bi, gi, qi: ((bi * g + gi) * nq + qi, 0))
                  for w in cast_weights]
    return pl.pallas_call(
        functools.partial(_nsa_kernel, seq=s, n_casts=len(cast_weights)),
        grid=(b, g, nq),
        in_specs=[pl.BlockSpec(memory_space=pltpu.SMEM),
                  q_spec(offs["nq"] // gw), q_spec(offs["ngate"] // gw),
                  pl.BlockSpec((tq, LANES), lambda bi, gi, qi: (bi * nq + qi, gi)),
                  cmp_spec, cmp_spec,
                  kv_spec(offs["ks"] // HEAD_DIM), kv_spec(offs["vs"] // HEAD_DIM),
                  kv_spec(offs["kw"] // HEAD_DIM), kv_spec(offs["vw"] // HEAD_DIM),
                  full(ovl_t.shape), full(eaug.shape),
                  pl.BlockSpec((tq, LANES), lambda bi, gi, qi: (qi, 0)),
                  pl.BlockSpec((1,) + wmask.shape[1:], lambda bi, gi, qi: (jnp.minimum(qi, n_wpat - 1), 0, 0)),
                  full(tri.shape)] + cast_specs,
        out_specs=[pl.BlockSpec((1, tq, gw), lambda bi, gi, qi: (bi, qi, gi))] + cast_specs,
        out_shape=[jax.ShapeDtypeStruct((b, s, NSA_WIDTH), BF16)]
        + [jax.ShapeDtypeStruct(w.shape, BF16) for w in cast_weights],
        scratch_shapes=[pltpu.VMEM((s // NSA_TK, tq, NSA_TK), F32),
                        pltpu.VMEM((rows4, LANES), F32), pltpu.VMEM((rows4, LANES), F32),
                        pltpu.VMEM((rows4, HEAD_DIM), F32),
                        pltpu.VMEM((s, 2 * HEAD_DIM), BF16), pltpu.VMEM((s, 2 * HEAD_DIM), BF16),
                        pltpu.VMEM((n_cmp_rows, 2 * HEAD_DIM), BF16)],
        compiler_params=pltpu.CompilerParams(
            dimension_semantics=("parallel", "parallel", "arbitrary"), vmem_limit_bytes=VMEM_LIMIT),
        name="nsa_attention",
    )(bound, proj, proj, gates, kcmp, vcmp, proj, proj, proj, proj, ovl_t, eaug, cmask, wmask, tri, *cast_weights)


def _diff_general(q1, q2, qi, k_ref, v_ref, m1, l1, a1, m2, l2, a2):
    tq, tk = DIFF_TQ, DIFF_TK
    _flash_init(m1, l1, a1)
    _flash_init(m2, l2, a2)

    def step(rows, k0, width, bias):
        k = k_ref[0, pl.ds(k0, width), :]
        v = v_ref[0, pl.ds(k0, width), :]
        _flash_step(q1[rows], k[:, :HEAD_DIM], v, bias, m1.at[rows], l1.at[rows], a1.at[rows])
        _flash_step(q2[rows], k[:, HEAD_DIM:], v, bias, m2.at[rows], l2.at[rows], a2.at[rows])

    def body(kt, carry):
        step(slice(0, tq), pl.multiple_of(kt * tk, tk), tk, None)
        return carry

    lax.fori_loop(0, qi * (tq // tk), body, 0)
    dd = DIFF_DIAG
    q0 = qi * tq
    for c in range(tq // dd):
        n_rows = tq - c * dd
        ri = lax.broadcasted_iota(jnp.int32, (n_rows, dd), 0)
        ci = lax.broadcasted_iota(jnp.int32, (n_rows, dd), 1)
        step(slice(c * dd, tq), pl.multiple_of(q0 + c * dd, dd), dd, jnp.where(ri >= ci, 0.0, MASKED))
    return _flash_finish(l1, a1), _flash_finish(l2, a2)


def _lane_tile_sum(p):
    cols = p[:, 0:LANES]
    for c in range(1, p.shape[1] // LANES):
        cols = cols + p[:, c * LANES:(c + 1) * LANES]
    return cols


def _diff_bounded_stream(q, n, bound, k_ref, v_ref, tri_ref, kcols):
    tq, dd = DIFF_TQ, DIFF_DIAG
    kmain = n * tq + dd
    p = jnp.exp2(_dot_nt(q, k_ref[0, 0:kmain, kcols]) - bound)
    p_diag = p[:, kmain - dd:] * tri_ref[...]
    p = p_diag if kmain == dd else jnp.concatenate([p[:, :kmain - dd], p_diag], axis=1)
    l = _lane_tile_sum(p)
    acc = _dot(p.astype(BF16), v_ref[0, 0:kmain, :])
    pc = jnp.exp2(_dot_nt(q[dd:], k_ref[0, kmain:kmain + dd, kcols]) - bound) * tri_ref[0:tq - dd, :]
    l = jnp.concatenate([l[:dd], l[dd:] + _lane_tile_sum(pc)], axis=0)
    acc = jnp.concatenate([acc[:dd], acc[dd:] + _dot(pc.astype(BF16), v_ref[0, kmain:kmain + dd, :])], axis=0)
    return acc * (1.0 / jnp.sum(l, axis=-1, keepdims=True))


def _diff_kernel(bound_ref, q_ref, gate_ref, k_ref, v_ref, lam_ref, tri_ref, o_ref, m1, l1, a1, m2, l2, a2,
                 *, lambda_init, n_tiles):
    qi = pl.program_id(2)
    q = q_ref[0]
    q1 = q[:, :HEAD_DIM]
    q2 = q[:, HEAD_DIM:]
    bound = bound_ref[0]

    def finish(o1, o2):
        lp = lam_ref[...]
        lam = (jnp.exp(jnp.sum(lp[0:1] * lp[1:2], axis=-1, keepdims=True))
               - jnp.exp(jnp.sum(lp[2:3] * lp[3:4], axis=-1, keepdims=True)) + lambda_init)
        o = _rms(o1 - lam * o2) * (1.0 - lambda_init)
        o_ref[0] = (o * gate_ref[0].astype(F32)).astype(o_ref.dtype)

    def bounded_variant(n):
        @pl.when((bound <= MAX_SCORE_BOUND) & (qi == n))
        def _():
            finish(_diff_bounded_stream(q1, n, bound, k_ref, v_ref, tri_ref, slice(0, HEAD_DIM)),
                   _diff_bounded_stream(q2, n, bound, k_ref, v_ref, tri_ref, slice(HEAD_DIM, 2 * HEAD_DIM)))

    for n in range(n_tiles):
        bounded_variant(n)

    @pl.when(bound > MAX_SCORE_BOUND)
    def _():
        finish(*_diff_general(q1, q2, qi, k_ref, v_ref, m1, l1, a1, m2, l2, a2))


def _diff_attention(proj, lam_params, bound, lambda_init):
    b, s, _ = proj.shape
    h = DIFF_HEADS
    tq, dd = DIFF_TQ, DIFF_DIAG
    assert tq == 2 * dd
    w = DIFF_V_DIM
    tri = jnp.asarray(np.arange(tq)[:, None] >= np.arange(dd)[None, :], F32)
    q_spec = lambda off: pl.BlockSpec((1, tq, w), lambda bi, hi, qi: (bi, qi, off + hi))
    kv_spec = lambda off: pl.BlockSpec((1, s, w), lambda bi, hi, qi: (bi, 0, off + hi))
    stat = pltpu.VMEM((tq, LANES), F32)
    acc = pltpu.VMEM((tq, w), F32)
    return pl.pallas_call(
        functools.partial(_diff_kernel, lambda_init=lambda_init, n_tiles=s // tq),
        grid=(b, h, s // tq),
        in_specs=[pl.BlockSpec(memory_space=pltpu.SMEM),
                  q_spec(0), q_spec(3 * h), kv_spec(h), kv_spec(2 * h),
                  pl.BlockSpec(lam_params.shape, lambda bi, hi, qi: (0, 0)),
                  pl.BlockSpec(tri.shape, lambda bi, hi, qi: (0, 0))],
        out_specs=pl.BlockSpec((1, tq, w), lambda bi, hi, qi: (bi, qi, hi)),
        out_shape=jax.ShapeDtypeStruct((b, s, DIFF_WIDTH), BF16),
        scratch_shapes=[stat, stat, acc, stat, stat, acc],
        compiler_params=pltpu.CompilerParams(
            dimension_semantics=("parallel", "parallel", "arbitrary"), vmem_limit_bytes=VMEM_LIMIT),
        name="diff_attention",
    )(bound, proj, proj, proj, proj, lam_params, tri)


def _rope_tables(pos):
    inv = 1.0 / (ROPE_THETA ** (np.arange(0, HEAD_DIM, 2, dtype=np.float64) / HEAD_DIM))
    ang = np.asarray(pos, np.float64)[:, None] * inv[None, :]
    cos, sin = np.cos(ang), np.sin(ang)
    return (jnp.asarray(np.concatenate([cos, cos], axis=-1), F32),
            jnp.asarray(np.concatenate([-sin, sin], axis=-1), F32))


def _retention_tables():
    h, c = RET_HEADS, RET_CHUNK
    log_g = np.log1p(-np.exp2(-5.0 - np.arange(h, dtype=np.float64)))
    j = np.arange(c, dtype=np.float64)
    diff = j[:, None] - j[None, :]
    intra = np.where(diff >= 0, np.exp(log_g[:, None, None] * np.maximum(diff, 0.0)), 0.0)
    q_dec = np.exp(log_g[:, None] * (j + 1.0))
    k_dec = np.exp(log_g[:, None] * (c - 1.0 - j))
    chunk_dec = np.exp(log_g * c)
    wide = lambda t: jnp.asarray(np.broadcast_to(t[:, :, None], (h, t.shape[1], HEAD_DIM)), F32)
    return jnp.asarray(intra, F32), wide(q_dec), wide(k_dec), wide(chunk_dec[:, None])


def _selection_tables(seq):
    tq, tk = NSA_TQ, NSA_TK
    n_cmp_rows = seq // CMP_STRIDE
    n_slc = seq // SLC_BLOCK
    assert tq == tk and n_slc <= AUG_ONE_LANE < LANES
    c_start = np.arange(n_cmp_rows) * CMP_STRIDE
    s_start = np.arange(n_slc) * SLC_BLOCK
    overlap_t = ((c_start[None, :] <= s_start[:, None] + SLC_BLOCK - 1)
                 & (c_start[None, :] + CMP_BLOCK - 1 >= s_start[:, None]))
    lane = np.arange(LANES)[None, :]
    key = np.arange(seq)[:, None]
    eaug = ((key // SLC_BLOCK) == lane) | (lane == AUG_ONE_LANE)
    cmask = lane * CMP_STRIDE + CMP_BLOCK - 1 <= key
    r = np.arange(tq)[:, None]
    c = np.arange(WINDOW + tq)[None, :]
    wmask = []
    for pat in range(WINDOW // tq + 1):
        d = min(pat * tq, WINDOW) + r - c
        wmask.append((d >= 0) & (d < WINDOW))
    tri = np.arange(tq)[:, None] >= np.arange(tk)[None, :]
    return (jnp.asarray(overlap_t, BF16), jnp.asarray(eaug, BF16), jnp.asarray(cmask, F32),
            jnp.asarray(np.stack(wmask), F32), jnp.asarray(tri, F32))


def _tile_modes(segments, tn):
    off, kinds, col = {}, [], 0
    for name, width, kind in segments:
        off[name] = col
        kinds += [kind] * (width // LANES)
        col += width
    per = tn // LANES
    tiles = [kinds[i:i + per] for i in range(0, len(kinds), per)]
    assert col % tn == 0
    modes = []
    for j, tile in enumerate(tiles):
        if modes and modes[-1][2] == tile:
            modes[-1] = (modes[-1][0], j + 1, tile)
        else:
            modes.append((j, j + 1, tile))
    return modes, off


def _l0_tile_modes():
    plain, silu = ("plain", 0), ("silu", 0)
    k_norm = ("nrope", L0_TAB_K)
    return _tile_modes([
        ("rq", RET_WIDTH, ("rope", L0_TAB_ROPE)), ("rk", RET_WIDTH, ("rope", L0_TAB_ROPE_SCALED)),
        ("rv", RET_WIDTH, plain), ("rgate", RET_WIDTH, silu), ("nq", NSA_WIDTH, ("nrope", L0_TAB_Q)),
        ("kc", NSA_KV_WIDTH, plain), ("vc", NSA_KV_WIDTH, plain), ("ks", NSA_KV_WIDTH, k_norm),
        ("vs", NSA_KV_WIDTH, plain), ("kw", NSA_KV_WIDTH, k_norm), ("vw", NSA_KV_WIDTH, plain),
        ("ngate", NSA_WIDTH, silu)], PROJ_TN)


def _l1_tile_modes():
    return _tile_modes([("q", DIFF_WIDTH, ("nrope", L1_TAB_Q)), ("k", DIFF_WIDTH, ("nrope", L1_TAB_K)),
                        ("v", DIFF_WIDTH, ("plain", 0)), ("gate", DIFF_WIDTH, ("silu", 0))], L1_PROJ_TN)[0]


def _rope_pair(cos, sin_signed, gain=None, scale=1.0):
    if gain is None:
        return [cos * scale, sin_signed * scale]
    return [cos * (gain * scale)[None, :], sin_signed * (jnp.roll(gain, HALF) * scale)[None, :]]


def kernel(x, l0_norm_g, l0_w_in, l0_w_out, l0_nsa_q_norm_g, l0_nsa_k_norm_g, l0_cmp_pe_k, l0_cmp_w1_k, l0_cmp_w2_k, l0_cmp_pe_v, l0_cmp_w1_v, l0_cmp_w2_v, l1_norm_g, l1_w_in, l1_w_out, l1_q_norm_g, l1_k_norm_g, l1_lambda_q1, l1_lambda_k1, l1_lambda_q2, l1_lambda_k2):
    b, s, d = x.shape
    m = b * s
    x2d = x.reshape(m, d)
    cos, sin = _rope_tables(np.arange(s))
    cos_c, sin_c = _rope_tables(np.arange(s // CMP_STRIDE) * CMP_STRIDE + CMP_BLOCK - 1)
    intra, qdec, kdec, cdec = _retention_tables()
    nsa_tables = _selection_tables(s)

    modes0, off = _l0_tile_modes()
    w0 = l0_w_in.astype(BF16)
    wg = l0_w_in[:, AB_MAIN_COLS:].reshape(d, 3, NSA_KV_HEADS, NSA_GROUP).transpose(0, 2, 1, 3)
    wg = wg.reshape(d, NSA_KV_HEADS, 3 * NSA_GROUP)
    wg = jnp.pad(wg, ((0, 0), (0, 0), (0, LANES - 3 * NSA_GROUP))).reshape(d, NSA_KV_HEADS * LANES).astype(BF16)
    tabs0 = jnp.stack(_rope_pair(cos, sin) + _rope_pair(cos, sin, scale=QK_SCALE)
                      + _rope_pair(cos, sin, l0_nsa_q_norm_g, Q_SCALE) + _rope_pair(cos, sin, l0_nsa_k_norm_g))
    proj0, gates = _norm_proj(x2d, l0_norm_g, w0, tabs0, modes0, s, PROJ_TN, wg=wg)
    proj0 = proj0.reshape(b, s, AB_MAIN_COLS)
    y_ret = _retention(proj0, intra, qdec, kdec, cdec)
    w1k = l0_cmp_w1_k.astype(BF16).reshape(CMP_BLOCK, HEAD_DIM, HEAD_DIM)
    w1v = l0_cmp_w1_v.astype(BF16).reshape(CMP_BLOCK, HEAD_DIM, HEAD_DIM)
    kcmp, vcmp = _nsa_compress(proj0, off["kc"] // HEAD_DIM, off["vc"] // HEAD_DIM,
                               l0_cmp_pe_k, w1k, l0_cmp_w2_k.astype(BF16),
                               l0_cmp_pe_v, w1v, l0_cmp_w2_v.astype(BF16),
                               l0_nsa_k_norm_g, cos_c, sin_c)
    bound = (BOUND_MARGIN * HEAD_DIM * Q_SCALE * jnp.max(jnp.abs(l0_nsa_q_norm_g))
             * jnp.max(jnp.abs(l0_nsa_k_norm_g))).reshape(1).astype(F32)
    y_nsa, w_out0, w_in1, w_out1 = _nsa_attention(proj0, gates, kcmp, vcmp, bound, nsa_tables, off,
                                                  cast_weights=[l0_w_out, l1_w_in, l1_w_out])
    x1 = _out_proj([y_ret.reshape(m, RET_WIDTH), y_nsa.reshape(m, NSA_WIDTH)],
                   [w_out0[:RET_WIDTH], w_out0[RET_WIDTH:]], x2d)

    lambda_init = 0.8 - 0.6 * math.exp(-0.3 * 1)
    tabs1 = jnp.stack(_rope_pair(cos, sin, l1_q_norm_g, Q_SCALE) + _rope_pair(cos, sin, l1_k_norm_g))
    proj1 = _norm_proj(x1, l1_norm_g, w_in1, tabs1, _l1_tile_modes(), s, L1_PROJ_TN)[0]
    lam_params = jnp.stack([l1_lambda_q1, l1_lambda_k1, l1_lambda_q2, l1_lambda_k2]).astype(F32)
    bound1 = (BOUND_MARGIN * HEAD_DIM * Q_SCALE * jnp.max(jnp.abs(l1_q_norm_g))
              * jnp.max(jnp.abs(l1_k_norm_g))).reshape(1).astype(F32)
    y1 = _diff_attention(proj1.reshape(b, s, C_IN_COLS), lam_params, bound1, lambda_init)
    out = _out_proj([y1.reshape(m, DIFF_WIDTH)], [w_out1], x1)
    return out.reshape(b, s, d)
```

```python
import functools
import math

import numpy as np
import jax
import jax.numpy as jnp
from jax import lax
from jax.experimental import pallas as pl
from jax.experimental.pallas import tpu as pltpu

F32 = jnp.float32
BF16 = jnp.bfloat16

D_MODEL = 2048
HEAD_DIM = 128
HALF = HEAD_DIM // 2
ROPE_THETA = 10000.0
EPS = 1e-6
RET_HEADS = 8
RET_CHUNK = 256
RET_HEADS_PER_STEP = 2
NSA_HEADS = 8
NSA_KV_HEADS = 2
NSA_GROUP = NSA_HEADS // NSA_KV_HEADS
CMP_BLOCK = 32
CMP_STRIDE = 16
SLC_BLOCK = 64
SLC_TOPK = 16
N_LOCAL_BLOCKS = 2
WINDOW = 512
DIFF_HEADS = 8
DIFF_V_DIM = 2 * HEAD_DIM
QK_SCALE = HEAD_DIM ** -0.5
LOG2E = math.log2(math.e)
Q_SCALE = QK_SCALE * LOG2E

RET_WIDTH = RET_HEADS * HEAD_DIM
NSA_WIDTH = NSA_HEADS * HEAD_DIM
NSA_KV_WIDTH = NSA_KV_HEADS * HEAD_DIM
AB_MAIN_COLS = 4 * RET_WIDTH + 2 * NSA_WIDTH + 6 * NSA_KV_WIDTH
N_GATE_COLS = 3 * NSA_HEADS
DIFF_WIDTH = DIFF_HEADS * DIFF_V_DIM
C_IN_COLS = 4 * DIFF_WIDTH

LANES = 128
MASKED = -1e30
M_INIT = -1e29
MAX_SCORE_BOUND = 60.0
BOUND_MARGIN = 1.05
AUG_ONE_LANE = 32
VMEM_LIMIT = 56 * 1024 * 1024

PROJ_TM = 1024
PROJ_TN = 1280
L1_PROJ_TN = 1024
OPROJ_TM = 2048
OPROJ_TN = 512
PROJ_SUB = 256
L0_TAB_ROPE, L0_TAB_ROPE_SCALED, L0_TAB_Q, L0_TAB_K = range(4)
L1_TAB_Q, L1_TAB_K = range(2)
NSA_TQ = 512
NSA_TK = 512
DIFF_TQ = 512
DIFF_TK = 512
DIFF_DIAG = 256


def _dot(a, b):
    return jnp.dot(a, b, preferred_element_type=F32)


def _dot_nt(a, b):
    return lax.dot_general(a, b, (((1,), (1,)), ((), ())), preferred_element_type=F32)


def _dot_tn(a, b):
    return lax.dot_general(a, b, (((0,), (0,)), ((), ())), preferred_element_type=F32)


def _silu(x):
    return x / (1.0 + jnp.exp(-x))


def _rms(x):
    return x * lax.rsqrt(jnp.mean(x * x, axis=-1, keepdims=True) + EPS)


def _rope(x, cos, sin_signed):
    return x * cos + pltpu.roll(x, HALF, axis=1) * sin_signed


def _rope_tab(seg, t, tabs_ref, rows):
    return seg * tabs_ref[2 * t, rows, :] + pltpu.roll(seg, HALF, axis=1) * tabs_ref[2 * t + 1, rows, :]


def _tile_epilogue(acc, kinds, tabs_ref, rows, o_ref):
    n_seg = len(kinds)
    ri = lax.broadcasted_iota(jnp.int32, (2 * LANES, 2 * LANES), 0)
    ci = lax.broadcasted_iota(jnp.int32, (2 * LANES, 2 * LANES), 1)
    pair_mean = jnp.where((ri < LANES) == (ci < LANES), 1.0 / HEAD_DIM, 0.0).astype(BF16)
    c = 0
    while c < n_seg:
        op, t = kinds[c]
        cols = slice(c * LANES, (c + 1) * LANES)
        if op == "nrope" and c + 1 < n_seg and kinds[c + 1][0] == "nrope":
            cols2 = slice(c * LANES, (c + 2) * LANES)
            seg2 = acc[:, cols2]
            seg2 = seg2 * lax.rsqrt(_dot((seg2 * seg2).astype(BF16), pair_mean) + EPS)
            for half in range(2):
                out = _rope_tab(seg2[:, half * LANES:(half + 1) * LANES], kinds[c + half][1], tabs_ref, rows)
                o_ref[rows, (c + half) * LANES:(c + half + 1) * LANES] = out.astype(o_ref.dtype)
            c += 2
            continue
        seg = acc[:, cols]
        if op == "silu":
            seg = _silu(seg)
        elif op == "nrope":
            seg = seg * lax.rsqrt(_dot((seg * seg).astype(BF16), pair_mean[:LANES, :LANES]) + EPS)
            seg = _rope_tab(seg, t, tabs_ref, rows)
        elif op == "rope":
            seg = _rope_tab(seg, t, tabs_ref, rows)
        o_ref[rows, cols] = seg.astype(o_ref.dtype)
        c += 1


def _proj_kernel(*refs, tile_modes, has_gates, w_transposed):
    if has_gates:
        x_ref, g_ref, w_ref, tabs_ref, wg_ref, o_ref, og_ref, h_scr = refs
    else:
        x_ref, g_ref, w_ref, tabs_ref, o_ref, h_scr = refs
    j = pl.program_id(1)
    n_sub = h_scr.shape[0] // PROJ_SUB

    def make_branch(cond, kinds, first):
        @pl.when(cond)
        def _():
            for r in range(n_sub):
                rows = slice(r * PROJ_SUB, (r + 1) * PROJ_SUB)
                if first:
                    h = (_rms(x_ref[rows, :]) * g_ref[...]).astype(BF16)
                    h_scr[rows, :] = h
                    if has_gates:
                        og_ref[rows, :] = 1.0 / (1.0 + jnp.exp(-_dot(h, wg_ref[...])))
                else:
                    h = h_scr[rows, :]
                acc = _dot_nt(h, w_ref[...]) if w_transposed else _dot(h, w_ref[...])
                _tile_epilogue(acc, kinds, tabs_ref, rows, o_ref)

    for lo, hi, kinds in tile_modes:
        if lo == 0:
            make_branch(j == 0, kinds, True)
            lo = 1
        if hi > lo:
            make_branch((j >= lo) & (j < hi), kinds, False)


def _norm_proj(x2d, g, w, tabs, tile_modes, seq, tn, wg=None, w_transposed=False):
    m, d = x2d.shape
    tm = PROJ_TM
    n = tile_modes[-1][1] * tn
    s_tiles = seq // tm
    has_gates = wg is not None
    in_specs = [
        pl.BlockSpec((tm, d), lambda i, j: (i, 0)),
        pl.BlockSpec((1, d), lambda i, j: (0, 0)),
        pl.BlockSpec((tn, d), lambda i, j: (j, 0)) if w_transposed else pl.BlockSpec((d, tn), lambda i, j: (0, j)),
        pl.BlockSpec((tabs.shape[0], tm, LANES), lambda i, j: (0, i % s_tiles, 0)),
    ]
    args = [x2d, g.reshape(1, d), w, tabs]
    out_shape = [jax.ShapeDtypeStruct((m, n), BF16)]
    out_specs = [pl.BlockSpec((tm, tn), lambda i, j: (i, j))]
    if has_gates:
        ng = wg.shape[1]
        in_specs.append(pl.BlockSpec((d, ng), lambda i, j: (0, 0)))
        args.append(wg)
        out_shape.append(jax.ShapeDtypeStruct((m, ng), F32))
        out_specs.append(pl.BlockSpec((tm, ng), lambda i, j: (i, 0)))
    return pl.pallas_call(
        functools.partial(_proj_kernel, tile_modes=tile_modes, has_gates=has_gates, w_transposed=w_transposed),
        grid=(m // tm, n // tn),
        in_specs=in_specs,
        out_specs=out_specs,
        out_shape=out_shape,
        scratch_shapes=[pltpu.VMEM((tm, d), BF16)],
        compiler_params=pltpu.CompilerParams(
            dimension_semantics=("parallel", "arbitrary"), vmem_limit_bytes=VMEM_LIMIT),
        name="norm_proj",
    )(*args)


def _oproj_kernel(*refs, n_terms):
    y_refs, w_refs = refs[:n_terms], refs[n_terms:2 * n_terms]
    x_ref, o_ref = refs[2 * n_terms:]
    acc = x_ref[...]
    for y_ref, w_ref in zip(y_refs, w_refs):
        acc = acc + _dot(y_ref[...], w_ref[...])
    o_ref[...] = acc


def _out_proj(ys, ws, x2d):
    m, n = x2d.shape
    tm, tn = OPROJ_TM, OPROJ_TN
    y_specs = [pl.BlockSpec((tm, y.shape[1]), lambda i, j: (i, 0)) for y in ys]
    w_specs = [pl.BlockSpec((w.shape[0], tn), lambda i, j: (0, j)) for w in ws]
    return pl.pallas_call(
        functools.partial(_oproj_kernel, n_terms=len(ys)),
        grid=(m // tm, n // tn),
        in_specs=y_specs + w_specs + [pl.BlockSpec((tm, tn), lambda i, j: (i, j))],
        out_specs=pl.BlockSpec((tm, tn), lambda i, j: (i, j)),
        out_shape=jax.ShapeDtypeStruct((m, n), F32),
        compiler_params=pltpu.CompilerParams(
            dimension_semantics=("parallel", "arbitrary"), vmem_limit_bytes=VMEM_LIMIT),
        name="out_proj",
    )(*ys, *ws, x2d)


def _ret_kernel(q_ref, k_ref, v_ref, gate_ref, intra_ref, qdec_ref, kdec_ref, cdec_ref, o_ref, *, n_chunks):
    c = RET_CHUNK
    states = [jnp.zeros((HEAD_DIM, HEAD_DIM), F32)] * RET_HEADS_PER_STEP
    for n in range(n_chunks):
        rows = slice(n * c, (n + 1) * c)
        for hh in range(RET_HEADS_PER_STEP):
            cols = slice(hh * HEAD_DIM, (hh + 1) * HEAD_DIM)
            q = q_ref[0, rows, cols]
            k = k_ref[0, rows, cols]
            v = v_ref[0, rows, cols]
            scores = _dot_nt(q, k) * intra_ref[hh]
            inner = _dot(scores.astype(BF16), v)
            cross = _dot((q.astype(F32) * qdec_ref[hh]).astype(BF16), states[hh].astype(BF16))
            kv = _dot_tn((k.astype(F32) * kdec_ref[hh]).astype(BF16), v)
            states[hh] = states[hh] * cdec_ref[hh] + kv
            o = _rms(inner + cross) * gate_ref[0, rows, cols].astype(F32)
            o_ref[0, rows, cols] = o.astype(o_ref.dtype)


def _retention(proj, intra, qdec, kdec, cdec):
    b, s, _ = proj.shape
    per = RET_HEADS_PER_STEP
    h = RET_HEADS // per
    y_shape = (b, s, RET_WIDTH)
    head = lambda off: pl.BlockSpec((1, s, per * HEAD_DIM), lambda bi, hi: (bi, 0, off + hi))
    table = lambda rows, cols=HEAD_DIM: pl.BlockSpec((per, rows, cols), lambda bi, hi: (hi, 0, 0))
    return pl.pallas_call(
        functools.partial(_ret_kernel, n_chunks=s // RET_CHUNK),
        grid=(b, h),
        in_specs=[head(0), head(h), head(2 * h), head(3 * h),
                  table(RET_CHUNK, RET_CHUNK), table(RET_CHUNK), table(RET_CHUNK), table(1)],
        out_specs=pl.BlockSpec((1, s, per * HEAD_DIM), lambda bi, hi: (bi, 0, hi)),
        out_shape=jax.ShapeDtypeStruct(y_shape, BF16),
        compiler_params=pltpu.CompilerParams(
            dimension_semantics=("parallel", "arbitrary"), vmem_limit_bytes=VMEM_LIMIT),
        name="retention",
    )(proj, proj, proj, proj, intra, qdec, kdec, cdec)


def _cmp_kernel(kc_ref, vc_ref, pek_ref, w1k_ref, w2k_ref, pev_ref, w1v_ref, w2v_ref, kg_ref,
                cos_ref, sin_ref, ko_ref, vo_ref, t_scr, *, seq):
    n_rows = seq // CMP_STRIDE

    def compress(src_ref, pe_ref, w1_ref, w2_ref):
        t_scr[0:seq, :] = src_ref[0].astype(F32)
        t_scr[seq:seq + CMP_STRIDE, :] = jnp.zeros((CMP_STRIDE, HEAD_DIM), F32)
        acc = jnp.zeros((n_rows, HEAD_DIM), F32)
        for r in range(CMP_BLOCK):
            rows = t_scr[pl.ds(r, n_rows, stride=CMP_STRIDE), :] + pe_ref[r:r + 1, :]
            acc = acc + _dot(rows.astype(BF16), w1_ref[r])
        return _dot(_silu(acc).astype(BF16), w2_ref[...])

    kc = compress(kc_ref, pek_ref, w1k_ref, w2k_ref)
    kc = _rope(_rms(kc) * kg_ref[...], cos_ref[...], sin_ref[...])
    ko_ref[0, 0] = kc.astype(ko_ref.dtype)
    vo_ref[0, 0] = compress(vc_ref, pev_ref, w1v_ref, w2v_ref).astype(vo_ref.dtype)


def _nsa_compress(proj, kc_off, vc_off, pe_k, w1_k, w2_k, pe_v, w1_v, w2_v, k_g, cos_c, sin_c):
    b, s, _ = proj.shape
    g = NSA_KV_HEADS
    n_rows = s // CMP_STRIDE
    head = lambda off: pl.BlockSpec((1, s, HEAD_DIM), lambda bi, gi: (bi, 0, off + gi))
    full = lambda shape: pl.BlockSpec(shape, lambda bi, gi: (0,) * len(shape))
    out_spec = pl.BlockSpec((1, 1, n_rows, HEAD_DIM), lambda bi, gi: (bi, gi, 0, 0))
    out_sds = jax.ShapeDtypeStruct((b, g, n_rows, HEAD_DIM), BF16)
    return pl.pallas_call(
        functools.partial(_cmp_kernel, seq=s),
        grid=(b, g),
        in_specs=[head(kc_off), head(vc_off),
                  full((CMP_BLOCK, HEAD_DIM)), full((CMP_BLOCK, HEAD_DIM, HEAD_DIM)), full((HEAD_DIM, HEAD_DIM)),
                  full((CMP_BLOCK, HEAD_DIM)), full((CMP_BLOCK, HEAD_DIM, HEAD_DIM)), full((HEAD_DIM, HEAD_DIM)),
                  full((1, HEAD_DIM)), full((n_rows, HEAD_DIM)), full((n_rows, HEAD_DIM))],
        out_specs=[out_spec, out_spec],
        out_shape=[out_sds, out_sds],
        scratch_shapes=[pltpu.VMEM((s + CMP_STRIDE, HEAD_DIM), F32)],
        compiler_params=pltpu.CompilerParams(
            dimension_semantics=("parallel", "arbitrary"), vmem_limit_bytes=VMEM_LIMIT),
        name="nsa_compress",
    )(proj, proj, pe_k, w1_k, w2_k, pe_v, w1_v, w2_v, k_g.reshape(1, HEAD_DIM), cos_c, sin_c)


def _flash_init(m_scr, l_scr, acc_scr):
    m_scr[...] = jnp.full(m_scr.shape, M_INIT, F32)
    l_scr[...] = jnp.zeros(l_scr.shape, F32)
    acc_scr[...] = jnp.zeros(acc_scr.shape, F32)


def _lane_tiles(x, width):
    return x if width == LANES else jnp.concatenate([x] * (width // LANES), axis=1)


def _flash_step(q, k, v, bias, m_scr, l_scr, acc_scr):
    s = _dot_nt(q, k)
    if bias is not None:
        s = s + bias
    tk = s.shape[1]
    m_old = m_scr[...]
    m_new = jnp.maximum(m_old, jnp.max(s, axis=-1, keepdims=True))
    alpha = jnp.exp2(m_old - m_new)
    p = jnp.exp2(s - _lane_tiles(m_new, tk))
    p_cols = p[:, 0:LANES]
    for c in range(1, tk // LANES):
        p_cols = p_cols + p[:, c * LANES:(c + 1) * LANES]
    l_scr[...] = alpha * l_scr[...] + p_cols
    acc_scr[...] = _lane_tiles(alpha, acc_scr.shape[1]) * acc_scr[...] + _dot(p.astype(BF16), v)
    m_scr[...] = m_new


def _flash_finish(l_scr, acc_scr):
    return acc_scr[...] * (1.0 / jnp.sum(l_scr[...], axis=-1, keepdims=True))


def _rep_heads(bias):
    return jnp.concatenate([bias] * NSA_GROUP, axis=0)


def _stack_heads(qblk):
    return jnp.concatenate([qblk[:, r * HEAD_DIM:(r + 1) * HEAD_DIM] for r in range(NSA_GROUP)], axis=0)


def _select_blocks(p, ovl, q0, tq, n_slc):
    psum = p[0:tq]
    for r in range(1, NSA_GROUP):
        psum = psum + p[r * tq:(r + 1) * tq]
    p_hi = psum.astype(BF16)
    rem = psum - p_hi.astype(F32)
    p_mid = rem.astype(BF16)
    p_lo = (rem - p_mid.astype(F32)).astype(BF16)
    imp = _dot_nt(ovl, p_hi) + _dot_nt(ovl, p_mid) + _dot_nt(ovl, p_lo)
    jb = lax.broadcasted_iota(jnp.int32, (n_slc, tq), 0)
    blk_t = jnp.right_shift(q0 + lax.broadcasted_iota(jnp.int32, (n_slc, tq), 1), int(math.log2(SLC_BLOCK)))
    back = blk_t - jb
    forced = (jb == 0) | ((back >= 0) & (back < N_LOCAL_BLOCKS))
    score = jnp.where(forced, 1e9, jnp.where(back >= 0, imp, -1e9))
    rank = jnp.zeros((n_slc, tq), F32)
    for mp in range(n_slc):
        row = score[mp:mp + 1, :]
        ahead = (row > score) | ((row == score) & (jb > mp))
        rank = rank + jnp.where(ahead, 1.0, 0.0)
    sel_t = jnp.where(rank < float(min(SLC_TOPK, n_slc)), 1.0, 0.0)
    sel_t = jnp.concatenate([sel_t, jnp.zeros((LANES - n_slc, tq), F32)], axis=0).astype(BF16)
    ri = lax.broadcasted_iota(jnp.int32, (tq, tq), 0)
    ci = lax.broadcasted_iota(jnp.int32, (tq, tq), 1)
    eye = jnp.where(ri == ci, 1.0, 0.0).astype(BF16)
    return _dot_nt(eye, sel_t)


def _nsa_combine(o_cmp, o_slc, o_win, gates_ref, ngate_ref, o_ref, tq):
    gates = gates_ref[...]
    for r in range(NSA_GROUP):
        rows = slice(r * tq, (r + 1) * tq)
        g_cmp = gates[:, r:r + 1]
        g_slc = gates[:, NSA_GROUP + r:NSA_GROUP + r + 1]
        g_win = gates[:, 2 * NSA_GROUP + r:2 * NSA_GROUP + r + 1]
        y = g_cmp * o_cmp[rows] + g_slc * o_slc[rows] + g_win * o_win[rows]
        cols = slice(r * HEAD_DIM, (r + 1) * HEAD_DIM)
        o_ref[0, :, cols] = (y * ngate_ref[0, :, cols].astype(F32)).astype(o_ref.dtype)


def _nsa_general(q4, q0, qi, gates_ref, ngate_ref, kcmp_ref, vcmp_ref, ks_ref, vs_ref, kw_ref, vw_ref,
                 ovl_ref, eaug_ref, o_ref, bias_scr, m_scr, l_scr, acc_scr, *, seq):
    tq, tk = NSA_TQ, NSA_TK
    n_slc = seq // SLC_BLOCK

    tpos = q0 + lax.broadcasted_iota(jnp.int32, (tq, LANES), 0)
    cidx = lax.broadcasted_iota(jnp.int32, (tq, LANES), 1)
    cbias = jnp.where(cidx * CMP_STRIDE + (CMP_BLOCK - 1) <= tpos, 0.0, MASKED)
    s = _dot_nt(q4, kcmp_ref[0, 0]) + _rep_heads(cbias)
    m = jnp.maximum(jnp.max(s, axis=-1, keepdims=True), M_INIT)
    e = jnp.exp2(s - m)
    l = jnp.sum(e, axis=-1, keepdims=True)
    p = e * (1.0 / jnp.maximum(l, 1e-30))
    o_cmp = _dot(p.astype(BF16), vcmp_ref[0, 0])

    sel = _select_blocks(p, ovl_ref[...], q0, tq, n_slc).astype(BF16)
    sel_keys = _dot_nt(sel, eaug_ref[...])
    for kt in range(seq // tk):
        bias_scr[kt] = jnp.where(sel_keys[:, kt * tk:(kt + 1) * tk] > 0.5, 0.0, MASKED)

    _flash_init(m_scr, l_scr, acc_scr)

    def slc_body(kt, carry):
        k0 = pl.multiple_of(kt * tk, tk)
        _flash_step(q4, ks_ref[0, pl.ds(k0, tk), :], vs_ref[0, pl.ds(k0, tk), :],
                    _rep_heads(bias_scr[kt]), m_scr, l_scr, acc_scr)
        return carry

    n_full = lax.div(q0, tk)
    lax.fori_loop(0, n_full, slc_body, 0)
    kd = pl.multiple_of(n_full * tk, tk)
    qpos = q0 + lax.broadcasted_iota(jnp.int32, (tq, tk), 0)
    kpos = kd + lax.broadcasted_iota(jnp.int32, (tq, tk), 1)
    causal = jnp.where(kpos <= qpos, 0.0, MASKED)
    _flash_step(q4, ks_ref[0, pl.ds(kd, tk), :], vs_ref[0, pl.ds(kd, tk), :],
                _rep_heads(bias_scr[n_full] + causal), m_scr, l_scr, acc_scr)
    o_slc = _flash_finish(l_scr, acc_scr)

    wk = WINDOW + tq
    ws = pl.multiple_of(jnp.maximum(q0 - WINDOW, 0), tq)
    d = (q0 - ws) + lax.broadcasted_iota(jnp.int32, (tq, wk), 0) - lax.broadcasted_iota(jnp.int32, (tq, wk), 1)
    wbias = jnp.where((d >= 0) & (d < WINDOW), 0.0, MASKED)
    s = _dot_nt(q4, kw_ref[0, pl.ds(ws, wk), :]) + _rep_heads(wbias)
    e = jnp.exp2(s - jnp.max(s, axis=-1, keepdims=True))
    o_win = _dot(e.astype(BF16), vw_ref[0, pl.ds(ws, wk), :]) * (1.0 / jnp.sum(e, axis=-1, keepdims=True))

    _nsa_combine(o_cmp, o_slc, o_win, gates_ref, ngate_ref, o_ref, tq)


def _nsa_bounded(q4, q0, qi, bound, gates_ref, ngate_ref, kcmp_ref, vcmp_ref, ks_ref, vs_ref, kw_ref, vw_ref,
                 ovl_ref, eaug_ref, cmask_ref, wmask_ref, tri_ref, o_ref,
                 ksa_scr, kwa_scr, kca_scr, l_scr, acc_scr, *, seq):
    tq, tk = NSA_TQ, NSA_TK
    n_slc = seq // SLC_BLOCK
    n_cmp_rows = kca_scr.shape[0]

    @pl.when(qi == 0)
    def _():
        one_col = eaug_ref[...]
        lane = lax.broadcasted_iota(jnp.int32, one_col.shape, 1)
        ksa_scr[:, :HEAD_DIM] = ks_ref[0]
        ksa_scr[:, HEAD_DIM:] = one_col
        one_col = jnp.where(lane == AUG_ONE_LANE, one_col, jnp.zeros_like(one_col))
        kwa_scr[:, :HEAD_DIM] = kw_ref[0]
        kwa_scr[:, HEAD_DIM:] = one_col
        kca_scr[:, :HEAD_DIM] = kcmp_ref[0, 0]
        kca_scr[:, HEAD_DIM:] = one_col[:n_cmp_rows]

    lane = lax.broadcasted_iota(jnp.int32, (tq, LANES), 1)
    shift_cols = jnp.where(lane == AUG_ONE_LANE, -bound, 0.0)
    qa = jnp.concatenate([q4, _rep_heads(shift_cols.astype(BF16))], axis=1)

    e = jnp.exp2(_dot_nt(qa, kca_scr[...])) * _rep_heads(cmask_ref[...])
    l = jnp.sum(e, axis=-1, keepdims=True)
    p = e * jnp.where(l > 0.0, 1.0 / l, 0.0)
    o_cmp = _dot(p.astype(BF16), vcmp_ref[0, 0])

    sel = _select_blocks(p, ovl_ref[...], q0, tq, n_slc)
    sel_cols = jnp.where(lane < n_slc, jnp.where(sel > 0.5, 0.0, MASKED), shift_cols)
    qs = jnp.concatenate([q4, _rep_heads(sel_cols.astype(BF16))], axis=1)

    l_scr[...] = jnp.zeros(l_scr.shape, F32)
    acc_scr[...] = jnp.zeros(acc_scr.shape, F32)

    def accumulate(k0, width, mask):
        p_t = jnp.exp2(_dot_nt(qs, ksa_scr[pl.ds(k0, width), :]))
        if mask is not None:
            p_t = p_t * mask
        l_scr[...] += _lane_tile_sum(p_t)
        acc_scr[...] += _dot(p_t.astype(BF16), vs_ref[0, pl.ds(k0, width), :])

    big = 2 * tk

    def slc_body(kt, carry):
        accumulate(pl.multiple_of(kt * big, big), big, None)
        return carry

    n_full = lax.div(q0, tk)
    n_big = lax.div(n_full, 2)
    lax.fori_loop(0, n_big, slc_body, 0)

    @pl.when(n_full > 2 * n_big)
    def _():
        accumulate(pl.multiple_of(n_big * big, big), tk, None)

    accumulate(pl.multiple_of(n_full * tk, tk), tk, _rep_heads(tri_ref[...]))
    o_slc = _flash_finish(l_scr, acc_scr)

    wk = WINDOW + tq
    ws = pl.multiple_of(jnp.maximum(q0 - WINDOW, 0), tq)
    e = jnp.exp2(_dot_nt(qa, kwa_scr[pl.ds(ws, wk), :])) * _rep_heads(wmask_ref[0])
    o_win = _dot(e.astype(BF16), vw_ref[0, pl.ds(ws, wk), :]) * (1.0 / jnp.sum(e, axis=-1, keepdims=True))

    _nsa_combine(o_cmp, o_slc, o_win, gates_ref, ngate_ref, o_ref, tq)


def _nsa_kernel(bound_ref, q_ref, ngate_ref, gates_ref, kcmp_ref, vcmp_ref, ks_ref, vs_ref, kw_ref, vw_ref,
                ovl_ref, eaug_ref, cmask_ref, wmask_ref, tri_ref, *rest, seq, n_casts):
    cast_in, (o_ref, *cast_out) = rest[:n_casts], rest[n_casts:2 * n_casts + 1]
    bias_scr, m_scr, l_scr, acc_scr, ksa_scr, kwa_scr, kca_scr = rest[2 * n_casts + 1:]
    for src, dst in zip(cast_in, cast_out):
        dst[...] = src[...].astype(dst.dtype)
    qi = pl.program_id(2)
    q0 = qi * NSA_TQ
    q4 = _stack_heads(q_ref[0])
    bound = bound_ref[0]

    @pl.when(bound <= MAX_SCORE_BOUND)
    def _():
        _nsa_bounded(q4, q0, qi, bound, gates_ref, ngate_ref, kcmp_ref, vcmp_ref, ks_ref, vs_ref, kw_ref, vw_ref,
                     ovl_ref, eaug_ref, cmask_ref, wmask_ref, tri_ref, o_ref,
                     ksa_scr, kwa_scr, kca_scr, l_scr, acc_scr, seq=seq)

    @pl.when(bound > MAX_SCORE_BOUND)
    def _():
        _nsa_general(q4, q0, qi, gates_ref, ngate_ref, kcmp_ref, vcmp_ref, ks_ref, vs_ref, kw_ref, vw_ref,
                     ovl_ref, eaug_ref, o_ref, bias_scr, m_scr, l_scr, acc_scr, seq=seq)


def _nsa_attention(proj, gates, kcmp, vcmp, bound, tables, offs, cast_weights):
    b, s, _ = proj.shape
    g = NSA_KV_HEADS
    tq = NSA_TQ
    nq = s // tq
    gw = NSA_GROUP * HEAD_DIM
    n_cmp_rows = s // CMP_STRIDE
    ovl_t, eaug, cmask, wmask, tri = tables
    q_spec = lambda off: pl.BlockSpec((1, tq, gw), lambda bi, gi, qi: (bi, qi, off + gi))
    kv_spec = lambda off: pl.BlockSpec((1, s, HEAD_DIM), lambda bi, gi, qi: (bi, 0, off + gi))
    cmp_spec = pl.BlockSpec((1, 1, n_cmp_rows, HEAD_DIM), lambda bi, gi, qi: (bi, gi, 0, 0))
    full = lambda shape: pl.BlockSpec(shape, lambda bi, gi, qi: (0,) * len(shape))
    n_wpat = wmask.shape[0]
    rows4 = NSA_GROUP * tq
    n_steps = b * g * nq
    cast_specs = [pl.BlockSpec((w.shape[0] // n_steps, w.shape[1]), lambda bi, gi, qi: ((bi * g + gi) * nq + qi, 0))
                  for w in cast_weights]
    return pl.pallas_call(
        functools.partial(_nsa_kernel, seq=s, n_casts=len(cast_weights)),
        grid=(b, g, nq),
        in_specs=[pl.BlockSpec(memory_space=pltpu.SMEM),
                  q_spec(offs["nq"] // gw), q_spec(offs["ngate"] // gw),
                  pl.BlockSpec((tq, LANES), lambda bi, gi, qi: (bi * nq + qi, gi)),
                  cmp_spec, cmp_spec,
                  kv_spec(offs["ks"] // HEAD_DIM), kv_spec(offs["vs"] // HEAD_DIM),
                  kv_spec(offs["kw"] // HEAD_DIM), kv_spec(offs["vw"] // HEAD_DIM),
                  full(ovl_t.shape), full(eaug.shape),
                  pl.BlockSpec((tq, LANES), lambda bi, gi, qi: (qi, 0)),
                  pl.BlockSpec((1,) + wmask.shape[1:], lambda bi, gi, qi: (jnp.minimum(qi, n_wpat - 1), 0, 0)),
                  full(tri.shape)] + cast_specs,
        out_specs=[pl.BlockSpec((1, tq, gw), lambda bi, gi, qi: (bi, qi, gi))] + cast_specs,
        out_shape=[jax.ShapeDtypeStruct((b, s, NSA_WIDTH), BF16)]
        + [jax.ShapeDtypeStruct(w.shape, BF16) for w in cast_weights],
        scratch_shapes=[pltpu.VMEM((s // NSA_TK, tq, NSA_TK), F32),
                        pltpu.VMEM((rows4, LANES), F32), pltpu.VMEM((rows4, LANES), F32),
                        pltpu.VMEM((rows4, HEAD_DIM), F32),
                        pltpu.VMEM((s, 2 * HEAD_DIM), BF16), pltpu.VMEM((s, 2 * HEAD_DIM), BF16),
                        pltpu.VMEM((n_cmp_rows, 2 * HEAD_DIM), BF16)],
        compiler_params=pltpu.CompilerParams(
            dimension_semantics=("parallel", "parallel", "arbitrary"), vmem_limit_bytes=VMEM_LIMIT),
        name="nsa_attention",
    )(bound, proj, proj, gates, kcmp, vcmp, proj, proj, proj, proj, ovl_t, eaug, cmask, wmask, tri, *cast_weights)


def _diff_general(q1, q2, qi, k_ref, v_ref, m1, l1, a1, m2, l2, a2):
    tq, tk = DIFF_TQ, DIFF_TK
    _flash_init(m1, l1, a1)
    _flash_init(m2, l2, a2)

    def step(rows, k0, width, bias):
        k = k_ref[0, pl.ds(k0, width), :]
        v = v_ref[0, pl.ds(k0, width), :]
        _flash_step(q1[rows], k[:, :HEAD_DIM], v, bias, m1.at[rows], l1.at[rows], a1.at[rows])
        _flash_step(q2[rows], k[:, HEAD_DIM:], v, bias, m2.at[rows], l2.at[rows], a2.at[rows])

    def body(kt, carry):
        step(slice(0, tq), pl.multiple_of(kt * tk, tk), tk, None)
        return carry

    lax.fori_loop(0, qi * (tq // tk), body, 0)
    dd = DIFF_DIAG
    q0 = qi * tq
    for c in range(tq // dd):
        n_rows = tq - c * dd
        ri = lax.broadcasted_iota(jnp.int32, (n_rows, dd), 0)
        ci = lax.broadcasted_iota(jnp.int32, (n_rows, dd), 1)
        step(slice(c * dd, tq), pl.multiple_of(q0 + c * dd, dd), dd, jnp.where(ri >= ci, 0.0, MASKED))
    return _flash_finish(l1, a1), _flash_finish(l2, a2)


def _lane_tile_sum(p):
    cols = p[:, 0:LANES]
    for c in range(1, p.shape[1] // LANES):
        cols = cols + p[:, c * LANES:(c + 1) * LANES]
    return cols


def _diff_bounded_stream(q, n, bound, k_ref, v_ref, tri_ref, kcols):
    tq, dd = DIFF_TQ, DIFF_DIAG
    kmain = n * tq + dd
    p = jnp.exp2(_dot_nt(q, k_ref[0, 0:kmain, kcols]) - bound)
    p_diag = p[:, kmain - dd:] * tri_ref[...]
    p = p_diag if kmain == dd else jnp.concatenate([p[:, :kmain - dd], p_diag], axis=1)
    l = _lane_tile_sum(p)
    acc = _dot(p.astype(BF16), v_ref[0, 0:kmain, :])
    pc = jnp.exp2(_dot_nt(q[dd:], k_ref[0, kmain:kmain + dd, kcols]) - bound) * tri_ref[0:tq - dd, :]
    l = jnp.concatenate([l[:dd], l[dd:] + _lane_tile_sum(pc)], axis=0)
    acc = jnp.concatenate([acc[:dd], acc[dd:] + _dot(pc.astype(BF16), v_ref[0, kmain:kmain + dd, :])], axis=0)
    return acc * (1.0 / jnp.sum(l, axis=-1, keepdims=True))


def _diff_kernel(bound_ref, q_ref, gate_ref, k_ref, v_ref, lam_ref, tri_ref, o_ref, m1, l1, a1, m2, l2, a2,
                 *, lambda_init, n_tiles):
    qi = pl.program_id(2)
    q = q_ref[0]
    q1 = q[:, :HEAD_DIM]
    q2 = q[:, HEAD_DIM:]
    bound = bound_ref[0]

    def finish(o1, o2):
        lp = lam_ref[...]
        lam = (jnp.exp(jnp.sum(lp[0:1] * lp[1:2], axis=-1, keepdims=True))
               - jnp.exp(jnp.sum(lp[2:3] * lp[3:4], axis=-1, keepdims=True)) + lambda_init)
        o = _rms(o1 - lam * o2) * (1.0 - lambda_init)
        o_ref[0] = (o * gate_ref[0].astype(F32)).astype(o_ref.dtype)

    def bounded_variant(n):
        @pl.when((bound <= MAX_SCORE_BOUND) & (qi == n))
        def _():
            finish(_diff_bounded_stream(q1, n, bound, k_ref, v_ref, tri_ref, slice(0, HEAD_DIM)),
                   _diff_bounded_stream(q2, n, bound, k_ref, v_ref, tri_ref, slice(HEAD_DIM, 2 * HEAD_DIM)))

    for n in range(n_tiles):
        bounded_variant(n)

    @pl.when(bound > MAX_SCORE_BOUND)
    def _():
        finish(*_diff_general(q1, q2, qi, k_ref, v_ref, m1, l1, a1, m2, l2, a2))


def _diff_attention(proj, lam_params, bound, lambda_init):
    b, s, _ = proj.shape
    h = DIFF_HEADS
    tq, dd = DIFF_TQ, DIFF_DIAG
    assert tq == 2 * dd
    w = DIFF_V_DIM
    tri = jnp.asarray(np.arange(tq)[:, None] >= np.arange(dd)[None, :], F32)
    q_spec = lambda off: pl.BlockSpec((1, tq, w), lambda bi, hi, qi: (bi, qi, off + hi))
    kv_spec = lambda off: pl.BlockSpec((1, s, w), lambda bi, hi, qi: (bi, 0, off + hi))
    stat = pltpu.VMEM((tq, LANES), F32)
    acc = pltpu.VMEM((tq, w), F32)
    return pl.pallas_call(
        functools.partial(_diff_kernel, lambda_init=lambda_init, n_tiles=s // tq),
        grid=(b, h, s // tq),
        in_specs=[pl.BlockSpec(memory_space=pltpu.SMEM),
                  q_spec(0), q_spec(3 * h), kv_spec(h), kv_spec(2 * h),
                  pl.BlockSpec(lam_params.shape, lambda bi, hi, qi: (0, 0)),
                  pl.BlockSpec(tri.shape, lambda bi, hi, qi: (0, 0))],
        out_specs=pl.BlockSpec((1, tq, w), lambda bi, hi, qi: (bi, qi, hi)),
        out_shape=jax.ShapeDtypeStruct((b, s, DIFF_WIDTH), BF16),
        scratch_shapes=[stat, stat, acc, stat, stat, acc],
        compiler_params=pltpu.CompilerParams(
            dimension_semantics=("parallel", "parallel", "arbitrary"), vmem_limit_bytes=VMEM_LIMIT),
        name="diff_attention",
    )(bound, proj, proj, proj, proj, lam_params, tri)


def _rope_tables(pos):
    inv = 1.0 / (ROPE_THETA ** (np.arange(0, HEAD_DIM, 2, dtype=np.float64) / HEAD_DIM))
    ang = np.asarray(pos, np.float64)[:, None] * inv[None, :]
    cos, sin = np.cos(ang), np.sin(ang)
    return (jnp.asarray(np.concatenate([cos, cos], axis=-1), F32),
            jnp.asarray(np.concatenate([-sin, sin], axis=-1), F32))


def _retention_tables():
    h, c = RET_HEADS, RET_CHUNK
    log_g = np.log1p(-np.exp2(-5.0 - np.arange(h, dtype=np.float64)))
    j = np.arange(c, dtype=np.float64)
    diff = j[:, None] - j[None, :]
    intra = np.where(diff >= 0, np.exp(log_g[:, None, None] * np.maximum(diff, 0.0)), 0.0)
    q_dec = np.exp(log_g[:, None] * (j + 1.0))
    k_dec = np.exp(log_g[:, None] * (c - 1.0 - j))
    chunk_dec = np.exp(log_g * c)
    wide = lambda t: jnp.asarray(np.broadcast_to(t[:, :, None], (h, t.shape[1], HEAD_DIM)), F32)
    return jnp.asarray(intra, F32), wide(q_dec), wide(k_dec), wide(chunk_dec[:, None])


def _selection_tables(seq):
    tq, tk = NSA_TQ, NSA_TK
    n_cmp_rows = seq // CMP_STRIDE
    n_slc = seq // SLC_BLOCK
    assert tq == tk and n_slc <= AUG_ONE_LANE < LANES
    c_start = np.arange(n_cmp_rows) * CMP_STRIDE
    s_start = np.arange(n_slc) * SLC_BLOCK
    overlap_t = ((c_start[None, :] <= s_start[:, None] + SLC_BLOCK - 1)
                 & (c_start[None, :] + CMP_BLOCK - 1 >= s_start[:, None]))
    lane = np.arange(LANES)[None, :]
    key = np.arange(seq)[:, None]
    eaug = ((key // SLC_BLOCK) == lane) | (lane == AUG_ONE_LANE)
    cmask = lane * CMP_STRIDE + CMP_BLOCK - 1 <= key
    r = np.arange(tq)[:, None]
    c = np.arange(WINDOW + tq)[None, :]
    wmask = []
    for pat in range(WINDOW // tq + 1):
        d = min(pat * tq, WINDOW) + r - c
        wmask.append((d >= 0) & (d < WINDOW))
    tri = np.arange(tq)[:, None] >= np.arange(tk)[None, :]
    return (jnp.asarray(overlap_t, BF16), jnp.asarray(eaug, BF16), jnp.asarray(cmask, F32),
            jnp.asarray(np.stack(wmask), F32), jnp.asarray(tri, F32))


def _tile_modes(segments, tn):
    off, kinds, col = {}, [], 0
    for name, width, kind in segments:
        off[name] = col
        kinds += [kind] * (width // LANES)
        col += width
    per = tn // LANES
    tiles = [kinds[i:i + per] for i in range(0, len(kinds), per)]
    assert col % tn == 0
    modes = []
    for j, tile in enumerate(tiles):
        if modes and modes[-1][2] == tile:
            modes[-1] = (modes[-1][0], j + 1, tile)
        else:
            modes.append((j, j + 1, tile))
    return modes, off


def _l0_tile_modes():
    plain, silu = ("plain", 0), ("silu", 0)
    k_norm = ("nrope", L0_TAB_K)
    return _tile_modes([
        ("rq", RET_WIDTH, ("rope", L0_TAB_ROPE)), ("rk", RET_WIDTH, ("rope", L0_TAB_ROPE_SCALED)),
        ("rv", RET_WIDTH, plain), ("rgate", RET_WIDTH, silu), ("nq", NSA_WIDTH, ("nrope", L0_TAB_Q)),
        ("kc", NSA_KV_WIDTH, plain), ("vc", NSA_KV_WIDTH, plain), ("ks", NSA_KV_WIDTH, k_norm),
        ("vs", NSA_KV_WIDTH, plain), ("kw", NSA_KV_WIDTH, k_norm), ("vw", NSA_KV_WIDTH, plain),
        ("ngate", NSA_WIDTH, silu)], PROJ_TN)


def _l1_tile_modes():
    return _tile_modes([("q", DIFF_WIDTH, ("nrope", L1_TAB_Q)), ("k", DIFF_WIDTH, ("nrope", L1_TAB_K)),
                        ("v", DIFF_WIDTH, ("plain", 0)), ("gate", DIFF_WIDTH, ("silu", 0))], L1_PROJ_TN)[0]


def _rope_pair(cos, sin_signed, gain=None, scale=1.0):
    if gain is None:
        return [cos * scale, sin_signed * scale]
    return [cos * (gain * scale)[None, :], sin_signed * (jnp.roll(gain, HALF) * scale)[None, :]]


def kernel(x, l0_norm_g, l0_w_in, l0_w_out, l0_nsa_q_norm_g, l0_nsa_k_norm_g, l0_cmp_pe_k, l0_cmp_w1_k, l0_cmp_w2_k, l0_cmp_pe_v, l0_cmp_w1_v, l0_cmp_w2_v, l1_norm_g, l1_w_in, l1_w_out, l1_q_norm_g, l1_k_norm_g, l1_lambda_q1, l1_lambda_k1, l1_lambda_q2, l1_lambda_k2):
    b, s, d = x.shape
    m = b * s
    x2d = x.reshape(m, d)
    cos, sin = _rope_tables(np.arange(s))
    cos_c, sin_c = _rope_tables(np.arange(s // CMP_STRIDE) * CMP_STRIDE + CMP_BLOCK - 1)
    intra, qdec, kdec, cdec = _retention_tables()
    nsa_tables = _selection_tables(s)

    modes0, off = _l0_tile_modes()
    w0 = l0_w_in.T.astype(BF16)
    wg = w0[AB_MAIN_COLS:].T.reshape(d, 3, NSA_KV_HEADS, NSA_GROUP).transpose(0, 2, 1, 3)
    wg = wg.reshape(d, NSA_KV_HEADS, 3 * NSA_GROUP)
    wg = jnp.pad(wg, ((0, 0), (0, 0), (0, LANES - 3 * NSA_GROUP))).reshape(d, NSA_KV_HEADS * LANES)
    tabs0 = jnp.stack(_rope_pair(cos, sin) + _rope_pair(cos, sin, scale=QK_SCALE)
                      + _rope_pair(cos, sin, l0_nsa_q_norm_g, Q_SCALE) + _rope_pair(cos, sin, l0_nsa_k_norm_g))
    proj0, gates = _norm_proj(x2d, l0_norm_g, w0, tabs0, modes0, s, PROJ_TN, wg=wg, w_transposed=True)
    proj0 = proj0.reshape(b, s, AB_MAIN_COLS)
    y_ret = _retention(proj0, intra, qdec, kdec, cdec)
    w1k = l0_cmp_w1_k.astype(BF16).reshape(CMP_BLOCK, HEAD_DIM, HEAD_DIM)
    w1v = l0_cmp_w1_v.astype(BF16).reshape(CMP_BLOCK, HEAD_DIM, HEAD_DIM)
    kcmp, vcmp = _nsa_compress(proj0, off["kc"] // HEAD_DIM, off["vc"] // HEAD_DIM,
                               l0_cmp_pe_k, w1k, l0_cmp_w2_k.astype(BF16),
                               l0_cmp_pe_v, w1v, l0_cmp_w2_v.astype(BF16),
                               l0_nsa_k_norm_g, cos_c, sin_c)
    bound = (BOUND_MARGIN * HEAD_DIM * Q_SCALE * jnp.max(jnp.abs(l0_nsa_q_norm_g))
             * jnp.max(jnp.abs(l0_nsa_k_norm_g))).reshape(1).astype(F32)
    y_nsa, w_out0, w_in1, w_out1 = _nsa_attention(proj0, gates, kcmp, vcmp, bound, nsa_tables, off,
                                                  cast_weights=[l0_w_out, l1_w_in, l1_w_out])
    x1 = _out_proj([y_ret.reshape(m, RET_WIDTH), y_nsa.reshape(m, NSA_WIDTH)],
                   [w_out0[:RET_WIDTH], w_out0[RET_WIDTH:]], x2d)

    lambda_init = 0.8 - 0.6 * math.exp(-0.3 * 1)
    tabs1 = jnp.stack(_rope_pair(cos, sin, l1_q_norm_g, Q_SCALE) + _rope_pair(cos, sin, l1_k_norm_g))
    proj1 = _norm_proj(x1, l1_norm_g, w_in1, tabs1, _l1_tile_modes(), s, L1_PROJ_TN)[0]
    lam_params = jnp.stack([l1_lambda_q1, l1_lambda_k1, l1_lambda_q2, l1_lambda_k2]).astype(F32)
    bound1 = (BOUND_MARGIN * HEAD_DIM * Q_SCALE * jnp.max(jnp.abs(l1_q_norm_g))
              * jnp.max(jnp.abs(l1_k_norm_g))).reshape(1).astype(F32)
    y1 = _diff_attention(proj1.reshape(b, s, C_IN_COLS), lam_params, bound1, lambda_init)
    out = _out_proj([y1.reshape(m, DIFF_WIDTH)], [w_out1], x1)
    return out.reshape(b, s, d)
```

```python
import functools
import math

import numpy as np
import jax
import jax.numpy as jnp
from jax import lax
from jax.experimental import pallas as pl
from jax.experimental.pallas import tpu as pltpu

F32 = jnp.float32
BF16 = jnp.bfloat16

HEAD_DIM = 128
HALF = HEAD_DIM // 2
ROPE_THETA = 10000.0
EPS = 1e-6
RET_HEADS = 8
RET_CHUNK = 256
RET_HEADS_PER_STEP = 4
NSA_HEADS = 8
NSA_KV_HEADS = 2
NSA_GROUP = NSA_HEADS // NSA_KV_HEADS
CMP_BLOCK = 32
CMP_STRIDE = 16
SLC_BLOCK = 64
SLC_TOPK = 16
N_LOCAL_BLOCKS = 2
WINDOW = 512
DIFF_HEADS = 8
DIFF_V_DIM = 2 * HEAD_DIM
QK_SCALE = HEAD_DIM ** -0.5
LOG2E = math.log2(math.e)
Q_SCALE = QK_SCALE * LOG2E

RET_WIDTH = RET_HEADS * HEAD_DIM
NSA_WIDTH = NSA_HEADS * HEAD_DIM
NSA_KV_WIDTH = NSA_KV_HEADS * HEAD_DIM
AB_MAIN_COLS = 4 * RET_WIDTH + 2 * NSA_WIDTH + 6 * NSA_KV_WIDTH
DIFF_WIDTH = DIFF_HEADS * DIFF_V_DIM
C_IN_COLS = 4 * DIFF_WIDTH

LANES = 128
MASKED = -1e30
M_INIT = -1e29
MAX_SCORE_BOUND = 60.0
BOUND_MARGIN = 1.05
AUG_ONE_LANE = 32
VMEM_LIMIT = 56 * 1024 * 1024

PROJ_TM = 1024
PROJ_TN = 1280
L1_PROJ_TN = 1024
OPROJ_TM = 512
OPROJ_TN = 2048
PROJ_SUB = 256
L0_TAB_ROPE, L0_TAB_ROPE_SCALED, L0_TAB_Q, L0_TAB_K = range(4)
L1_TAB_Q, L1_TAB_K = range(2)
NSA_TQ = 512
NSA_TK = 512
DIFF_TQ = 512
DIFF_TK = 512
DIFF_DIAG = 256


def _dot(a, b):
    return jnp.dot(a, b, preferred_element_type=F32)


def _dot_nt(a, b):
    return lax.dot_general(a, b, (((1,), (1,)), ((), ())), preferred_element_type=F32)


def _dot_tn(a, b):
    return lax.dot_general(a, b, (((0,), (0,)), ((), ())), preferred_element_type=F32)


def _silu(x):
    return x / (1.0 + jnp.exp(-x))


def _rms(x):
    return x * lax.rsqrt(jnp.mean(x * x, axis=-1, keepdims=True) + EPS)


def _rope(x, cos, sin_signed):
    return x * cos + pltpu.roll(x, HALF, axis=1) * sin_signed


def _rope_tab(seg, t, tabs_ref, rows):
    return seg * tabs_ref[2 * t, rows, :] + pltpu.roll(seg, HALF, axis=1) * tabs_ref[2 * t + 1, rows, :]


def _tile_epilogue(acc, kinds, tabs_ref, rows, o_ref):
    n_seg = len(kinds)
    ri = lax.broadcasted_iota(jnp.int32, (2 * LANES, 2 * LANES), 0)
    ci = lax.broadcasted_iota(jnp.int32, (2 * LANES, 2 * LANES), 1)
    pair_mean = jnp.where((ri < LANES) == (ci < LANES), 1.0 / HEAD_DIM, 0.0).astype(BF16)
    c = 0
    while c < n_seg:
        op, t = kinds[c]
        cols = slice(c * LANES, (c + 1) * LANES)
        if op == "nrope" and c + 1 < n_seg and kinds[c + 1][0] == "nrope":
            cols2 = slice(c * LANES, (c + 2) * LANES)
            seg2 = acc[:, cols2]
            seg2 = seg2 * lax.rsqrt(_dot((seg2 * seg2).astype(BF16), pair_mean) + EPS)
            for half in range(2):
                out = _rope_tab(seg2[:, half * LANES:(half + 1) * LANES], kinds[c + half][1], tabs_ref, rows)
                o_ref[rows, (c + half) * LANES:(c + half + 1) * LANES] = out.astype(o_ref.dtype)
            c += 2
            continue
        seg = acc[:, cols]
        if op == "silu":
            seg = _silu(seg)
        elif op == "nrope":
            seg = seg * lax.rsqrt(_dot((seg * seg).astype(BF16), pair_mean[:LANES, :LANES]) + EPS)
            seg = _rope_tab(seg, t, tabs_ref, rows)
        elif op == "rope":
            seg = _rope_tab(seg, t, tabs_ref, rows)
        o_ref[rows, cols] = seg.astype(o_ref.dtype)
        c += 1


def _proj_kernel(*refs, tile_modes, has_gates):
    if has_gates:
        x_ref, g_ref, w_ref, tabs_ref, wg_ref, o_ref, og_ref, h_scr = refs
    else:
        x_ref, g_ref, w_ref, tabs_ref, o_ref, h_scr = refs
    j = pl.program_id(1)
    n_sub = h_scr.shape[0] // PROJ_SUB

    def make_branch(cond, kinds, first):
        @pl.when(cond)
        def _():
            for r in range(n_sub):
                rows = slice(r * PROJ_SUB, (r + 1) * PROJ_SUB)
                if first:
                    h = (_rms(x_ref[rows, :]) * g_ref[...]).astype(BF16)
                    h_scr[rows, :] = h
                    if has_gates:
                        og_ref[rows, :] = 1.0 / (1.0 + jnp.exp(-_dot(h, wg_ref[...])))
                else:
                    h = h_scr[rows, :]
                _tile_epilogue(_dot(h, w_ref[...]), kinds, tabs_ref, rows, o_ref)

    for lo, hi, kinds in tile_modes:
        if lo == 0:
            make_branch(j == 0, kinds, True)
            lo = 1
        if hi > lo:
            make_branch((j >= lo) & (j < hi), kinds, False)


def _norm_proj(x2d, g, w, tabs, tile_modes, seq, tn, wg=None):
    m, d = x2d.shape
    tm = PROJ_TM
    n = tile_modes[-1][1] * tn
    s_tiles = seq // tm
    has_gates = wg is not None
    in_specs = [
        pl.BlockSpec((tm, d), lambda i, j: (i, 0)),
        pl.BlockSpec((1, d), lambda i, j: (0, 0)),
        pl.BlockSpec((d, tn), lambda i, j: (0, j)),
        pl.BlockSpec((tabs.shape[0], tm, LANES), lambda i, j: (0, i % s_tiles, 0)),
    ]
    args = [x2d, g.reshape(1, d), w, tabs]
    out_shape = [jax.ShapeDtypeStruct((m, n), BF16)]
    out_specs = [pl.BlockSpec((tm, tn), lambda i, j: (i, j))]
    if has_gates:
        ng = wg.shape[1]
        in_specs.append(pl.BlockSpec((d, ng), lambda i, j: (0, 0)))
        args.append(wg)
        out_shape.append(jax.ShapeDtypeStruct((m, ng), F32))
        out_specs.append(pl.BlockSpec((tm, ng), lambda i, j: (i, 0)))
    return pl.pallas_call(
        functools.partial(_proj_kernel, tile_modes=tile_modes, has_gates=has_gates),
        grid=(m // tm, n // tn),
        in_specs=in_specs,
        out_specs=out_specs,
        out_shape=out_shape,
        scratch_shapes=[pltpu.VMEM((tm, d), BF16)],
        compiler_params=pltpu.CompilerParams(
            dimension_semantics=("parallel", "arbitrary"), vmem_limit_bytes=VMEM_LIMIT),
        name="norm_proj",
    )(*args)


def _oproj_kernel(*refs, n_terms):
    y_refs, w_refs = refs[:n_terms], refs[n_terms:2 * n_terms]
    x_ref, o_ref = refs[2 * n_terms:]
    acc = x_ref[...]
    for y_ref, w_ref in zip(y_refs, w_refs):
        acc = acc + _dot(y_ref[...], w_ref[...])
    o_ref[...] = acc


def _out_proj(ys, ws, x2d):
    m, n = x2d.shape
    tm, tn = OPROJ_TM, OPROJ_TN
    y_specs = [pl.BlockSpec((tm, y.shape[1]), lambda i, j: (i, 0)) for y in ys]
    w_specs = [pl.BlockSpec((w.shape[0], tn), lambda i, j: (0, j)) for w in ws]
    return pl.pallas_call(
        functools.partial(_oproj_kernel, n_terms=len(ys)),
        grid=(m // tm, n // tn),
        in_specs=y_specs + w_specs + [pl.BlockSpec((tm, tn), lambda i, j: (i, j))],
        out_specs=pl.BlockSpec((tm, tn), lambda i, j: (i, j)),
        out_shape=jax.ShapeDtypeStruct((m, n), F32),
        compiler_params=pltpu.CompilerParams(
            dimension_semantics=("parallel", "arbitrary"), vmem_limit_bytes=VMEM_LIMIT),
        name="out_proj",
    )(*ys, *ws, x2d)


def _ret_kernel(q_ref, k_ref, v_ref, gate_ref, intra_ref, qdec_ref, kdec_ref, cdec_ref, o_ref, *, n_chunks):
    c = RET_CHUNK
    states = [jnp.zeros((HEAD_DIM, HEAD_DIM), F32)] * RET_HEADS_PER_STEP
    for n in range(n_chunks):
        rows = slice(n * c, (n + 1) * c)
        for hh in range(RET_HEADS_PER_STEP):
            cols = slice(hh * HEAD_DIM, (hh + 1) * HEAD_DIM)
            q = q_ref[0, rows, cols]
            k = k_ref[0, rows, cols]
            v = v_ref[0, rows, cols]
            scores = _dot_nt(q, k) * intra_ref[hh]
            inner = _dot(scores.astype(BF16), v)
            cross = _dot((q.astype(F32) * qdec_ref[hh]).astype(BF16), states[hh].astype(BF16))
            kv = _dot_tn((k.astype(F32) * kdec_ref[hh]).astype(BF16), v)
            states[hh] = states[hh] * cdec_ref[hh] + kv
            o = _rms(inner + cross) * gate_ref[0, rows, cols].astype(F32)
            o_ref[0, rows, cols] = o.astype(o_ref.dtype)


def _retention(proj, intra, qdec, kdec, cdec):
    b, s, _ = proj.shape
    per = RET_HEADS_PER_STEP
    h = RET_HEADS // per
    y_shape = (b, s, RET_WIDTH)
    head = lambda off: pl.BlockSpec((1, s, per * HEAD_DIM), lambda bi, hi: (bi, 0, off + hi))
    table = lambda rows, cols=HEAD_DIM: pl.BlockSpec((per, rows, cols), lambda bi, hi: (hi, 0, 0))
    return pl.pallas_call(
        functools.partial(_ret_kernel, n_chunks=s // RET_CHUNK),
        grid=(b, h),
        in_specs=[head(0), head(h), head(2 * h), head(3 * h),
                  table(RET_CHUNK, RET_CHUNK), table(RET_CHUNK), table(RET_CHUNK), table(1)],
        out_specs=pl.BlockSpec((1, s, per * HEAD_DIM), lambda bi, hi: (bi, 0, hi)),
        out_shape=jax.ShapeDtypeStruct(y_shape, BF16),
        compiler_params=pltpu.CompilerParams(
            dimension_semantics=("parallel", "arbitrary"), vmem_limit_bytes=VMEM_LIMIT),
        name="retention",
    )(proj, proj, proj, proj, intra, qdec, kdec, cdec)


def _cmp_kernel(kc_ref, vc_ref, pek_ref, w1k_ref, w2k_ref, pev_ref, w1v_ref, w2v_ref, kg_ref,
                cos_ref, sin_ref, ko_ref, vo_ref, t_scr, *, seq):
    n_rows = seq // CMP_STRIDE

    def compress(src_ref, pe_ref, w1_ref, w2_ref):
        t_scr[0:seq, :] = src_ref[0].astype(F32)
        t_scr[seq:seq + CMP_STRIDE, :] = jnp.zeros((CMP_STRIDE, HEAD_DIM), F32)
        acc = jnp.zeros((n_rows, HEAD_DIM), F32)
        for r in range(CMP_BLOCK):
            rows = t_scr[pl.ds(r, n_rows, stride=CMP_STRIDE), :] + pe_ref[r:r + 1, :]
            acc = acc + _dot(rows.astype(BF16), w1_ref[r])
        return _dot(_silu(acc).astype(BF16), w2_ref[...])

    kc = compress(kc_ref, pek_ref, w1k_ref, w2k_ref)
    kc = _rope(_rms(kc) * kg_ref[...], cos_ref[...], sin_ref[...])
    ko_ref[0, 0] = kc.astype(ko_ref.dtype)
    vo_ref[0, 0] = compress(vc_ref, pev_ref, w1v_ref, w2v_ref).astype(vo_ref.dtype)


def _nsa_compress(proj, kc_off, vc_off, pe_k, w1_k, w2_k, pe_v, w1_v, w2_v, k_g, cos_c, sin_c):
    b, s, _ = proj.shape
    g = NSA_KV_HEADS
    n_rows = s // CMP_STRIDE
    head = lambda off: pl.BlockSpec((1, s, HEAD_DIM), lambda bi, gi: (bi, 0, off + gi))
    full = lambda shape: pl.BlockSpec(shape, lambda bi, gi: (0,) * len(shape))
    out_spec = pl.BlockSpec((1, 1, n_rows, HEAD_DIM), lambda bi, gi: (bi, gi, 0, 0))
    out_sds = jax.ShapeDtypeStruct((b, g, n_rows, HEAD_DIM), BF16)
    return pl.pallas_call(
        functools.partial(_cmp_kernel, seq=s),
        grid=(b, g),
        in_specs=[head(kc_off), head(vc_off),
                  full((CMP_BLOCK, HEAD_DIM)), full((CMP_BLOCK, HEAD_DIM, HEAD_DIM)), full((HEAD_DIM, HEAD_DIM)),
                  full((CMP_BLOCK, HEAD_DIM)), full((CMP_BLOCK, HEAD_DIM, HEAD_DIM)), full((HEAD_DIM, HEAD_DIM)),
                  full((1, HEAD_DIM)), full((n_rows, HEAD_DIM)), full((n_rows, HEAD_DIM))],
        out_specs=[out_spec, out_spec],
        out_shape=[out_sds, out_sds],
        scratch_shapes=[pltpu.VMEM((s + CMP_STRIDE, HEAD_DIM), F32)],
        compiler_params=pltpu.CompilerParams(
            dimension_semantics=("parallel", "arbitrary"), vmem_limit_bytes=VMEM_LIMIT),
        name="nsa_compress",
    )(proj, proj, pe_k, w1_k, w2_k, pe_v, w1_v, w2_v, k_g.reshape(1, HEAD_DIM), cos_c, sin_c)


def _flash_init(m_scr, l_scr, acc_scr):
    m_scr[...] = jnp.full(m_scr.shape, M_INIT, F32)
    l_scr[...] = jnp.zeros(l_scr.shape, F32)
    acc_scr[...] = jnp.zeros(acc_scr.shape, F32)


def _lane_tiles(x, width):
    return x if width == LANES else jnp.concatenate([x] * (width // LANES), axis=1)


def _flash_step(q, k, v, bias, m_scr, l_scr, acc_scr):
    s = _dot_nt(q, k)
    if bias is not None:
        s = s + bias
    tk = s.shape[1]
    m_old = m_scr[...]
    m_new = jnp.maximum(m_old, jnp.max(s, axis=-1, keepdims=True))
    alpha = jnp.exp2(m_old - m_new)
    p = jnp.exp2(s - _lane_tiles(m_new, tk))
    p_cols = p[:, 0:LANES]
    for c in range(1, tk // LANES):
        p_cols = p_cols + p[:, c * LANES:(c + 1) * LANES]
    l_scr[...] = alpha * l_scr[...] + p_cols
    acc_scr[...] = _lane_tiles(alpha, acc_scr.shape[1]) * acc_scr[...] + _dot(p.astype(BF16), v)
    m_scr[...] = m_new


def _flash_finish(l_scr, acc_scr):
    return acc_scr[...] * (1.0 / jnp.sum(l_scr[...], axis=-1, keepdims=True))


def _rep_heads(bias):
    return jnp.concatenate([bias] * NSA_GROUP, axis=0)


def _stack_heads(qblk):
    return jnp.concatenate([qblk[:, r * HEAD_DIM:(r + 1) * HEAD_DIM] for r in range(NSA_GROUP)], axis=0)


def _select_blocks(p, ovl, q0, tq, n_slc):
    psum = p[0:tq]
    for r in range(1, NSA_GROUP):
        psum = psum + p[r * tq:(r + 1) * tq]
    p_hi = psum.astype(BF16)
    rem = psum - p_hi.astype(F32)
    p_mid = rem.astype(BF16)
    p_lo = (rem - p_mid.astype(F32)).astype(BF16)
    imp = _dot_nt(ovl, p_hi) + _dot_nt(ovl, p_mid) + _dot_nt(ovl, p_lo)
    jb = lax.broadcasted_iota(jnp.int32, (n_slc, tq), 0)
    blk_t = jnp.right_shift(q0 + lax.broadcasted_iota(jnp.int32, (n_slc, tq), 1), int(math.log2(SLC_BLOCK)))
    back = blk_t - jb
    forced = (jb == 0) | ((back >= 0) & (back < N_LOCAL_BLOCKS))
    score = jnp.where(forced, 1e9, jnp.where(back >= 0, imp, -1e9))
    rank = jnp.zeros((n_slc, tq), F32)
    for mp in range(n_slc):
        row = score[mp:mp + 1, :]
        ahead = (row > score) | ((row == score) & (jb > mp))
        rank = rank + jnp.where(ahead, 1.0, 0.0)
    sel_t = jnp.where(rank < float(min(SLC_TOPK, n_slc)), 1.0, 0.0)
    sel_t = jnp.concatenate([sel_t, jnp.zeros((LANES - n_slc, tq), F32)], axis=0).astype(BF16)
    ri = lax.broadcasted_iota(jnp.int32, (tq, tq), 0)
    ci = lax.broadcasted_iota(jnp.int32, (tq, tq), 1)
    eye = jnp.where(ri == ci, 1.0, 0.0).astype(BF16)
    return _dot_nt(eye, sel_t)


def _nsa_combine(o_cmp, o_slc, o_win, gates_ref, ngate_ref, o_ref, tq):
    gates = gates_ref[...]
    for r in range(NSA_GROUP):
        rows = slice(r * tq, (r + 1) * tq)
        g_cmp = gates[:, r:r + 1]
        g_slc = gates[:, NSA_GROUP + r:NSA_GROUP + r + 1]
        g_win = gates[:, 2 * NSA_GROUP + r:2 * NSA_GROUP + r + 1]
        y = g_cmp * o_cmp[rows] + g_slc * o_slc[rows] + g_win * o_win[rows]
        cols = slice(r * HEAD_DIM, (r + 1) * HEAD_DIM)
        o_ref[0, :, cols] = (y * ngate_ref[0, :, cols].astype(F32)).astype(o_ref.dtype)


def _nsa_general(q4, q0, qi, gates_ref, ngate_ref, kcmp_ref, vcmp_ref, ks_ref, vs_ref, kw_ref, vw_ref,
                 ovl_ref, eaug_ref, o_ref, bias_scr, m_scr, l_scr, acc_scr, *, seq):
    tq, tk = NSA_TQ, NSA_TK
    n_slc = seq // SLC_BLOCK

    tpos = q0 + lax.broadcasted_iota(jnp.int32, (tq, LANES), 0)
    cidx = lax.broadcasted_iota(jnp.int32, (tq, LANES), 1)
    cbias = jnp.where(cidx * CMP_STRIDE + (CMP_BLOCK - 1) <= tpos, 0.0, MASKED)
    s = _dot_nt(q4, kcmp_ref[0, 0]) + _rep_heads(cbias)
    m = jnp.maximum(jnp.max(s, axis=-1, keepdims=True), M_INIT)
    e = jnp.exp2(s - m)
    l = jnp.sum(e, axis=-1, keepdims=True)
    p = e * (1.0 / jnp.maximum(l, 1e-30))
    o_cmp = _dot(p.astype(BF16), vcmp_ref[0, 0])

    sel = _select_blocks(p, ovl_ref[...], q0, tq, n_slc).astype(BF16)
    sel_keys = _dot_nt(sel, eaug_ref[...])
    for kt in range(seq // tk):
        bias_scr[kt] = jnp.where(sel_keys[:, kt * tk:(kt + 1) * tk] > 0.5, 0.0, MASKED)

    _flash_init(m_scr, l_scr, acc_scr)

    def slc_body(kt, carry):
        k0 = pl.multiple_of(kt * tk, tk)
        _flash_step(q4, ks_ref[0, pl.ds(k0, tk), :], vs_ref[0, pl.ds(k0, tk), :],
                    _rep_heads(bias_scr[kt]), m_scr, l_scr, acc_scr)
        return carry

    n_full = lax.div(q0, tk)
    lax.fori_loop(0, n_full, slc_body, 0)
    kd = pl.multiple_of(n_full * tk, tk)
    qpos = q0 + lax.broadcasted_iota(jnp.int32, (tq, tk), 0)
    kpos = kd + lax.broadcasted_iota(jnp.int32, (tq, tk), 1)
    causal = jnp.where(kpos <= qpos, 0.0, MASKED)
    _flash_step(q4, ks_ref[0, pl.ds(kd, tk), :], vs_ref[0, pl.ds(kd, tk), :],
                _rep_heads(bias_scr[n_full] + causal), m_scr, l_scr, acc_scr)
    o_slc = _flash_finish(l_scr, acc_scr)

    wk = WINDOW + tq
    ws = pl.multiple_of(jnp.maximum(q0 - WINDOW, 0), tq)
    d = (q0 - ws) + lax.broadcasted_iota(jnp.int32, (tq, wk), 0) - lax.broadcasted_iota(jnp.int32, (tq, wk), 1)
    wbias = jnp.where((d >= 0) & (d < WINDOW), 0.0, MASKED)
    s = _dot_nt(q4, kw_ref[0, pl.ds(ws, wk), :]) + _rep_heads(wbias)
    e = jnp.exp2(s - jnp.max(s, axis=-1, keepdims=True))
    o_win = _dot(e.astype(BF16), vw_ref[0, pl.ds(ws, wk), :]) * (1.0 / jnp.sum(e, axis=-1, keepdims=True))

    _nsa_combine(o_cmp, o_slc, o_win, gates_ref, ngate_ref, o_ref, tq)


def _nsa_bounded(q4, q0, qi, bound, gates_ref, ngate_ref, kcmp_ref, vcmp_ref, ks_ref, vs_ref, kw_ref, vw_ref,
                 ovl_ref, eaug_ref, cmask_ref, wmask_ref, tri_ref, o_ref,
                 ksa_scr, kwa_scr, kca_scr, l_scr, acc_scr, *, seq):
    tq, tk = NSA_TQ, NSA_TK
    n_slc = seq // SLC_BLOCK
    n_cmp_rows = kca_scr.shape[0]

    @pl.when(qi == 0)
    def _():
        one_col = eaug_ref[...]
        lane = lax.broadcasted_iota(jnp.int32, one_col.shape, 1)
        ksa_scr[:, :HEAD_DIM] = ks_ref[0]
        ksa_scr[:, HEAD_DIM:] = one_col
        one_col = jnp.where(lane == AUG_ONE_LANE, one_col, jnp.zeros_like(one_col))
        kwa_scr[:, :HEAD_DIM] = kw_ref[0]
        kwa_scr[:, HEAD_DIM:] = one_col
        kca_scr[:, :HEAD_DIM] = kcmp_ref[0, 0]
        kca_scr[:, HEAD_DIM:] = one_col[:n_cmp_rows]

    lane = lax.broadcasted_iota(jnp.int32, (tq, LANES), 1)
    shift_cols = jnp.where(lane == AUG_ONE_LANE, -bound, 0.0)
    qa = jnp.concatenate([q4, _rep_heads(shift_cols.astype(BF16))], axis=1)

    e = jnp.exp2(_dot_nt(qa, kca_scr[...])) * _rep_heads(cmask_ref[...])
    l = jnp.sum(e, axis=-1, keepdims=True)
    p = e * jnp.where(l > 0.0, 1.0 / l, 0.0)
    o_cmp = _dot(p.astype(BF16), vcmp_ref[0, 0])

    sel = _select_blocks(p, ovl_ref[...], q0, tq, n_slc)
    sel_cols = jnp.where(lane < n_slc, jnp.where(sel > 0.5, 0.0, MASKED), shift_cols)
    qs = jnp.concatenate([q4, _rep_heads(sel_cols.astype(BF16))], axis=1)

    l_scr[...] = jnp.zeros(l_scr.shape, F32)
    acc_scr[...] = jnp.zeros(acc_scr.shape, F32)

    def accumulate(k0, width, mask):
        p_t = jnp.exp2(_dot_nt(qs, ksa_scr[pl.ds(k0, width), :]))
        if mask is not None:
            p_t = p_t * mask
        l_scr[...] += _lane_tile_sum(p_t)
        acc_scr[...] += _dot(p_t.astype(BF16), vs_ref[0, pl.ds(k0, width), :])

    big = 2 * tk

    def slc_body(kt, carry):
        accumulate(pl.multiple_of(kt * big, big), big, None)
        return carry

    n_full = lax.div(q0, tk)
    n_big = lax.div(n_full, 2)
    lax.fori_loop(0, n_big, slc_body, 0)

    @pl.when(n_full > 2 * n_big)
    def _():
        accumulate(pl.multiple_of(n_big * big, big), tk, None)

    accumulate(pl.multiple_of(n_full * tk, tk), tk, _rep_heads(tri_ref[...]))
    o_slc = _flash_finish(l_scr, acc_scr)

    wk = WINDOW + tq
    ws = pl.multiple_of(jnp.maximum(q0 - WINDOW, 0), tq)
    e = jnp.exp2(_dot_nt(qa, kwa_scr[pl.ds(ws, wk), :])) * _rep_heads(wmask_ref[0])
    o_win = _dot(e.astype(BF16), vw_ref[0, pl.ds(ws, wk), :]) * (1.0 / jnp.sum(e, axis=-1, keepdims=True))

    _nsa_combine(o_cmp, o_slc, o_win, gates_ref, ngate_ref, o_ref, tq)


def _nsa_kernel(bound_ref, q_ref, ngate_ref, gates_ref, kcmp_ref, vcmp_ref, ks_ref, vs_ref, kw_ref, vw_ref,
                ovl_ref, eaug_ref, cmask_ref, wmask_ref, tri_ref, *rest, seq, n_casts):
    cast_in, (o_ref, *cast_out) = rest[:n_casts], rest[n_casts:2 * n_casts + 1]
    bias_scr, m_scr, l_scr, acc_scr, ksa_scr, kwa_scr, kca_scr = rest[2 * n_casts + 1:]
    for src, dst in zip(cast_in, cast_out):
        dst[...] = src[...].astype(dst.dtype)
    qi = pl.program_id(2)
    q0 = qi * NSA_TQ
    q4 = _stack_heads(q_ref[0])
    bound = bound_ref[0]

    @pl.when(bound <= MAX_SCORE_BOUND)
    def _():
        _nsa_bounded(q4, q0, qi, bound, gates_ref, ngate_ref, kcmp_ref, vcmp_ref, ks_ref, vs_ref, kw_ref, vw_ref,
                     ovl_ref, eaug_ref, cmask_ref, wmask_ref, tri_ref, o_ref,
                     ksa_scr, kwa_scr, kca_scr, l_scr, acc_scr, seq=seq)

    @pl.when(bound > MAX_SCORE_BOUND)
    def _():
        _nsa_general(q4, q0, qi, gates_ref, ngate_ref, kcmp_ref, vcmp_ref, ks_ref, vs_ref, kw_ref, vw_ref,
                     ovl_ref, eaug_ref, o_ref, bias_scr, m_scr, l_scr, acc_scr, seq=seq)


def _nsa_attention(proj, gates, kcmp, vcmp, bound, tables, offs, cast_weights):
    b, s, _ = proj.shape
    g = NSA_KV_HEADS
    tq = NSA_TQ
    nq = s // tq
    gw = NSA_GROUP * HEAD_DIM
    n_cmp_rows = s // CMP_STRIDE
    ovl_t, eaug, cmask, wmask, tri = tables
    q_spec = lambda off: pl.BlockSpec((1, tq, gw), lambda bi, gi, qi: (bi, qi, off + gi))
    kv_spec = lambda off: pl.BlockSpec((1, s, HEAD_DIM), lambda bi, gi, qi: (bi, 0, off + gi))
    cmp_spec = pl.BlockSpec((1, 1, n_cmp_rows, HEAD_DIM), lambda bi, gi, qi: (bi, gi, 0, 0))
    full = lambda shape: pl.BlockSpec(shape, lambda bi, gi, qi: (0,) * len(shape))
    n_wpat = wmask.shape[0]
    rows4 = NSA_GROUP * tq
    n_steps = b * g * nq
    cast_specs = [pl.BlockSpec((w.shape[0] // n_steps, w.shape[1]), lambda bi, gi, qi: ((bi * g + gi) * nq + qi, 0))
                  for w in cast_weights]
    return pl.pallas_call(
        functools.partial(_nsa_kernel, seq=s, n_casts=len(cast_weights)),
        grid=(b, g, nq),
        in_specs=[pl.BlockSpec(memory_space=pltpu.SMEM),
                  q_spec(offs["nq"] // gw), q_spec(offs["ngate"] // gw),
                  pl.BlockSpec((tq, LANES), lambda bi, gi, qi: (bi * nq + qi, gi)),
                  cmp_spec, cmp_spec,
                  kv_spec(offs["ks"] // HEAD_DIM), kv_spec(offs["vs"] // HEAD_DIM),
                  kv_spec(offs["kw"] // HEAD_DIM), kv_spec(offs["vw"] // HEAD_DIM),
                  full(ovl_t.shape), full(eaug.shape),
                  pl.BlockSpec((tq, LANES), lambda bi, gi, qi: (qi, 0)),
                  pl.BlockSpec((1,) + wmask.shape[1:], lambda bi, gi, qi: (jnp.minimum(qi, n_wpat - 1), 0, 0)),
                  full(tri.shape)] + cast_specs,
        out_specs=[pl.BlockSpec((1, tq, gw), lambda bi, gi, qi: (bi, qi, gi))] + cast_specs,
        out_shape=[jax.ShapeDtypeStruct((b, s, NSA_WIDTH), BF16)]
        + [jax.ShapeDtypeStruct(w.shape, BF16) for w in cast_weights],
        scratch_shapes=[pltpu.VMEM((s // NSA_TK, tq, NSA_TK), F32),
                        pltpu.VMEM((rows4, LANES), F32), pltpu.VMEM((rows4, LANES), F32),
                        pltpu.VMEM((rows4, HEAD_DIM), F32),
                        pltpu.VMEM((s, 2 * HEAD_DIM), BF16), pltpu.VMEM((s, 2 * HEAD_DIM), BF16),
                        pltpu.VMEM((n_cmp_rows, 2 * HEAD_DIM), BF16)],
        compiler_params=pltpu.CompilerParams(
            dimension_semantics=("parallel", "parallel", "arbitrary"), vmem_limit_bytes=VMEM_LIMIT),
        name="nsa_attention",
    )(bound, proj, proj, gates, kcmp, vcmp, proj, proj, proj, proj, ovl_t, eaug, cmask, wmask, tri, *cast_weights)


def _diff_general(q1, q2, qi, k_ref, v_ref, m1, l1, a1, m2, l2, a2):
    tq, tk = DIFF_TQ, DIFF_TK
    _flash_init(m1, l1, a1)
    _flash_init(m2, l2, a2)

    def step(rows, k0, width, bias):
        k = k_ref[0, pl.ds(k0, width), :]
        v = v_ref[0, pl.ds(k0, width), :]
        _flash_step(q1[rows], k[:, :HEAD_DIM], v, bias, m1.at[rows], l1.at[rows], a1.at[rows])
        _flash_step(q2[rows], k[:, HEAD_DIM:], v, bias, m2.at[rows], l2.at[rows], a2.at[rows])

    def body(kt, carry):
        step(slice(0, tq), pl.multiple_of(kt * tk, tk), tk, None)
        return carry

    lax.fori_loop(0, qi * (tq // tk), body, 0)
    dd = DIFF_DIAG
    q0 = qi * tq
    for c in range(tq // dd):
        n_rows = tq - c * dd
        ri = lax.broadcasted_iota(jnp.int32, (n_rows, dd), 0)
        ci = lax.broadcasted_iota(jnp.int32, (n_rows, dd), 1)
        step(slice(c * dd, tq), pl.multiple_of(q0 + c * dd, dd), dd, jnp.where(ri >= ci, 0.0, MASKED))
    return _flash_finish(l1, a1), _flash_finish(l2, a2)


def _lane_tile_sum(p):
    cols = p[:, 0:LANES]
    for c in range(1, p.shape[1] // LANES):
        cols = cols + p[:, c * LANES:(c + 1) * LANES]
    return cols


def _diff_bounded_stream(q, n, bound, k_ref, v_ref, tri_ref, kcols):
    tq, dd = DIFF_TQ, DIFF_DIAG
    kmain = n * tq + dd
    p = jnp.exp2(_dot_nt(q, k_ref[0, 0:kmain, kcols]) - bound)
    p_diag = p[:, kmain - dd:] * tri_ref[...]
    p = p_diag if kmain == dd else jnp.concatenate([p[:, :kmain - dd], p_diag], axis=1)
    l = _lane_tile_sum(p)
    acc = _dot(p.astype(BF16), v_ref[0, 0:kmain, :])
    pc = jnp.exp2(_dot_nt(q[dd:], k_ref[0, kmain:kmain + dd, kcols]) - bound) * tri_ref[0:tq - dd, :]
    l = jnp.concatenate([l[:dd], l[dd:] + _lane_tile_sum(pc)], axis=0)
    acc = jnp.concatenate([acc[:dd], acc[dd:] + _dot(pc.astype(BF16), v_ref[0, kmain:kmain + dd, :])], axis=0)
    return acc * (1.0 / jnp.sum(l, axis=-1, keepdims=True))


def _diff_kernel(bound_ref, q_ref, gate_ref, k_ref, v_ref, lam_ref, tri_ref, o_ref, m1, l1, a1, m2, l2, a2,
                 *, lambda_init, n_tiles):
    qi = pl.program_id(2)
    q = q_ref[0]
    q1 = q[:, :HEAD_DIM]
    q2 = q[:, HEAD_DIM:]
    bound = bound_ref[0]

    def finish(o1, o2):
        lp = lam_ref[...]
        lam = (jnp.exp(jnp.sum(lp[0:1] * lp[1:2], axis=-1, keepdims=True))
               - jnp.exp(jnp.sum(lp[2:3] * lp[3:4], axis=-1, keepdims=True)) + lambda_init)
        o = _rms(o1 - lam * o2) * (1.0 - lambda_init)
        o_ref[0] = (o * gate_ref[0].astype(F32)).astype(o_ref.dtype)

    def bounded_variant(n):
        @pl.when((bound <= MAX_SCORE_BOUND) & (qi == n))
        def _():
            finish(_diff_bounded_stream(q1, n, bound, k_ref, v_ref, tri_ref, slice(0, HEAD_DIM)),
                   _diff_bounded_stream(q2, n, bound, k_ref, v_ref, tri_ref, slice(HEAD_DIM, 2 * HEAD_DIM)))

    for n in range(n_tiles):
        bounded_variant(n)

    @pl.when(bound > MAX_SCORE_BOUND)
    def _():
        finish(*_diff_general(q1, q2, qi, k_ref, v_ref, m1, l1, a1, m2, l2, a2))


def _diff_attention(proj, lam_params, bound, lambda_init):
    b, s, _ = proj.shape
    h = DIFF_HEADS
    tq, dd = DIFF_TQ, DIFF_DIAG
    assert tq == 2 * dd
    w = DIFF_V_DIM
    tri = jnp.asarray(np.arange(tq)[:, None] >= np.arange(dd)[None, :], F32)
    q_spec = lambda off: pl.BlockSpec((1, tq, w), lambda bi, hi, qi: (bi, qi, off + hi))
    kv_spec = lambda off: pl.BlockSpec((1, s, w), lambda bi, hi, qi: (bi, 0, off + hi))
    stat = pltpu.VMEM((tq, LANES), F32)
    acc = pltpu.VMEM((tq, w), F32)
    return pl.pallas_call(
        functools.partial(_diff_kernel, lambda_init=lambda_init, n_tiles=s // tq),
        grid=(b, h, s // tq),
        in_specs=[pl.BlockSpec(memory_space=pltpu.SMEM),
                  q_spec(0), q_spec(3 * h), kv_spec(h), kv_spec(2 * h),
                  pl.BlockSpec(lam_params.shape, lambda bi, hi, qi: (0, 0)),
                  pl.BlockSpec(tri.shape, lambda bi, hi, qi: (0, 0))],
        out_specs=pl.BlockSpec((1, tq, w), lambda bi, hi, qi: (bi, qi, hi)),
        out_shape=jax.ShapeDtypeStruct((b, s, DIFF_WIDTH), BF16),
        scratch_shapes=[stat, stat, acc, stat, stat, acc],
        compiler_params=pltpu.CompilerParams(
            dimension_semantics=("parallel", "parallel", "arbitrary"), vmem_limit_bytes=VMEM_LIMIT),
        name="diff_attention",
    )(bound, proj, proj, proj, proj, lam_params, tri)


def _rope_tables(pos):
    inv = 1.0 / (ROPE_THETA ** (np.arange(0, HEAD_DIM, 2, dtype=np.float64) / HEAD_DIM))
    ang = np.asarray(pos, np.float64)[:, None] * inv[None, :]
    cos, sin = np.cos(ang), np.sin(ang)
    return (jnp.asarray(np.concatenate([cos, cos], axis=-1), F32),
            jnp.asarray(np.concatenate([-sin, sin], axis=-1), F32))


def _retention_tables():
    h, c = RET_HEADS, RET_CHUNK
    log_g = np.log1p(-np.exp2(-5.0 - np.arange(h, dtype=np.float64)))
    j = np.arange(c, dtype=np.float64)
    diff = j[:, None] - j[None, :]
    intra = np.where(diff >= 0, np.exp(log_g[:, None, None] * np.maximum(diff, 0.0)), 0.0)
    q_dec = np.exp(log_g[:, None] * (j + 1.0))
    k_dec = np.exp(log_g[:, None] * (c - 1.0 - j))
    chunk_dec = np.exp(log_g * c)
    wide = lambda t: jnp.asarray(np.broadcast_to(t[:, :, None], (h, t.shape[1], HEAD_DIM)), F32)
    return jnp.asarray(intra, F32), wide(q_dec), wide(k_dec), wide(chunk_dec[:, None])


def _selection_tables(seq):
    tq, tk = NSA_TQ, NSA_TK
    n_cmp_rows = seq // CMP_STRIDE
    n_slc = seq // SLC_BLOCK
    assert tq == tk and n_slc <= AUG_ONE_LANE < LANES
    c_start = np.arange(n_cmp_rows) * CMP_STRIDE
    s_start = np.arange(n_slc) * SLC_BLOCK
    overlap_t = ((c_start[None, :] <= s_start[:, None] + SLC_BLOCK - 1)
                 & (c_start[None, :] + CMP_BLOCK - 1 >= s_start[:, None]))
    lane = np.arange(LANES)[None, :]
    key = np.arange(seq)[:, None]
    eaug = ((key // SLC_BLOCK) == lane) | (lane == AUG_ONE_LANE)
    cmask = lane * CMP_STRIDE + CMP_BLOCK - 1 <= key
    r = np.arange(tq)[:, None]
    c = np.arange(WINDOW + tq)[None, :]
    wmask = []
    for pat in range(WINDOW // tq + 1):
        d = min(pat * tq, WINDOW) + r - c
        wmask.append((d >= 0) & (d < WINDOW))
    tri = np.arange(tq)[:, None] >= np.arange(tk)[None, :]
    return (jnp.asarray(overlap_t, BF16), jnp.asarray(eaug, BF16), jnp.asarray(cmask, F32),
            jnp.asarray(np.stack(wmask), F32), jnp.asarray(tri, F32))


def _tile_modes(segments, tn):
    off, kinds, col = {}, [], 0
    for name, width, kind in segments:
        off[name] = col
        kinds += [kind] * (width // LANES)
        col += width
    per = tn // LANES
    tiles = [kinds[i:i + per] for i in range(0, len(kinds), per)]
    assert col % tn == 0
    modes = []
    for j, tile in enumerate(tiles):
        if modes and modes[-1][2] == tile:
            modes[-1] = (modes[-1][0], j + 1, tile)
        else:
            modes.append((j, j + 1, tile))
    return modes, off


def _l0_tile_modes():
    plain, silu = ("plain", 0), ("silu", 0)
    k_norm = ("nrope", L0_TAB_K)
    return _tile_modes([
        ("rq", RET_WIDTH, ("rope", L0_TAB_ROPE)), ("rk", RET_WIDTH, ("rope", L0_TAB_ROPE_SCALED)),
        ("rv", RET_WIDTH, plain), ("rgate", RET_WIDTH, silu), ("nq", NSA_WIDTH, ("nrope", L0_TAB_Q)),
        ("kc", NSA_KV_WIDTH, plain), ("vc", NSA_KV_WIDTH, plain), ("ks", NSA_KV_WIDTH, k_norm),
        ("vs", NSA_KV_WIDTH, plain), ("kw", NSA_KV_WIDTH, k_norm), ("vw", NSA_KV_WIDTH, plain),
        ("ngate", NSA_WIDTH, silu)], PROJ_TN)


def _l1_tile_modes():
    return _tile_modes([("q", DIFF_WIDTH, ("nrope", L1_TAB_Q)), ("k", DIFF_WIDTH, ("nrope", L1_TAB_K)),
                        ("v", DIFF_WIDTH, ("plain", 0)), ("gate", DIFF_WIDTH, ("silu", 0))], L1_PROJ_TN)[0]


def _rope_pair(cos, sin_signed, gain=None, scale=1.0):
    if gain is None:
        return [cos * scale, sin_signed * scale]
    return [cos * (gain * scale)[None, :], sin_signed * (jnp.roll(gain, HALF) * scale)[None, :]]


def kernel(x, l0_norm_g, l0_w_in, l0_w_out, l0_nsa_q_norm_g, l0_nsa_k_norm_g, l0_cmp_pe_k, l0_cmp_w1_k, l0_cmp_w2_k, l0_cmp_pe_v, l0_cmp_w1_v, l0_cmp_w2_v, l1_norm_g, l1_w_in, l1_w_out, l1_q_norm_g, l1_k_norm_g, l1_lambda_q1, l1_lambda_k1, l1_lambda_q2, l1_lambda_k2):
    b, s, d = x.shape
    m = b * s
    x2d = x.reshape(m, d)
    cos, sin = _rope_tables(np.arange(s))
    cos_c, sin_c = _rope_tables(np.arange(s // CMP_STRIDE) * CMP_STRIDE + CMP_BLOCK - 1)
    intra, qdec, kdec, cdec = _retention_tables()
    nsa_tables = _selection_tables(s)

    modes0, off = _l0_tile_modes()
    w0 = l0_w_in.astype(BF16)
    wg = l0_w_in[:, AB_MAIN_COLS:].reshape(d, 3, NSA_KV_HEADS, NSA_GROUP).transpose(0, 2, 1, 3)
    wg = wg.reshape(d, NSA_KV_HEADS, 3 * NSA_GROUP)
    wg = jnp.pad(wg, ((0, 0), (0, 0), (0, LANES - 3 * NSA_GROUP))).reshape(d, NSA_KV_HEADS * LANES).astype(BF16)
    tabs0 = jnp.stack(_rope_pair(cos, sin) + _rope_pair(cos, sin, scale=QK_SCALE)
                      + _rope_pair(cos, sin, l0_nsa_q_norm_g, Q_SCALE) + _rope_pair(cos, sin, l0_nsa_k_norm_g))
    proj0, gates = _norm_proj(x2d, l0_norm_g, w0, tabs0, modes0, s, PROJ_TN, wg=wg)
    proj0 = proj0.reshape(b, s, AB_MAIN_COLS)
    y_ret = _retention(proj0, intra, qdec, kdec, cdec)
    w1k = l0_cmp_w1_k.astype(BF16).reshape(CMP_BLOCK, HEAD_DIM, HEAD_DIM)
    w1v = l0_cmp_w1_v.astype(BF16).reshape(CMP_BLOCK, HEAD_DIM, HEAD_DIM)
    kcmp, vcmp = _nsa_compress(proj0, off["kc"] // HEAD_DIM, off["vc"] // HEAD_DIM,
                               l0_cmp_pe_k, w1k, l0_cmp_w2_k.astype(BF16),
                               l0_cmp_pe_v, w1v, l0_cmp_w2_v.astype(BF16),
                               l0_nsa_k_norm_g, cos_c, sin_c)
    bound = (BOUND_MARGIN * HEAD_DIM * Q_SCALE * jnp.max(jnp.abs(l0_nsa_q_norm_g))
             * jnp.max(jnp.abs(l0_nsa_k_norm_g))).reshape(1).astype(F32)
    y_nsa, w_out0, w_in1, w_out1 = _nsa_attention(proj0, gates, kcmp, vcmp, bound, nsa_tables, off,
                                                  cast_weights=[l0_w_out, l1_w_in, l1_w_out])
    x1 = _out_proj([y_ret.reshape(m, RET_WIDTH), y_nsa.reshape(m, NSA_WIDTH)],
                   [w_out0[:RET_WIDTH], w_out0[RET_WIDTH:]], x2d)

    lambda_init = 0.8 - 0.6 * math.exp(-0.3 * 1)
    tabs1 = jnp.stack(_rope_pair(cos, sin, l1_q_norm_g, Q_SCALE) + _rope_pair(cos, sin, l1_k_norm_g))
    proj1 = _norm_proj(x1, l1_norm_g, w_in1, tabs1, _l1_tile_modes(), s, L1_PROJ_TN)[0]
    lam_params = jnp.stack([l1_lambda_q1, l1_lambda_k1, l1_lambda_q2, l1_lambda_k2]).astype(F32)
    bound1 = (BOUND_MARGIN * HEAD_DIM * Q_SCALE * jnp.max(jnp.abs(l1_q_norm_g))
              * jnp.max(jnp.abs(l1_k_norm_g))).reshape(1).astype(F32)
    y1 = _diff_attention(proj1.reshape(b, s, C_IN_COLS), lam_params, bound1, lambda_init)
    out = _out_proj([y1.reshape(m, DIFF_WIDTH)], [w_out1], x1)
    return out.reshape(b, s, d)
```

```python
import functools
import math

import numpy as np
import jax
import jax.numpy as jnp
from jax import lax
from jax.experimental import pallas as pl
from jax.experimental.pallas import tpu as pltpu

F32 = jnp.float32
BF16 = jnp.bfloat16

HEAD_DIM = 128
HALF = HEAD_DIM // 2
ROPE_THETA = 10000.0
EPS = 1e-6
RET_HEADS = 8
RET_CHUNK = 256
RET_HEADS_PER_STEP = 4
NSA_HEADS = 8
NSA_KV_HEADS = 2
NSA_GROUP = NSA_HEADS // NSA_KV_HEADS
CMP_BLOCK = 32
CMP_STRIDE = 16
SLC_BLOCK = 64
SLC_TOPK = 16
N_LOCAL_BLOCKS = 2
WINDOW = 512
DIFF_HEADS = 8
DIFF_V_DIM = 2 * HEAD_DIM
QK_SCALE = HEAD_DIM ** -0.5
LOG2E = math.log2(math.e)
Q_SCALE = QK_SCALE * LOG2E

RET_WIDTH = RET_HEADS * HEAD_DIM
NSA_WIDTH = NSA_HEADS * HEAD_DIM
NSA_KV_WIDTH = NSA_KV_HEADS * HEAD_DIM
AB_MAIN_COLS = 4 * RET_WIDTH + 2 * NSA_WIDTH + 6 * NSA_KV_WIDTH
DIFF_WIDTH = DIFF_HEADS * DIFF_V_DIM
C_IN_COLS = 4 * DIFF_WIDTH

LANES = 128
MASKED = -1e30
M_INIT = -1e29
MAX_SCORE_BOUND = 60.0
BOUND_MARGIN = 1.05
AUG_ONE_LANE = 32
VMEM_LIMIT = 56 * 1024 * 1024

PROJ_TM = 1024
PROJ_TN = 1280
L1_PROJ_TN = 1024
OPROJ_TM = 512
OPROJ_TN = 2048
PROJ_SUB = 256
L0_TAB_ROPE, L0_TAB_ROPE_SCALED, L0_TAB_Q, L0_TAB_K = range(4)
L1_TAB_Q, L1_TAB_K = range(2)
NSA_TQ = 512
NSA_TK = 512
DIFF_TQ = 512
DIFF_TK = 512
DIFF_DIAG = 256


def _dot(a, b):
    return jnp.dot(a, b, preferred_element_type=F32)


def _dot_nt(a, b):
    return lax.dot_general(a, b, (((1,), (1,)), ((), ())), preferred_element_type=F32)


def _dot_tn(a, b):
    return lax.dot_general(a, b, (((0,), (0,)), ((), ())), preferred_element_type=F32)


def _silu(x):
    hx = 0.5 * x
    return hx + hx * jnp.tanh(hx)


def _rms(x):
    return x * lax.rsqrt(jnp.mean(x * x, axis=-1, keepdims=True) + EPS)


def _rope(x, cos, sin_signed):
    return x * cos + pltpu.roll(x, HALF, axis=1) * sin_signed


def _rope_tab(seg, t, tabs_ref, rows):
    return seg * tabs_ref[2 * t, rows, :] + pltpu.roll(seg, HALF, axis=1) * tabs_ref[2 * t + 1, rows, :]


def _tile_epilogue(acc, kinds, tabs_ref, rows, o_ref):
    n_seg = len(kinds)
    ri = lax.broadcasted_iota(jnp.int32, (2 * LANES, 2 * LANES), 0)
    ci = lax.broadcasted_iota(jnp.int32, (2 * LANES, 2 * LANES), 1)
    pair_mean = jnp.where((ri < LANES) == (ci < LANES), 1.0 / HEAD_DIM, 0.0).astype(BF16)
    c = 0
    while c < n_seg:
        op, t = kinds[c]
        cols = slice(c * LANES, (c + 1) * LANES)
        if op == "nrope" and c + 1 < n_seg and kinds[c + 1][0] == "nrope":
            cols2 = slice(c * LANES, (c + 2) * LANES)
            seg2 = acc[:, cols2]
            seg2 = seg2 * lax.rsqrt(_dot((seg2 * seg2).astype(BF16), pair_mean) + EPS)
            for half in range(2):
                out = _rope_tab(seg2[:, half * LANES:(half + 1) * LANES], kinds[c + half][1], tabs_ref, rows)
                o_ref[rows, (c + half) * LANES:(c + half + 1) * LANES] = out.astype(o_ref.dtype)
            c += 2
            continue
        seg = acc[:, cols]
        if op == "silu":
            seg = _silu(seg)
        elif op == "nrope":
            seg = seg * lax.rsqrt(_dot((seg * seg).astype(BF16), pair_mean[:LANES, :LANES]) + EPS)
            seg = _rope_tab(seg, t, tabs_ref, rows)
        elif op == "rope":
            seg = _rope_tab(seg, t, tabs_ref, rows)
        o_ref[rows, cols] = seg.astype(o_ref.dtype)
        c += 1


def _proj_kernel(*refs, tile_modes, has_gates):
    if has_gates:
        x_ref, g_ref, w_ref, tabs_ref, wg_ref, o_ref, og_ref, h_scr = refs
    else:
        x_ref, g_ref, w_ref, tabs_ref, o_ref, h_scr = refs
    j = pl.program_id(1)
    n_sub = h_scr.shape[0] // PROJ_SUB

    def make_branch(cond, kinds, first):
        @pl.when(cond)
        def _():
            for r in range(n_sub):
                rows = slice(r * PROJ_SUB, (r + 1) * PROJ_SUB)
                if first:
                    h = (_rms(x_ref[rows, :]) * g_ref[...]).astype(BF16)
                    h_scr[rows, :] = h
                    if has_gates:
                        og_ref[rows, :] = 1.0 / (1.0 + jnp.exp(-_dot(h, wg_ref[...])))
                else:
                    h = h_scr[rows, :]
                _tile_epilogue(_dot(h, w_ref[...]), kinds, tabs_ref, rows, o_ref)

    for lo, hi, kinds in tile_modes:
        if lo == 0:
            make_branch(j == 0, kinds, True)
            lo = 1
        if hi > lo:
            make_branch((j >= lo) & (j < hi), kinds, False)


def _norm_proj(x2d, g, w, tabs, tile_modes, seq, tn, wg=None):
    m, d = x2d.shape
    tm = PROJ_TM
    n = tile_modes[-1][1] * tn
    s_tiles = seq // tm
    has_gates = wg is not None
    in_specs = [
        pl.BlockSpec((tm, d), lambda i, j: (i, 0)),
        pl.BlockSpec((1, d), lambda i, j: (0, 0)),
        pl.BlockSpec((d, tn), lambda i, j: (0, j)),
        pl.BlockSpec((tabs.shape[0], tm, LANES), lambda i, j: (0, i % s_tiles, 0)),
    ]
    args = [x2d, g.reshape(1, d), w, tabs]
    out_shape = [jax.ShapeDtypeStruct((m, n), BF16)]
    out_specs = [pl.BlockSpec((tm, tn), lambda i, j: (i, j))]
    if has_gates:
        ng = wg.shape[1]
        in_specs.append(pl.BlockSpec((d, ng), lambda i, j: (0, 0)))
        args.append(wg)
        out_shape.append(jax.ShapeDtypeStruct((m, ng), F32))
        out_specs.append(pl.BlockSpec((tm, ng), lambda i, j: (i, 0)))
    return pl.pallas_call(
        functools.partial(_proj_kernel, tile_modes=tile_modes, has_gates=has_gates),
        grid=(m // tm, n // tn),
        in_specs=in_specs,
        out_specs=out_specs,
        out_shape=out_shape,
        scratch_shapes=[pltpu.VMEM((tm, d), BF16)],
        compiler_params=pltpu.CompilerParams(
            dimension_semantics=("parallel", "arbitrary"), vmem_limit_bytes=VMEM_LIMIT),
        name="norm_proj",
    )(*args)


def _oproj_kernel(*refs, n_terms):
    y_refs, w_refs = refs[:n_terms], refs[n_terms:2 * n_terms]
    x_ref, o_ref = refs[2 * n_terms:]
    acc = x_ref[...]
    for y_ref, w_ref in zip(y_refs, w_refs):
        acc = acc + _dot(y_ref[...], w_ref[...])
    o_ref[...] = acc


def _out_proj(ys, ws, x2d):
    m, n = x2d.shape
    tm, tn = OPROJ_TM, OPROJ_TN
    y_specs = [pl.BlockSpec((tm, y.shape[1]), lambda i, j: (i, 0)) for y in ys]
    w_specs = [pl.BlockSpec((w.shape[0], tn), lambda i, j: (0, j)) for w in ws]
    return pl.pallas_call(
        functools.partial(_oproj_kernel, n_terms=len(ys)),
        grid=(m // tm, n // tn),
        in_specs=y_specs + w_specs + [pl.BlockSpec((tm, tn), lambda i, j: (i, j))],
        out_specs=pl.BlockSpec((tm, tn), lambda i, j: (i, j)),
        out_shape=jax.ShapeDtypeStruct((m, n), F32),
        compiler_params=pltpu.CompilerParams(
            dimension_semantics=("parallel", "arbitrary"), vmem_limit_bytes=VMEM_LIMIT),
        name="out_proj",
    )(*ys, *ws, x2d)


def _ret_kernel(q_ref, k_ref, v_ref, gate_ref, intra_ref, qdec_ref, kdec_ref, cdec_ref, o_ref, *, n_chunks):
    c = RET_CHUNK
    states = [jnp.zeros((HEAD_DIM, HEAD_DIM), F32)] * RET_HEADS_PER_STEP
    for n in range(n_chunks):
        rows = slice(n * c, (n + 1) * c)
        for hh in range(RET_HEADS_PER_STEP):
            cols = slice(hh * HEAD_DIM, (hh + 1) * HEAD_DIM)
            q = q_ref[0, rows, cols]
            k = k_ref[0, rows, cols]
            v = v_ref[0, rows, cols]
            scores = _dot_nt(q, k) * intra_ref[hh]
            inner = _dot(scores.astype(BF16), v)
            cross = _dot((q.astype(F32) * qdec_ref[hh]).astype(BF16), states[hh].astype(BF16))
            kv = _dot_tn((k.astype(F32) * kdec_ref[hh]).astype(BF16), v)
            states[hh] = states[hh] * cdec_ref[hh] + kv
            o = _rms(inner + cross) * gate_ref[0, rows, cols].astype(F32)
            o_ref[0, rows, cols] = o.astype(o_ref.dtype)


def _retention(proj, intra, qdec, kdec, cdec):
    b, s, _ = proj.shape
    per = RET_HEADS_PER_STEP
    h = RET_HEADS // per
    y_shape = (b, s, RET_WIDTH)
    head = lambda off: pl.BlockSpec((1, s, per * HEAD_DIM), lambda bi, hi: (bi, 0, off + hi))
    table = lambda rows, cols=HEAD_DIM: pl.BlockSpec((per, rows, cols), lambda bi, hi: (hi, 0, 0))
    return pl.pallas_call(
        functools.partial(_ret_kernel, n_chunks=s // RET_CHUNK),
        grid=(b, h),
        in_specs=[head(0), head(h), head(2 * h), head(3 * h),
                  table(RET_CHUNK, RET_CHUNK), table(RET_CHUNK), table(RET_CHUNK), table(1)],
        out_specs=pl.BlockSpec((1, s, per * HEAD_DIM), lambda bi, hi: (bi, 0, hi)),
        out_shape=jax.ShapeDtypeStruct(y_shape, BF16),
        compiler_params=pltpu.CompilerParams(
            dimension_semantics=("parallel", "arbitrary"), vmem_limit_bytes=VMEM_LIMIT),
        name="retention",
    )(proj, proj, proj, proj, intra, qdec, kdec, cdec)


def _cmp_kernel(kc_ref, vc_ref, pek_ref, w1k_ref, w2k_ref, pev_ref, w1v_ref, w2v_ref, kg_ref,
                cos_ref, sin_ref, ko_ref, vo_ref, t_scr, *, seq):
    n_rows = seq // CMP_STRIDE

    def compress(src_ref, pe_ref, w1_ref, w2_ref):
        t_scr[0:seq, :] = src_ref[0].astype(F32)
        t_scr[seq:seq + CMP_STRIDE, :] = jnp.zeros((CMP_STRIDE, HEAD_DIM), F32)
        acc = jnp.zeros((n_rows, HEAD_DIM), F32)
        for r in range(0, CMP_BLOCK, 2):
            rows = [t_scr[pl.ds(r + i, n_rows, stride=CMP_STRIDE), :] + pe_ref[r + i:r + i + 1, :] for i in range(2)]
            acc = acc + _dot(jnp.concatenate(rows, axis=1).astype(BF16), w1_ref[r // 2])
        return _dot(_silu(acc).astype(BF16), w2_ref[...])

    kc = compress(kc_ref, pek_ref, w1k_ref, w2k_ref)
    kc = _rope(_rms(kc) * kg_ref[...], cos_ref[...], sin_ref[...])
    ko_ref[0, 0] = kc.astype(ko_ref.dtype)
    vo_ref[0, 0] = compress(vc_ref, pev_ref, w1v_ref, w2v_ref).astype(vo_ref.dtype)


def _nsa_compress(proj, kc_off, vc_off, pe_k, w1_k, w2_k, pe_v, w1_v, w2_v, k_g, cos_c, sin_c):
    b, s, _ = proj.shape
    g = NSA_KV_HEADS
    n_rows = s // CMP_STRIDE
    head = lambda off: pl.BlockSpec((1, s, HEAD_DIM), lambda bi, gi: (bi, 0, off + gi))
    full = lambda shape: pl.BlockSpec(shape, lambda bi, gi: (0,) * len(shape))
    out_spec = pl.BlockSpec((1, 1, n_rows, HEAD_DIM), lambda bi, gi: (bi, gi, 0, 0))
    out_sds = jax.ShapeDtypeStruct((b, g, n_rows, HEAD_DIM), BF16)
    return pl.pallas_call(
        functools.partial(_cmp_kernel, seq=s),
        grid=(b, g),
        in_specs=[head(kc_off), head(vc_off),
                  full((CMP_BLOCK, HEAD_DIM)), full(w1_k.shape), full((HEAD_DIM, HEAD_DIM)),
                  full((CMP_BLOCK, HEAD_DIM)), full(w1_v.shape), full((HEAD_DIM, HEAD_DIM)),
                  full((1, HEAD_DIM)), full((n_rows, HEAD_DIM)), full((n_rows, HEAD_DIM))],
        out_specs=[out_spec, out_spec],
        out_shape=[out_sds, out_sds],
        scratch_shapes=[pltpu.VMEM((s + CMP_STRIDE, HEAD_DIM), F32)],
        compiler_params=pltpu.CompilerParams(
            dimension_semantics=("parallel", "arbitrary"), vmem_limit_bytes=VMEM_LIMIT),
        name="nsa_compress",
    )(proj, proj, pe_k, w1_k, w2_k, pe_v, w1_v, w2_v, k_g.reshape(1, HEAD_DIM), cos_c, sin_c)


def _flash_init(m_scr, l_scr, acc_scr):
    m_scr[...] = jnp.full(m_scr.shape, M_INIT, F32)
    l_scr[...] = jnp.zeros(l_scr.shape, F32)
    acc_scr[...] = jnp.zeros(acc_scr.shape, F32)


def _lane_tiles(x, width):
    return x if width == LANES else jnp.concatenate([x] * (width // LANES), axis=1)


def _flash_step(q, k, v, bias, m_scr, l_scr, acc_scr):
    s = _dot_nt(q, k)
    if bias is not None:
        s = s + bias
    tk = s.shape[1]
    m_old = m_scr[...]
    m_new = jnp.maximum(m_old, jnp.max(s, axis=-1, keepdims=True))
    alpha = jnp.exp2(m_old - m_new)
    p = jnp.exp2(s - _lane_tiles(m_new, tk))
    p_cols = p[:, 0:LANES]
    for c in range(1, tk // LANES):
        p_cols = p_cols + p[:, c * LANES:(c + 1) * LANES]
    l_scr[...] = alpha * l_scr[...] + p_cols
    acc_scr[...] = _lane_tiles(alpha, acc_scr.shape[1]) * acc_scr[...] + _dot(p.astype(BF16), v)
    m_scr[...] = m_new


def _flash_finish(l_scr, acc_scr):
    return acc_scr[...] * (1.0 / jnp.sum(l_scr[...], axis=-1, keepdims=True))


def _rep_heads(bias):
    return jnp.concatenate([bias] * NSA_GROUP, axis=0)


def _stack_heads(qblk):
    return jnp.concatenate([qblk[:, r * HEAD_DIM:(r + 1) * HEAD_DIM] for r in range(NSA_GROUP)], axis=0)


def _select_blocks(p, ovl, q0, tq, n_slc):
    psum = p[0:tq]
    for r in range(1, NSA_GROUP):
        psum = psum + p[r * tq:(r + 1) * tq]
    p_hi = psum.astype(BF16)
    rem = psum - p_hi.astype(F32)
    p_mid = rem.astype(BF16)
    p_lo = (rem - p_mid.astype(F32)).astype(BF16)
    imp = _dot_nt(ovl, p_hi) + _dot_nt(ovl, p_mid) + _dot_nt(ovl, p_lo)
    jb = lax.broadcasted_iota(jnp.int32, (n_slc, tq), 0)
    blk_t = jnp.right_shift(q0 + lax.broadcasted_iota(jnp.int32, (n_slc, tq), 1), int(math.log2(SLC_BLOCK)))
    back = blk_t - jb
    forced = (jb == 0) | ((back >= 0) & (back < N_LOCAL_BLOCKS))
    score = jnp.where(forced, 1e9, jnp.where(back >= 0, imp, -1e9))
    rank = jnp.zeros((n_slc, tq), F32)
    for mp in range(n_slc):
        row = score[mp:mp + 1, :]
        ahead = (row > score) | ((row == score) & (jb > mp))
        rank = rank + jnp.where(ahead, 1.0, 0.0)
    sel_t = jnp.where(rank < float(min(SLC_TOPK, n_slc)), 1.0, 0.0)
    sel_t = jnp.concatenate([sel_t, jnp.zeros((LANES - n_slc, tq), F32)], axis=0).astype(BF16)
    ri = lax.broadcasted_iota(jnp.int32, (tq, tq), 0)
    ci = lax.broadcasted_iota(jnp.int32, (tq, tq), 1)
    eye = jnp.where(ri == ci, 1.0, 0.0).astype(BF16)
    return _dot_nt(eye, sel_t)


def _nsa_combine(o_cmp, o_slc, o_win, gates_ref, ngate_ref, o_ref, tq):
    gates = gates_ref[...]
    for r in range(NSA_GROUP):
        rows = slice(r * tq, (r + 1) * tq)
        g_cmp = gates[:, r:r + 1]
        g_slc = gates[:, NSA_GROUP + r:NSA_GROUP + r + 1]
        g_win = gates[:, 2 * NSA_GROUP + r:2 * NSA_GROUP + r + 1]
        y = g_cmp * o_cmp[rows] + g_slc * o_slc[rows] + g_win * o_win[rows]
        cols = slice(r * HEAD_DIM, (r + 1) * HEAD_DIM)
        o_ref[0, :, cols] = (y * ngate_ref[0, :, cols].astype(F32)).astype(o_ref.dtype)


def _nsa_general(q4, q0, qi, gates_ref, ngate_ref, kcmp_ref, vcmp_ref, ks_ref, vs_ref, kw_ref, vw_ref,
                 ovl_ref, eaug_ref, o_ref, bias_scr, m_scr, l_scr, acc_scr, *, seq):
    tq, tk = NSA_TQ, NSA_TK
    n_slc = seq // SLC_BLOCK

    tpos = q0 + lax.broadcasted_iota(jnp.int32, (tq, LANES), 0)
    cidx = lax.broadcasted_iota(jnp.int32, (tq, LANES), 1)
    cbias = jnp.where(cidx * CMP_STRIDE + (CMP_BLOCK - 1) <= tpos, 0.0, MASKED)
    s = _dot_nt(q4, kcmp_ref[0, 0]) + _rep_heads(cbias)
    m = jnp.maximum(jnp.max(s, axis=-1, keepdims=True), M_INIT)
    e = jnp.exp2(s - m)
    l = jnp.sum(e, axis=-1, keepdims=True)
    p = e * (1.0 / jnp.maximum(l, 1e-30))
    o_cmp = _dot(p.astype(BF16), vcmp_ref[0, 0])

    sel = _select_blocks(p, ovl_ref[...], q0, tq, n_slc).astype(BF16)
    sel_keys = _dot_nt(sel, eaug_ref[...])
    for kt in range(seq // tk):
        bias_scr[kt] = jnp.where(sel_keys[:, kt * tk:(kt + 1) * tk] > 0.5, 0.0, MASKED)

    _flash_init(m_scr, l_scr, acc_scr)

    def slc_body(kt, carry):
        k0 = pl.multiple_of(kt * tk, tk)
        _flash_step(q4, ks_ref[0, pl.ds(k0, tk), :], vs_ref[0, pl.ds(k0, tk), :],
                    _rep_heads(bias_scr[kt]), m_scr, l_scr, acc_scr)
        return carry

    n_full = lax.div(q0, tk)
    lax.fori_loop(0, n_full, slc_body, 0)
    kd = pl.multiple_of(n_full * tk, tk)
    qpos = q0 + lax.broadcasted_iota(jnp.int32, (tq, tk), 0)
    kpos = kd + lax.broadcasted_iota(jnp.int32, (tq, tk), 1)
    causal = jnp.where(kpos <= qpos, 0.0, MASKED)
    _flash_step(q4, ks_ref[0, pl.ds(kd, tk), :], vs_ref[0, pl.ds(kd, tk), :],
                _rep_heads(bias_scr[n_full] + causal), m_scr, l_scr, acc_scr)
    o_slc = _flash_finish(l_scr, acc_scr)

    wk = WINDOW + tq
    ws = pl.multiple_of(jnp.maximum(q0 - WINDOW, 0), tq)
    d = (q0 - ws) + lax.broadcasted_iota(jnp.int32, (tq, wk), 0) - lax.broadcasted_iota(jnp.int32, (tq, wk), 1)
    wbias = jnp.where((d >= 0) & (d < WINDOW), 0.0, MASKED)
    s = _dot_nt(q4, kw_ref[0, pl.ds(ws, wk), :]) + _rep_heads(wbias)
    e = jnp.exp2(s - jnp.max(s, axis=-1, keepdims=True))
    o_win = _dot(e.astype(BF16), vw_ref[0, pl.ds(ws, wk), :]) * (1.0 / jnp.sum(e, axis=-1, keepdims=True))

    _nsa_combine(o_cmp, o_slc, o_win, gates_ref, ngate_ref, o_ref, tq)


def _nsa_bounded(q4, q0, qi, bound, gates_ref, ngate_ref, kcmp_ref, vcmp_ref, ks_ref, vs_ref, kw_ref, vw_ref,
                 ovl_ref, eaug_ref, cmask_ref, wmask_ref, tri_ref, o_ref,
                 ksa_scr, kwa_scr, kca_scr, l_scr, acc_scr, *, seq):
    tq, tk = NSA_TQ, NSA_TK
    n_slc = seq // SLC_BLOCK
    n_cmp_rows = kca_scr.shape[0]

    @pl.when(qi == 0)
    def _():
        one_col = eaug_ref[...]
        lane = lax.broadcasted_iota(jnp.int32, one_col.shape, 1)
        ksa_scr[:, :HEAD_DIM] = ks_ref[0]
        ksa_scr[:, HEAD_DIM:] = one_col
        one_col = jnp.where(lane == AUG_ONE_LANE, one_col, jnp.zeros_like(one_col))
        kwa_scr[:, :HEAD_DIM] = kw_ref[0]
        kwa_scr[:, HEAD_DIM:] = one_col
        kca_scr[:, :HEAD_DIM] = kcmp_ref[0, 0]
        kca_scr[:, HEAD_DIM:] = one_col[:n_cmp_rows]

    lane = lax.broadcasted_iota(jnp.int32, (tq, LANES), 1)
    shift_cols = jnp.where(lane == AUG_ONE_LANE, -bound, 0.0)
    qa = jnp.concatenate([q4, _rep_heads(shift_cols.astype(BF16))], axis=1)

    e = jnp.exp2(_dot_nt(qa, kca_scr[...])) * _rep_heads(cmask_ref[...])
    l = jnp.sum(e, axis=-1, keepdims=True)
    p = e * jnp.where(l > 0.0, 1.0 / l, 0.0)
    o_cmp = _dot(p.astype(BF16), vcmp_ref[0, 0])

    sel = _select_blocks(p, ovl_ref[...], q0, tq, n_slc)
    sel_cols = jnp.where(lane < n_slc, jnp.where(sel > 0.5, 0.0, MASKED), shift_cols)
    qs = jnp.concatenate([q4, _rep_heads(sel_cols.astype(BF16))], axis=1)

    l_scr[...] = jnp.zeros(l_scr.shape, F32)
    acc_scr[...] = jnp.zeros(acc_scr.shape, F32)

    def accumulate(k0, width, mask):
        p_t = jnp.exp2(_dot_nt(qs, ksa_scr[pl.ds(k0, width), :]))
        if mask is not None:
            p_t = p_t * mask
        l_scr[...] += _lane_tile_sum(p_t)
        acc_scr[...] += _dot(p_t.astype(BF16), vs_ref[0, pl.ds(k0, width), :])

    big = 2 * tk

    def slc_body(kt, carry):
        accumulate(pl.multiple_of(kt * big, big), big, None)
        return carry

    n_full = lax.div(q0, tk)
    n_big = lax.div(n_full, 2)
    lax.fori_loop(0, n_big, slc_body, 0)

    @pl.when(n_full > 2 * n_big)
    def _():
        accumulate(pl.multiple_of(n_big * big, big), tk, None)

    accumulate(pl.multiple_of(n_full * tk, tk), tk, _rep_heads(tri_ref[...]))
    o_slc = _flash_finish(l_scr, acc_scr)

    wk = WINDOW + tq
    ws = pl.multiple_of(jnp.maximum(q0 - WINDOW, 0), tq)
    e = jnp.exp2(_dot_nt(qa, kwa_scr[pl.ds(ws, wk), :])) * _rep_heads(wmask_ref[0])
    o_win = _dot(e.astype(BF16), vw_ref[0, pl.ds(ws, wk), :]) * (1.0 / jnp.sum(e, axis=-1, keepdims=True))

    _nsa_combine(o_cmp, o_slc, o_win, gates_ref, ngate_ref, o_ref, tq)


def _nsa_kernel(bound_ref, q_ref, ngate_ref, gates_ref, kcmp_ref, vcmp_ref, ks_ref, vs_ref, kw_ref, vw_ref,
                ovl_ref, eaug_ref, cmask_ref, wmask_ref, tri_ref, *rest, seq, n_casts):
    cast_in, (o_ref, *cast_out) = rest[:n_casts], rest[n_casts:2 * n_casts + 1]
    bias_scr, m_scr, l_scr, acc_scr, ksa_scr, kwa_scr, kca_scr = rest[2 * n_casts + 1:]
    for src, dst in zip(cast_in, cast_out):
        dst[...] = src[...].astype(dst.dtype)
    qi = pl.program_id(2)
    q0 = qi * NSA_TQ
    q4 = _stack_heads(q_ref[0])
    bound = bound_ref[0]

    @pl.when(bound <= MAX_SCORE_BOUND)
    def _():
        _nsa_bounded(q4, q0, qi, bound, gates_ref, ngate_ref, kcmp_ref, vcmp_ref, ks_ref, vs_ref, kw_ref, vw_ref,
                     ovl_ref, eaug_ref, cmask_ref, wmask_ref, tri_ref, o_ref,
                     ksa_scr, kwa_scr, kca_scr, l_scr, acc_scr, seq=seq)

    @pl.when(bound > MAX_SCORE_BOUND)
    def _():
        _nsa_general(q4, q0, qi, gates_ref, ngate_ref, kcmp_ref, vcmp_ref, ks_ref, vs_ref, kw_ref, vw_ref,
                     ovl_ref, eaug_ref, o_ref, bias_scr, m_scr, l_scr, acc_scr, seq=seq)


def _nsa_attention(proj, gates, kcmp, vcmp, bound, tables, offs, cast_weights):
    b, s, _ = proj.shape
    g = NSA_KV_HEADS
    tq = NSA_TQ
    nq = s // tq
    gw = NSA_GROUP * HEAD_DIM
    n_cmp_rows = s // CMP_STRIDE
    ovl_t, eaug, cmask, wmask, tri = tables
    q_spec = lambda off: pl.BlockSpec((1, tq, gw), lambda bi, gi, qi: (bi, qi, off + gi))
    kv_spec = lambda off: pl.BlockSpec((1, s, HEAD_DIM), lambda bi, gi, qi: (bi, 0, off + gi))
    cmp_spec = pl.BlockSpec((1, 1, n_cmp_rows, HEAD_DIM), lambda bi, gi, qi: (bi, gi, 0, 0))
    full = lambda shape: pl.BlockSpec(shape, lambda bi, gi, qi: (0,) * len(shape))
    n_wpat = wmask.shape[0]
    rows4 = NSA_GROUP * tq
    n_steps = b * g * nq
    cast_specs = [pl.BlockSpec((w.shape[0] // n_steps, w.shape[1]), lambda bi, gi, qi: ((bi * g + gi) * nq + qi, 0))
                  for w in cast_weights]
    return pl.pallas_call(
        functools.partial(_nsa_kernel, seq=s, n_casts=len(cast_weights)),
        grid=(b, g, nq),
        in_specs=[pl.BlockSpec(memory_space=pltpu.SMEM),
                  q_spec(offs["nq"] // gw), q_spec(offs["ngate"] // gw),
                  pl.BlockSpec((tq, LANES), lambda bi, gi, qi: (bi * nq + qi, gi)),
                  cmp_spec, cmp_spec,
                  kv_spec(offs["ks"] // HEAD_DIM), kv_spec(offs["vs"] // HEAD_DIM),
                  kv_spec(offs["kw"] // HEAD_DIM), kv_spec(offs["vw"] // HEAD_DIM),
                  full(ovl_t.shape), full(eaug.shape),
                  pl.BlockSpec((tq, LANES), lambda bi, gi, qi: (qi, 0)),
                  pl.BlockSpec((1,) + wmask.shape[1:], lambda bi, gi, qi: (jnp.minimum(qi, n_wpat - 1), 0, 0)),
                  full(tri.shape)] + cast_specs,
        out_specs=[pl.BlockSpec((1, tq, gw), lambda bi, gi, qi: (bi, qi, gi))] + cast_specs,
        out_shape=[jax.ShapeDtypeStruct((b, s, NSA_WIDTH), BF16)]
        + [jax.ShapeDtypeStruct(w.shape, BF16) for w in cast_weights],
        scratch_shapes=[pltpu.VMEM((s // NSA_TK, tq, NSA_TK), F32),
                        pltpu.VMEM((rows4, LANES), F32), pltpu.VMEM((rows4, LANES), F32),
                        pltpu.VMEM((rows4, HEAD_DIM), F32),
                        pltpu.VMEM((s, 2 * HEAD_DIM), BF16), pltpu.VMEM((s, 2 * HEAD_DIM), BF16),
                        pltpu.VMEM((n_cmp_rows, 2 * HEAD_DIM), BF16)],
        compiler_params=pltpu.CompilerParams(
            dimension_semantics=("parallel", "parallel", "arbitrary"), vmem_limit_bytes=VMEM_LIMIT),
        name="nsa_attention",
    )(bound, proj, proj, gates, kcmp, vcmp, proj, proj, proj, proj, ovl_t, eaug, cmask, wmask, tri, *cast_weights)


def _diff_general(q1, q2, qi, k_ref, v_ref, m1, l1, a1, m2, l2, a2):
    tq, tk = DIFF_TQ, DIFF_TK
    _flash_init(m1, l1, a1)
    _flash_init(m2, l2, a2)

    def step(rows, k0, width, bias):
        k = k_ref[0, pl.ds(k0, width), :]
        v = v_ref[0, pl.ds(k0, width), :]
        _flash_step(q1[rows], k[:, :HEAD_DIM], v, bias, m1.at[rows], l1.at[rows], a1.at[rows])
        _flash_step(q2[rows], k[:, HEAD_DIM:], v, bias, m2.at[rows], l2.at[rows], a2.at[rows])

    def body(kt, carry):
        step(slice(0, tq), pl.multiple_of(kt * tk, tk), tk, None)
        return carry

    lax.fori_loop(0, qi * (tq // tk), body, 0)
    dd = DIFF_DIAG
    q0 = qi * tq
    for c in range(tq // dd):
        n_rows = tq - c * dd
        ri = lax.broadcasted_iota(jnp.int32, (n_rows, dd), 0)
        ci = lax.broadcasted_iota(jnp.int32, (n_rows, dd), 1)
        step(slice(c * dd, tq), pl.multiple_of(q0 + c * dd, dd), dd, jnp.where(ri >= ci, 0.0, MASKED))
    return _flash_finish(l1, a1), _flash_finish(l2, a2)


def _lane_tile_sum(p):
    cols = p[:, 0:LANES]
    for c in range(1, p.shape[1] // LANES):
        cols = cols + p[:, c * LANES:(c + 1) * LANES]
    return cols


def _diff_bounded_stream(q, n, bound, k_ref, v_ref, tri_ref, kcols):
    tq, dd = DIFF_TQ, DIFF_DIAG
    kmain = n * tq + dd
    p = jnp.exp2(_dot_nt(q, k_ref[0, 0:kmain, kcols]) - bound)
    p_diag = p[:, kmain - dd:] * tri_ref[...]
    p = p_diag if kmain == dd else jnp.concatenate([p[:, :kmain - dd], p_diag], axis=1)
    l = _lane_tile_sum(p)
    acc = _dot(p.astype(BF16), v_ref[0, 0:kmain, :])
    pc = jnp.exp2(_dot_nt(q[dd:], k_ref[0, kmain:kmain + dd, kcols]) - bound) * tri_ref[0:tq - dd, :]
    l = jnp.concatenate([l[:dd], l[dd:] + _lane_tile_sum(pc)], axis=0)
    acc = jnp.concatenate([acc[:dd], acc[dd:] + _dot(pc.astype(BF16), v_ref[0, kmain:kmain + dd, :])], axis=0)
    return acc * (1.0 / jnp.sum(l, axis=-1, keepdims=True))


def _diff_kernel(bound_ref, q_ref, gate_ref, k_ref, v_ref, lam_ref, tri_ref, o_ref, m1, l1, a1, m2, l2, a2,
                 *, lambda_init, n_tiles):
    qi = pl.program_id(2)
    q = q_ref[0]
    q1 = q[:, :HEAD_DIM]
    q2 = q[:, HEAD_DIM:]
    bound = bound_ref[0]

    def finish(o1, o2):
        lp = lam_ref[...]
        lam = (jnp.exp(jnp.sum(lp[0:1] * lp[1:2], axis=-1, keepdims=True))
               - jnp.exp(jnp.sum(lp[2:3] * lp[3:4], axis=-1, keepdims=True)) + lambda_init)
        o = _rms(o1 - lam * o2) * (1.0 - lambda_init)
        o_ref[0] = (o * gate_ref[0].astype(F32)).astype(o_ref.dtype)

    def bounded_variant(n):
        @pl.when((bound <= MAX_SCORE_BOUND) & (qi == n))
        def _():
            finish(_diff_bounded_stream(q1, n, bound, k_ref, v_ref, tri_ref, slice(0, HEAD_DIM)),
                   _diff_bounded_stream(q2, n, bound, k_ref, v_ref, tri_ref, slice(HEAD_DIM, 2 * HEAD_DIM)))

    for n in range(n_tiles):
        bounded_variant(n)

    @pl.when(bound > MAX_SCORE_BOUND)
    def _():
        finish(*_diff_general(q1, q2, qi, k_ref, v_ref, m1, l1, a1, m2, l2, a2))


def _diff_attention(proj, lam_params, bound, lambda_init):
    b, s, _ = proj.shape
    h = DIFF_HEADS
    tq, dd = DIFF_TQ, DIFF_DIAG
    assert tq == 2 * dd
    w = DIFF_V_DIM
    tri = jnp.asarray(np.arange(tq)[:, None] >= np.arange(dd)[None, :], F32)
    q_spec = lambda off: pl.BlockSpec((1, tq, w), lambda bi, hi, qi: (bi, qi, off + hi))
    kv_spec = lambda off: pl.BlockSpec((1, s, w), lambda bi, hi, qi: (bi, 0, off + hi))
    stat = pltpu.VMEM((tq, LANES), F32)
    acc = pltpu.VMEM((tq, w), F32)
    return pl.pallas_call(
        functools.partial(_diff_kernel, lambda_init=lambda_init, n_tiles=s // tq),
        grid=(b, h, s // tq),
        in_specs=[pl.BlockSpec(memory_space=pltpu.SMEM),
                  q_spec(0), q_spec(3 * h), kv_spec(h), kv_spec(2 * h),
                  pl.BlockSpec(lam_params.shape, lambda bi, hi, qi: (0, 0)),
                  pl.BlockSpec(tri.shape, lambda bi, hi, qi: (0, 0))],
        out_specs=pl.BlockSpec((1, tq, w), lambda bi, hi, qi: (bi, qi, hi)),
        out_shape=jax.ShapeDtypeStruct((b, s, DIFF_WIDTH), BF16),
        scratch_shapes=[stat, stat, acc, stat, stat, acc],
        compiler_params=pltpu.CompilerParams(
            dimension_semantics=("parallel", "parallel", "arbitrary"), vmem_limit_bytes=VMEM_LIMIT),
        name="diff_attention",
    )(bound, proj, proj, proj, proj, lam_params, tri)


def _rope_tables(pos):
    inv = 1.0 / (ROPE_THETA ** (np.arange(0, HEAD_DIM, 2, dtype=np.float64) / HEAD_DIM))
    ang = np.asarray(pos, np.float64)[:, None] * inv[None, :]
    cos, sin = np.cos(ang), np.sin(ang)
    return (jnp.asarray(np.concatenate([cos, cos], axis=-1), F32),
            jnp.asarray(np.concatenate([-sin, sin], axis=-1), F32))


def _retention_tables():
    h, c = RET_HEADS, RET_CHUNK
    log_g = np.log1p(-np.exp2(-5.0 - np.arange(h, dtype=np.float64)))
    j = np.arange(c, dtype=np.float64)
    diff = j[:, None] - j[None, :]
    intra = np.where(diff >= 0, np.exp(log_g[:, None, None] * np.maximum(diff, 0.0)), 0.0)
    q_dec = np.exp(log_g[:, None] * (j + 1.0))
    k_dec = np.exp(log_g[:, None] * (c - 1.0 - j))
    chunk_dec = np.exp(log_g * c)
    wide = lambda t: jnp.asarray(np.broadcast_to(t[:, :, None], (h, t.shape[1], HEAD_DIM)), F32)
    return jnp.asarray(intra, F32), wide(q_dec), wide(k_dec), wide(chunk_dec[:, None])


def _selection_tables(seq):
    tq, tk = NSA_TQ, NSA_TK
    n_cmp_rows = seq // CMP_STRIDE
    n_slc = seq // SLC_BLOCK
    assert tq == tk and n_slc <= AUG_ONE_LANE < LANES
    c_start = np.arange(n_cmp_rows) * CMP_STRIDE
    s_start = np.arange(n_slc) * SLC_BLOCK
    overlap_t = ((c_start[None, :] <= s_start[:, None] + SLC_BLOCK - 1)
                 & (c_start[None, :] + CMP_BLOCK - 1 >= s_start[:, None]))
    lane = np.arange(LANES)[None, :]
    key = np.arange(seq)[:, None]
    eaug = ((key // SLC_BLOCK) == lane) | (lane == AUG_ONE_LANE)
    cmask = lane * CMP_STRIDE + CMP_BLOCK - 1 <= key
    r = np.arange(tq)[:, None]
    c = np.arange(WINDOW + tq)[None, :]
    wmask = []
    for pat in range(WINDOW // tq + 1):
        d = min(pat * tq, WINDOW) + r - c
        wmask.append((d >= 0) & (d < WINDOW))
    tri = np.arange(tq)[:, None] >= np.arange(tk)[None, :]
    return (jnp.asarray(overlap_t, BF16), jnp.asarray(eaug, BF16), jnp.asarray(cmask, F32),
            jnp.asarray(np.stack(wmask), F32), jnp.asarray(tri, F32))


def _tile_modes(segments, tn):
    off, kinds, col = {}, [], 0
    for name, width, kind in segments:
        off[name] = col
        kinds += [kind] * (width // LANES)
        col += width
    per = tn // LANES
    tiles = [kinds[i:i + per] for i in range(0, len(kinds), per)]
    assert col % tn == 0
    modes = []
    for j, tile in enumerate(tiles):
        if modes and modes[-1][2] == tile:
            modes[-1] = (modes[-1][0], j + 1, tile)
        else:
            modes.append((j, j + 1, tile))
    return modes, off


def _l0_tile_modes():
    plain, silu = ("plain", 0), ("silu", 0)
    k_norm = ("nrope", L0_TAB_K)
    return _tile_modes([
        ("rq", RET_WIDTH, ("rope", L0_TAB_ROPE)), ("rk", RET_WIDTH, ("rope", L0_TAB_ROPE_SCALED)),
        ("rv", RET_WIDTH, plain), ("rgate", RET_WIDTH, silu), ("nq", NSA_WIDTH, ("nrope", L0_TAB_Q)),
        ("kc", NSA_KV_WIDTH, plain), ("vc", NSA_KV_WIDTH, plain), ("ks", NSA_KV_WIDTH, k_norm),
        ("vs", NSA_KV_WIDTH, plain), ("kw", NSA_KV_WIDTH, k_norm), ("vw", NSA_KV_WIDTH, plain),
        ("ngate", NSA_WIDTH, silu)], PROJ_TN)


def _l1_tile_modes():
    return _tile_modes([("q", DIFF_WIDTH, ("nrope", L1_TAB_Q)), ("k", DIFF_WIDTH, ("nrope", L1_TAB_K)),
                        ("v", DIFF_WIDTH, ("plain", 0)), ("gate", DIFF_WIDTH, ("silu", 0))], L1_PROJ_TN)[0]


def _rope_pair(cos, sin_signed, gain=None, scale=1.0):
    if gain is None:
        return [cos * scale, sin_signed * scale]
    return [cos * (gain * scale)[None, :], sin_signed * (jnp.roll(gain, HALF) * scale)[None, :]]


def kernel(x, l0_norm_g, l0_w_in, l0_w_out, l0_nsa_q_norm_g, l0_nsa_k_norm_g, l0_cmp_pe_k, l0_cmp_w1_k, l0_cmp_w2_k, l0_cmp_pe_v, l0_cmp_w1_v, l0_cmp_w2_v, l1_norm_g, l1_w_in, l1_w_out, l1_q_norm_g, l1_k_norm_g, l1_lambda_q1, l1_lambda_k1, l1_lambda_q2, l1_lambda_k2):
    b, s, d = x.shape
    m = b * s
    x2d = x.reshape(m, d)
    cos, sin = _rope_tables(np.arange(s))
    cos_c, sin_c = _rope_tables(np.arange(s // CMP_STRIDE) * CMP_STRIDE + CMP_BLOCK - 1)
    intra, qdec, kdec, cdec = _retention_tables()
    nsa_tables = _selection_tables(s)

    modes0, off = _l0_tile_modes()
    w0 = l0_w_in.astype(BF16)
    wg = l0_w_in[:, AB_MAIN_COLS:].reshape(d, 3, NSA_KV_HEADS, NSA_GROUP).transpose(0, 2, 1, 3)
    wg = wg.reshape(d, NSA_KV_HEADS, 3 * NSA_GROUP)
    wg = jnp.pad(wg, ((0, 0), (0, 0), (0, LANES - 3 * NSA_GROUP))).reshape(d, NSA_KV_HEADS * LANES).astype(BF16)
    tabs0 = jnp.stack(_rope_pair(cos, sin) + _rope_pair(cos, sin, scale=QK_SCALE)
                      + _rope_pair(cos, sin, l0_nsa_q_norm_g, Q_SCALE) + _rope_pair(cos, sin, l0_nsa_k_norm_g))
    proj0, gates = _norm_proj(x2d, l0_norm_g, w0, tabs0, modes0, s, PROJ_TN, wg=wg)
    proj0 = proj0.reshape(b, s, AB_MAIN_COLS)
    y_ret = _retention(proj0, intra, qdec, kdec, cdec)
    w1k = l0_cmp_w1_k.astype(BF16).reshape(CMP_BLOCK // 2, 2 * HEAD_DIM, HEAD_DIM)
    w1v = l0_cmp_w1_v.astype(BF16).reshape(CMP_BLOCK // 2, 2 * HEAD_DIM, HEAD_DIM)
    kcmp, vcmp = _nsa_compress(proj0, off["kc"] // HEAD_DIM, off["vc"] // HEAD_DIM,
                               l0_cmp_pe_k, w1k, l0_cmp_w2_k.astype(BF16),
                               l0_cmp_pe_v, w1v, l0_cmp_w2_v.astype(BF16),
                               l0_nsa_k_norm_g, cos_c, sin_c)
    bound = (BOUND_MARGIN * HEAD_DIM * Q_SCALE * jnp.max(jnp.abs(l0_nsa_q_norm_g))
             * jnp.max(jnp.abs(l0_nsa_k_norm_g))).reshape(1).astype(F32)
    y_nsa, w_out0, w_in1, w_out1 = _nsa_attention(proj0, gates, kcmp, vcmp, bound, nsa_tables, off,
                                                  cast_weights=[l0_w_out, l1_w_in, l1_w_out])
    x1 = _out_proj([y_ret.reshape(m, RET_WIDTH), y_nsa.reshape(m, NSA_WIDTH)],
                   [w_out0[:RET_WIDTH], w_out0[RET_WIDTH:]], x2d)

    lambda_init = 0.8 - 0.6 * math.exp(-0.3 * 1)
    tabs1 = jnp.stack(_rope_pair(cos, sin, l1_q_norm_g, Q_SCALE) + _rope_pair(cos, sin, l1_k_norm_g))
    proj1 = _norm_proj(x1, l1_norm_g, w_in1, tabs1, _l1_tile_modes(), s, L1_PROJ_TN)[0]
    lam_params = jnp.stack([l1_lambda_q1, l1_lambda_k1, l1_lambda_q2, l1_lambda_k2]).astype(F32)
    bound1 = (BOUND_MARGIN * HEAD_DIM * Q_SCALE * jnp.max(jnp.abs(l1_q_norm_g))
              * jnp.max(jnp.abs(l1_k_norm_g))).reshape(1).astype(F32)
    y1 = _diff_attention(proj1.reshape(b, s, C_IN_COLS), lam_params, bound1, lambda_init)
    out = _out_proj([y1.reshape(m, DIFF_WIDTH)], [w_out1], x1)
    return out.reshape(b, s, d)
```

```python
import functools
import math

import numpy as np
import jax
import jax.numpy as jnp
from jax import lax
from jax.experimental import pallas as pl
from jax.experimental.pallas import tpu as pltpu

F32 = jnp.float32
BF16 = jnp.bfloat16

HEAD_DIM = 128
HALF = HEAD_DIM // 2
ROPE_THETA = 10000.0
EPS = 1e-6
RET_HEADS = 8
RET_CHUNK = 256
RET_HEADS_PER_STEP = 4
NSA_HEADS = 8
NSA_KV_HEADS = 2
NSA_GROUP = NSA_HEADS // NSA_KV_HEADS
CMP_BLOCK = 32
CMP_STRIDE = 16
SLC_BLOCK = 64
SLC_TOPK = 16
N_LOCAL_BLOCKS = 2
WINDOW = 512
DIFF_HEADS = 8
DIFF_V_DIM = 2 * HEAD_DIM
QK_SCALE = HEAD_DIM ** -0.5
LOG2E = math.log2(math.e)
Q_SCALE = QK_SCALE * LOG2E

RET_WIDTH = RET_HEADS * HEAD_DIM
NSA_WIDTH = NSA_HEADS * HEAD_DIM
NSA_KV_WIDTH = NSA_KV_HEADS * HEAD_DIM
AB_MAIN_COLS = 4 * RET_WIDTH + 2 * NSA_WIDTH + 6 * NSA_KV_WIDTH
DIFF_WIDTH = DIFF_HEADS * DIFF_V_DIM
C_IN_COLS = 4 * DIFF_WIDTH

LANES = 128
MASKED = -1e30
M_INIT = -1e29
MAX_SCORE_BOUND = 60.0
BOUND_MARGIN = 1.05
AUG_ONE_LANE = 32
VMEM_LIMIT = 56 * 1024 * 1024

PROJ_TM = 1024
PROJ_TN = 1536
L1_PROJ_TN = 1024
OPROJ_TM = 512
OPROJ_TN = 2048
PROJ_SUB = 256
L0_TAB_ROPE, L0_TAB_ROPE_SCALED, L0_TAB_Q, L0_TAB_K = range(4)
L1_TAB_Q, L1_TAB_K = range(2)
NSA_TQ = 512
NSA_TK = 512
DIFF_TQ = 512
DIFF_TK = 512
DIFF_DIAG = 256


def _dot(a, b):
    return jnp.dot(a, b, preferred_element_type=F32)


def _dot_nt(a, b):
    return lax.dot_general(a, b, (((1,), (1,)), ((), ())), preferred_element_type=F32)


def _dot_tn(a, b):
    return lax.dot_general(a, b, (((0,), (0,)), ((), ())), preferred_element_type=F32)


def _silu(x):
    hx = 0.5 * x
    return hx + hx * jnp.tanh(hx)


def _rms(x):
    return x * lax.rsqrt(jnp.mean(x * x, axis=-1, keepdims=True) + EPS)


def _rope(x, cos, sin_signed):
    return x * cos + pltpu.roll(x, HALF, axis=1) * sin_signed


def _rope_tab(seg, t, tabs_ref, rows):
    return seg * tabs_ref[2 * t, rows, :] + pltpu.roll(seg, HALF, axis=1) * tabs_ref[2 * t + 1, rows, :]


def _tile_epilogue(acc, kinds, tabs_ref, rows, o_ref):
    n_seg = len(kinds)
    ri = lax.broadcasted_iota(jnp.int32, (2 * LANES, 2 * LANES), 0)
    ci = lax.broadcasted_iota(jnp.int32, (2 * LANES, 2 * LANES), 1)
    pair_mean = jnp.where((ri < LANES) == (ci < LANES), 1.0 / HEAD_DIM, 0.0).astype(BF16)
    c = 0
    while c < n_seg:
        op, t = kinds[c]
        cols = slice(c * LANES, (c + 1) * LANES)
        if op == "nrope" and c + 1 < n_seg and kinds[c + 1][0] == "nrope":
            cols2 = slice(c * LANES, (c + 2) * LANES)
            seg2 = acc[:, cols2]
            seg2 = seg2 * lax.rsqrt(_dot((seg2 * seg2).astype(BF16), pair_mean) + EPS)
            for half in range(2):
                out = _rope_tab(seg2[:, half * LANES:(half + 1) * LANES], kinds[c + half][1], tabs_ref, rows)
                o_ref[rows, (c + half) * LANES:(c + half + 1) * LANES] = out.astype(o_ref.dtype)
            c += 2
            continue
        seg = acc[:, cols]
        if op == "silu":
            seg = _silu(seg)
        elif op == "nrope":
            seg = seg * lax.rsqrt(_dot((seg * seg).astype(BF16), pair_mean[:LANES, :LANES]) + EPS)
            seg = _rope_tab(seg, t, tabs_ref, rows)
        elif op == "rope":
            seg = _rope_tab(seg, t, tabs_ref, rows)
        o_ref[rows, cols] = seg.astype(o_ref.dtype)
        c += 1


def _proj_kernel(*refs, tile_modes, has_gates):
    if has_gates:
        x_ref, g_ref, w_ref, tabs_ref, wg_ref, o_ref, og_ref, h_scr = refs
    else:
        x_ref, g_ref, w_ref, tabs_ref, o_ref, h_scr = refs
    j = pl.program_id(1)
    n_sub = h_scr.shape[0] // PROJ_SUB

    def make_branch(cond, kinds, first):
        @pl.when(cond)
        def _():
            for r in range(n_sub):
                rows = slice(r * PROJ_SUB, (r + 1) * PROJ_SUB)
                if first:
                    h = (_rms(x_ref[rows, :]) * g_ref[...]).astype(BF16)
                    h_scr[rows, :] = h
                    if has_gates:
                        og_ref[rows, :] = 1.0 / (1.0 + jnp.exp(-_dot(h, wg_ref[...])))
                else:
                    h = h_scr[rows, :]
                _tile_epilogue(_dot(h, w_ref[...]), kinds, tabs_ref, rows, o_ref)

    for lo, hi, kinds in tile_modes:
        if lo == 0:
            make_branch(j == 0, kinds, True)
            lo = 1
        if hi > lo:
            make_branch((j >= lo) & (j < hi), kinds, False)


def _norm_proj(x2d, g, w, tabs, tile_modes, seq, tn, wg=None):
    m, d = x2d.shape
    tm = PROJ_TM
    n = tile_modes[-1][1] * tn
    s_tiles = seq // tm
    has_gates = wg is not None
    in_specs = [
        pl.BlockSpec((tm, d), lambda i, j: (i, 0)),
        pl.BlockSpec((1, d), lambda i, j: (0, 0)),
        pl.BlockSpec((d, tn), lambda i, j: (0, j)),
        pl.BlockSpec((tabs.shape[0], tm, LANES), lambda i, j: (0, i % s_tiles, 0)),
    ]
    args = [x2d, g.reshape(1, d), w, tabs]
    out_shape = [jax.ShapeDtypeStruct((m, n), BF16)]
    out_specs = [pl.BlockSpec((tm, tn), lambda i, j: (i, j))]
    if has_gates:
        ng = wg.shape[1]
        in_specs.append(pl.BlockSpec((d, ng), lambda i, j: (0, 0)))
        args.append(wg)
        out_shape.append(jax.ShapeDtypeStruct((m, ng), F32))
        out_specs.append(pl.BlockSpec((tm, ng), lambda i, j: (i, 0)))
    return pl.pallas_call(
        functools.partial(_proj_kernel, tile_modes=tile_modes, has_gates=has_gates),
        grid=(m // tm, n // tn),
        in_specs=in_specs,
        out_specs=out_specs,
        out_shape=out_shape,
        scratch_shapes=[pltpu.VMEM((tm, d), BF16)],
        compiler_params=pltpu.CompilerParams(
            dimension_semantics=("parallel", "arbitrary"), vmem_limit_bytes=VMEM_LIMIT),
        name="norm_proj",
    )(*args)


def _oproj_kernel(*refs, n_terms):
    y_refs, w_refs = refs[:n_terms], refs[n_terms:2 * n_terms]
    x_ref, o_ref = refs[2 * n_terms:]
    acc = x_ref[...]
    for y_ref, w_ref in zip(y_refs, w_refs):
        acc = acc + _dot(y_ref[...], w_ref[...])
    o_ref[...] = acc


def _out_proj(ys, ws, x2d):
    m, n = x2d.shape
    tm, tn = OPROJ_TM, OPROJ_TN
    y_specs = [pl.BlockSpec((tm, y.shape[1]), lambda i, j: (i, 0)) for y in ys]
    w_specs = [pl.BlockSpec((w.shape[0], tn), lambda i, j: (0, j)) for w in ws]
    return pl.pallas_call(
        functools.partial(_oproj_kernel, n_terms=len(ys)),
        grid=(m // tm, n // tn),
        in_specs=y_specs + w_specs + [pl.BlockSpec((tm, tn), lambda i, j: (i, j))],
        out_specs=pl.BlockSpec((tm, tn), lambda i, j: (i, j)),
        out_shape=jax.ShapeDtypeStruct((m, n), F32),
        compiler_params=pltpu.CompilerParams(
            dimension_semantics=("parallel", "arbitrary"), vmem_limit_bytes=VMEM_LIMIT),
        name="out_proj",
    )(*ys, *ws, x2d)


def _ret_kernel(q_ref, k_ref, v_ref, gate_ref, intra_ref, qdec_ref, kdec_ref, cdec_ref, o_ref, *, n_chunks):
    c = RET_CHUNK
    states = [jnp.zeros((HEAD_DIM, HEAD_DIM), F32)] * RET_HEADS_PER_STEP
    for n in range(n_chunks):
        rows = slice(n * c, (n + 1) * c)
        for hh in range(RET_HEADS_PER_STEP):
            cols = slice(hh * HEAD_DIM, (hh + 1) * HEAD_DIM)
            q = q_ref[0, rows, cols]
            k = k_ref[0, rows, cols]
            v = v_ref[0, rows, cols]
            scores = _dot_nt(q, k) * intra_ref[hh]
            inner = _dot(scores.astype(BF16), v)
            cross = _dot((q.astype(F32) * qdec_ref[hh]).astype(BF16), states[hh].astype(BF16))
            kv = _dot_tn((k.astype(F32) * kdec_ref[hh]).astype(BF16), v)
            states[hh] = states[hh] * cdec_ref[hh] + kv
            o = _rms(inner + cross) * gate_ref[0, rows, cols].astype(F32)
            o_ref[0, rows, cols] = o.astype(o_ref.dtype)


def _retention(proj, intra, qdec, kdec, cdec):
    b, s, _ = proj.shape
    per = RET_HEADS_PER_STEP
    h = RET_HEADS // per
    y_shape = (b, s, RET_WIDTH)
    head = lambda off: pl.BlockSpec((1, s, per * HEAD_DIM), lambda bi, hi: (bi, 0, off + hi))
    table = lambda rows, cols=HEAD_DIM: pl.BlockSpec((per, rows, cols), lambda bi, hi: (hi, 0, 0))
    return pl.pallas_call(
        functools.partial(_ret_kernel, n_chunks=s // RET_CHUNK),
        grid=(b, h),
        in_specs=[head(0), head(h), head(2 * h), head(3 * h),
                  table(RET_CHUNK, RET_CHUNK), table(RET_CHUNK), table(RET_CHUNK), table(1)],
        out_specs=pl.BlockSpec((1, s, per * HEAD_DIM), lambda bi, hi: (bi, 0, hi)),
        out_shape=jax.ShapeDtypeStruct(y_shape, BF16),
        compiler_params=pltpu.CompilerParams(
            dimension_semantics=("parallel", "arbitrary"), vmem_limit_bytes=VMEM_LIMIT),
        name="retention",
    )(proj, proj, proj, proj, intra, qdec, kdec, cdec)


def _cmp_kernel(kc_ref, vc_ref, pek_ref, w1k_ref, w2k_ref, pev_ref, w1v_ref, w2v_ref, kg_ref,
                cos_ref, sin_ref, ko_ref, vo_ref, t_scr, *, seq):
    n_rows = seq // CMP_STRIDE

    def compress(src_ref, pe_ref, w1_ref, w2_ref):
        t_scr[0:seq, :] = src_ref[0].astype(F32)
        t_scr[seq:seq + CMP_STRIDE, :] = jnp.zeros((CMP_STRIDE, HEAD_DIM), F32)
        acc = jnp.zeros((n_rows, HEAD_DIM), F32)
        for r in range(0, CMP_BLOCK, 2):
            rows = [t_scr[pl.ds(r + i, n_rows, stride=CMP_STRIDE), :] + pe_ref[r + i:r + i + 1, :] for i in range(2)]
            acc = acc + _dot(jnp.concatenate(rows, axis=1).astype(BF16), w1_ref[r // 2])
        return _dot(_silu(acc).astype(BF16), w2_ref[...])

    kc = compress(kc_ref, pek_ref, w1k_ref, w2k_ref)
    kc = _rope(_rms(kc) * kg_ref[...], cos_ref[...], sin_ref[...])
    ko_ref[0, 0] = kc.astype(ko_ref.dtype)
    vo_ref[0, 0] = compress(vc_ref, pev_ref, w1v_ref, w2v_ref).astype(vo_ref.dtype)


def _nsa_compress(proj, kc_off, vc_off, pe_k, w1_k, w2_k, pe_v, w1_v, w2_v, k_g, cos_c, sin_c):
    b, s, _ = proj.shape
    g = NSA_KV_HEADS
    n_rows = s // CMP_STRIDE
    head = lambda off: pl.BlockSpec((1, s, HEAD_DIM), lambda bi, gi: (bi, 0, off + gi))
    full = lambda shape: pl.BlockSpec(shape, lambda bi, gi: (0,) * len(shape))
    out_spec = pl.BlockSpec((1, 1, n_rows, HEAD_DIM), lambda bi, gi: (bi, gi, 0, 0))
    out_sds = jax.ShapeDtypeStruct((b, g, n_rows, HEAD_DIM), BF16)
    return pl.pallas_call(
        functools.partial(_cmp_kernel, seq=s),
        grid=(b, g),
        in_specs=[head(kc_off), head(vc_off),
                  full((CMP_BLOCK, HEAD_DIM)), full(w1_k.shape), full((HEAD_DIM, HEAD_DIM)),
                  full((CMP_BLOCK, HEAD_DIM)), full(w1_v.shape), full((HEAD_DIM, HEAD_DIM)),
                  full((1, HEAD_DIM)), full((n_rows, HEAD_DIM)), full((n_rows, HEAD_DIM))],
        out_specs=[out_spec, out_spec],
        out_shape=[out_sds, out_sds],
        scratch_shapes=[pltpu.VMEM((s + CMP_STRIDE, HEAD_DIM), F32)],
        compiler_params=pltpu.CompilerParams(
            dimension_semantics=("parallel", "arbitrary"), vmem_limit_bytes=VMEM_LIMIT),
        name="nsa_compress",
    )(proj, proj, pe_k, w1_k, w2_k, pe_v, w1_v, w2_v, k_g.reshape(1, HEAD_DIM), cos_c, sin_c)


def _flash_init(m_scr, l_scr, acc_scr):
    m_scr[...] = jnp.full(m_scr.shape, M_INIT, F32)
    l_scr[...] = jnp.zeros(l_scr.shape, F32)
    acc_scr[...] = jnp.zeros(acc_scr.shape, F32)


def _lane_tiles(x, width):
    return x if width == LANES else jnp.concatenate([x] * (width // LANES), axis=1)


def _flash_step(q, k, v, bias, m_scr, l_scr, acc_scr):
    s = _dot_nt(q, k)
    if bias is not None:
        s = s + bias
    tk = s.shape[1]
    m_old = m_scr[...]
    m_new = jnp.maximum(m_old, jnp.max(s, axis=-1, keepdims=True))
    alpha = jnp.exp2(m_old - m_new)
    p = jnp.exp2(s - _lane_tiles(m_new, tk))
    p_cols = p[:, 0:LANES]
    for c in range(1, tk // LANES):
        p_cols = p_cols + p[:, c * LANES:(c + 1) * LANES]
    l_scr[...] = alpha * l_scr[...] + p_cols
    acc_scr[...] = _lane_tiles(alpha, acc_scr.shape[1]) * acc_scr[...] + _dot(p.astype(BF16), v)
    m_scr[...] = m_new


def _flash_finish(l_scr, acc_scr):
    return acc_scr[...] * (1.0 / jnp.sum(l_scr[...], axis=-1, keepdims=True))


def _rep_heads(bias):
    return jnp.concatenate([bias] * NSA_GROUP, axis=0)


def _stack_heads(qblk):
    return jnp.concatenate([qblk[:, r * HEAD_DIM:(r + 1) * HEAD_DIM] for r in range(NSA_GROUP)], axis=0)


def _select_blocks(p, ovl, q0, tq, n_slc):
    psum = p[0:tq]
    for r in range(1, NSA_GROUP):
        psum = psum + p[r * tq:(r + 1) * tq]
    p_hi = psum.astype(BF16)
    rem = psum - p_hi.astype(F32)
    p_mid = rem.astype(BF16)
    p_lo = (rem - p_mid.astype(F32)).astype(BF16)
    imp = _dot_nt(ovl, p_hi) + _dot_nt(ovl, p_mid) + _dot_nt(ovl, p_lo)
    jb = lax.broadcasted_iota(jnp.int32, (n_slc, tq), 0)
    blk_t = jnp.right_shift(q0 + lax.broadcasted_iota(jnp.int32, (n_slc, tq), 1), int(math.log2(SLC_BLOCK)))
    back = blk_t - jb
    forced = (jb == 0) | ((back >= 0) & (back < N_LOCAL_BLOCKS))
    score = jnp.where(forced, 1e9, jnp.where(back >= 0, imp, -1e9))
    rank = jnp.zeros((n_slc, tq), F32)
    for mp in range(n_slc):
        row = score[mp:mp + 1, :]
        ahead = (row > score) | ((row == score) & (jb > mp))
        rank = rank + jnp.where(ahead, 1.0, 0.0)
    sel_t = jnp.where(rank < float(min(SLC_TOPK, n_slc)), 1.0, 0.0)
    sel_t = jnp.concatenate([sel_t, jnp.zeros((LANES - n_slc, tq), F32)], axis=0).astype(BF16)
    ri = lax.broadcasted_iota(jnp.int32, (tq, tq), 0)
    ci = lax.broadcasted_iota(jnp.int32, (tq, tq), 1)
    eye = jnp.where(ri == ci, 1.0, 0.0).astype(BF16)
    return _dot_nt(eye, sel_t)


def _nsa_combine(o_cmp, o_slc, o_win, gates_ref, ngate_ref, o_ref, tq):
    gates = gates_ref[...]
    for r in range(NSA_GROUP):
        rows = slice(r * tq, (r + 1) * tq)
        g_cmp = gates[:, r:r + 1]
        g_slc = gates[:, NSA_GROUP + r:NSA_GROUP + r + 1]
        g_win = gates[:, 2 * NSA_GROUP + r:2 * NSA_GROUP + r + 1]
        y = g_cmp * o_cmp[rows] + g_slc * o_slc[rows] + g_win * o_win[rows]
        cols = slice(r * HEAD_DIM, (r + 1) * HEAD_DIM)
        o_ref[0, :, cols] = (y * ngate_ref[0, :, cols].astype(F32)).astype(o_ref.dtype)


def _nsa_general(q4, q0, qi, gates_ref, ngate_ref, kcmp_ref, vcmp_ref, ks_ref, vs_ref, kw_ref, vw_ref,
                 ovl_ref, eaug_ref, o_ref, bias_scr, m_scr, l_scr, acc_scr, *, seq):
    tq, tk = NSA_TQ, NSA_TK
    n_slc = seq // SLC_BLOCK

    tpos = q0 + lax.broadcasted_iota(jnp.int32, (tq, LANES), 0)
    cidx = lax.broadcasted_iota(jnp.int32, (tq, LANES), 1)
    cbias = jnp.where(cidx * CMP_STRIDE + (CMP_BLOCK - 1) <= tpos, 0.0, MASKED)
    s = _dot_nt(q4, kcmp_ref[0, 0]) + _rep_heads(cbias)
    m = jnp.maximum(jnp.max(s, axis=-1, keepdims=True), M_INIT)
    e = jnp.exp2(s - m)
    l = jnp.sum(e, axis=-1, keepdims=True)
    p = e * (1.0 / jnp.maximum(l, 1e-30))
    o_cmp = _dot(p.astype(BF16), vcmp_ref[0, 0])

    sel = _select_blocks(p, ovl_ref[...], q0, tq, n_slc).astype(BF16)
    sel_keys = _dot_nt(sel, eaug_ref[...])
    for kt in range(seq // tk):
        bias_scr[kt] = jnp.where(sel_keys[:, kt * tk:(kt + 1) * tk] > 0.5, 0.0, MASKED)

    _flash_init(m_scr, l_scr, acc_scr)

    def slc_body(kt, carry):
        k0 = pl.multiple_of(kt * tk, tk)
        _flash_step(q4, ks_ref[0, pl.ds(k0, tk), :], vs_ref[0, pl.ds(k0, tk), :],
                    _rep_heads(bias_scr[kt]), m_scr, l_scr, acc_scr)
        return carry

    n_full = lax.div(q0, tk)
    lax.fori_loop(0, n_full, slc_body, 0)
    kd = pl.multiple_of(n_full * tk, tk)
    qpos = q0 + lax.broadcasted_iota(jnp.int32, (tq, tk), 0)
    kpos = kd + lax.broadcasted_iota(jnp.int32, (tq, tk), 1)
    causal = jnp.where(kpos <= qpos, 0.0, MASKED)
    _flash_step(q4, ks_ref[0, pl.ds(kd, tk), :], vs_ref[0, pl.ds(kd, tk), :],
                _rep_heads(bias_scr[n_full] + causal), m_scr, l_scr, acc_scr)
    o_slc = _flash_finish(l_scr, acc_scr)

    wk = WINDOW + tq
    ws = pl.multiple_of(jnp.maximum(q0 - WINDOW, 0), tq)
    d = (q0 - ws) + lax.broadcasted_iota(jnp.int32, (tq, wk), 0) - lax.broadcasted_iota(jnp.int32, (tq, wk), 1)
    wbias = jnp.where((d >= 0) & (d < WINDOW), 0.0, MASKED)
    s = _dot_nt(q4, kw_ref[0, pl.ds(ws, wk), :]) + _rep_heads(wbias)
    e = jnp.exp2(s - jnp.max(s, axis=-1, keepdims=True))
    o_win = _dot(e.astype(BF16), vw_ref[0, pl.ds(ws, wk), :]) * (1.0 / jnp.sum(e, axis=-1, keepdims=True))

    _nsa_combine(o_cmp, o_slc, o_win, gates_ref, ngate_ref, o_ref, tq)


def _nsa_bounded(q4, q0, qi, bound, gates_ref, ngate_ref, kcmp_ref, vcmp_ref, ks_ref, vs_ref, kw_ref, vw_ref,
                 ovl_ref, eaug_ref, cmask_ref, wmask_ref, tri_ref, o_ref,
                 ksa_scr, kwa_scr, kca_scr, l_scr, acc_scr, *, seq):
    tq, tk = NSA_TQ, NSA_TK
    n_slc = seq // SLC_BLOCK
    n_cmp_rows = kca_scr.shape[0]

    @pl.when(qi == 0)
    def _():
        one_col = eaug_ref[...]
        lane = lax.broadcasted_iota(jnp.int32, one_col.shape, 1)
        ksa_scr[:, :HEAD_DIM] = ks_ref[0]
        ksa_scr[:, HEAD_DIM:] = one_col
        one_col = jnp.where(lane == AUG_ONE_LANE, one_col, jnp.zeros_like(one_col))
        kwa_scr[:, :HEAD_DIM] = kw_ref[0]
        kwa_scr[:, HEAD_DIM:] = one_col
        kca_scr[:, :HEAD_DIM] = kcmp_ref[0, 0]
        kca_scr[:, HEAD_DIM:] = one_col[:n_cmp_rows]

    lane = lax.broadcasted_iota(jnp.int32, (tq, LANES), 1)
    shift_cols = jnp.where(lane == AUG_ONE_LANE, -bound, 0.0)
    qa = jnp.concatenate([q4, _rep_heads(shift_cols.astype(BF16))], axis=1)

    e = jnp.exp2(_dot_nt(qa, kca_scr[...])) * _rep_heads(cmask_ref[...])
    l = jnp.sum(e, axis=-1, keepdims=True)
    p = e * jnp.where(l > 0.0, 1.0 / l, 0.0)
    o_cmp = _dot(p.astype(BF16), vcmp_ref[0, 0])

    sel = _select_blocks(p, ovl_ref[...], q0, tq, n_slc)
    sel_cols = jnp.where(lane < n_slc, jnp.where(sel > 0.5, 0.0, MASKED), shift_cols)
    qs = jnp.concatenate([q4, _rep_heads(sel_cols.astype(BF16))], axis=1)

    l_scr[...] = jnp.zeros(l_scr.shape, F32)
    acc_scr[...] = jnp.zeros(acc_scr.shape, F32)

    def accumulate(k0, width, mask):
        p_t = jnp.exp2(_dot_nt(qs, ksa_scr[pl.ds(k0, width), :]))
        if mask is not None:
            p_t = p_t * mask
        l_scr[...] += _lane_tile_sum(p_t)
        acc_scr[...] += _dot(p_t.astype(BF16), vs_ref[0, pl.ds(k0, width), :])

    big = 2 * tk

    def slc_body(kt, carry):
        accumulate(pl.multiple_of(kt * big, big), big, None)
        return carry

    n_full = lax.div(q0, tk)
    n_big = lax.div(n_full, 2)
    lax.fori_loop(0, n_big, slc_body, 0)

    @pl.when(n_full > 2 * n_big)
    def _():
        accumulate(pl.multiple_of(n_big * big, big), tk, None)

    accumulate(pl.multiple_of(n_full * tk, tk), tk, _rep_heads(tri_ref[...]))
    o_slc = _flash_finish(l_scr, acc_scr)

    wk = WINDOW + tq
    ws = pl.multiple_of(jnp.maximum(q0 - WINDOW, 0), tq)
    e = jnp.exp2(_dot_nt(qa, kwa_scr[pl.ds(ws, wk), :])) * _rep_heads(wmask_ref[0])
    o_win = _dot(e.astype(BF16), vw_ref[0, pl.ds(ws, wk), :]) * (1.0 / jnp.sum(e, axis=-1, keepdims=True))

    _nsa_combine(o_cmp, o_slc, o_win, gates_ref, ngate_ref, o_ref, tq)


def _nsa_kernel(bound_ref, q_ref, ngate_ref, gates_ref, kcmp_ref, vcmp_ref, ks_ref, vs_ref, kw_ref, vw_ref,
                ovl_ref, eaug_ref, cmask_ref, wmask_ref, tri_ref, *rest, seq, n_casts):
    cast_in, (o_ref, *cast_out) = rest[:n_casts], rest[n_casts:2 * n_casts + 1]
    bias_scr, m_scr, l_scr, acc_scr, ksa_scr, kwa_scr, kca_scr = rest[2 * n_casts + 1:]
    for src, dst in zip(cast_in, cast_out):
        dst[...] = src[...].astype(dst.dtype)
    qi = pl.program_id(2)
    q0 = qi * NSA_TQ
    q4 = _stack_heads(q_ref[0])
    bound = bound_ref[0]

    @pl.when(bound <= MAX_SCORE_BOUND)
    def _():
        _nsa_bounded(q4, q0, qi, bound, gates_ref, ngate_ref, kcmp_ref, vcmp_ref, ks_ref, vs_ref, kw_ref, vw_ref,
                     ovl_ref, eaug_ref, cmask_ref, wmask_ref, tri_ref, o_ref,
                     ksa_scr, kwa_scr, kca_scr, l_scr, acc_scr, seq=seq)

    @pl.when(bound > MAX_SCORE_BOUND)
    def _():
        _nsa_general(q4, q0, qi, gates_ref, ngate_ref, kcmp_ref, vcmp_ref, ks_ref, vs_ref, kw_ref, vw_ref,
                     ovl_ref, eaug_ref, o_ref, bias_scr, m_scr, l_scr, acc_scr, seq=seq)


def _nsa_attention(proj, gates, kcmp, vcmp, bound, tables, offs, cast_weights):
    b, s, _ = proj.shape
    g = NSA_KV_HEADS
    tq = NSA_TQ
    nq = s // tq
    gw = NSA_GROUP * HEAD_DIM
    n_cmp_rows = s // CMP_STRIDE
    ovl_t, eaug, cmask, wmask, tri = tables
    q_spec = lambda off: pl.BlockSpec((1, tq, gw), lambda bi, gi, qi: (bi, qi, off + gi))
    kv_spec = lambda off: pl.BlockSpec((1, s, HEAD_DIM), lambda bi, gi, qi: (bi, 0, off + gi))
    cmp_spec = pl.BlockSpec((1, 1, n_cmp_rows, HEAD_DIM), lambda bi, gi, qi: (bi, gi, 0, 0))
    full = lambda shape: pl.BlockSpec(shape, lambda bi, gi, qi: (0,) * len(shape))
    n_wpat = wmask.shape[0]
    rows4 = NSA_GROUP * tq
    n_steps = b * g * nq
    cast_specs = [pl.BlockSpec((w.shape[0] // n_steps, w.shape[1]), lambda bi, gi, qi: ((bi * g + gi) * nq + qi, 0))
                  for w in cast_weights]
    return pl.pallas_call(
        functools.partial(_nsa_kernel, seq=s, n_casts=len(cast_weights)),
        grid=(b, g, nq),
        in_specs=[pl.BlockSpec(memory_space=pltpu.SMEM),
                  q_spec(offs["nq"] // gw), q_spec(offs["ngate"] // gw),
                  pl.BlockSpec((tq, LANES), lambda bi, gi, qi: (bi * nq + qi, gi)),
                  cmp_spec, cmp_spec,
                  kv_spec(offs["ks"] // HEAD_DIM), kv_spec(offs["vs"] // HEAD_DIM),
                  kv_spec(offs["kw"] // HEAD_DIM), kv_spec(offs["vw"] // HEAD_DIM),
                  full(ovl_t.shape), full(eaug.shape),
                  pl.BlockSpec((tq, LANES), lambda bi, gi, qi: (qi, 0)),
                  pl.BlockSpec((1,) + wmask.shape[1:], lambda bi, gi, qi: (jnp.minimum(qi, n_wpat - 1), 0, 0)),
                  full(tri.shape)] + cast_specs,
        out_specs=[pl.BlockSpec((1, tq, gw), lambda bi, gi, qi: (bi, qi, gi))] + cast_specs,
        out_shape=[jax.ShapeDtypeStruct((b, s, NSA_WIDTH), BF16)]
        + [jax.ShapeDtypeStruct(w.shape, BF16) for w in cast_weights],
        scratch_shapes=[pltpu.VMEM((s // NSA_TK, tq, NSA_TK), F32),
                        pltpu.VMEM((rows4, LANES), F32), pltpu.VMEM((rows4, LANES), F32),
                        pltpu.VMEM((rows4, HEAD_DIM), F32),
                        pltpu.VMEM((s, 2 * HEAD_DIM), BF16), pltpu.VMEM((s, 2 * HEAD_DIM), BF16),
                        pltpu.VMEM((n_cmp_rows, 2 * HEAD_DIM), BF16)],
        compiler_params=pltpu.CompilerParams(
            dimension_semantics=("parallel", "parallel", "arbitrary"), vmem_limit_bytes=VMEM_LIMIT),
        name="nsa_attention",
    )(bound, proj, proj, gates, kcmp, vcmp, proj, proj, proj, proj, ovl_t, eaug, cmask, wmask, tri, *cast_weights)


def _diff_general(q1, q2, qi, k_ref, v_ref, m1, l1, a1, m2, l2, a2):
    tq, tk = DIFF_TQ, DIFF_TK
    _flash_init(m1, l1, a1)
    _flash_init(m2, l2, a2)

    def step(rows, k0, width, bias):
        k = k_ref[0, pl.ds(k0, width), :]
        v = v_ref[0, pl.ds(k0, width), :]
        _flash_step(q1[rows], k[:, :HEAD_DIM], v, bias, m1.at[rows], l1.at[rows], a1.at[rows])
        _flash_step(q2[rows], k[:, HEAD_DIM:], v, bias, m2.at[rows], l2.at[rows], a2.at[rows])

    def body(kt, carry):
        step(slice(0, tq), pl.multiple_of(kt * tk, tk), tk, None)
        return carry

    lax.fori_loop(0, qi * (tq // tk), body, 0)
    dd = DIFF_DIAG
    q0 = qi * tq
    for c in range(tq // dd):
        n_rows = tq - c * dd
        ri = lax.broadcasted_iota(jnp.int32, (n_rows, dd), 0)
        ci = lax.broadcasted_iota(jnp.int32, (n_rows, dd), 1)
        step(slice(c * dd, tq), pl.multiple_of(q0 + c * dd, dd), dd, jnp.where(ri >= ci, 0.0, MASKED))
    return _flash_finish(l1, a1), _flash_finish(l2, a2)


def _lane_tile_sum(p):
    cols = p[:, 0:LANES]
    for c in range(1, p.shape[1] // LANES):
        cols = cols + p[:, c * LANES:(c + 1) * LANES]
    return cols


def _diff_bounded_stream(q, n, bound, k_ref, v_ref, tri_ref, kcols):
    tq, dd = DIFF_TQ, DIFF_DIAG
    kmain = n * tq + dd
    p = jnp.exp2(_dot_nt(q, k_ref[0, 0:kmain, kcols]) - bound)
    p_diag = p[:, kmain - dd:] * tri_ref[...]
    p = p_diag if kmain == dd else jnp.concatenate([p[:, :kmain - dd], p_diag], axis=1)
    l = _lane_tile_sum(p)
    acc = _dot(p.astype(BF16), v_ref[0, 0:kmain, :])
    pc = jnp.exp2(_dot_nt(q[dd:], k_ref[0, kmain:kmain + dd, kcols]) - bound) * tri_ref[0:tq - dd, :]
    l = jnp.concatenate([l[:dd], l[dd:] + _lane_tile_sum(pc)], axis=0)
    acc = jnp.concatenate([acc[:dd], acc[dd:] + _dot(pc.astype(BF16), v_ref[0, kmain:kmain + dd, :])], axis=0)
    return acc * (1.0 / jnp.sum(l, axis=-1, keepdims=True))


def _diff_kernel(bound_ref, q_ref, gate_ref, k_ref, v_ref, lam_ref, tri_ref, o_ref, m1, l1, a1, m2, l2, a2,
                 *, lambda_init, n_tiles):
    qi = pl.program_id(2)
    q = q_ref[0]
    q1 = q[:, :HEAD_DIM]
    q2 = q[:, HEAD_DIM:]
    bound = bound_ref[0]

    def finish(o1, o2):
        lp = lam_ref[...]
        lam = (jnp.exp(jnp.sum(lp[0:1] * lp[1:2], axis=-1, keepdims=True))
               - jnp.exp(jnp.sum(lp[2:3] * lp[3:4], axis=-1, keepdims=True)) + lambda_init)
        o = _rms(o1 - lam * o2) * (1.0 - lambda_init)
        o_ref[0] = (o * gate_ref[0].astype(F32)).astype(o_ref.dtype)

    def bounded_variant(n):
        @pl.when((bound <= MAX_SCORE_BOUND) & (qi == n))
        def _():
            finish(_diff_bounded_stream(q1, n, bound, k_ref, v_ref, tri_ref, slice(0, HEAD_DIM)),
                   _diff_bounded_stream(q2, n, bound, k_ref, v_ref, tri_ref, slice(HEAD_DIM, 2 * HEAD_DIM)))

    for n in range(n_tiles):
        bounded_variant(n)

    @pl.when(bound > MAX_SCORE_BOUND)
    def _():
        finish(*_diff_general(q1, q2, qi, k_ref, v_ref, m1, l1, a1, m2, l2, a2))


def _diff_attention(proj, lam_params, bound, lambda_init):
    b, s, _ = proj.shape
    h = DIFF_HEADS
    tq, dd = DIFF_TQ, DIFF_DIAG
    assert tq == 2 * dd
    w = DIFF_V_DIM
    tri = jnp.asarray(np.arange(tq)[:, None] >= np.arange(dd)[None, :], F32)
    q_spec = lambda off: pl.BlockSpec((1, tq, w), lambda bi, hi, qi: (bi, qi, off + hi))
    kv_spec = lambda off: pl.BlockSpec((1, s, w), lambda bi, hi, qi: (bi, 0, off + hi))
    stat = pltpu.VMEM((tq, LANES), F32)
    acc = pltpu.VMEM((tq, w), F32)
    return pl.pallas_call(
        functools.partial(_diff_kernel, lambda_init=lambda_init, n_tiles=s // tq),
        grid=(b, h, s // tq),
        in_specs=[pl.BlockSpec(memory_space=pltpu.SMEM),
                  q_spec(0), q_spec(3 * h), kv_spec(h), kv_spec(2 * h),
                  pl.BlockSpec(lam_params.shape, lambda bi, hi, qi: (0, 0)),
                  pl.BlockSpec(tri.shape, lambda bi, hi, qi: (0, 0))],
        out_specs=pl.BlockSpec((1, tq, w), lambda bi, hi, qi: (bi, qi, hi)),
        out_shape=jax.ShapeDtypeStruct((b, s, DIFF_WIDTH), BF16),
        scratch_shapes=[stat, stat, acc, stat, stat, acc],
        compiler_params=pltpu.CompilerParams(
            dimension_semantics=("parallel", "parallel", "arbitrary"), vmem_limit_bytes=VMEM_LIMIT),
        name="diff_attention",
    )(bound, proj, proj, proj, proj, lam_params, tri)


def _rope_tables(pos):
    inv = 1.0 / (ROPE_THETA ** (np.arange(0, HEAD_DIM, 2, dtype=np.float64) / HEAD_DIM))
    ang = np.asarray(pos, np.float64)[:, None] * inv[None, :]
    cos, sin = np.cos(ang), np.sin(ang)
    return (jnp.asarray(np.concatenate([cos, cos], axis=-1), F32),
            jnp.asarray(np.concatenate([-sin, sin], axis=-1), F32))


def _retention_tables():
    h, c = RET_HEADS, RET_CHUNK
    log_g = np.log1p(-np.exp2(-5.0 - np.arange(h, dtype=np.float64)))
    j = np.arange(c, dtype=np.float64)
    diff = j[:, None] - j[None, :]
    intra = np.where(diff >= 0, np.exp(log_g[:, None, None] * np.maximum(diff, 0.0)), 0.0)
    q_dec = np.exp(log_g[:, None] * (j + 1.0))
    k_dec = np.exp(log_g[:, None] * (c - 1.0 - j))
    chunk_dec = np.exp(log_g * c)
    wide = lambda t: jnp.asarray(np.broadcast_to(t[:, :, None], (h, t.shape[1], HEAD_DIM)), F32)
    return jnp.asarray(intra, F32), wide(q_dec), wide(k_dec), wide(chunk_dec[:, None])


def _selection_tables(seq):
    tq, tk = NSA_TQ, NSA_TK
    n_cmp_rows = seq // CMP_STRIDE
    n_slc = seq // SLC_BLOCK
    assert tq == tk and n_slc <= AUG_ONE_LANE < LANES
    c_start = np.arange(n_cmp_rows) * CMP_STRIDE
    s_start = np.arange(n_slc) * SLC_BLOCK
    overlap_t = ((c_start[None, :] <= s_start[:, None] + SLC_BLOCK - 1)
                 & (c_start[None, :] + CMP_BLOCK - 1 >= s_start[:, None]))
    lane = np.arange(LANES)[None, :]
    key = np.arange(seq)[:, None]
    eaug = ((key // SLC_BLOCK) == lane) | (lane == AUG_ONE_LANE)
    cmask = lane * CMP_STRIDE + CMP_BLOCK - 1 <= key
    r = np.arange(tq)[:, None]
    c = np.arange(WINDOW + tq)[None, :]
    wmask = []
    for pat in range(WINDOW // tq + 1):
        d = min(pat * tq, WINDOW) + r - c
        wmask.append((d >= 0) & (d < WINDOW))
    tri = np.arange(tq)[:, None] >= np.arange(tk)[None, :]
    return (jnp.asarray(overlap_t, BF16), jnp.asarray(eaug, BF16), jnp.asarray(cmask, F32),
            jnp.asarray(np.stack(wmask), F32), jnp.asarray(tri, F32))


def _tile_modes(segments, tn):
    off, kinds, col = {}, [], 0
    for name, width, kind in segments:
        off[name] = col
        kinds += [kind] * (width // LANES)
        col += width
    per = tn // LANES
    tiles = [kinds[i:i + per] for i in range(0, len(kinds), per)]
    assert col % tn == 0
    modes = []
    for j, tile in enumerate(tiles):
        if modes and modes[-1][2] == tile:
            modes[-1] = (modes[-1][0], j + 1, tile)
        else:
            modes.append((j, j + 1, tile))
    return modes, off


def _l0_tile_modes():
    plain, silu = ("plain", 0), ("silu", 0)
    k_norm = ("nrope", L0_TAB_K)
    return _tile_modes([
        ("rq", RET_WIDTH, ("rope", L0_TAB_ROPE)), ("rk", RET_WIDTH, ("rope", L0_TAB_ROPE_SCALED)),
        ("rv", RET_WIDTH, plain), ("rgate", RET_WIDTH, silu), ("nq", NSA_WIDTH, ("nrope", L0_TAB_Q)),
        ("kc", NSA_KV_WIDTH, plain), ("vc", NSA_KV_WIDTH, plain), ("ks", NSA_KV_WIDTH, k_norm),
        ("vs", NSA_KV_WIDTH, plain), ("kw", NSA_KV_WIDTH, k_norm), ("vw", NSA_KV_WIDTH, plain),
        ("ngate", NSA_WIDTH, silu)], PROJ_TN)


def _l1_tile_modes():
    return _tile_modes([("q", DIFF_WIDTH, ("nrope", L1_TAB_Q)), ("k", DIFF_WIDTH, ("nrope", L1_TAB_K)),
                        ("v", DIFF_WIDTH, ("plain", 0)), ("gate", DIFF_WIDTH, ("silu", 0))], L1_PROJ_TN)[0]


def _rope_pair(cos, sin_signed, gain=None, scale=1.0):
    if gain is None:
        return [cos * scale, sin_signed * scale]
    return [cos * (gain * scale)[None, :], sin_signed * (jnp.roll(gain, HALF) * scale)[None, :]]


def kernel(x, l0_norm_g, l0_w_in, l0_w_out, l0_nsa_q_norm_g, l0_nsa_k_norm_g, l0_cmp_pe_k, l0_cmp_w1_k, l0_cmp_w2_k, l0_cmp_pe_v, l0_cmp_w1_v, l0_cmp_w2_v, l1_norm_g, l1_w_in, l1_w_out, l1_q_norm_g, l1_k_norm_g, l1_lambda_q1, l1_lambda_k1, l1_lambda_q2, l1_lambda_k2):
    b, s, d = x.shape
    m = b * s
    x2d = x.reshape(m, d)
    cos, sin = _rope_tables(np.arange(s))
    cos_c, sin_c = _rope_tables(np.arange(s // CMP_STRIDE) * CMP_STRIDE + CMP_BLOCK - 1)
    intra, qdec, kdec, cdec = _retention_tables()
    nsa_tables = _selection_tables(s)

    modes0, off = _l0_tile_modes()
    w0 = l0_w_in.astype(BF16)
    wg = l0_w_in[:, AB_MAIN_COLS:].reshape(d, 3, NSA_KV_HEADS, NSA_GROUP).transpose(0, 2, 1, 3)
    wg = wg.reshape(d, NSA_KV_HEADS, 3 * NSA_GROUP)
    wg = jnp.pad(wg, ((0, 0), (0, 0), (0, LANES - 3 * NSA_GROUP))).reshape(d, NSA_KV_HEADS * LANES).astype(BF16)
    tabs0 = jnp.stack(_rope_pair(cos, sin) + _rope_pair(cos, sin, scale=QK_SCALE)
                      + _rope_pair(cos, sin, l0_nsa_q_norm_g, Q_SCALE) + _rope_pair(cos, sin, l0_nsa_k_norm_g))
    proj0, gates = _norm_proj(x2d, l0_norm_g, w0, tabs0, modes0, s, PROJ_TN, wg=wg)
    proj0 = proj0.reshape(b, s, AB_MAIN_COLS)
    y_ret = _retention(proj0, intra, qdec, kdec, cdec)
    w1k = l0_cmp_w1_k.astype(BF16).reshape(CMP_BLOCK // 2, 2 * HEAD_DIM, HEAD_DIM)
    w1v = l0_cmp_w1_v.astype(BF16).reshape(CMP_BLOCK // 2, 2 * HEAD_DIM, HEAD_DIM)
    kcmp, vcmp = _nsa_compress(proj0, off["kc"] // HEAD_DIM, off["vc"] // HEAD_DIM,
                               l0_cmp_pe_k, w1k, l0_cmp_w2_k.astype(BF16),
                               l0_cmp_pe_v, w1v, l0_cmp_w2_v.astype(BF16),
                               l0_nsa_k_norm_g, cos_c, sin_c)
    bound = (BOUND_MARGIN * HEAD_DIM * Q_SCALE * jnp.max(jnp.abs(l0_nsa_q_norm_g))
             * jnp.max(jnp.abs(l0_nsa_k_norm_g))).reshape(1).astype(F32)
    y_nsa, w_out0, w_in1, w_out1 = _nsa_attention(proj0, gates, kcmp, vcmp, bound, nsa_tables, off,
                                                  cast_weights=[l0_w_out, l1_w_in, l1_w_out])
    x1 = _out_proj([y_ret.reshape(m, RET_WIDTH), y_nsa.reshape(m, NSA_WIDTH)],
                   [w_out0[:RET_WIDTH], w_out0[RET_WIDTH:]], x2d)

    lambda_init = 0.8 - 0.6 * math.exp(-0.3 * 1)
    tabs1 = jnp.stack(_rope_pair(cos, sin, l1_q_norm_g, Q_SCALE) + _rope_pair(cos, sin, l1_k_norm_g))
    proj1 = _norm_proj(x1, l1_norm_g, w_in1, tabs1, _l1_tile_modes(), s, L1_PROJ_TN)[0]
    lam_params = jnp.stack([l1_lambda_q1, l1_lambda_k1, l1_lambda_q2, l1_lambda_k2]).astype(F32)
    bound1 = (BOUND_MARGIN * HEAD_DIM * Q_SCALE * jnp.max(jnp.abs(l1_q_norm_g))
              * jnp.max(jnp.abs(l1_k_norm_g))).reshape(1).astype(F32)
    y1 = _diff_attention(proj1.reshape(b, s, C_IN_COLS), lam_params, bound1, lambda_init)
    out = _out_proj([y1.reshape(m, DIFF_WIDTH)], [w_out1], x1)
    return out.reshape(b, s, d)
```

```python
import functools
import math

import numpy as np
import jax
import jax.numpy as jnp
from jax import lax
from jax.experimental import pallas as pl
from jax.experimental.pallas import tpu as pltpu

F32 = jnp.float32
BF16 = jnp.bfloat16

HEAD_DIM = 128
HALF = HEAD_DIM // 2
ROPE_THETA = 10000.0
EPS = 1e-6
RET_HEADS = 8
RET_CHUNK = 256
RET_HEADS_PER_STEP = 4
NSA_HEADS = 8
NSA_KV_HEADS = 2
NSA_GROUP = NSA_HEADS // NSA_KV_HEADS
CMP_BLOCK = 32
CMP_STRIDE = 16
SLC_BLOCK = 64
SLC_TOPK = 16
N_LOCAL_BLOCKS = 2
WINDOW = 512
DIFF_HEADS = 8
DIFF_V_DIM = 2 * HEAD_DIM
QK_SCALE = HEAD_DIM ** -0.5
LOG2E = math.log2(math.e)
Q_SCALE = QK_SCALE * LOG2E

RET_WIDTH = RET_HEADS * HEAD_DIM
NSA_WIDTH = NSA_HEADS * HEAD_DIM
NSA_KV_WIDTH = NSA_KV_HEADS * HEAD_DIM
AB_MAIN_COLS = 4 * RET_WIDTH + 2 * NSA_WIDTH + 6 * NSA_KV_WIDTH
DIFF_WIDTH = DIFF_HEADS * DIFF_V_DIM
C_IN_COLS = 4 * DIFF_WIDTH

LANES = 128
MASKED = -1e30
M_INIT = -1e29
MAX_SCORE_BOUND = 60.0
BOUND_MARGIN = 1.05
AUG_ONE_LANE = 32
VMEM_LIMIT = 56 * 1024 * 1024

PROJ_TM = 1024
PROJ_TN = 1536
L1_PROJ_TN = 1024
OPROJ_TM = 512
OPROJ_TN = 2048
PROJ_SUB = 256
L0_TAB_ROPE, L0_TAB_ROPE_SCALED, L0_TAB_Q, L0_TAB_K = range(4)
L1_TAB_Q, L1_TAB_K = range(2)
NSA_TQ = 512
NSA_TK = 512
DIFF_TQ = 512
DIFF_TK = 512
DIFF_DIAG = 256


def _dot(a, b):
    return jnp.dot(a, b, preferred_element_type=F32)


def _dot_nt(a, b):
    return lax.dot_general(a, b, (((1,), (1,)), ((), ())), preferred_element_type=F32)


def _dot_tn(a, b):
    return lax.dot_general(a, b, (((0,), (0,)), ((), ())), preferred_element_type=F32)


def _silu(x):
    hx = 0.5 * x
    return hx + hx * jnp.tanh(hx)


def _rms(x):
    return x * lax.rsqrt(jnp.mean(x * x, axis=-1, keepdims=True) + EPS)


def _rope(x, cos, sin_signed):
    return x * cos + pltpu.roll(x, HALF, axis=1) * sin_signed


def _rope_tab(seg, t, tabs_ref, rows):
    return seg * tabs_ref[2 * t, rows, :] + pltpu.roll(seg, HALF, axis=1) * tabs_ref[2 * t + 1, rows, :]


def _tile_epilogue(acc, kinds, tabs_ref, rows, o_ref):
    n_seg = len(kinds)
    ri = lax.broadcasted_iota(jnp.int32, (2 * LANES, 2 * LANES), 0)
    ci = lax.broadcasted_iota(jnp.int32, (2 * LANES, 2 * LANES), 1)
    pair_mean = jnp.where((ri < LANES) == (ci < LANES), 1.0 / HEAD_DIM, 0.0).astype(BF16)
    c = 0
    while c < n_seg:
        op, t = kinds[c]
        cols = slice(c * LANES, (c + 1) * LANES)
        if op == "nrope" and c + 1 < n_seg and kinds[c + 1][0] == "nrope":
            cols2 = slice(c * LANES, (c + 2) * LANES)
            seg2 = acc[:, cols2]
            seg2 = seg2 * lax.rsqrt(_dot((seg2 * seg2).astype(BF16), pair_mean) + EPS)
            for half in range(2):
                out = _rope_tab(seg2[:, half * LANES:(half + 1) * LANES], kinds[c + half][1], tabs_ref, rows)
                o_ref[rows, (c + half) * LANES:(c + half + 1) * LANES] = out.astype(o_ref.dtype)
            c += 2
            continue
        seg = acc[:, cols]
        if op == "silu":
            seg = _silu(seg)
        elif op == "nrope":
            seg = seg * lax.rsqrt(_dot((seg * seg).astype(BF16), pair_mean[:LANES, :LANES]) + EPS)
            seg = _rope_tab(seg, t, tabs_ref, rows)
        elif op == "rope":
            seg = _rope_tab(seg, t, tabs_ref, rows)
        o_ref[rows, cols] = seg.astype(o_ref.dtype)
        c += 1


def _proj_kernel(*refs, tile_modes, has_gates):
    if has_gates:
        x_ref, g_ref, w_ref, tabs_ref, wg_ref, o_ref, og_ref, h_scr = refs
    else:
        x_ref, g_ref, w_ref, tabs_ref, o_ref, h_scr = refs
    j = pl.program_id(1)
    n_sub = h_scr.shape[0] // PROJ_SUB

    def make_branch(cond, kinds, first):
        @pl.when(cond)
        def _():
            for r in range(n_sub):
                rows = slice(r * PROJ_SUB, (r + 1) * PROJ_SUB)
                if first:
                    h = (_rms(x_ref[rows, :]) * g_ref[...]).astype(BF16)
                    h_scr[rows, :] = h
                    if has_gates:
                        og_ref[rows, :] = 1.0 / (1.0 + jnp.exp(-_dot(h, wg_ref[...])))
                else:
                    h = h_scr[rows, :]
                _tile_epilogue(_dot(h, w_ref[...]), kinds, tabs_ref, rows, o_ref)

    for lo, hi, kinds in tile_modes:
        if lo == 0:
            make_branch(j == 0, kinds, True)
            lo = 1
        if hi > lo:
            make_branch((j >= lo) & (j < hi), kinds, False)


def _norm_proj(x2d, g, w, tabs, tile_modes, seq, tn, wg=None):
    m, d = x2d.shape
    tm = PROJ_TM
    n = tile_modes[-1][1] * tn
    s_tiles = seq // tm
    has_gates = wg is not None
    in_specs = [
        pl.BlockSpec((tm, d), lambda i, j: (i, 0)),
        pl.BlockSpec((1, d), lambda i, j: (0, 0)),
        pl.BlockSpec((d, tn), lambda i, j: (0, j)),
        pl.BlockSpec((tabs.shape[0], tm, LANES), lambda i, j: (0, i % s_tiles, 0)),
    ]
    args = [x2d, g.reshape(1, d), w, tabs]
    out_shape = [jax.ShapeDtypeStruct((m, n), BF16)]
    out_specs = [pl.BlockSpec((tm, tn), lambda i, j: (i, j))]
    if has_gates:
        ng = wg.shape[1]
        in_specs.append(pl.BlockSpec((d, ng), lambda i, j: (0, 0)))
        args.append(wg)
        out_shape.append(jax.ShapeDtypeStruct((m, ng), F32))
        out_specs.append(pl.BlockSpec((tm, ng), lambda i, j: (i, 0)))
    return pl.pallas_call(
        functools.partial(_proj_kernel, tile_modes=tile_modes, has_gates=has_gates),
        grid=(m // tm, n // tn),
        in_specs=in_specs,
        out_specs=out_specs,
        out_shape=out_shape,
        scratch_shapes=[pltpu.VMEM((tm, d), BF16)],
        compiler_params=pltpu.CompilerParams(
            dimension_semantics=("parallel", "arbitrary"), vmem_limit_bytes=VMEM_LIMIT),
        name="norm_proj",
    )(*args)


def _oproj_kernel(*refs, n_terms):
    y_refs, w_refs = refs[:n_terms], refs[n_terms:2 * n_terms]
    x_ref, o_ref = refs[2 * n_terms:]
    acc = x_ref[...]
    for y_ref, w_ref in zip(y_refs, w_refs):
        acc = acc + _dot(y_ref[...], w_ref[...])
    o_ref[...] = acc


def _out_proj(ys, ws, x2d):
    m, n = x2d.shape
    tm, tn = OPROJ_TM, OPROJ_TN
    y_specs = [pl.BlockSpec((tm, y.shape[1]), lambda i, j: (i, 0)) for y in ys]
    w_specs = [pl.BlockSpec((w.shape[0], tn), lambda i, j: (0, j)) for w in ws]
    return pl.pallas_call(
        functools.partial(_oproj_kernel, n_terms=len(ys)),
        grid=(m // tm, n // tn),
        in_specs=y_specs + w_specs + [pl.BlockSpec((tm, tn), lambda i, j: (i, j))],
        out_specs=pl.BlockSpec((tm, tn), lambda i, j: (i, j)),
        out_shape=jax.ShapeDtypeStruct((m, n), F32),
        compiler_params=pltpu.CompilerParams(
            dimension_semantics=("parallel", "arbitrary"), vmem_limit_bytes=VMEM_LIMIT),
        name="out_proj",
    )(*ys, *ws, x2d)


def _ret_kernel(q_ref, k_ref, v_ref, gate_ref, intra_ref, qdec_ref, kdec_ref, cdec_ref, o_ref, *, n_chunks):
    c = RET_CHUNK
    states = [jnp.zeros((HEAD_DIM, HEAD_DIM), F32)] * RET_HEADS_PER_STEP
    for n in range(n_chunks):
        rows = slice(n * c, (n + 1) * c)
        for hh in range(RET_HEADS_PER_STEP):
            cols = slice(hh * HEAD_DIM, (hh + 1) * HEAD_DIM)
            q = q_ref[0, rows, cols]
            k = k_ref[0, rows, cols]
            v = v_ref[0, rows, cols]
            scores = _dot_nt(q, k) * intra_ref[hh]
            inner = _dot(scores.astype(BF16), v)
            cross = _dot((q.astype(F32) * qdec_ref[hh]).astype(BF16), states[hh].astype(BF16))
            kv = _dot_tn((k.astype(F32) * kdec_ref[hh]).astype(BF16), v)
            states[hh] = states[hh] * cdec_ref[hh] + kv
            o = _rms(inner + cross) * gate_ref[0, rows, cols].astype(F32)
            o_ref[0, rows, cols] = o.astype(o_ref.dtype)


def _retention(proj, intra, qdec, kdec, cdec):
    b, s, _ = proj.shape
    per = RET_HEADS_PER_STEP
    h = RET_HEADS // per
    y_shape = (b, s, RET_WIDTH)
    head = lambda off: pl.BlockSpec((1, s, per * HEAD_DIM), lambda bi, hi: (bi, 0, off + hi))
    table = lambda rows, cols=HEAD_DIM: pl.BlockSpec((per, rows, cols), lambda bi, hi: (hi, 0, 0))
    return pl.pallas_call(
        functools.partial(_ret_kernel, n_chunks=s // RET_CHUNK),
        grid=(b, h),
        in_specs=[head(0), head(h), head(2 * h), head(3 * h),
                  table(RET_CHUNK, RET_CHUNK), table(RET_CHUNK), table(RET_CHUNK), table(1)],
        out_specs=pl.BlockSpec((1, s, per * HEAD_DIM), lambda bi, hi: (bi, 0, hi)),
        out_shape=jax.ShapeDtypeStruct(y_shape, BF16),
        compiler_params=pltpu.CompilerParams(
            dimension_semantics=("parallel", "arbitrary"), vmem_limit_bytes=VMEM_LIMIT),
        name="retention",
    )(proj, proj, proj, proj, intra, qdec, kdec, cdec)


def _cmp_kernel(kc_ref, vc_ref, pek_ref, w1k_ref, w2k_ref, pev_ref, w1v_ref, w2v_ref, kg_ref,
                cos_ref, sin_ref, ko_ref, vo_ref, t_scr, *, seq):
    n_rows = seq // CMP_STRIDE

    def compress(src_ref, pe_ref, w1_ref, w2_ref):
        t_scr[0:seq, :] = src_ref[0].astype(F32)
        t_scr[seq:seq + CMP_STRIDE, :] = jnp.zeros((CMP_STRIDE, HEAD_DIM), F32)
        acc = jnp.zeros((n_rows, HEAD_DIM), F32)
        for r in range(0, CMP_BLOCK, 2):
            rows = [t_scr[pl.ds(r + i, n_rows, stride=CMP_STRIDE), :] + pe_ref[r + i:r + i + 1, :] for i in range(2)]
            acc = acc + _dot(jnp.concatenate(rows, axis=1).astype(BF16), w1_ref[r // 2])
        return _dot(_silu(acc).astype(BF16), w2_ref[...])

    kc = compress(kc_ref, pek_ref, w1k_ref, w2k_ref)
    kc = _rope(_rms(kc) * kg_ref[...], cos_ref[...], sin_ref[...])
    ko_ref[0, 0] = kc.astype(ko_ref.dtype)
    vo_ref[0, 0] = compress(vc_ref, pev_ref, w1v_ref, w2v_ref).astype(vo_ref.dtype)


def _nsa_compress(proj, kc_off, vc_off, pe_k, w1_k, w2_k, pe_v, w1_v, w2_v, k_g, cos_c, sin_c):
    b, s, _ = proj.shape
    g = NSA_KV_HEADS
    n_rows = s // CMP_STRIDE
    head = lambda off: pl.BlockSpec((1, s, HEAD_DIM), lambda bi, gi: (bi, 0, off + gi))
    full = lambda shape: pl.BlockSpec(shape, lambda bi, gi: (0,) * len(shape))
    out_spec = pl.BlockSpec((1, 1, n_rows, HEAD_DIM), lambda bi, gi: (bi, gi, 0, 0))
    out_sds = jax.ShapeDtypeStruct((b, g, n_rows, HEAD_DIM), BF16)
    return pl.pallas_call(
        functools.partial(_cmp_kernel, seq=s),
        grid=(b, g),
        in_specs=[head(kc_off), head(vc_off),
                  full((CMP_BLOCK, HEAD_DIM)), full(w1_k.shape), full((HEAD_DIM, HEAD_DIM)),
                  full((CMP_BLOCK, HEAD_DIM)), full(w1_v.shape), full((HEAD_DIM, HEAD_DIM)),
                  full((1, HEAD_DIM)), full((n_rows, HEAD_DIM)), full((n_rows, HEAD_DIM))],
        out_specs=[out_spec, out_spec],
        out_shape=[out_sds, out_sds],
        scratch_shapes=[pltpu.VMEM((s + CMP_STRIDE, HEAD_DIM), F32)],
        compiler_params=pltpu.CompilerParams(
            dimension_semantics=("parallel", "arbitrary"), vmem_limit_bytes=VMEM_LIMIT),
        name="nsa_compress",
    )(proj, proj, pe_k, w1_k, w2_k, pe_v, w1_v, w2_v, k_g.reshape(1, HEAD_DIM), cos_c, sin_c)


def _flash_init(m_scr, l_scr, acc_scr):
    m_scr[...] = jnp.full(m_scr.shape, M_INIT, F32)
    l_scr[...] = jnp.zeros(l_scr.shape, F32)
    acc_scr[...] = jnp.zeros(acc_scr.shape, F32)


def _lane_tiles(x, width):
    return x if width == LANES else jnp.concatenate([x] * (width // LANES), axis=1)


def _flash_step(q, k, v, bias, m_scr, l_scr, acc_scr):
    s = _dot_nt(q, k)
    if bias is not None:
        s = s + bias
    tk = s.shape[1]
    m_old = m_scr[...]
    m_new = jnp.maximum(m_old, jnp.max(s, axis=-1, keepdims=True))
    alpha = jnp.exp2(m_old - m_new)
    p = jnp.exp2(s - _lane_tiles(m_new, tk))
    p_cols = p[:, 0:LANES]
    for c in range(1, tk // LANES):
        p_cols = p_cols + p[:, c * LANES:(c + 1) * LANES]
    l_scr[...] = alpha * l_scr[...] + p_cols
    acc_scr[...] = _lane_tiles(alpha, acc_scr.shape[1]) * acc_scr[...] + _dot(p.astype(BF16), v)
    m_scr[...] = m_new


def _flash_finish(l_scr, acc_scr):
    return acc_scr[...] * (1.0 / jnp.sum(l_scr[...], axis=-1, keepdims=True))


def _rep_heads(bias):
    return jnp.concatenate([bias] * NSA_GROUP, axis=0)


def _stack_heads(qblk):
    return jnp.concatenate([qblk[:, r * HEAD_DIM:(r + 1) * HEAD_DIM] for r in range(NSA_GROUP)], axis=0)


def _select_blocks(p, ovl, q0, tq, n_slc):
    psum = p[0:tq]
    for r in range(1, NSA_GROUP):
        psum = psum + p[r * tq:(r + 1) * tq]
    p_hi = psum.astype(BF16)
    rem = psum - p_hi.astype(F32)
    p_mid = rem.astype(BF16)
    p_lo = (rem - p_mid.astype(F32)).astype(BF16)
    imp = _dot_nt(ovl, p_hi) + _dot_nt(ovl, p_mid) + _dot_nt(ovl, p_lo)
    jb = lax.broadcasted_iota(jnp.int32, (n_slc, tq), 0)
    blk_t = jnp.right_shift(q0 + lax.broadcasted_iota(jnp.int32, (n_slc, tq), 1), int(math.log2(SLC_BLOCK)))
    back = blk_t - jb
    forced = (jb == 0) | ((back >= 0) & (back < N_LOCAL_BLOCKS))
    score = jnp.where(forced, 1e9, jnp.where(back >= 0, imp, -1e9))
    rank = jnp.zeros((n_slc, tq), F32)
    for mp in range(n_slc):
        row = score[mp:mp + 1, :]
        ahead = (row > score) | ((row == score) & (jb > mp))
        rank = rank + jnp.where(ahead, 1.0, 0.0)
    sel_t = jnp.where(rank < float(min(SLC_TOPK, n_slc)), 1.0, 0.0)
    sel_t = jnp.concatenate([sel_t, jnp.zeros((LANES - n_slc, tq), F32)], axis=0).astype(BF16)
    ri = lax.broadcasted_iota(jnp.int32, (tq, tq), 0)
    ci = lax.broadcasted_iota(jnp.int32, (tq, tq), 1)
    eye = jnp.where(ri == ci, 1.0, 0.0).astype(BF16)
    return _dot_nt(eye, sel_t)


def _nsa_combine(o_cmp, o_slc, o_win, gates_ref, ngate_ref, o_ref, tq):
    gates = gates_ref[...]
    for r in range(NSA_GROUP):
        rows = slice(r * tq, (r + 1) * tq)
        g_cmp = gates[:, r:r + 1]
        g_slc = gates[:, NSA_GROUP + r:NSA_GROUP + r + 1]
        g_win = gates[:, 2 * NSA_GROUP + r:2 * NSA_GROUP + r + 1]
        y = g_cmp * o_cmp[rows] + g_slc * o_slc[rows] + g_win * o_win[rows]
        cols = slice(r * HEAD_DIM, (r + 1) * HEAD_DIM)
        o_ref[0, :, cols] = (y * ngate_ref[0, :, cols].astype(F32)).astype(o_ref.dtype)


def _nsa_general(q4, q0, qi, gates_ref, ngate_ref, kcmp_ref, vcmp_ref, ks_ref, vs_ref, kw_ref, vw_ref,
                 ovl_ref, eaug_ref, o_ref, bias_scr, m_scr, l_scr, acc_scr, *, seq):
    tq, tk = NSA_TQ, NSA_TK
    n_slc = seq // SLC_BLOCK

    tpos = q0 + lax.broadcasted_iota(jnp.int32, (tq, LANES), 0)
    cidx = lax.broadcasted_iota(jnp.int32, (tq, LANES), 1)
    cbias = jnp.where(cidx * CMP_STRIDE + (CMP_BLOCK - 1) <= tpos, 0.0, MASKED)
    s = _dot_nt(q4, kcmp_ref[0, 0]) + _rep_heads(cbias)
    m = jnp.maximum(jnp.max(s, axis=-1, keepdims=True), M_INIT)
    e = jnp.exp2(s - m)
    l = jnp.sum(e, axis=-1, keepdims=True)
    p = e * (1.0 / jnp.maximum(l, 1e-30))
    o_cmp = _dot(p.astype(BF16), vcmp_ref[0, 0])

    sel = _select_blocks(p, ovl_ref[...], q0, tq, n_slc).astype(BF16)
    sel_keys = _dot_nt(sel, eaug_ref[...])
    for kt in range(seq // tk):
        bias_scr[kt] = jnp.where(sel_keys[:, kt * tk:(kt + 1) * tk] > 0.5, 0.0, MASKED)

    _flash_init(m_scr, l_scr, acc_scr)

    def slc_body(kt, carry):
        k0 = pl.multiple_of(kt * tk, tk)
        _flash_step(q4, ks_ref[0, pl.ds(k0, tk), :], vs_ref[0, pl.ds(k0, tk), :],
                    _rep_heads(bias_scr[kt]), m_scr, l_scr, acc_scr)
        return carry

    n_full = lax.div(q0, tk)
    lax.fori_loop(0, n_full, slc_body, 0)
    kd = pl.multiple_of(n_full * tk, tk)
    qpos = q0 + lax.broadcasted_iota(jnp.int32, (tq, tk), 0)
    kpos = kd + lax.broadcasted_iota(jnp.int32, (tq, tk), 1)
    causal = jnp.where(kpos <= qpos, 0.0, MASKED)
    _flash_step(q4, ks_ref[0, pl.ds(kd, tk), :], vs_ref[0, pl.ds(kd, tk), :],
                _rep_heads(bias_scr[n_full] + causal), m_scr, l_scr, acc_scr)
    o_slc = _flash_finish(l_scr, acc_scr)

    wk = WINDOW + tq
    ws = pl.multiple_of(jnp.maximum(q0 - WINDOW, 0), tq)
    d = (q0 - ws) + lax.broadcasted_iota(jnp.int32, (tq, wk), 0) - lax.broadcasted_iota(jnp.int32, (tq, wk), 1)
    wbias = jnp.where((d >= 0) & (d < WINDOW), 0.0, MASKED)
    s = _dot_nt(q4, kw_ref[0, pl.ds(ws, wk), :]) + _rep_heads(wbias)
    e = jnp.exp2(s - jnp.max(s, axis=-1, keepdims=True))
    o_win = _dot(e.astype(BF16), vw_ref[0, pl.ds(ws, wk), :]) * (1.0 / jnp.sum(e, axis=-1, keepdims=True))

    _nsa_combine(o_cmp, o_slc, o_win, gates_ref, ngate_ref, o_ref, tq)


def _nsa_bounded(q4, q0, qi, bound, gates_ref, ngate_ref, kcmp_ref, vcmp_ref, ks_ref, vs_ref, kw_ref, vw_ref,
                 ovl_ref, eaug_ref, cmask_ref, wmask_ref, tri_ref, o_ref,
                 ksa_scr, kwa_scr, kca_scr, l_scr, acc_scr, *, seq):
    tq, tk = NSA_TQ, NSA_TK
    n_slc = seq // SLC_BLOCK
    n_cmp_rows = kca_scr.shape[0]

    @pl.when(qi == 0)
    def _():
        one_col = eaug_ref[...]
        lane = lax.broadcasted_iota(jnp.int32, one_col.shape, 1)
        ksa_scr[:, :HEAD_DIM] = ks_ref[0]
        ksa_scr[:, HEAD_DIM:] = one_col
        one_col = jnp.where(lane == AUG_ONE_LANE, one_col, jnp.zeros_like(one_col))
        kwa_scr[:, :HEAD_DIM] = kw_ref[0]
        kwa_scr[:, HEAD_DIM:] = one_col
        kca_scr[:, :HEAD_DIM] = kcmp_ref[0, 0]
        kca_scr[:, HEAD_DIM:] = one_col[:n_cmp_rows]

    lane = lax.broadcasted_iota(jnp.int32, (tq, LANES), 1)
    shift_cols = jnp.where(lane == AUG_ONE_LANE, -bound, 0.0)
    qa = jnp.concatenate([q4, _rep_heads(shift_cols.astype(BF16))], axis=1)

    e = jnp.exp2(_dot_nt(qa, kca_scr[...])) * _rep_heads(cmask_ref[...])
    l = jnp.sum(e, axis=-1, keepdims=True)
    p = e * jnp.where(l > 0.0, 1.0 / l, 0.0)
    o_cmp = _dot(p.astype(BF16), vcmp_ref[0, 0])

    sel = _select_blocks(p, ovl_ref[...], q0, tq, n_slc)
    sel_cols = jnp.where(lane < n_slc, jnp.where(sel > 0.5, 0.0, MASKED), shift_cols)
    qs = jnp.concatenate([q4, _rep_heads(sel_cols.astype(BF16))], axis=1)

    l_scr[...] = jnp.zeros(l_scr.shape, F32)
    acc_scr[...] = jnp.zeros(acc_scr.shape, F32)

    def accumulate(k0, width, mask):
        p_t = jnp.exp2(_dot_nt(qs, ksa_scr[pl.ds(k0, width), :]))
        if mask is not None:
            p_t = p_t * mask
        l_scr[...] += _lane_tile_sum(p_t)
        acc_scr[...] += _dot(p_t.astype(BF16), vs_ref[0, pl.ds(k0, width), :])

    big = 2 * tk

    def slc_body(kt, carry):
        accumulate(pl.multiple_of(kt * big, big), big, None)
        return carry

    n_full = lax.div(q0, tk)
    n_big = lax.div(n_full, 2)
    lax.fori_loop(0, n_big, slc_body, 0)

    @pl.when(n_full > 2 * n_big)
    def _():
        accumulate(pl.multiple_of(n_big * big, big), tk, None)

    accumulate(pl.multiple_of(n_full * tk, tk), tk, _rep_heads(tri_ref[...]))
    o_slc = _flash_finish(l_scr, acc_scr)

    wk = WINDOW + tq
    ws = pl.multiple_of(jnp.maximum(q0 - WINDOW, 0), tq)
    e = jnp.exp2(_dot_nt(qa, kwa_scr[pl.ds(ws, wk), :])) * _rep_heads(wmask_ref[0])
    o_win = _dot(e.astype(BF16), vw_ref[0, pl.ds(ws, wk), :]) * (1.0 / jnp.sum(e, axis=-1, keepdims=True))

    _nsa_combine(o_cmp, o_slc, o_win, gates_ref, ngate_ref, o_ref, tq)


def _nsa_kernel(bound_ref, q_ref, ngate_ref, gates_ref, kcmp_ref, vcmp_ref, ks_ref, vs_ref, kw_ref, vw_ref,
                ovl_ref, eaug_ref, cmask_ref, wmask_ref, tri_ref, *rest, seq, n_casts):
    cast_in, (o_ref, *cast_out) = rest[:n_casts], rest[n_casts:2 * n_casts + 1]
    bias_scr, m_scr, l_scr, acc_scr, ksa_scr, kwa_scr, kca_scr = rest[2 * n_casts + 1:]
    for src, dst in zip(cast_in, cast_out):
        dst[...] = src[...].astype(dst.dtype)
    qi = pl.program_id(2)
    q0 = qi * NSA_TQ
    q4 = _stack_heads(q_ref[0])
    bound = bound_ref[0]

    @pl.when(bound <= MAX_SCORE_BOUND)
    def _():
        _nsa_bounded(q4, q0, qi, bound, gates_ref, ngate_ref, kcmp_ref, vcmp_ref, ks_ref, vs_ref, kw_ref, vw_ref,
                     ovl_ref, eaug_ref, cmask_ref, wmask_ref, tri_ref, o_ref,
                     ksa_scr, kwa_scr, kca_scr, l_scr, acc_scr, seq=seq)

    @pl.when(bound > MAX_SCORE_BOUND)
    def _():
        _nsa_general(q4, q0, qi, gates_ref, ngate_ref, kcmp_ref, vcmp_ref, ks_ref, vs_ref, kw_ref, vw_ref,
                     ovl_ref, eaug_ref, o_ref, bias_scr, m_scr, l_scr, acc_scr, seq=seq)


def _nsa_attention(proj, gates, kcmp, vcmp, bound, tables, offs, cast_weights):
    b, s, _ = proj.shape
    g = NSA_KV_HEADS
    tq = NSA_TQ
    nq = s // tq
    gw = NSA_GROUP * HEAD_DIM
    n_cmp_rows = s // CMP_STRIDE
    ovl_t, eaug, cmask, wmask, tri = tables
    q_spec = lambda off: pl.BlockSpec((1, tq, gw), lambda bi, gi, qi: (bi, qi, off + gi))
    kv_spec = lambda off: pl.BlockSpec((1, s, HEAD_DIM), lambda bi, gi, qi: (bi, 0, off + gi))
    cmp_spec = pl.BlockSpec((1, 1, n_cmp_rows, HEAD_DIM), lambda bi, gi, qi: (bi, gi, 0, 0))
    full = lambda shape: pl.BlockSpec(shape, lambda bi, gi, qi: (0,) * len(shape))
    n_wpat = wmask.shape[0]
    rows4 = NSA_GROUP * tq
    n_steps = b * g * nq
    cast_specs = [pl.BlockSpec((w.shape[0] // n_steps, w.shape[1]), lambda bi, gi, qi: ((bi * g + gi) * nq + qi, 0))
                  for w in cast_weights]
    return pl.pallas_call(
        functools.partial(_nsa_kernel, seq=s, n_casts=len(cast_weights)),
        grid=(b, g, nq),
        in_specs=[pl.BlockSpec(memory_space=pltpu.SMEM),
                  q_spec(offs["nq"] // gw), q_spec(offs["ngate"] // gw),
                  pl.BlockSpec((tq, LANES), lambda bi, gi, qi: (bi * nq + qi, gi)),
                  cmp_spec, cmp_spec,
                  kv_spec(offs["ks"] // HEAD_DIM), kv_spec(offs["vs"] // HEAD_DIM),
                  kv_spec(offs["kw"] // HEAD_DIM), kv_spec(offs["vw"] // HEAD_DIM),
                  full(ovl_t.shape), full(eaug.shape),
                  pl.BlockSpec((tq, LANES), lambda bi, gi, qi: (qi, 0)),
                  pl.BlockSpec((1,) + wmask.shape[1:], lambda bi, gi, qi: (jnp.minimum(qi, n_wpat - 1), 0, 0)),
                  full(tri.shape)] + cast_specs,
        out_specs=[pl.BlockSpec((1, tq, gw), lambda bi, gi, qi: (bi, qi, gi))] + cast_specs,
        out_shape=[jax.ShapeDtypeStruct((b, s, NSA_WIDTH), BF16)]
        + [jax.ShapeDtypeStruct(w.shape, BF16) for w in cast_weights],
        scratch_shapes=[pltpu.VMEM((s // NSA_TK, tq, NSA_TK), F32),
                        pltpu.VMEM((rows4, LANES), F32), pltpu.VMEM((rows4, LANES), F32),
                        pltpu.VMEM((rows4, HEAD_DIM), F32),
                        pltpu.VMEM((s, 2 * HEAD_DIM), BF16), pltpu.VMEM((s, 2 * HEAD_DIM), BF16),
                        pltpu.VMEM((n_cmp_rows, 2 * HEAD_DIM), BF16)],
        compiler_params=pltpu.CompilerParams(
            dimension_semantics=("parallel", "parallel", "arbitrary"), vmem_limit_bytes=VMEM_LIMIT),
        name="nsa_attention",
    )(bound, proj, proj, gates, kcmp, vcmp, proj, proj, proj, proj, ovl_t, eaug, cmask, wmask, tri, *cast_weights)


def _diff_general(q1, q2, qi, k_ref, v_ref, m1, l1, a1, m2, l2, a2):
    tq, tk = DIFF_TQ, DIFF_TK
    _flash_init(m1, l1, a1)
    _flash_init(m2, l2, a2)

    def step(rows, k0, width, bias):
        k = k_ref[0, pl.ds(k0, width), :]
        v = v_ref[0, pl.ds(k0, width), :]
        _flash_step(q1[rows], k[:, :HEAD_DIM], v, bias, m1.at[rows], l1.at[rows], a1.at[rows])
        _flash_step(q2[rows], k[:, HEAD_DIM:], v, bias, m2.at[rows], l2.at[rows], a2.at[rows])

    def body(kt, carry):
        step(slice(0, tq), pl.multiple_of(kt * tk, tk), tk, None)
        return carry

    lax.fori_loop(0, qi * (tq // tk), body, 0)
    dd = DIFF_DIAG
    q0 = qi * tq
    for c in range(tq // dd):
        n_rows = tq - c * dd
        ri = lax.broadcasted_iota(jnp.int32, (n_rows, dd), 0)
        ci = lax.broadcasted_iota(jnp.int32, (n_rows, dd), 1)
        step(slice(c * dd, tq), pl.multiple_of(q0 + c * dd, dd), dd, jnp.where(ri >= ci, 0.0, MASKED))
    return _flash_finish(l1, a1), _flash_finish(l2, a2)


def _lane_tile_sum(p):
    cols = p[:, 0:LANES]
    for c in range(1, p.shape[1] // LANES):
        cols = cols + p[:, c * LANES:(c + 1) * LANES]
    return cols


def _diff_bounded_stream(q, n, bound, k_ref, v_ref, tri_ref, kcols):
    tq, dd = DIFF_TQ, DIFF_DIAG
    kmain = n * tq + dd
    p = jnp.exp2(_dot_nt(q, k_ref[0, 0:kmain, kcols]) - bound)
    p_diag = p[:, kmain - dd:] * tri_ref[...]
    p = p_diag if kmain == dd else jnp.concatenate([p[:, :kmain - dd], p_diag], axis=1)
    l = _lane_tile_sum(p)
    acc = _dot(p.astype(BF16), v_ref[0, 0:kmain, :])
    pc = jnp.exp2(_dot_nt(q[dd:], k_ref[0, kmain:kmain + dd, kcols]) - bound) * tri_ref[0:tq - dd, :]
    l = jnp.concatenate([l[:dd], l[dd:] + _lane_tile_sum(pc)], axis=0)
    acc = jnp.concatenate([acc[:dd], acc[dd:] + _dot(pc.astype(BF16), v_ref[0, kmain:kmain + dd, :])], axis=0)
    return acc * (1.0 / jnp.sum(l, axis=-1, keepdims=True))


def _diff_kernel(bound_ref, q_ref, gate_ref, k_ref, v_ref, lam_ref, tri_ref, wsrc_ref, o_ref, wdst_ref,
                 m1, l1, a1, m2, l2, a2, *, lambda_init, n_tiles):
    wdst_ref[...] = wsrc_ref[...].astype(wdst_ref.dtype)
    qi = pl.program_id(2)
    q = q_ref[0]
    q1 = q[:, :HEAD_DIM]
    q2 = q[:, HEAD_DIM:]
    bound = bound_ref[0]

    def finish(o1, o2):
        lp = lam_ref[...]
        lam = (jnp.exp(jnp.sum(lp[0:1] * lp[1:2], axis=-1, keepdims=True))
               - jnp.exp(jnp.sum(lp[2:3] * lp[3:4], axis=-1, keepdims=True)) + lambda_init)
        o = _rms(o1 - lam * o2) * (1.0 - lambda_init)
        o_ref[0] = (o * gate_ref[0].astype(F32)).astype(o_ref.dtype)

    def bounded_variant(n):
        @pl.when((bound <= MAX_SCORE_BOUND) & (qi == n))
        def _():
            finish(_diff_bounded_stream(q1, n, bound, k_ref, v_ref, tri_ref, slice(0, HEAD_DIM)),
                   _diff_bounded_stream(q2, n, bound, k_ref, v_ref, tri_ref, slice(HEAD_DIM, 2 * HEAD_DIM)))

    for n in range(n_tiles):
        bounded_variant(n)

    @pl.when(bound > MAX_SCORE_BOUND)
    def _():
        finish(*_diff_general(q1, q2, qi, k_ref, v_ref, m1, l1, a1, m2, l2, a2))


def _diff_attention(proj, lam_params, bound, lambda_init, cast_weight):
    b, s, _ = proj.shape
    h = DIFF_HEADS
    tq, dd = DIFF_TQ, DIFF_DIAG
    assert tq == 2 * dd
    w = DIFF_V_DIM
    tri = jnp.asarray(np.arange(tq)[:, None] >= np.arange(dd)[None, :], F32)
    q_spec = lambda off: pl.BlockSpec((1, tq, w), lambda bi, hi, qi: (bi, qi, off + hi))
    kv_spec = lambda off: pl.BlockSpec((1, s, w), lambda bi, hi, qi: (bi, 0, off + hi))
    stat = pltpu.VMEM((tq, LANES), F32)
    acc = pltpu.VMEM((tq, w), F32)
    nq = s // tq
    n_steps = b * h * nq
    cast_spec = pl.BlockSpec((cast_weight.shape[0] // n_steps, cast_weight.shape[1]),
                             lambda bi, hi, qi: ((bi * h + hi) * nq + qi, 0))
    return pl.pallas_call(
        functools.partial(_diff_kernel, lambda_init=lambda_init, n_tiles=nq),
        grid=(b, h, nq),
        in_specs=[pl.BlockSpec(memory_space=pltpu.SMEM),
                  q_spec(0), q_spec(3 * h), kv_spec(h), kv_spec(2 * h),
                  pl.BlockSpec(lam_params.shape, lambda bi, hi, qi: (0, 0)),
                  pl.BlockSpec(tri.shape, lambda bi, hi, qi: (0, 0)), cast_spec],
        out_specs=[pl.BlockSpec((1, tq, w), lambda bi, hi, qi: (bi, qi, hi)), cast_spec],
        out_shape=[jax.ShapeDtypeStruct((b, s, DIFF_WIDTH), BF16), jax.ShapeDtypeStruct(cast_weight.shape, BF16)],
        scratch_shapes=[stat, stat, acc, stat, stat, acc],
        compiler_params=pltpu.CompilerParams(
            dimension_semantics=("parallel", "parallel", "arbitrary"), vmem_limit_bytes=VMEM_LIMIT),
        name="diff_attention",
    )(bound, proj, proj, proj, proj, lam_params, tri, cast_weight)


def _rope_tables(pos):
    inv = 1.0 / (ROPE_THETA ** (np.arange(0, HEAD_DIM, 2, dtype=np.float64) / HEAD_DIM))
    ang = np.asarray(pos, np.float64)[:, None] * inv[None, :]
    cos, sin = np.cos(ang), np.sin(ang)
    return (jnp.asarray(np.concatenate([cos, cos], axis=-1), F32),
            jnp.asarray(np.concatenate([-sin, sin], axis=-1), F32))


def _retention_tables():
    h, c = RET_HEADS, RET_CHUNK
    log_g = np.log1p(-np.exp2(-5.0 - np.arange(h, dtype=np.float64)))
    j = np.arange(c, dtype=np.float64)
    diff = j[:, None] - j[None, :]
    intra = np.where(diff >= 0, np.exp(log_g[:, None, None] * np.maximum(diff, 0.0)), 0.0)
    q_dec = np.exp(log_g[:, None] * (j + 1.0))
    k_dec = np.exp(log_g[:, None] * (c - 1.0 - j))
    chunk_dec = np.exp(log_g * c)
    wide = lambda t: jnp.asarray(np.broadcast_to(t[:, :, None], (h, t.shape[1], HEAD_DIM)), F32)
    return jnp.asarray(intra, F32), wide(q_dec), wide(k_dec), wide(chunk_dec[:, None])


def _selection_tables(seq):
    tq, tk = NSA_TQ, NSA_TK
    n_cmp_rows = seq // CMP_STRIDE
    n_slc = seq // SLC_BLOCK
    assert tq == tk and n_slc <= AUG_ONE_LANE < LANES
    c_start = np.arange(n_cmp_rows) * CMP_STRIDE
    s_start = np.arange(n_slc) * SLC_BLOCK
    overlap_t = ((c_start[None, :] <= s_start[:, None] + SLC_BLOCK - 1)
                 & (c_start[None, :] + CMP_BLOCK - 1 >= s_start[:, None]))
    lane = np.arange(LANES)[None, :]
    key = np.arange(seq)[:, None]
    eaug = ((key // SLC_BLOCK) == lane) | (lane == AUG_ONE_LANE)
    cmask = lane * CMP_STRIDE + CMP_BLOCK - 1 <= key
    r = np.arange(tq)[:, None]
    c = np.arange(WINDOW + tq)[None, :]
    wmask = []
    for pat in range(WINDOW // tq + 1):
        d = min(pat * tq, WINDOW) + r - c
        wmask.append((d >= 0) & (d < WINDOW))
    tri = np.arange(tq)[:, None] >= np.arange(tk)[None, :]
    return (jnp.asarray(overlap_t, BF16), jnp.asarray(eaug, BF16), jnp.asarray(cmask, F32),
            jnp.asarray(np.stack(wmask), F32), jnp.asarray(tri, F32))


def _tile_modes(segments, tn):
    off, kinds, col = {}, [], 0
    for name, width, kind in segments:
        off[name] = col
        kinds += [kind] * (width // LANES)
        col += width
    per = tn // LANES
    tiles = [kinds[i:i + per] for i in range(0, len(kinds), per)]
    assert col % tn == 0
    modes = []
    for j, tile in enumerate(tiles):
        if modes and modes[-1][2] == tile:
            modes[-1] = (modes[-1][0], j + 1, tile)
        else:
            modes.append((j, j + 1, tile))
    return modes, off


def _l0_tile_modes():
    plain, silu = ("plain", 0), ("silu", 0)
    k_norm = ("nrope", L0_TAB_K)
    return _tile_modes([
        ("rq", RET_WIDTH, ("rope", L0_TAB_ROPE)), ("rk", RET_WIDTH, ("rope", L0_TAB_ROPE_SCALED)),
        ("rv", RET_WIDTH, plain), ("rgate", RET_WIDTH, silu), ("nq", NSA_WIDTH, ("nrope", L0_TAB_Q)),
        ("kc", NSA_KV_WIDTH, plain), ("vc", NSA_KV_WIDTH, plain), ("ks", NSA_KV_WIDTH, k_norm),
        ("vs", NSA_KV_WIDTH, plain), ("kw", NSA_KV_WIDTH, k_norm), ("vw", NSA_KV_WIDTH, plain),
        ("ngate", NSA_WIDTH, silu)], PROJ_TN)


def _l1_tile_modes():
    return _tile_modes([("q", DIFF_WIDTH, ("nrope", L1_TAB_Q)), ("k", DIFF_WIDTH, ("nrope", L1_TAB_K)),
                        ("v", DIFF_WIDTH, ("plain", 0)), ("gate", DIFF_WIDTH, ("silu", 0))], L1_PROJ_TN)[0]


def _rope_pair(cos, sin_signed, gain=None, scale=1.0):
    if gain is None:
        return [cos * scale, sin_signed * scale]
    return [cos * (gain * scale)[None, :], sin_signed * (jnp.roll(gain, HALF) * scale)[None, :]]


def kernel(x, l0_norm_g, l0_w_in, l0_w_out, l0_nsa_q_norm_g, l0_nsa_k_norm_g, l0_cmp_pe_k, l0_cmp_w1_k, l0_cmp_w2_k, l0_cmp_pe_v, l0_cmp_w1_v, l0_cmp_w2_v, l1_norm_g, l1_w_in, l1_w_out, l1_q_norm_g, l1_k_norm_g, l1_lambda_q1, l1_lambda_k1, l1_lambda_q2, l1_lambda_k2):
    b, s, d = x.shape
    m = b * s
    x2d = x.reshape(m, d)
    cos, sin = _rope_tables(np.arange(s))
    cos_c, sin_c = _rope_tables(np.arange(s // CMP_STRIDE) * CMP_STRIDE + CMP_BLOCK - 1)
    intra, qdec, kdec, cdec = _retention_tables()
    nsa_tables = _selection_tables(s)

    modes0, off = _l0_tile_modes()
    w0 = l0_w_in.astype(BF16)
    wg = l0_w_in[:, AB_MAIN_COLS:].reshape(d, 3, NSA_KV_HEADS, NSA_GROUP).transpose(0, 2, 1, 3)
    wg = wg.reshape(d, NSA_KV_HEADS, 3 * NSA_GROUP)
    wg = jnp.pad(wg, ((0, 0), (0, 0), (0, LANES - 3 * NSA_GROUP))).reshape(d, NSA_KV_HEADS * LANES).astype(BF16)
    tabs0 = jnp.stack(_rope_pair(cos, sin) + _rope_pair(cos, sin, scale=QK_SCALE)
                      + _rope_pair(cos, sin, l0_nsa_q_norm_g, Q_SCALE) + _rope_pair(cos, sin, l0_nsa_k_norm_g))
    proj0, gates = _norm_proj(x2d, l0_norm_g, w0, tabs0, modes0, s, PROJ_TN, wg=wg)
    proj0 = proj0.reshape(b, s, AB_MAIN_COLS)
    y_ret = _retention(proj0, intra, qdec, kdec, cdec)
    w1k = l0_cmp_w1_k.astype(BF16).reshape(CMP_BLOCK // 2, 2 * HEAD_DIM, HEAD_DIM)
    w1v = l0_cmp_w1_v.astype(BF16).reshape(CMP_BLOCK // 2, 2 * HEAD_DIM, HEAD_DIM)
    kcmp, vcmp = _nsa_compress(proj0, off["kc"] // HEAD_DIM, off["vc"] // HEAD_DIM,
                               l0_cmp_pe_k, w1k, l0_cmp_w2_k.astype(BF16),
                               l0_cmp_pe_v, w1v, l0_cmp_w2_v.astype(BF16),
                               l0_nsa_k_norm_g, cos_c, sin_c)
    bound = (BOUND_MARGIN * HEAD_DIM * Q_SCALE * jnp.max(jnp.abs(l0_nsa_q_norm_g))
             * jnp.max(jnp.abs(l0_nsa_k_norm_g))).reshape(1).astype(F32)
    y_nsa, w_out0, w_in1 = _nsa_attention(proj0, gates, kcmp, vcmp, bound, nsa_tables, off,
                                          cast_weights=[l0_w_out, l1_w_in])
    x1 = _out_proj([y_ret.reshape(m, RET_WIDTH), y_nsa.reshape(m, NSA_WIDTH)],
                   [w_out0[:RET_WIDTH], w_out0[RET_WIDTH:]], x2d)

    lambda_init = 0.8 - 0.6 * math.exp(-0.3 * 1)
    tabs1 = jnp.stack(_rope_pair(cos, sin, l1_q_norm_g, Q_SCALE) + _rope_pair(cos, sin, l1_k_norm_g))
    proj1 = _norm_proj(x1, l1_norm_g, w_in1, tabs1, _l1_tile_modes(), s, L1_PROJ_TN)[0]
    lam_params = jnp.stack([l1_lambda_q1, l1_lambda_k1, l1_lambda_q2, l1_lambda_k2]).astype(F32)
    bound1 = (BOUND_MARGIN * HEAD_DIM * Q_SCALE * jnp.max(jnp.abs(l1_q_norm_g))
              * jnp.max(jnp.abs(l1_k_norm_g))).reshape(1).astype(F32)
    y1, w_out1 = _diff_attention(proj1.reshape(b, s, C_IN_COLS), lam_params, bound1, lambda_init, l1_w_out)
    out = _out_proj([y1.reshape(m, DIFF_WIDTH)], [w_out1], x1)
    return out.reshape(b, s, d)
```
